```python
import math
import jax, jax.numpy as jnp
from jax import lax
import numpy as np

D_MODEL = 2048
BATCH = 8
SEQ = 8192
DEPTH = 4

N_MIXERS = 2
HEAD_DIM = 64
N_HEADS = D_MODEL // HEAD_DIM
SWA_KV_HEADS = 8
SWA_GROUP = N_HEADS // SWA_KV_HEADS
WINDOW = 128
BLOCK = 128
ROPE_THETA = 500000.0
ROT_DIM = HEAD_DIM // 4
D_FF = int(math.ceil((8 * D_MODEL / 3) / 256) * 256)
N_SWA = (DEPTH + 1) // 2
N_FOX = DEPTH // 2
RMS_EPS = 1e-6

kernel_name = "hybrid_swa_sink_fox_swiglu_sandwich"


def rmsnorm(x, g):
    x32 = x.astype(jnp.float32)
    y = x32 * lax.rsqrt(jnp.mean(x32 * x32, axis=-1, keepdims=True) + RMS_EPS) * g.astype(jnp.float32)
    return y.astype(x.dtype)


def rotary_tables(positions, dtype):
    inv_freq = ROPE_THETA ** (-jnp.arange(0, ROT_DIM, 2, dtype=jnp.float32) / ROT_DIM)
    ang = positions.astype(jnp.float32)[..., None] * inv_freq
    return jnp.cos(ang)[:, :, None, :].astype(dtype), jnp.sin(ang)[:, :, None, :].astype(dtype)


def partial_rotary(x, cos, sin):
    xr, xp = x[..., :ROT_DIM], x[..., ROT_DIM:]
    x1, x2 = xr[..., :ROT_DIM // 2], xr[..., ROT_DIM // 2:]
    rot = jnp.concatenate([x1 * cos - x2 * sin, x2 * cos + x1 * sin], axis=-1)
    return jnp.concatenate([rot, xp], axis=-1)


def swa_sink_attention(h, w_in, sinks, w_out, cos, sin):
    B, S, _ = h.shape
    nblk = S // BLOCK
    qkv = h @ w_in
    qd, kd = N_HEADS * HEAD_DIM, SWA_KV_HEADS * HEAD_DIM
    q = qkv[..., :qd].reshape(B, S, N_HEADS, HEAD_DIM)
    k = qkv[..., qd:qd + kd].reshape(B, S, SWA_KV_HEADS, HEAD_DIM)
    v = qkv[..., qd + kd:].reshape(B, S, SWA_KV_HEADS, HEAD_DIM)
    q = partial_rotary(q, cos, sin)
    k = partial_rotary(k, cos, sin)

    qb = q.reshape(B, nblk, BLOCK, SWA_KV_HEADS, SWA_GROUP, HEAD_DIM)

    def band(t):
        tb = t.reshape(B, nblk, BLOCK, SWA_KV_HEADS, HEAD_DIM)
        prev = jnp.pad(tb[:, :-1], ((0, 0), (1, 0), (0, 0), (0, 0), (0, 0)))
        return jnp.concatenate([prev, tb], axis=2)

    kband, vband = band(k), band(v)
    scale = HEAD_DIM ** -0.5
    s = jnp.einsum('bnqkgd,bnskd->bnkgqs', qb, kband).astype(jnp.float32) * scale

    q_loc = jnp.arange(BLOCK)[:, None]
    s_loc = jnp.arange(2 * BLOCK)[None, :]
    rel = BLOCK + q_loc - s_loc
    in_win = (rel >= 0) & (rel < WINDOW)
    has_prev = (jnp.arange(nblk) > 0)[:, None, None] | (s_loc >= BLOCK)[None]
    valid = (in_win[None] & has_prev)[None, :, None, None]
    s = jnp.where(valid, s, -jnp.inf)

    sink = sinks.astype(jnp.float32).reshape(SWA_KV_HEADS, SWA_GROUP)[None, None, :, :, None, None]
    m = jnp.maximum(jnp.max(s, axis=-1, keepdims=True), sink)
    p = jnp.exp(s - m)
    denom = jnp.sum(p, axis=-1, keepdims=True) + jnp.exp(sink - m)
    probs = (p / denom).astype(v.dtype)
    out = jnp.einsum('bnkgqs,bnskd->bnqkgd', probs, vband).reshape(B, S, N_HEADS * HEAD_DIM)
    return out @ w_out


def forgetting_attention(h, w_in, b_f, w_out):
    B, S, _ = h.shape
    nblk = S // BLOCK
    hd = N_HEADS * HEAD_DIM
    proj = h @ w_in
    q = proj[..., :hd].reshape(B, S, N_HEADS, HEAD_DIM)
    k = proj[..., hd:2 * hd].reshape(B, S, N_HEADS, HEAD_DIM)
    v = proj[..., 2 * hd:3 * hd].reshape(B, S, N_HEADS, HEAD_DIM)
    f_logit = proj[..., 3 * hd:]
    log_f = jax.nn.log_sigmoid(f_logit.astype(jnp.float32) + b_f.astype(jnp.float32))
    c = jnp.transpose(jnp.cumsum(log_f, axis=1), (0, 2, 1))
    key_pos = jnp.arange(S)
    scale = HEAD_DIM ** -0.5

    def block(i):
        start = i * BLOCK
        qi = lax.dynamic_slice_in_dim(q, start, BLOCK, axis=1)
        ci = lax.dynamic_slice_in_dim(c, start, BLOCK, axis=2)
        s = jnp.einsum('bqhd,bshd->bhqs', qi, k).astype(jnp.float32) * scale
        s = s + ci[..., :, None] - c[:, :, None, :]
        t_pos = start + jnp.arange(BLOCK)
        s = jnp.where(key_pos[None, :] <= t_pos[:, None], s, -jnp.inf)
        p = jax.nn.softmax(s, axis=-1).astype(v.dtype)
        return jnp.einsum('bhqs,bshd->bqhd', p, v)

    out = lax.map(block, jnp.arange(nblk))
    out = jnp.transpose(out, (1, 0, 2, 3, 4)).reshape(B, S, hd)
    return out @ w_out


def swiglu_ffn(h, w_gate_up, w_down):
    gu = h @ w_gate_up
    gate, up = gu[..., :D_FF], gu[..., D_FF:]
    return (jax.nn.silu(gate) * up) @ w_down


def _fwd_setup_inputs(seed: int = 0) -> dict:
    key = jax.random.key(seed)
    ks = jax.random.split(key, 10)
    f32 = jnp.float32
    x = jax.random.normal(ks[0], (BATCH, SEQ, D_MODEL), f32)
    positions = jnp.broadcast_to(jnp.arange(SEQ, dtype=jnp.int32), (BATCH, SEQ)).astype(jnp.int32)
    norm_gains = 1.0 + 0.05 * jax.random.normal(ks[1], (DEPTH, 4, D_MODEL), f32)
    swa_in_dim = (N_HEADS + 2 * SWA_KV_HEADS) * HEAD_DIM
    swa_w_in = jax.random.normal(ks[2], (N_SWA, D_MODEL, swa_in_dim), f32) * D_MODEL ** -0.5
    swa_sinks = 0.5 * jax.random.normal(ks[3], (N_SWA, N_HEADS), f32)
    swa_w_out = jax.random.normal(ks[4], (N_SWA, N_HEADS * HEAD_DIM, D_MODEL), f32) * (N_HEADS * HEAD_DIM) ** -0.5
    fox_in_dim = 3 * N_HEADS * HEAD_DIM + N_HEADS
    fox_w_in = jax.random.normal(ks[5], (N_FOX, D_MODEL, fox_in_dim), f32) * D_MODEL ** -0.5
    fox_b_f = 2.0 + 0.5 * jax.random.normal(ks[6], (N_FOX, N_HEADS), f32)
    fox_w_out = jax.random.normal(ks[7], (N_FOX, N_HEADS * HEAD_DIM, D_MODEL), f32) * (N_HEADS * HEAD_DIM) ** -0.5
    ffn_w_gate_up = jax.random.normal(ks[8], (DEPTH, D_MODEL, 2 * D_FF), f32) * D_MODEL ** -0.5
    ffn_w_down = jax.random.normal(ks[9], (DEPTH, D_FF, D_MODEL), f32) * D_FF ** -0.5
    return {"x": x, "positions": positions, "norm_gains": norm_gains,
            "swa_w_in": swa_w_in, "swa_sinks": swa_sinks, "swa_w_out": swa_w_out,
            "fox_w_in": fox_w_in, "fox_b_f": fox_b_f, "fox_w_out": fox_w_out,
            "ffn_w_gate_up": ffn_w_gate_up, "ffn_w_down": ffn_w_down}


def _fwd_reference(x, positions, norm_gains, swa_w_in, swa_sinks, swa_w_out,
              fox_w_in, fox_b_f, fox_w_out, ffn_w_gate_up, ffn_w_down):
    cos, sin = rotary_tables(positions, x.dtype)
    for i in range(DEPTH):
        g = norm_gains[i]
        h = rmsnorm(x, g[0])
        if i % N_MIXERS == 0:
            j = i // N_MIXERS
            y = swa_sink_attention(h, swa_w_in[j], swa_sinks[j], swa_w_out[j], cos, sin)
        else:
            j = i // N_MIXERS
            y = forgetting_attention(h, fox_w_in[j], fox_b_f[j], fox_w_out[j])
        x = x + rmsnorm(y, g[1])
        h = rmsnorm(x, g[2])
        y = swiglu_ffn(h, ffn_w_gate_up[i], ffn_w_down[i])
        x = x + rmsnorm(y, g[3])
    return x


import jax as _jax
import jax.numpy as _jnp

TWIN_FORMAT = 'train_step'
FWD_PARAMS = ['x', 'positions', 'norm_gains', 'swa_w_in', 'swa_sinks', 'swa_w_out', 'fox_w_in', 'fox_b_f', 'fox_w_out', 'ffn_w_gate_up', 'ffn_w_down']
TWIN_WEIGHTS = ['norm_gains', 'swa_w_in', 'swa_sinks', 'swa_w_out', 'fox_w_in', 'fox_b_f', 'fox_w_out', 'ffn_w_gate_up', 'ffn_w_down']
TWIN_DIFF_INPUT = 'x'
TWIN_INPUTS = ['x', 'positions', 'norm_gains', 'swa_w_in', 'swa_sinks', 'swa_w_out', 'fox_w_in', 'fox_b_f', 'fox_w_out', 'ffn_w_gate_up', 'ffn_w_down', 'loss_target', 'm_norm_gains', 'm_swa_w_in', 'm_swa_sinks', 'm_swa_w_out', 'm_fox_w_in', 'm_fox_b_f', 'm_fox_w_out', 'm_ffn_w_gate_up', 'm_ffn_w_down', 'v_norm_gains', 'v_swa_w_in', 'v_swa_sinks', 'v_swa_w_out', 'v_fox_w_in', 'v_fox_b_f', 'v_fox_w_out', 'v_ffn_w_gate_up', 'v_ffn_w_down']
TWIN_OUTPUTS = ['loss', 'grad_x', 'grad_norm_gains', 'grad_swa_w_in', 'grad_swa_sinks', 'grad_swa_w_out', 'grad_fox_w_in', 'grad_fox_b_f', 'grad_fox_w_out', 'grad_ffn_w_gate_up', 'grad_ffn_w_down', 'delta_norm_gains', 'delta_swa_w_in', 'delta_swa_sinks', 'delta_swa_w_out', 'delta_fox_w_in', 'delta_fox_b_f', 'delta_fox_w_out', 'delta_ffn_w_gate_up', 'delta_ffn_w_down', 'new_m_norm_gains', 'new_m_swa_w_in', 'new_m_swa_sinks', 'new_m_swa_w_out', 'new_m_fox_w_in', 'new_m_fox_b_f', 'new_m_fox_w_out', 'new_m_ffn_w_gate_up', 'new_m_ffn_w_down', 'new_v_norm_gains', 'new_v_swa_w_in', 'new_v_swa_sinks', 'new_v_swa_w_out', 'new_v_fox_w_in', 'new_v_fox_b_f', 'new_v_fox_w_out', 'new_v_ffn_w_gate_up', 'new_v_ffn_w_down']
TWIN_LEAF_KINDS = {'loss': 'loss', 'grad_x': 'grad_x', 'grad_norm_gains': 'grad_w', 'grad_swa_w_in': 'grad_w', 'grad_swa_sinks': 'grad_w', 'grad_swa_w_out': 'grad_w', 'grad_fox_w_in': 'grad_w', 'grad_fox_b_f': 'grad_w', 'grad_fox_w_out': 'grad_w', 'grad_ffn_w_gate_up': 'grad_w', 'grad_ffn_w_down': 'grad_w', 'delta_norm_gains': 'delta_w', 'delta_swa_w_in': 'delta_w', 'delta_swa_sinks': 'delta_w', 'delta_swa_w_out': 'delta_w', 'delta_fox_w_in': 'delta_w', 'delta_fox_b_f': 'delta_w', 'delta_fox_w_out': 'delta_w', 'delta_ffn_w_gate_up': 'delta_w', 'delta_ffn_w_down': 'delta_w', 'new_m_norm_gains': 'new_m', 'new_m_swa_w_in': 'new_m', 'new_m_swa_sinks': 'new_m', 'new_m_swa_w_out': 'new_m', 'new_m_fox_w_in': 'new_m', 'new_m_fox_b_f': 'new_m', 'new_m_fox_w_out': 'new_m', 'new_m_ffn_w_gate_up': 'new_m', 'new_m_ffn_w_down': 'new_m', 'new_v_norm_gains': 'new_v', 'new_v_swa_w_in': 'new_v', 'new_v_swa_sinks': 'new_v', 'new_v_swa_w_out': 'new_v', 'new_v_fox_w_in': 'new_v', 'new_v_fox_b_f': 'new_v', 'new_v_fox_w_out': 'new_v', 'new_v_ffn_w_gate_up': 'new_v', 'new_v_ffn_w_down': 'new_v'}


def _forward(args):
    return _fwd_reference(*[args[k] for k in FWD_PARAMS])


def _output_shape():
    def fwd():
        inp = _fwd_setup_inputs(0)
        return _fwd_reference(*[inp[k] for k in FWD_PARAMS])
    out = _jax.eval_shape(fwd)
    return out.shape, out.dtype

N_MICROBATCH = 1
ADAM_LR = 0.001
ADAM_B1 = 0.9
ADAM_B2 = 0.999
ADAM_EPS = 1e-08
ADAM_WD = 0.01
ADAM_STEP = 10
PER_EXAMPLE_BATCH_AXIS = {'x': 0, 'positions': 0, 'loss_target': 0}
SHARED_INPUTS = []
_WEIGHT_DTYPES = {'norm_gains': _jnp.float32, 'swa_w_in': _jnp.float32, 'swa_sinks': _jnp.float32, 'swa_w_out': _jnp.float32, 'fox_w_in': _jnp.float32, 'fox_b_f': _jnp.float32, 'fox_w_out': _jnp.float32, 'ffn_w_gate_up': _jnp.float32, 'ffn_w_down': _jnp.float32}
MOMENT_SCALE = {'norm_gains': 2.024095e+01, 'swa_w_in': 3.059396e+00, 'swa_sinks': 1.098341e+00, 'swa_w_out': 3.511199e+00, 'fox_w_in': 1.532721e+00, 'fox_b_f': 2.734488e+00, 'fox_w_out': 2.635563e+00, 'ffn_w_gate_up': 6.482779e-01, 'ffn_w_down': 1.196015e+00}


def _to_microbatches(a, axis):
    t = _jnp.moveaxis(a, axis, 0)
    t = t.reshape((N_MICROBATCH, t.shape[0] // N_MICROBATCH) + t.shape[1:])
    return _jnp.moveaxis(t, 1, axis + 1)


def setup_inputs(seed: int = 0) -> dict:
    inp = _fwd_setup_inputs(seed)
    key = _jax.random.fold_in(_jax.random.key(seed), 7919)
    shape, _ = _output_shape()
    out = dict(inp)
    out["loss_target"] = _jax.random.normal(_jax.random.fold_in(key, 0), shape, _jnp.float32)
    for i, name in enumerate(TWIN_WEIGHTS):
        w = inp[name].astype(_jnp.float32)
        if MOMENT_SCALE is None:
            s = _jnp.sqrt(_jnp.mean(_jnp.square(w)) + 1e-30)
        else:
            s = MOMENT_SCALE[name]
        km, kv = _jax.random.split(_jax.random.fold_in(key, i + 1))
        out[name] = w
        out["m_" + name] = s * _jax.random.normal(km, w.shape, _jnp.float32)
        out["v_" + name] = (s * s) * _jax.random.uniform(kv, w.shape, _jnp.float32, 0.5, 1.5)
    if N_MICROBATCH > 1:
        for name, axis in PER_EXAMPLE_BATCH_AXIS.items():
            out[name] = _to_microbatches(out[name], axis)
    return {'x': out['x'], 'positions': out['positions'], 'norm_gains': out['norm_gains'], 'swa_w_in': out['swa_w_in'], 'swa_sinks': out['swa_sinks'], 'swa_w_out': out['swa_w_out'], 'fox_w_in': out['fox_w_in'], 'fox_b_f': out['fox_b_f'], 'fox_w_out': out['fox_w_out'], 'ffn_w_gate_up': out['ffn_w_gate_up'], 'ffn_w_down': out['ffn_w_down'], 'loss_target': out['loss_target'], 'm_norm_gains': out['m_norm_gains'], 'm_swa_w_in': out['m_swa_w_in'], 'm_swa_sinks': out['m_swa_sinks'], 'm_swa_w_out': out['m_swa_w_out'], 'm_fox_w_in': out['m_fox_w_in'], 'm_fox_b_f': out['m_fox_b_f'], 'm_fox_w_out': out['m_fox_w_out'], 'm_ffn_w_gate_up': out['m_ffn_w_gate_up'], 'm_ffn_w_down': out['m_ffn_w_down'], 'v_norm_gains': out['v_norm_gains'], 'v_swa_w_in': out['v_swa_w_in'], 'v_swa_sinks': out['v_swa_sinks'], 'v_swa_w_out': out['v_swa_w_out'], 'v_fox_w_in': out['v_fox_w_in'], 'v_fox_b_f': out['v_fox_b_f'], 'v_fox_w_out': out['v_fox_w_out'], 'v_ffn_w_gate_up': out['v_ffn_w_gate_up'], 'v_ffn_w_down': out['v_ffn_w_down']}


def _loss(weights, diff, rest, loss_target):
    with _jax.named_scope("forward"):
        args = {**rest, TWIN_DIFF_INPUT: diff, **{k: w.astype(_WEIGHT_DTYPES[k]) for k, w in weights.items()}}
        y = _forward(args)
    with _jax.named_scope("loss_head"):
        err = _jnp.square(y.astype(_jnp.float32) - loss_target)
        return 0.5 * _jnp.sum(_jnp.mean(err, axis=-1)) if err.ndim else 0.5 * err


def _adamw(w, g, m, v):
    m = ADAM_B1 * m + (1.0 - ADAM_B1) * g
    v = ADAM_B2 * v + (1.0 - ADAM_B2) * _jnp.square(g)
    m_hat = m / (1.0 - ADAM_B1 ** ADAM_STEP)
    v_hat = v / (1.0 - ADAM_B2 ** ADAM_STEP)
    delta = -ADAM_LR * (m_hat / (_jnp.sqrt(v_hat) + ADAM_EPS) + ADAM_WD * w)
    return delta, m, v


def reference(x, positions, norm_gains, swa_w_in, swa_sinks, swa_w_out, fox_w_in, fox_b_f, fox_w_out, ffn_w_gate_up, ffn_w_down, loss_target, m_norm_gains, m_swa_w_in, m_swa_sinks, m_swa_w_out, m_fox_w_in, m_fox_b_f, m_fox_w_out, m_ffn_w_gate_up, m_ffn_w_down, v_norm_gains, v_swa_w_in, v_swa_sinks, v_swa_w_out, v_fox_w_in, v_fox_b_f, v_fox_w_out, v_ffn_w_gate_up, v_ffn_w_down):
    given = dict(x=x, positions=positions, norm_gains=norm_gains, swa_w_in=swa_w_in, swa_sinks=swa_sinks, swa_w_out=swa_w_out, fox_w_in=fox_w_in, fox_b_f=fox_b_f, fox_w_out=fox_w_out, ffn_w_gate_up=ffn_w_gate_up, ffn_w_down=ffn_w_down, loss_target=loss_target, m_norm_gains=m_norm_gains, m_swa_w_in=m_swa_w_in, m_swa_sinks=m_swa_sinks, m_swa_w_out=m_swa_w_out, m_fox_w_in=m_fox_w_in, m_fox_b_f=m_fox_b_f, m_fox_w_out=m_fox_w_out, m_ffn_w_gate_up=m_ffn_w_gate_up, m_ffn_w_down=m_ffn_w_down, v_norm_gains=v_norm_gains, v_swa_w_in=v_swa_w_in, v_swa_sinks=v_swa_sinks, v_swa_w_out=v_swa_w_out, v_fox_w_in=v_fox_w_in, v_fox_b_f=v_fox_b_f, v_fox_w_out=v_fox_w_out, v_ffn_w_gate_up=v_ffn_w_gate_up, v_ffn_w_down=v_ffn_w_down)
    weights = {n: given[n] for n in TWIN_WEIGHTS}
    shared = {n: given[n] for n in SHARED_INPUTS}
    per_example = {n: given[n] for n in ['x', 'positions']}
    grad_fn = _jax.value_and_grad(_loss, argnums=(0, 1))

    def one_microbatch(ex, loss_target):
        ex = dict(ex)
        diff = ex.pop(TWIN_DIFF_INPUT)
        return grad_fn(weights, diff, {**shared, **ex}, loss_target)

    if N_MICROBATCH == 1:
        loss, (grad_w, grad_x) = one_microbatch(per_example, given["loss_target"])
    else:
        def body(carry, xs):
            loss_sum, grad_sum = carry
            l_k, (gw_k, gx_k) = one_microbatch(xs[0], xs[1])
            with _jax.named_scope("update"):
                return (loss_sum + l_k, _jax.tree.map(_jnp.add, grad_sum, gw_k)), gx_k

        init = (_jnp.zeros((), _jnp.float32), _jax.tree.map(_jnp.zeros_like, weights))
        (loss, grad_w), grad_x = _jax.lax.scan(body, init, (per_example, given["loss_target"]))
    with _jax.named_scope("update"):
        delta_w, new_m, new_v = {}, {}, {}
        for n in TWIN_WEIGHTS:
            delta_w[n], new_m[n], new_v[n] = _adamw(weights[n], grad_w[n], given["m_" + n], given["v_" + n])
    return (loss, grad_x, *[grad_w[n] for n in TWIN_WEIGHTS], *[delta_w[n] for n in TWIN_WEIGHTS],
            *[new_m[n] for n in TWIN_WEIGHTS], *[new_v[n] for n in TWIN_WEIGHTS])
```

```python
import functools

import jax
import jax.numpy as jnp
from jax import lax
from jax.experimental import pallas as pl
from jax.experimental.pallas import tpu as pltpu

F32 = jnp.float32
BF16 = jnp.bfloat16
HEAD_DIM = 64
LANES = 128
WINDOW = 128
ROPE_THETA = 500000.0
ROT_DIM = HEAD_DIM // 4
RMS_EPS = 1e-6
ADAM_LR, ADAM_B1, ADAM_B2, ADAM_EPS, ADAM_WD, ADAM_STEP = 0.001, 0.9, 0.999, 1e-08, 0.01, 10
NEG = -1e30
VMEM_BIG = 56 * 1024 * 1024
N_CHIPS = 4
MESH = pl.DeviceIdType.MESH
ANY = pl.BlockSpec(memory_space=pl.ANY)


def _pcall(body, **kw):
    return pl.pallas_call(body, **kw)


def _cp(sem=None, vmem=None):
    return pltpu.CompilerParams(dimension_semantics=sem, vmem_limit_bytes=vmem)


def _tile(n, pref):
    if n <= pref:
        return n
    t = pref - pref % 8
    while n % t:
        t -= 8
    return t


_DIMS = {"nn": (((1,), (0,)), ((), ())), "nt": (((1,), (1,)), ((), ())), "tn": (((0,), (0,)), ((), ()))}


def _matmul(kind, a, b, *, grid, a_spec, b_spec, o_spec, out_shape, acc_shape, name, into=None):
    nk = grid[2]
    dims = _DIMS[kind]

    def body(a_ref, b_ref, *rest):
        rest = rest[1:] if into is not None else rest
        o_ref = rest[0]

        def prod():
            return lax.dot_general(a_ref[...], b_ref[...], dims, preferred_element_type=F32)

        if nk == 1:
            o_ref[...] = prod().astype(o_ref.dtype)
        else:
            acc = rest[1]
            k = pl.program_id(2)

            @pl.when(k == 0)
            def _():
                acc[...] = prod()

            @pl.when(k > 0)
            def _():
                acc[...] += prod()

            @pl.when(k == nk - 1)
            def _():
                o_ref[...] = acc[...].astype(o_ref.dtype)

    args, in_specs, aliases = [a, b], [a_spec, b_spec], {}
    if into is not None:
        args.append(into)
        in_specs.append(ANY)
        aliases = {2: 0}
        out_shape = jax.ShapeDtypeStruct(into.shape, into.dtype)
    return _pcall(body, name=name, grid=grid, in_specs=in_specs, out_specs=o_spec, out_shape=out_shape,
                  scratch_shapes=[] if nk == 1 else [pltpu.VMEM(acc_shape, F32)], input_output_aliases=aliases,
                  compiler_params=_cp(("parallel", "parallel", "arbitrary"), VMEM_BIG))(*args)


def mm_nn(a, w, w_map, n_out, *, tm, tn, tk, out_dtype, name):
    m, kdim = a.shape
    tm = min(tm, m)
    lead = (None,) * (w.ndim - 2)
    return _matmul("nn", a, w, grid=(m // tm, n_out // tn, kdim // tk),
                   a_spec=pl.BlockSpec((tm, tk), lambda i, j, k: (i, k)),
                   b_spec=pl.BlockSpec(lead + (tk, tn), lambda i, j, k: w_map(j, k)),
                   o_spec=pl.BlockSpec((tm, tn), lambda i, j, k: (i, j)),
                   out_shape=jax.ShapeDtypeStruct((m, n_out), out_dtype), acc_shape=(tm, tn), name=name)


def mm_nt(a, w, w_map, k_out, *, tm, tko, tn, out_dtype, name):
    m, ndim = a.shape
    tm = min(tm, m)
    lead = (None,) * (w.ndim - 2)
    return _matmul("nt", a, w, grid=(m // tm, k_out // tko, ndim // tn),
                   a_spec=pl.BlockSpec((tm, tn), lambda i, j, n: (i, n)),
                   b_spec=pl.BlockSpec(lead + (tko, tn), lambda i, j, n: w_map(j, n)),
                   o_spec=pl.BlockSpec((tm, tko), lambda i, j, n: (i, j)),
                   out_shape=jax.ShapeDtypeStruct((m, k_out), out_dtype), acc_shape=(tm, tko), name=name)


def mm_tn(a, b, o_map, *, tka, tn, tm, out_block, name, into=None, out_shape=None):
    m, kdim = a.shape
    n = b.shape[1]
    tm = min(tm, m)
    return _matmul("tn", a, b, grid=(kdim // tka, n // tn, m // tm),
                   a_spec=pl.BlockSpec((tm, tka), lambda i, j, mm: (mm, i)),
                   b_spec=pl.BlockSpec((tm, tn), lambda i, j, mm: (mm, j)),
                   o_spec=pl.BlockSpec(out_block, lambda i, j, mm: o_map(i, j)),
                   out_shape=out_shape, acc_shape=(tka, tn), name=name, into=into)


def _rstd(v):
    return lax.rsqrt(jnp.mean(v * v, axis=-1, keepdims=True) + RMS_EPS)


def _row_spec(tr, d):
    return pl.BlockSpec((tr, d), lambda i: (i, 0))


def _vec_spec(d):
    return pl.BlockSpec((1, d), lambda i: (0, 0))


def prenorm(x, g, name):
    s, d = x.shape
    tr = _tile(s, 256)

    def body(x_ref, g_ref, h_ref):
        v = x_ref[...]
        h_ref[...] = (v * _rstd(v) * g_ref[...]).astype(BF16)

    return _pcall(body, name=name, grid=(s // tr,), in_specs=[_row_spec(tr, d), _vec_spec(d)],
                  out_specs=_row_spec(tr, d), out_shape=jax.ShapeDtypeStruct((s, d), BF16),
                  compiler_params=_cp(("parallel",)))(x, g)


def postnorm_residual(x, y, g_post, g_next, name):
    s, d = x.shape
    tr = _tile(s, 256)

    def body(x_ref, y_ref, gp_ref, gn_ref, xo_ref, h_ref):
        v = y_ref[...]
        xn = x_ref[...] + v * _rstd(v) * gp_ref[...]
        xo_ref[...] = xn
        h_ref[...] = (xn * _rstd(xn) * gn_ref[...]).astype(BF16)

    return _pcall(body, name=name, grid=(s // tr,),
                  in_specs=[_row_spec(tr, d), _row_spec(tr, d), _vec_spec(d), _vec_spec(d)],
                  out_specs=[_row_spec(tr, d), _row_spec(tr, d)],
                  out_shape=[jax.ShapeDtypeStruct((s, d), F32), jax.ShapeDtypeStruct((s, d), BF16)],
                  compiler_params=_cp(("parallel",)))(x, y, g_post, g_next)


def postnorm_loss(x, y, g_post, target, name):
    s, d = x.shape
    tr = _tile(s, 256)

    def body(x_ref, y_ref, gp_ref, t_ref, dx_ref, loss_ref):
        v = y_ref[...]
        err = x_ref[...] + v * _rstd(v) * gp_ref[...] - t_ref[...]
        dx_ref[...] = err / d
        part = 0.5 * jnp.sum(jnp.mean(err * err, axis=-1, keepdims=True), axis=0, keepdims=True)

        @pl.when(pl.program_id(0) == 0)
        def _():
            loss_ref[...] = jnp.zeros_like(loss_ref)

        loss_ref[...] += part

    return _pcall(body, name=name, grid=(s // tr,),
                  in_specs=[_row_spec(tr, d), _row_spec(tr, d), _vec_spec(d), _row_spec(tr, d)],
                  out_specs=[_row_spec(tr, d), pl.BlockSpec((8, LANES), lambda i: (0, 0))],
                  out_shape=[jax.ShapeDtypeStruct((s, d), F32), jax.ShapeDtypeStruct((8, LANES), F32)],
                  compiler_params=_cp(("arbitrary",)))(x, y, g_post, target)


def _norm_bwd(v, g, dz):
    r = _rstd(v)
    vhat = v * r
    u = dz * g
    dv = r * (u - vhat * jnp.mean(u * vhat, axis=-1, keepdims=True))
    return dv, jnp.sum(dz * vhat, axis=0, keepdims=True)


def _acc_rows(ref, val):
    @pl.when(pl.program_id(0) == 0)
    def _():
        ref[...] = jnp.zeros_like(ref)

    ref[...] += val


def postnorm_bwd(dz, y, g, name):
    s, d = y.shape
    tr = _tile(s, 256)

    def body(dz_ref, y_ref, g_ref, dy_ref, dg_ref):
        dv, dg = _norm_bwd(y_ref[...], g_ref[...], dz_ref[...])
        dy_ref[...] = dv.astype(BF16)
        _acc_rows(dg_ref, dg)

    return _pcall(body, name=name, grid=(s // tr,), in_specs=[_row_spec(tr, d), _row_spec(tr, d), _vec_spec(d)],
                  out_specs=[_row_spec(tr, d), _vec_spec(d)],
                  out_shape=[jax.ShapeDtypeStruct((s, d), BF16), jax.ShapeDtypeStruct((1, d), F32)],
                  compiler_params=_cp(("arbitrary",)))(dz, y, g)


def prenorm_bwd(x, dh, dskip, g, name):
    s, d = x.shape
    tr = _tile(s, 256)

    def body(x_ref, dh_ref, ds_ref, g_ref, dx_ref, dg_ref):
        dv, dg = _norm_bwd(x_ref[...], g_ref[...], dh_ref[...])
        dx_ref[...] = ds_ref[...] + dv
        _acc_rows(dg_ref, dg)

    return _pcall(body, name=name, grid=(s // tr,),
                  in_specs=[_row_spec(tr, d), _row_spec(tr, d), _row_spec(tr, d), _vec_spec(d)],
                  out_specs=[_row_spec(tr, d), _vec_spec(d)],
                  out_shape=[jax.ShapeDtypeStruct((s, d), F32), jax.ShapeDtypeStruct((1, d), F32)],
                  compiler_params=_cp(("arbitrary",)))(x, dh, dskip, g)


def swiglu_fwd(gu, name):
    s, two_ff = gu.shape
    ff = two_ff // 2
    tr = _tile(s, 128)

    def body(gu_ref, act_ref):
        gate = gu_ref[:, :ff].astype(F32)
        up = gu_ref[:, ff:].astype(F32)
        act_ref[...] = (gate * jax.nn.sigmoid(gate) * up).astype(BF16)

    return _pcall(body, name=name, grid=(s // tr,), in_specs=[_row_spec(tr, two_ff)], out_specs=_row_spec(tr, ff),
                  out_shape=jax.ShapeDtypeStruct((s, ff), BF16), compiler_params=_cp(("parallel",)))(gu)


def swiglu_bwd(dact, gu, name):
    s, two_ff = gu.shape
    ff = two_ff // 2
    tr = _tile(s, 128)

    def body(da_ref, gu_ref, dgu_ref):
        gate = gu_ref[:, :ff].astype(F32)
        up = gu_ref[:, ff:].astype(F32)
        da = da_ref[...].astype(F32)
        sig = jax.nn.sigmoid(gate)
        dgu_ref[:, :ff] = (da * up * sig * (1.0 + gate * (1.0 - sig))).astype(BF16)
        dgu_ref[:, ff:] = (da * gate * sig).astype(BF16)

    return _pcall(body, name=name, grid=(s // tr,), in_specs=[_row_spec(tr, ff), _row_spec(tr, two_ff)],
                  out_specs=_row_spec(tr, two_ff), out_shape=jax.ShapeDtypeStruct((s, two_ff), BF16),
                  compiler_params=_cp(("parallel",)))(dact, gu)


def rope_tables(pos_col, invf_row, name):
    s = pos_col.shape[0]
    tr = _tile(s, 1024)

    def body(p_ref, f_ref, c_ref, sa_ref, sb_ref):
        ang = p_ref[...].astype(F32) * f_ref[...]
        d = lax.broadcasted_iota(jnp.int32, (1, LANES), 1) % HEAD_DIM
        cs, sn = jnp.cos(ang), jnp.sin(ang)
        c_ref[...] = jnp.where(d < ROT_DIM, cs, 1.0)
        sa_ref[...] = jnp.where(d < ROT_DIM // 2, -sn, 0.0)
        sb_ref[...] = jnp.where((d >= ROT_DIM // 2) & (d < ROT_DIM), sn, 0.0)

    tab = jax.ShapeDtypeStruct((s, LANES), F32)
    return _pcall(body, name=name, grid=(s // tr,),
                  in_specs=[pl.BlockSpec((tr, 1), lambda i: (i, 0)), _vec_spec(LANES)],
                  out_specs=[_row_spec(tr, LANES)] * 3, out_shape=[tab] * 3, compiler_params=_cp(("parallel",)))(pos_col, invf_row)


def _rot(t, c, sa, sb):
    half = ROT_DIM // 2
    return t * c + pltpu.roll(t, LANES - half, 1) * sa + pltpu.roll(t, half, 1) * sb


def _rot_t(t, c, sa, sb):
    half = ROT_DIM // 2
    return t * c + pltpu.roll(t * sa, half, 1) + pltpu.roll(t * sb, LANES - half, 1)


def swa_split(proj, tabs, n_heads, n_kv, name):
    s, width = proj.shape
    qd, kd = n_heads * HEAD_DIM, n_kv * HEAD_DIM
    tr = _tile(s, 256)

    def body(p_ref, c_ref, sa_ref, sb_ref, q_ref, k_ref, v_ref):
        c, sa, sb = c_ref[...], sa_ref[...], sb_ref[...]
        low = lax.broadcasted_iota(jnp.int32, (1, LANES), 1) < HEAD_DIM
        for g in range(qd // LANES):
            q_ref[:, g * LANES:(g + 1) * LANES] = _rot(p_ref[:, g * LANES:(g + 1) * LANES], c, sa, sb).astype(BF16)
        for g in range(kd // LANES):
            for src, dst, rot in ((qd, k_ref, True), (qd + kd, v_ref, False)):
                t = p_ref[:, src + g * LANES:src + (g + 1) * LANES]
                t = _rot(t, c, sa, sb) if rot else t
                sw = pltpu.roll(t, HEAD_DIM, 1)
                dst[:, (2 * g) * LANES:(2 * g + 1) * LANES] = jnp.where(low, t, sw).astype(BF16)
                dst[:, (2 * g + 1) * LANES:(2 * g + 2) * LANES] = jnp.where(low, sw, t).astype(BF16)

    return _pcall(body, name=name, grid=(s // tr,),
                  in_specs=[_row_spec(tr, width)] + [_row_spec(tr, LANES)] * 3,
                  out_specs=[_row_spec(tr, qd), _row_spec(tr, 2 * kd), _row_spec(tr, 2 * kd)],
                  out_shape=[jax.ShapeDtypeStruct((s, qd), BF16), jax.ShapeDtypeStruct((s, 2 * kd), BF16),
                             jax.ShapeDtypeStruct((s, 2 * kd), BF16)],
                  compiler_params=_cp(("parallel",)))(proj, *tabs)


def swa_merge_bwd(dq, dkd, dvd, tabs, name):
    s, qd = dq.shape
    kd = dkd.shape[1] // 2
    width = qd + 2 * kd
    tr = _tile(s, 256)

    def body(dq_ref, dk_ref, dv_ref, c_ref, sa_ref, sb_ref, o_ref):
        c, sa, sb = c_ref[...], sa_ref[...], sb_ref[...]
        low = lax.broadcasted_iota(jnp.int32, (1, LANES), 1) < HEAD_DIM
        for g in range(qd // LANES):
            t = dq_ref[:, g * LANES:(g + 1) * LANES].astype(F32)
            o_ref[:, g * LANES:(g + 1) * LANES] = _rot_t(t, c, sa, sb).astype(BF16)
        for g in range(kd // LANES):
            for dst, src, rot in ((qd, dk_ref, True), (qd + kd, dv_ref, False)):
                e = src[:, (2 * g) * LANES:(2 * g + 1) * LANES]
                o = src[:, (2 * g + 1) * LANES:(2 * g + 2) * LANES]
                t = jnp.where(low, e + pltpu.roll(e, HEAD_DIM, 1), o + pltpu.roll(o, HEAD_DIM, 1))
                t = _rot_t(t, c, sa, sb) if rot else t
                o_ref[:, dst + g * LANES:dst + (g + 1) * LANES] = t.astype(BF16)

    return _pcall(body, name=name, grid=(s // tr,),
                  in_specs=[_row_spec(tr, qd), _row_spec(tr, 2 * kd), _row_spec(tr, 2 * kd)] + [_row_spec(tr, LANES)] * 3,
                  out_specs=_row_spec(tr, width), out_shape=jax.ShapeDtypeStruct((s, width), BF16),
                  compiler_params=_cp(("parallel",)))(dq, dkd, dvd, *tabs)


def _halves():
    lane_half = lax.broadcasted_iota(jnp.int32, (1, LANES), 1) // HEAD_DIM
    return [lane_half == 0, lane_half == 1]


def _nt(a, b):
    return lax.dot_general(a, b, _DIMS["nt"], preferred_element_type=F32)


def _tn(a, b):
    return lax.dot_general(a, b, _DIMS["tn"], preferred_element_type=F32)


def _band_mask(i, tq):
    w = tq + WINDOW
    rel = lax.broadcasted_iota(jnp.int32, (tq, w), 1) - lax.broadcasted_iota(jnp.int32, (tq, w), 0)
    first = lax.broadcasted_iota(jnp.int32, (tq, w), 1) >= jnp.where(i > 0, 0, WINDOW)
    return (rel >= 1) & (rel <= WINDOW) & first


def swa_fwd(q, kd, vd, sink_row, name):
    s, qd = q.shape
    tq = min(256, s)
    r = tq // WINDOW
    pairs = qd // LANES
    group_pairs = pairs // (kd.shape[1] // LANES)
    scale = HEAD_DIM ** -0.5

    def body(q_ref, kp_ref, kc_ref, vp_ref, vc_ref, sk_ref, o_ref, lse_ref):
        i = pl.program_id(1)
        k = jnp.concatenate([kp_ref[...], kc_ref[...]], axis=0)
        v = jnp.concatenate([vp_ref[...], vc_ref[...]], axis=0)
        mask = _band_mask(i, tq)
        q2 = q_ref[...]
        outs, lses = [], []
        for a, hm in enumerate(_halves()):
            sc = _nt(jnp.where(hm, q2, jnp.zeros_like(q2)), k) * scale
            sc = jnp.where(mask, sc, NEG)
            sink = sk_ref[:, a * HEAD_DIM:a * HEAD_DIM + 1]
            m = jnp.maximum(jnp.max(sc, axis=-1, keepdims=True), sink)
            p = jnp.exp(sc - m)
            den = jnp.sum(p, axis=-1, keepdims=True) + jnp.exp(sink - m)
            outs.append(jnp.dot(p.astype(BF16), v, preferred_element_type=F32) / den)
            lses.append(m + jnp.log(den))
        hm0 = _halves()[0]
        o_ref[...] = jnp.where(hm0, outs[0], outs[1]).astype(BF16)
        lse_ref[...] = jnp.where(hm0, lses[0], lses[1])

    prev = lambda p, i: (jnp.maximum(i * r - 1, 0), p // group_pairs)
    cur = lambda p, i: (i, p // group_pairs)
    blk = pl.BlockSpec((tq, LANES), lambda p, i: (i, p))
    return _pcall(body, name=name, grid=(pairs, s // tq),
                  in_specs=[blk, pl.BlockSpec((WINDOW, LANES), prev), pl.BlockSpec((tq, LANES), cur),
                            pl.BlockSpec((WINDOW, LANES), prev), pl.BlockSpec((tq, LANES), cur),
                            pl.BlockSpec((1, LANES), lambda p, i: (0, p))],
                  out_specs=[blk, blk],
                  out_shape=[jax.ShapeDtypeStruct((s, qd), BF16), jax.ShapeDtypeStruct((s, qd), F32)],
                  compiler_params=_cp(("parallel", "parallel")))(q, kd, kd, vd, vd, sink_row)


def swa_bwd(q, kd, vd, sink_row, out, lse, dout, name):
    s, qd = q.shape
    tq = min(256, s)
    r = tq // WINDOW
    n_kv = kd.shape[1] // LANES
    gw = qd // n_kv
    scale = HEAD_DIM ** -0.5

    def body(q_ref, kp_ref, kc_ref, vp_ref, vc_ref, sk_ref, o_ref, lse_ref, do_ref, dq_ref, dk_ref, dv_ref, dsk_ref):
        i = pl.program_id(1)

        @pl.when(i == 0)
        def _():
            dk_ref[...] = jnp.zeros_like(dk_ref)
            dv_ref[...] = jnp.zeros_like(dv_ref)
            dsk_ref[...] = jnp.zeros_like(dsk_ref)

        k = jnp.concatenate([kp_ref[...], kc_ref[...]], axis=0)
        v = jnp.concatenate([vp_ref[...], vc_ref[...]], axis=0)
        mask = _band_mask(i, tq)
        dk = jnp.zeros((tq + WINDOW, LANES), F32)
        dv = jnp.zeros((tq + WINDOW, LANES), F32)
        for pp in range(gw // LANES):
            cols = slice(pp * LANES, (pp + 1) * LANES)
            q2, do2 = q_ref[:, cols], do_ref[:, cols]
            prod = do2.astype(F32) * o_ref[:, cols].astype(F32)
            dq2 = jnp.zeros((tq, LANES), F32)
            dsk = jnp.zeros((1, LANES), F32)
            for a, hm in enumerate(_halves()):
                qa = jnp.where(hm, q2, jnp.zeros_like(q2))
                doa = jnp.where(hm, do2, jnp.zeros_like(do2))
                lse_a = lse_ref[:, pp * LANES + a * HEAD_DIM:pp * LANES + a * HEAD_DIM + 1]
                sc = jnp.where(mask, _nt(qa, k) * scale, NEG)
                p = jnp.exp(sc - lse_a)
                delta = jnp.sum(jnp.where(hm, prod, 0.0), axis=-1, keepdims=True)
                ds = (p * (_nt(doa, v) - delta) * scale).astype(BF16)
                dv = dv + _tn(p.astype(BF16), doa)
                dk = dk + _tn(ds, qa)
                dq2 = dq2 + jnp.where(hm, jnp.dot(ds, k, preferred_element_type=F32), 0.0)
                sink = sk_ref[:, pp * LANES + a * HEAD_DIM:pp * LANES + a * HEAD_DIM + 1]
                dsink = -jnp.sum(jnp.exp(sink - lse_a) * delta, axis=0, keepdims=True)
                dsk = dsk + jnp.where(hm, dsink, 0.0)
            dq_ref[:, cols] = dq2.astype(BF16)
            dsk_ref[0:1, cols] += dsk
        start = pl.multiple_of(i * tq, tq)
        dk_ref[pl.ds(start, tq), :] += dk[WINDOW:, :]
        dv_ref[pl.ds(start, tq), :] += dv[WINDOW:, :]

        @pl.when(i > 0)
        def _():
            before = pl.multiple_of(i * tq - WINDOW, WINDOW)
            dk_ref[pl.ds(before, WINDOW), :] += dk[:WINDOW, :]
            dv_ref[pl.ds(before, WINDOW), :] += dv[:WINDOW, :]

    prev = lambda g, i: (jnp.maximum(i * r - 1, 0), g)
    cur = lambda g, i: (i, g)
    wide = pl.BlockSpec((tq, gw), cur)
    full = pl.BlockSpec((s, LANES), lambda g, i: (0, g))
    return _pcall(body, name=name, grid=(n_kv, s // tq),
                  in_specs=[wide, pl.BlockSpec((WINDOW, LANES), prev), pl.BlockSpec((tq, LANES), cur),
                            pl.BlockSpec((WINDOW, LANES), prev), pl.BlockSpec((tq, LANES), cur),
                            pl.BlockSpec((1, gw), lambda g, i: (0, g)), wide, wide, wide],
                  out_specs=[wide, full, full, pl.BlockSpec((8, gw), lambda g, i: (0, g))],
                  out_shape=[jax.ShapeDtypeStruct((s, qd), BF16), jax.ShapeDtypeStruct(kd.shape, F32),
                             jax.ShapeDtypeStruct(kd.shape, F32), jax.ShapeDtypeStruct((8, qd), F32)],
                  compiler_params=_cp(("parallel", "arbitrary"), VMEM_BIG))(q, kd, kd, vd, vd, sink_row, out, lse, dout)


def forget_cumsum(f_logit, b_row, name):
    s = f_logit.shape[0]

    def body(f_ref, b_ref, c_ref):
        z = f_ref[...] + b_ref[...]
        acc = jnp.minimum(z, 0.0) - jnp.log(1.0 + jnp.exp(-jnp.abs(z)))
        row = lax.broadcasted_iota(jnp.int32, (s, LANES), 0)
        d = 1
        while d < s:
            acc = acc + jnp.where(row >= d, pltpu.roll(acc, d, 0), 0.0)
            d *= 2
        c_ref[...] = acc

    return _pcall(body, name=name, in_specs=[pl.BlockSpec((s, LANES), lambda: (0, 0)), pl.BlockSpec((1, LANES), lambda: (0, 0))],
                  out_specs=pl.BlockSpec((s, LANES), lambda: (0, 0)), out_shape=jax.ShapeDtypeStruct((s, LANES), F32),
                  compiler_params=_cp(None, VMEM_BIG))(f_logit, b_row)


def forget_gate_bwd(dc_key, dc_query, f_logit, b_row, name):
    s = dc_key.shape[0]

    def body(dck_ref, dcq_ref, f_ref, b_ref, df_ref, db_ref):
        acc = dck_ref[...] + dcq_ref[...]
        row = lax.broadcasted_iota(jnp.int32, (s, LANES), 0)
        d = 1
        while d < s:
            acc = acc + jnp.where(row < s - d, pltpu.roll(acc, s - d, 0), 0.0)
            d *= 2
        df = acc * jax.nn.sigmoid(-(f_ref[...] + b_ref[...]))
        df_ref[...] = df.astype(BF16)
        db_ref[...] = jnp.sum(df, axis=0, keepdims=True)

    whole = pl.BlockSpec((s, LANES), lambda: (0, 0))
    vec = pl.BlockSpec((1, LANES), lambda: (0, 0))
    return _pcall(body, name=name, in_specs=[whole, whole, whole, vec], out_specs=[whole, vec],
                  out_shape=[jax.ShapeDtypeStruct((s, LANES), BF16), jax.ShapeDtypeStruct((1, LANES), F32)],
                  compiler_params=_cp(None, VMEM_BIG))(dc_key, dc_query, f_logit, b_row)


def fox_fwd(qkv, crow, n_heads, name):
    s = qkv.shape[0]
    hd = n_heads * HEAD_DIM
    pairs = hd // LANES
    t = min(512, s)
    scale = HEAD_DIM ** -0.5

    def body(q_ref, k_ref, v_ref, c_ref, o_ref, lse_ref, acc_ref, m_ref, l_ref):
        i = pl.program_id(1)
        halves = _halves()
        q2 = q_ref[...] * scale
        qs = [jnp.where(hm, q2, jnp.zeros_like(q2)) for hm in halves]
        acc_ref[...] = jnp.zeros_like(acc_ref)
        m_ref[...] = jnp.full_like(m_ref, NEG)
        l_ref[...] = jnp.zeros_like(l_ref)

        def step(j, diagonal):
            rows = pl.ds(pl.multiple_of(j * t, t), t)
            kc, vc = k_ref[rows, :], v_ref[rows, :]
            alphas, outs, ms, ls = [], [], [], []
            for a in range(2):
                sc = _nt(qs[a], kc) - c_ref[j, a:a + 1, :]
                if diagonal:
                    keep = lax.broadcasted_iota(jnp.int32, (t, t), 1) <= lax.broadcasted_iota(jnp.int32, (t, t), 0)
                    sc = jnp.where(keep, sc, NEG)
                m_old = m_ref[:, a * HEAD_DIM:a * HEAD_DIM + 1]
                m_new = jnp.maximum(m_old, jnp.max(sc, axis=-1, keepdims=True))
                alpha = jnp.exp(m_old - m_new)
                p = jnp.exp(sc - m_new)
                ls.append(alpha * l_ref[:, a * HEAD_DIM:a * HEAD_DIM + 1] + jnp.sum(p, axis=-1, keepdims=True))
                outs.append(jnp.dot(p.astype(BF16), vc, preferred_element_type=F32))
                alphas.append(alpha)
                ms.append(m_new)
            hm0 = halves[0]
            acc_ref[...] = jnp.where(hm0, alphas[0], alphas[1]) * acc_ref[...] + jnp.where(hm0, outs[0], outs[1])
            m_ref[...] = jnp.where(hm0, ms[0], ms[1])
            l_ref[...] = jnp.where(hm0, ls[0], ls[1])

        def loop_body(j, carry):
            step(j, False)
            return carry

        lax.fori_loop(0, i, loop_body, 0)
        step(i, True)
        o_ref[...] = (acc_ref[...] / l_ref[...]).astype(BF16)
        lse_ref[...] = m_ref[...] + jnp.log(l_ref[...])

    blk = pl.BlockSpec((t, LANES), lambda p, i: (i, p))
    return _pcall(body, name=name, grid=(pairs, s // t),
                  in_specs=[blk, pl.BlockSpec((s, LANES), lambda p, i: (0, pairs + p)),
                            pl.BlockSpec((s, LANES), lambda p, i: (0, 2 * pairs + p)),
                            pl.BlockSpec((None, s // t, 8, t), lambda p, i: (p, 0, 0, 0))],
                  out_specs=[blk, blk],
                  out_shape=[jax.ShapeDtypeStruct((s, hd), BF16), jax.ShapeDtypeStruct((s, hd), F32)],
                  scratch_shapes=[pltpu.VMEM((t, LANES), F32)] * 3,
                  compiler_params=_cp(("parallel", "arbitrary"), VMEM_BIG))(qkv, qkv, qkv, crow)


def head_rowdot(a, b, name):
    s, hd = a.shape
    tr = _tile(s, 256)

    def body(a_ref, b_ref, o_ref):
        halves = _halves()
        for g in range(hd // LANES):
            cols = slice(g * LANES, (g + 1) * LANES)
            prod = a_ref[:, cols].astype(F32) * b_ref[:, cols].astype(F32)
            d0 = jnp.sum(jnp.where(halves[0], prod, 0.0), axis=-1, keepdims=True)
            d1 = jnp.sum(jnp.where(halves[1], prod, 0.0), axis=-1, keepdims=True)
            o_ref[:, cols] = jnp.where(halves[0], d0, d1)

    return _pcall(body, name=name, grid=(s // tr,), in_specs=[_row_spec(tr, hd), _row_spec(tr, hd)],
                  out_specs=_row_spec(tr, hd), out_shape=jax.ShapeDtypeStruct((s, hd), F32),
                  compiler_params=_cp(("parallel",)))(a, b)


def fox_bwd(qkv, crow, lse, delta, dout, n_heads, name):
    s = qkv.shape[0]
    hd = n_heads * HEAD_DIM
    pairs = hd // LANES
    t = min(512, s)
    nblk = s // t
    scale = HEAD_DIM ** -0.5

    def body(q_ref, k_ref, v_ref, c_ref, lse_ref, dl_ref, do_ref, dq_ref, dk_ref, dv_ref, dc_ref, dr_ref, dka_ref, dva_ref, dca_ref):
        j = pl.program_id(1)
        halves = _halves()

        @pl.when(j == 0)
        def _():
            dq_ref[...] = jnp.zeros_like(dq_ref)
            dr_ref[...] = jnp.zeros_like(dr_ref)

        kc, vc = k_ref[...], v_ref[...]
        dka_ref[...] = jnp.zeros_like(dka_ref)
        dva_ref[...] = jnp.zeros_like(dva_ref)
        dca_ref[...] = jnp.zeros_like(dca_ref)

        def step(i, diagonal):
            rows = pl.ds(pl.multiple_of(i * t, t), t)
            q2 = q_ref[rows, :] * scale
            do2 = do_ref[rows, :]
            dq2 = jnp.zeros((t, LANES), F32)
            row_sums = []
            for a, hm in enumerate(halves):
                qa = jnp.where(hm, q2, jnp.zeros_like(q2))
                doa = jnp.where(hm, do2, jnp.zeros_like(do2))
                sc = _nt(qa, kc) - c_ref[a:a + 1, :]
                if diagonal:
                    keep = lax.broadcasted_iota(jnp.int32, (t, t), 1) <= lax.broadcasted_iota(jnp.int32, (t, t), 0)
                    sc = jnp.where(keep, sc, NEG)
                p = jnp.exp(sc - lse_ref[rows, a * HEAD_DIM:a * HEAD_DIM + 1])
                ds = p * (_nt(doa, vc) - dl_ref[rows, a * HEAD_DIM:a * HEAD_DIM + 1])
                dca_ref[a:a + 1, :] -= jnp.sum(ds, axis=0, keepdims=True)
                row_sums.append(jnp.sum(ds, axis=1, keepdims=True))
                ds = ds.astype(BF16)
                dva_ref[...] += _tn(p.astype(BF16), doa)
                dka_ref[...] += _tn(ds, qa)
                dq2 = dq2 + jnp.where(hm, jnp.dot(ds, kc, preferred_element_type=F32), 0.0)
            dq_ref[rows, :] += dq2 * scale
            dr_ref[rows, :] += jnp.where(halves[0], row_sums[0], row_sums[1])

        step(j, True)

        def loop_body(i, carry):
            step(i, False)
            return carry

        lax.fori_loop(j + 1, nblk, loop_body, 0)
        dk_ref[...] = dka_ref[...].astype(BF16)
        dv_ref[...] = dva_ref[...].astype(BF16)
        dc_ref[...] = dca_ref[...]

    def whole(off):
        return pl.BlockSpec((s, LANES), lambda p, j: (0, off + p))

    def blk(off):
        return pl.BlockSpec((t, LANES), lambda p, j: (j, off + p))

    cblk = pl.BlockSpec((None, None, 8, t), lambda p, j: (p, j, 0, 0))
    return _pcall(body, name=name, grid=(pairs, nblk),
                  in_specs=[whole(0), blk(pairs), blk(2 * pairs), cblk, whole(0), whole(0), whole(0)],
                  out_specs=[whole(0), blk(0), blk(0), cblk, whole(0)],
                  out_shape=[jax.ShapeDtypeStruct((s, hd), F32), jax.ShapeDtypeStruct((s, hd), BF16),
                             jax.ShapeDtypeStruct((s, hd), BF16), jax.ShapeDtypeStruct(crow.shape, F32),
                             jax.ShapeDtypeStruct((s, hd), F32)],
                  scratch_shapes=[pltpu.VMEM((t, LANES), F32), pltpu.VMEM((t, LANES), F32), pltpu.VMEM((8, t), F32)],
                  compiler_params=_cp(("parallel", "arbitrary"), VMEM_BIG))(qkv, qkv, qkv, crow, lse, delta, dout)


def _w_spec(shape, lead_map=None):
    _, r, c = shape[-3:]
    tr = _tile(r, 128)
    n_lead = len(shape) - 2
    if lead_map is None:
        lead_map = lambda l: (l,)
    return tr, pl.BlockSpec((None,) * n_lead + (tr, c), lambda l, i: (*lead_map(l), i, 0))


def cast_bf16(w, name):
    tr, spec = _w_spec(w.shape)

    def body(w_ref, o_ref):
        o_ref[...] = w_ref[...].astype(BF16)

    return _pcall(body, name=name, grid=(w.shape[0], w.shape[1] // tr), in_specs=[spec], out_specs=spec,
                  out_shape=jax.ShapeDtypeStruct(w.shape, BF16), compiler_params=_cp(("parallel", "parallel")))(w)


def adamw(w, g, m, v, name):
    tr, spec = _w_spec(w.shape)

    def body(w_ref, g_ref, m_ref, v_ref, d_ref, mo_ref, vo_ref):
        gg = g_ref[...]
        mn = ADAM_B1 * m_ref[...] + (1.0 - ADAM_B1) * gg
        vn = ADAM_B2 * v_ref[...] + (1.0 - ADAM_B2) * (gg * gg)
        m_hat = mn / (1.0 - ADAM_B1 ** ADAM_STEP)
        v_hat = vn / (1.0 - ADAM_B2 ** ADAM_STEP)
        d_ref[...] = -ADAM_LR * (m_hat / (jnp.sqrt(v_hat) + ADAM_EPS) + ADAM_WD * w_ref[...])
        mo_ref[...] = mn
        vo_ref[...] = vn

    out = jax.ShapeDtypeStruct(w.shape, F32)
    return _pcall(body, name=name, grid=(w.shape[0], w.shape[1] // tr), in_specs=[spec] * 4, out_specs=[spec] * 3,
                  out_shape=[out] * 3, compiler_params=_cp(("parallel", "parallel")))(w, g, m, v)


def add_core_halves(g, recv, c_idx, name):
    h = recv.shape[1]
    tr = _tile(g.shape[2], 128)
    blk = (None, None, tr, g.shape[3])

    def body(c_ref, g_ref, r_ref, o_ref):
        o_ref[...] = (g_ref[...].astype(F32) + r_ref[...].astype(F32)).astype(o_ref.dtype)

    grid_spec = pltpu.PrefetchScalarGridSpec(
        num_scalar_prefetch=1, grid=(N_CHIPS, h, g.shape[2] // tr),
        in_specs=[pl.BlockSpec(blk, lambda q, l, i, c: (q, c[0] * h + l, i, 0)), pl.BlockSpec(blk, lambda q, l, i, c: (q, l, i, 0))],
        out_specs=pl.BlockSpec(blk, lambda q, l, i, c: (q, l, i, 0)))
    return _pcall(body, name=name, grid_spec=grid_spec, out_shape=jax.ShapeDtypeStruct(recv.shape, g.dtype),
                  compiler_params=_cp(("parallel", "parallel", "parallel")))(c_idx, g, recv)


def add_chips(parts, name):
    _, h, r, c = parts.shape
    tr = _tile(r, 128)
    blk = (None, None, tr, c)

    def body(p0, p1, p2, p3, o_ref):
        o_ref[...] = ((p0[...].astype(F32) + p1[...].astype(F32)) + p2[...].astype(F32)) + p3[...].astype(F32)

    specs = [pl.BlockSpec(blk, functools.partial(lambda l, i, q: (q, l, i, 0), q=q)) for q in range(N_CHIPS)]
    return _pcall(body, name=name, grid=(h, r // tr), in_specs=specs,
                  out_specs=pl.BlockSpec((None, tr, c), lambda l, i: (l, i, 0)),
                  out_shape=jax.ShapeDtypeStruct((h, r, c), F32), compiler_params=_cp(("parallel", "parallel")))(*[parts] * N_CHIPS)


def _place():
    x, y, c = lax.axis_index("x"), lax.axis_index("y"), lax.axis_index("c")
    others = [(1 - x, y), (x, 1 - y), (1 - x, 1 - y)]
    return x, y, c, others


def _chip_id(chip):
    return 2 * chip[0] + chip[1]


def _comm_call(body, name, n_in, out_shapes, n_sems):
    return _pcall(body, name=name, in_specs=[ANY] * n_in, out_specs=[ANY] * len(out_shapes), out_shape=out_shapes,
                  scratch_shapes=[pltpu.SemaphoreType.DMA((n_sems,)), pltpu.SemaphoreType.DMA((n_sems,)),
                                  pltpu.SemaphoreType.DMA((n_in,))],
                  compiler_params=pltpu.CompilerParams(has_side_effects=True))


def all_gather_shards(shards, name):
    n = len(shards)

    def body(*refs):
        ins, outs = refs[:n], refs[n:2 * n]
        send, recv, local = refs[2 * n:]
        x, y, c, others = _place()
        me = _chip_id((x, y))

        def remote(src, dst, k, dev):
            return pltpu.make_async_remote_copy(src_ref=src, dst_ref=dst, send_sem=send.at[k], recv_sem=recv.at[k],
                                                device_id=dev, device_id_type=MESH)

        own, first, passed = [], [], []
        for t in range(n):
            h = ins[t].shape[0] // 2
            mine = pl.ds(c * h, h)
            cp = pltpu.make_async_copy(ins[t], outs[t].at[me], local.at[t])
            cp.start()
            own.append(cp)
            for j, chip in enumerate(others):
                cp = remote(ins[t].at[mine], outs[t].at[me, mine], 6 * t + j, (*chip, c))
                cp.start()
                first.append(cp)
        for t in range(n):
            h = ins[t].shape[0] // 2
            mine = pl.ds(c * h, h)
            for j, chip in enumerate(others):
                got = outs[t].at[_chip_id(chip), mine]
                remote(got, got, 6 * t + j, (x, y, c)).wait_recv()
                cp = remote(got, got, 6 * t + 3 + j, (x, y, 1 - c))
                cp.start()
                passed.append(cp)
        for t in range(n):
            h = ins[t].shape[0] // 2
            theirs = pl.ds((1 - c) * h, h)
            for j, chip in enumerate(others):
                got = outs[t].at[_chip_id(chip), theirs]
                remote(got, got, 6 * t + 3 + j, (x, y, c)).wait_recv()
        for cp in first + passed:
            cp.wait_send()
        for cp in own:
            cp.wait()

    outs = [jax.ShapeDtypeStruct((N_CHIPS,) + a.shape, a.dtype) for a in shards]
    return _comm_call(body, name, n, outs, 6 * n)(*shards)


def swap_core_halves(grads, name):
    n = len(grads)

    def body(*refs):
        ins, outs = refs[:n], refs[n:2 * n]
        send, recv, _ = refs[2 * n:]
        x, y, c, _ = _place()
        cps = []
        for t in range(n):
            h = ins[t].shape[1] // 2
            cp = pltpu.make_async_remote_copy(src_ref=ins[t].at[:, pl.ds((1 - c) * h, h)], dst_ref=outs[t],
                                              send_sem=send.at[t], recv_sem=recv.at[t],
                                              device_id=(x, y, 1 - c), device_id_type=MESH)
            cp.start()
            cps.append(cp)
        for cp in cps:
            cp.wait()

    outs = [jax.ShapeDtypeStruct((a.shape[0], a.shape[1] // 2) + a.shape[2:], a.dtype) for a in grads]
    return _comm_call(body, name, n, outs, n)(*grads)


def scatter_to_chips(sums, name):
    n = len(sums)

    def body(*refs):
        ins, outs = refs[:n], refs[n:2 * n]
        send, recv, local = refs[2 * n:]
        x, y, c, others = _place()
        me = _chip_id((x, y))
        cps = []
        for t in range(n):
            cp = pltpu.make_async_copy(ins[t].at[me], outs[t].at[me], local.at[t])
            cp.start()
            cps.append(cp)
            for j, chip in enumerate(others):
                cp = pltpu.make_async_remote_copy(src_ref=ins[t].at[_chip_id(chip)], dst_ref=outs[t].at[me],
                                                  send_sem=send.at[3 * t + j], recv_sem=recv.at[3 * t + j],
                                                  device_id=(*chip, c), device_id_type=MESH)
                cp.start()
                cps.append(cp)
        for t in range(n):
            for j, chip in enumerate(others):
                got = outs[t].at[_chip_id(chip)]
                pltpu.make_async_remote_copy(src_ref=got, dst_ref=got, send_sem=send.at[3 * t + j], recv_sem=recv.at[3 * t + j],
                                             device_id=(x, y, c), device_id_type=MESH).wait_recv()
        for t in range(n):
            cps[4 * t].wait()
            for j in range(3):
                cps[4 * t + 1 + j].wait_send()

    outs = [jax.ShapeDtypeStruct(a.shape, a.dtype) for a in sums]
    return _comm_call(body, name, n, outs, 3 * n)(*sums)


def join_core_halves(halves, name):
    n = len(halves)

    def body(*refs):
        ins, outs = refs[:n], refs[n:2 * n]
        send, recv, local = refs[2 * n:]
        x, y, c, _ = _place()
        cps = []
        for t in range(n):
            h = ins[t].shape[0]
            mine = outs[t].at[pl.ds(c * h, h)]
            lc = pltpu.make_async_copy(ins[t], mine, local.at[t])
            lc.start()
            cp = pltpu.make_async_remote_copy(src_ref=ins[t], dst_ref=mine, send_sem=send.at[t], recv_sem=recv.at[t],
                                              device_id=(x, y, 1 - c), device_id_type=MESH)
            cp.start()
            cps.append((lc, cp))
        for t in range(n):
            h = ins[t].shape[0]
            theirs = outs[t].at[pl.ds((1 - c) * h, h)]
            lc, cp = cps[t]
            cp.wait_send()
            pltpu.make_async_remote_copy(src_ref=theirs, dst_ref=theirs, send_sem=send.at[t], recv_sem=recv.at[t],
                                         device_id=(x, y, c), device_id_type=MESH).wait_recv()
            lc.wait()

    outs = [jax.ShapeDtypeStruct((2 * a.shape[0],) + a.shape[1:], a.dtype) for a in halves]
    return _comm_call(body, name, n, outs, n)(*halves)


def kernel(x, positions, norm_gains, swa_w_in, swa_sinks, swa_w_out, fox_w_in, fox_b_f, fox_w_out, ffn_w_gate_up, ffn_w_down, loss_target, m_norm_gains, m_swa_w_in, m_swa_sinks, m_swa_w_out, m_fox_w_in, m_fox_b_f, m_fox_w_out, m_ffn_w_gate_up, m_ffn_w_down, v_norm_gains, v_swa_w_in, v_swa_sinks, v_swa_w_out, v_fox_w_in, v_fox_b_f, v_fox_w_out, v_ffn_w_gate_up, v_ffn_w_down):
    s, d = x.shape[1], x.shape[2]
    depth = norm_gains.shape[0]
    n_heads = d // HEAD_DIM
    hd = n_heads * HEAD_DIM
    n_kv = (swa_w_in.shape[2] * N_CHIPS // HEAD_DIM - n_heads) // 2
    ff = ffn_w_down.shape[1] * N_CHIPS
    fox_cols = fox_w_in.shape[2]
    fox_pad = 3 * hd + LANES
    assert fox_cols * N_CHIPS == 3 * hd + n_heads and n_heads <= LANES
    t_fox = min(512, s)
    x0 = x[0]
    target = loss_target[0]
    c_idx = lax.axis_index("c").astype(jnp.int32).reshape(1)

    gains_sh = norm_gains.reshape(2, depth * 2, norm_gains.shape[2])
    w_names = ["swa_w_in", "swa_w_out", "fox_w_in", "fox_w_out", "ffn_w_gate_up", "ffn_w_down"]
    w_f32 = [swa_w_in, swa_w_out, fox_w_in, fox_w_out, ffn_w_gate_up, ffn_w_down]
    w_bf = [cast_bf16(w, "cast_" + nm) for w, nm in zip(w_f32, w_names)]
    gathered = all_gather_shards(w_bf + [gains_sh], "gather_weights")
    g_swa_in, g_swa_out, g_fox_in, g_fox_out, g_gu, g_down, g_gains = gathered
    gains = jnp.transpose(g_gains.reshape(N_CHIPS, depth * 4, -1), (1, 0, 2)).reshape(depth * 4, d)

    def gain(layer, which):
        return gains[layer * 4 + which][None, :]

    def fox_weight(layer):
        parts = [g_fox_in[q, layer] for q in range(N_CHIPS)]
        parts.append(jnp.zeros((d, fox_pad - fox_cols * N_CHIPS), BF16))
        return jnp.concatenate(parts, axis=1)

    inv_freq = ROPE_THETA ** (-jnp.arange(0, ROT_DIM, 2, dtype=F32) / ROT_DIM)
    lane_d = jnp.arange(LANES) % HEAD_DIM
    invf_row = jnp.where(lane_d < ROT_DIM, inv_freq[lane_d % (ROT_DIM // 2)], 0.0)[None, :]
    tabs = rope_tables(positions.reshape(s, 1), invf_row, "rope_tables")

    n_sh_in = swa_w_in.shape[2]
    gu_sh = ffn_w_gate_up.shape[2]
    tn_gu = gu_sh // 2 if (gu_sh // 2) % LANES == 0 else gu_sh
    down_sh = ffn_w_down.shape[1]
    out_sh = swa_w_out.shape[1]
    tm = min(1024, s)

    saved = []
    xin = x0
    h = prenorm(xin, gain(0, 0), "prenorm_first")
    for layer in range(depth):
        j = layer // 2
        rec = {"x_in": xin, "h1": h}
        if layer % 2 == 0:
            proj = mm_nn(h, g_swa_in, lambda n, k: (n, j, k, 0), n_sh_in * N_CHIPS, tm=min(512, s), tn=n_sh_in, tk=d,
                         out_dtype=F32, name=f"swa_proj_{j}")
            q, kd, vd = swa_split(proj, tabs, n_heads, n_kv, f"swa_split_{j}")
            sink_row = jnp.repeat(swa_sinks[j], HEAD_DIM)[None, :]
            attn, lse = swa_fwd(q, kd, vd, sink_row, f"swa_fwd_{j}")
            rec.update(q=q, kd=kd, vd=vd, sink_row=sink_row, lse=lse)
            w_out_g = g_swa_out
        else:
            wf = fox_weight(j)
            tn = 3 * hd // 6 if (3 * hd // 6) % LANES == 0 else LANES
            qkv = mm_nn(h, wf, lambda n, k: (k, n), 3 * hd, tm=tm, tn=tn, tk=d, out_dtype=BF16, name=f"fox_proj_{j}")
            f_off = 3 * hd // LANES
            f_logit = mm_nn(h, wf, lambda n, k: (k, f_off + n), LANES, tm=tm, tn=LANES, tk=d, out_dtype=F32, name=f"fox_gate_{j}")
            b_row = jnp.pad(fox_b_f[j], (0, LANES - n_heads))[None, :]
            c = forget_cumsum(f_logit, b_row, f"fox_cumsum_{j}")
            crow = jnp.transpose(c[:, :n_heads]).reshape(n_heads // 2, 2, s // t_fox, t_fox)
            crow = jnp.pad(jnp.transpose(crow, (0, 2, 1, 3)), ((0, 0), (0, 0), (0, 6), (0, 0)))
            attn, lse = fox_fwd(qkv, crow, n_heads, f"fox_fwd_{j}")
            rec.update(wf=wf, qkv=qkv, f_logit=f_logit, b_row=b_row, crow=crow, lse=lse)
            w_out_g = g_fox_out
        y = mm_nn(attn, w_out_g, lambda n, k: (k, j, 0, n), d, tm=tm, tn=min(1024, d), tk=out_sh, out_dtype=F32, name=f"out_proj_{layer}")
        xmid, h2 = postnorm_residual(xin, y, gain(layer, 1), gain(layer, 2), f"postnorm_mixer_{layer}")
        npb = gu_sh // tn_gu
        gu = mm_nn(h2, g_gu, lambda n, k: (n // npb, layer, k, n % npb), 2 * ff, tm=tm, tn=tn_gu, tk=d, out_dtype=BF16, name=f"ffn_up_{layer}")
        act = swiglu_fwd(gu, f"swiglu_{layer}")
        y2 = mm_nn(act, g_down, lambda n, k: (k, layer, 0, n), d, tm=tm, tn=min(1024, d), tk=down_sh, out_dtype=F32, name=f"ffn_down_{layer}")
        rec.update(attn=attn, y=y, x_mid=xmid, h2=h2, gu=gu, act=act, y2=y2)
        saved.append(rec)
        if layer + 1 < depth:
            xin, h = postnorm_residual(xmid, y2, gain(layer, 3), gain(layer + 1, 0), f"postnorm_ffn_{layer}")
    dx, loss_blk = postnorm_loss(xmid, y2, gain(depth - 1, 3), target, "loss")

    def grad_buf(w):
        return jnp.zeros((N_CHIPS,) + w.shape, BF16)

    gb_swa_in, gb_swa_out, gb_fox_out, gb_gu, gb_down = (grad_buf(w) for w in (swa_w_in, swa_w_out, fox_w_out, ffn_w_gate_up, ffn_w_down))
    fox_in_grads = [None] * (depth // 2)
    dgains = [None] * (depth * 4)
    dsinks = [None] * ((depth + 1) // 2)
    dbf = [None] * (depth // 2)
    tko = min(1024, d)
    for layer in reversed(range(depth)):
        j = layer // 2
        rec = saved[layer]
        dy2, dgains[layer * 4 + 3] = postnorm_bwd(dx, rec["y2"], gain(layer, 3), f"postnorm_ffn_bwd_{layer}")
        gb_down = mm_tn(rec["act"], dy2, lambda i, n: (i, layer, 0, n), tka=down_sh, tn=min(1024, d), tm=min(512, s),
                        out_block=(None, None, down_sh, min(1024, d)), name=f"ffn_down_dw_{layer}", into=gb_down)
        dact = mm_nt(dy2, g_down, lambda o, n: (o, layer, 0, n), ff, tm=tm, tko=down_sh, tn=d, out_dtype=BF16, name=f"ffn_down_dx_{layer}")
        dgu = swiglu_bwd(dact, rec["gu"], f"swiglu_bwd_{layer}")
        npb = gu_sh // tn_gu
        gb_gu = mm_tn(rec["h2"], dgu, lambda i, n: (n // npb, layer, i, n % npb), tka=tko, tn=tn_gu, tm=min(512, s),
                      out_block=(None, None, tko, tn_gu), name=f"ffn_up_dw_{layer}", into=gb_gu)
        dh2 = mm_nt(dgu, g_gu, lambda o, n: (n // npb, layer, o, n % npb), d, tm=tm, tko=tko, tn=tn_gu, out_dtype=F32, name=f"ffn_up_dx_{layer}")
        dxm, dgains[layer * 4 + 2] = prenorm_bwd(rec["x_mid"], dh2, dx, gain(layer, 2), f"prenorm_ffn_bwd_{layer}")
        dy, dgains[layer * 4 + 1] = postnorm_bwd(dxm, rec["y"], gain(layer, 1), f"postnorm_mixer_bwd_{layer}")
        w_out_g = g_swa_out if layer % 2 == 0 else g_fox_out
        gb_out = gb_swa_out if layer % 2 == 0 else gb_fox_out
        gb_out = mm_tn(rec["attn"], dy, lambda i, n: (i, j, 0, n), tka=out_sh, tn=min(1024, d), tm=min(512, s),
                       out_block=(None, None, out_sh, min(1024, d)), name=f"out_proj_dw_{layer}", into=gb_out)
        dattn = mm_nt(dy, w_out_g, lambda o, n: (o, j, 0, n), hd, tm=tm, tko=out_sh, tn=d, out_dtype=BF16, name=f"out_proj_dx_{layer}")
        if layer % 2 == 0:
            gb_swa_out = gb_out
            dq, dkd, dvd, dsk = swa_bwd(rec["q"], rec["kd"], rec["vd"], rec["sink_row"], rec["attn"], rec["lse"], dattn, f"swa_bwd_{j}")
            dsinks[j] = dsk[0].reshape(n_heads, HEAD_DIM)[:, 0]
            dproj = swa_merge_bwd(dq, dkd, dvd, tabs, f"swa_merge_bwd_{j}")
            gb_swa_in = mm_tn(rec["h1"], dproj, lambda i, n: (n, j, i, 0), tka=tko, tn=n_sh_in, tm=min(512, s),
                              out_block=(None, None, tko, n_sh_in), name=f"swa_proj_dw_{j}", into=gb_swa_in)
            dh1 = mm_nt(dproj, g_swa_in, lambda o, n: (n, j, o, 0), d, tm=tm, tko=tko, tn=n_sh_in, out_dtype=F32, name=f"swa_proj_dx_{j}")
        else:
            gb_fox_out = gb_out
            delta = head_rowdot(dattn, rec["attn"], f"fox_delta_{j}")
            dq, dk, dv, dcrow, dcq = fox_bwd(rec["qkv"], rec["crow"], rec["lse"], delta, dattn, n_heads, f"fox_bwd_{j}")
            dc = jnp.transpose(dcrow[:, :, :2, :], (0, 2, 1, 3)).reshape(n_heads, s)
            lane_pad = ((0, 0), (0, LANES - n_heads))
            df, db = forget_gate_bwd(jnp.pad(jnp.transpose(dc), lane_pad), jnp.pad(dcq[:, ::HEAD_DIM], lane_pad),
                                     rec["f_logit"], rec["b_row"], f"fox_gate_bwd_{j}")
            dbf[j] = db[0, :n_heads]
            dproj = jnp.concatenate([dq.astype(BF16), dk, dv, df], axis=1)
            tn_f = LANES * max(k for k in range(1, 9) if (fox_pad // LANES) % k == 0)
            dwf = mm_tn(rec["h1"], dproj, lambda i, n: (i, n), tka=tko, tn=tn_f, tm=min(512, s), out_block=(tko, tn_f),
                        name=f"fox_proj_dw_{j}", out_shape=jax.ShapeDtypeStruct((d, fox_pad), BF16))
            fox_in_grads[j] = dwf
            dh1 = mm_nt(dproj, rec["wf"], lambda o, n: (o, n), d, tm=tm, tko=tko, tn=tn_f, out_dtype=F32, name=f"fox_proj_dx_{j}")
        dx, dgains[layer * 4] = prenorm_bwd(rec["x_in"], dh1, dxm, gain(layer, 0), f"prenorm_mixer_bwd_{layer}")
    grad_x = dx[None]

    gb_fox_in = jnp.stack([jnp.stack([g[:, q * fox_cols:(q + 1) * fox_cols] for g in fox_in_grads]) for q in range(N_CHIPS)])
    dgain_full = jnp.concatenate(dgains, axis=0)
    gb_gains = jnp.transpose(dgain_full.reshape(depth * 4, N_CHIPS, -1), (1, 0, 2)).reshape(N_CHIPS, 2, depth * 2, -1)
    partials = [gb_swa_in, gb_swa_out, gb_fox_in, gb_fox_out, gb_gu, gb_down, gb_gains]
    names = w_names + ["norm_gains"]
    from_sibling = swap_core_halves(partials, "reduce_swap_cores")
    chip_sums = [add_core_halves(g, r, c_idx, "reduce_add_cores_" + nm) for g, r, nm in zip(partials, from_sibling, names)]
    from_chips = scatter_to_chips(chip_sums, "reduce_scatter_chips")
    my_half = [add_chips(p, "reduce_add_chips_" + nm) for p, nm in zip(from_chips, names)]
    full = join_core_halves(my_half, "reduce_join_cores")
    g_swa_in_f, g_swa_out_f, g_fox_in_f, g_fox_out_f, g_gu_f, g_down_f, g_gains_f = full
    g_gains_f = g_gains_f.reshape(norm_gains.shape)

    n_swa, n_fox = len(dsinks), len(dbf)
    small = jnp.concatenate([loss_blk[0, :1]] + dsinks + dbf)
    small = lax.psum(small, ("x", "y", "c"))
    loss = small[0]
    g_sinks = small[1:1 + n_swa * n_heads].reshape(n_swa, n_heads)
    g_bf = small[1 + n_swa * n_heads:].reshape(n_fox, n_heads)

    def pad_small(a):
        return jnp.pad(a, ((0, 8 - a.shape[0]), (0, LANES - a.shape[1])))[None]

    def update(w, g, m, v, nm):
        if w.ndim == 2:
            dl, mn, vn = adamw(pad_small(w), pad_small(g), pad_small(m), pad_small(v), "adamw_" + nm)
            return tuple(a[0, :w.shape[0], :w.shape[1]] for a in (dl, mn, vn))
        return adamw(w, g, m, v, "adamw_" + nm)

    grads = [g_gains_f, g_swa_in_f, g_sinks, g_swa_out_f, g_fox_in_f, g_bf, g_fox_out_f, g_gu_f, g_down_f]
    ws = [norm_gains, swa_w_in, swa_sinks, swa_w_out, fox_w_in, fox_b_f, fox_w_out, ffn_w_gate_up, ffn_w_down]
    ms = [m_norm_gains, m_swa_w_in, m_swa_sinks, m_swa_w_out, m_fox_w_in, m_fox_b_f, m_fox_w_out, m_ffn_w_gate_up, m_ffn_w_down]
    vs = [v_norm_gains, v_swa_w_in, v_swa_sinks, v_swa_w_out, v_fox_w_in, v_fox_b_f, v_fox_w_out, v_ffn_w_gate_up, v_ffn_w_down]
    nms = ["norm_gains", "swa_w_in", "swa_sinks", "swa_w_out", "fox_w_in", "fox_b_f", "fox_w_out", "ffn_w_gate_up", "ffn_w_down"]
    upd = [update(w, g, m, v, nm) for w, g, m, v, nm in zip(ws, grads, ms, vs, nms)]
    return (loss, grad_x, *grads, *[u[0] for u in upd], *[u[1] for u in upd], *[u[2] for u in upd])
```

```python
import jax
import jax.numpy as jnp
from jax import lax
from jax.experimental import pallas as pl
from jax.experimental.pallas import tpu as pltpu

F32 = jnp.float32
BF16 = jnp.bfloat16
HEAD_DIM = 64
LANES = 128
WINDOW = 128
ROPE_THETA = 500000.0
ROT_DIM = HEAD_DIM // 4
RMS_EPS = 1e-6
ADAM_LR, ADAM_B1, ADAM_B2, ADAM_EPS, ADAM_WD, ADAM_STEP = 0.001, 0.9, 0.999, 1e-08, 0.01, 10
NEG = -1e30
VMEM_BIG = 56 * 1024 * 1024
N_CHIPS = 4
MESH = pl.DeviceIdType.MESH
ANY = pl.BlockSpec(memory_space=pl.ANY)


def _pcall(body, **kw):
    return pl.pallas_call(body, **kw)


def _cp(sem=None, vmem=None):
    return pltpu.CompilerParams(dimension_semantics=sem, vmem_limit_bytes=vmem)


def _tile(n, pref):
    if n <= pref:
        return n
    t = pref - pref % 8
    while n % t:
        t -= 8
    return t


_DIMS = {"nn": (((1,), (0,)), ((), ())), "nt": (((1,), (1,)), ((), ())), "tn": (((0,), (0,)), ((), ()))}


def _matmul(kind, a, b, *, grid, a_spec, b_spec, o_spec, out_shape, acc_shape, name, into=None):
    nk = grid[2]
    dims = _DIMS[kind]

    def body(a_ref, b_ref, *rest):
        rest = rest[1:] if into is not None else rest
        o_ref = rest[0]

        def prod():
            return lax.dot_general(a_ref[...], b_ref[...], dims, preferred_element_type=F32)

        if nk == 1:
            o_ref[...] = prod().astype(o_ref.dtype)
        else:
            acc = rest[1]
            k = pl.program_id(2)

            @pl.when(k == 0)
            def _():
                acc[...] = prod()

            @pl.when(k > 0)
            def _():
                acc[...] += prod()

            @pl.when(k == nk - 1)
            def _():
                o_ref[...] = acc[...].astype(o_ref.dtype)

    args, in_specs, aliases = [a, b], [a_spec, b_spec], {}
    if into is not None:
        args.append(into)
        in_specs.append(ANY)
        aliases = {2: 0}
        out_shape = jax.ShapeDtypeStruct(into.shape, into.dtype)
    return _pcall(body, name=name, grid=grid, in_specs=in_specs, out_specs=o_spec, out_shape=out_shape,
                  scratch_shapes=[] if nk == 1 else [pltpu.VMEM(acc_shape, F32)], input_output_aliases=aliases,
                  compiler_params=_cp(("parallel", "parallel", "arbitrary"), VMEM_BIG))(*args)


def mm_nn(a, w, w_map, n_out, *, tm, tn, tk, out_dtype, name):
    m, kdim = a.shape
    tm = min(tm, m)
    lead = (None,) * (w.ndim - 2)
    return _matmul("nn", a, w, grid=(m // tm, n_out // tn, kdim // tk),
                   a_spec=pl.BlockSpec((tm, tk), lambda i, j, k: (i, k)),
                   b_spec=pl.BlockSpec(lead + (tk, tn), lambda i, j, k: w_map(j, k)),
                   o_spec=pl.BlockSpec((tm, tn), lambda i, j, k: (i, j)),
                   out_shape=jax.ShapeDtypeStruct((m, n_out), out_dtype), acc_shape=(tm, tn), name=name)


def mm_nt(a, w, w_map, k_out, *, tm, tko, tn, out_dtype, name):
    m, ndim = a.shape
    tm = min(tm, m)
    lead = (None,) * (w.ndim - 2)
    return _matmul("nt", a, w, grid=(m // tm, k_out // tko, ndim // tn),
                   a_spec=pl.BlockSpec((tm, tn), lambda i, j, n: (i, n)),
                   b_spec=pl.BlockSpec(lead + (tko, tn), lambda i, j, n: w_map(j, n)),
                   o_spec=pl.BlockSpec((tm, tko), lambda i, j, n: (i, j)),
                   out_shape=jax.ShapeDtypeStruct((m, k_out), out_dtype), acc_shape=(tm, tko), name=name)


def mm_tn(a, b, o_map, *, tka, tn, tm, out_block, name, into=None, out_shape=None):
    m, kdim = a.shape
    n = b.shape[1]
    tm = min(tm, m)
    return _matmul("tn", a, b, grid=(kdim // tka, n // tn, m // tm),
                   a_spec=pl.BlockSpec((tm, tka), lambda i, j, mm: (mm, i)),
                   b_spec=pl.BlockSpec((tm, tn), lambda i, j, mm: (mm, j)),
                   o_spec=pl.BlockSpec(out_block, lambda i, j, mm: o_map(i, j)),
                   out_shape=out_shape, acc_shape=(tka, tn), name=name, into=into)


def _rstd(v):
    return lax.rsqrt(jnp.mean(v * v, axis=-1, keepdims=True) + RMS_EPS)


def _row_spec(tr, d):
    return pl.BlockSpec((tr, d), lambda i: (i, 0))


def _vec_spec(d):
    return pl.BlockSpec((1, d), lambda i: (0, 0))


def prenorm(x, g, name):
    s, d = x.shape
    tr = _tile(s, 256)

    def body(x_ref, g_ref, h_ref):
        v = x_ref[...]
        h_ref[...] = (v * _rstd(v) * g_ref[...]).astype(BF16)

    return _pcall(body, name=name, grid=(s // tr,), in_specs=[_row_spec(tr, d), _vec_spec(d)],
                  out_specs=_row_spec(tr, d), out_shape=jax.ShapeDtypeStruct((s, d), BF16),
                  compiler_params=_cp(("parallel",)))(x, g)


def postnorm_residual(x, y, g_post, g_next, name):
    s, d = x.shape
    tr = _tile(s, 256)

    def body(x_ref, y_ref, gp_ref, gn_ref, xo_ref, h_ref):
        v = y_ref[...]
        xn = x_ref[...] + v * _rstd(v) * gp_ref[...]
        xo_ref[...] = xn
        h_ref[...] = (xn * _rstd(xn) * gn_ref[...]).astype(BF16)

    return _pcall(body, name=name, grid=(s // tr,),
                  in_specs=[_row_spec(tr, d), _row_spec(tr, d), _vec_spec(d), _vec_spec(d)],
                  out_specs=[_row_spec(tr, d), _row_spec(tr, d)],
                  out_shape=[jax.ShapeDtypeStruct((s, d), F32), jax.ShapeDtypeStruct((s, d), BF16)],
                  compiler_params=_cp(("parallel",)))(x, y, g_post, g_next)


def postnorm_loss(x, y, g_post, target, name):
    s, d = x.shape
    tr = _tile(s, 256)

    def body(x_ref, y_ref, gp_ref, t_ref, dx_ref, loss_ref):
        v = y_ref[...]
        err = x_ref[...] + v * _rstd(v) * gp_ref[...] - t_ref[...]
        dx_ref[...] = err / d
        part = 0.5 * jnp.sum(jnp.mean(err * err, axis=-1, keepdims=True), axis=0, keepdims=True)

        @pl.when(pl.program_id(0) == 0)
        def _():
            loss_ref[...] = jnp.zeros_like(loss_ref)

        loss_ref[...] += part

    return _pcall(body, name=name, grid=(s // tr,),
                  in_specs=[_row_spec(tr, d), _row_spec(tr, d), _vec_spec(d), _row_spec(tr, d)],
                  out_specs=[_row_spec(tr, d), pl.BlockSpec((8, LANES), lambda i: (0, 0))],
                  out_shape=[jax.ShapeDtypeStruct((s, d), F32), jax.ShapeDtypeStruct((8, LANES), F32)],
                  compiler_params=_cp(("arbitrary",)))(x, y, g_post, target)


def _norm_bwd(v, g, dz):
    r = _rstd(v)
    vhat = v * r
    u = dz * g
    dv = r * (u - vhat * jnp.mean(u * vhat, axis=-1, keepdims=True))
    return dv, jnp.sum(dz * vhat, axis=0, keepdims=True)


def _acc_rows(ref, val):
    @pl.when(pl.program_id(0) == 0)
    def _():
        ref[...] = jnp.zeros_like(ref)

    ref[...] += val


def postnorm_bwd(dz, y, g, name):
    s, d = y.shape
    tr = _tile(s, 256)

    def body(dz_ref, y_ref, g_ref, dy_ref, dg_ref):
        dv, dg = _norm_bwd(y_ref[...], g_ref[...], dz_ref[...])
        dy_ref[...] = dv.astype(BF16)
        _acc_rows(dg_ref, dg)

    return _pcall(body, name=name, grid=(s // tr,), in_specs=[_row_spec(tr, d), _row_spec(tr, d), _vec_spec(d)],
                  out_specs=[_row_spec(tr, d), _vec_spec(d)],
                  out_shape=[jax.ShapeDtypeStruct((s, d), BF16), jax.ShapeDtypeStruct((1, d), F32)],
                  compiler_params=_cp(("arbitrary",)))(dz, y, g)


def prenorm_bwd(x, dh, dskip, g, name):
    s, d = x.shape
    tr = _tile(s, 256)

    def body(x_ref, dh_ref, ds_ref, g_ref, dx_ref, dg_ref):
        dv, dg = _norm_bwd(x_ref[...], g_ref[...], dh_ref[...])
        dx_ref[...] = ds_ref[...] + dv
        _acc_rows(dg_ref, dg)

    return _pcall(body, name=name, grid=(s // tr,),
                  in_specs=[_row_spec(tr, d), _row_spec(tr, d), _row_spec(tr, d), _vec_spec(d)],
                  out_specs=[_row_spec(tr, d), _vec_spec(d)],
                  out_shape=[jax.ShapeDtypeStruct((s, d), F32), jax.ShapeDtypeStruct((1, d), F32)],
                  compiler_params=_cp(("arbitrary",)))(x, dh, dskip, g)


def swiglu_fwd(gu, name):
    s, two_ff = gu.shape
    ff = two_ff // 2
    tr = _tile(s, 128)

    def body(gu_ref, act_ref):
        gate = gu_ref[:, :ff].astype(F32)
        up = gu_ref[:, ff:].astype(F32)
        act_ref[...] = (gate * jax.nn.sigmoid(gate) * up).astype(BF16)

    return _pcall(body, name=name, grid=(s // tr,), in_specs=[_row_spec(tr, two_ff)], out_specs=_row_spec(tr, ff),
                  out_shape=jax.ShapeDtypeStruct((s, ff), BF16), compiler_params=_cp(("parallel",)))(gu)


def swiglu_bwd(dact, gu, name):
    s, two_ff = gu.shape
    ff = two_ff // 2
    tr = _tile(s, 128)

    def body(da_ref, gu_ref, dgu_ref):
        gate = gu_ref[:, :ff].astype(F32)
        up = gu_ref[:, ff:].astype(F32)
        da = da_ref[...].astype(F32)
        sig = jax.nn.sigmoid(gate)
        dgu_ref[:, :ff] = (da * up * sig * (1.0 + gate * (1.0 - sig))).astype(BF16)
        dgu_ref[:, ff:] = (da * gate * sig).astype(BF16)

    return _pcall(body, name=name, grid=(s // tr,), in_specs=[_row_spec(tr, ff), _row_spec(tr, two_ff)],
                  out_specs=_row_spec(tr, two_ff), out_shape=jax.ShapeDtypeStruct((s, two_ff), BF16),
                  compiler_params=_cp(("parallel",)))(dact, gu)


def rope_tables(pos_col, invf_row, name):
    s = pos_col.shape[0]
    tr = _tile(s, 1024)

    def body(p_ref, f_ref, c_ref, sa_ref, sb_ref):
        ang = p_ref[...].astype(F32) * f_ref[...]
        d = lax.broadcasted_iota(jnp.int32, (1, LANES), 1) % HEAD_DIM
        cs, sn = jnp.cos(ang), jnp.sin(ang)
        c_ref[...] = jnp.where(d < ROT_DIM, cs, 1.0)
        sa_ref[...] = jnp.where(d < ROT_DIM // 2, -sn, 0.0)
        sb_ref[...] = jnp.where((d >= ROT_DIM // 2) & (d < ROT_DIM), sn, 0.0)

    tab = jax.ShapeDtypeStruct((s, LANES), F32)
    return _pcall(body, name=name, grid=(s // tr,),
                  in_specs=[pl.BlockSpec((tr, 1), lambda i: (i, 0)), _vec_spec(LANES)],
                  out_specs=[_row_spec(tr, LANES)] * 3, out_shape=[tab] * 3, compiler_params=_cp(("parallel",)))(pos_col, invf_row)


def _rot(t, c, sa, sb):
    half = ROT_DIM // 2
    return t * c + pltpu.roll(t, LANES - half, 1) * sa + pltpu.roll(t, half, 1) * sb


def _rot_t(t, c, sa, sb):
    half = ROT_DIM // 2
    return t * c + pltpu.roll(t * sa, half, 1) + pltpu.roll(t * sb, LANES - half, 1)


def swa_split(proj, tabs, n_heads, n_kv, name):
    s, width = proj.shape
    qd, kd = n_heads * HEAD_DIM, n_kv * HEAD_DIM
    tr = _tile(s, 256)

    def body(p_ref, c_ref, sa_ref, sb_ref, q_ref, k_ref, v_ref):
        c, sa, sb = c_ref[...], sa_ref[...], sb_ref[...]
        low = lax.broadcasted_iota(jnp.int32, (1, LANES), 1) < HEAD_DIM
        for g in range(qd // LANES):
            q_ref[:, g * LANES:(g + 1) * LANES] = _rot(p_ref[:, g * LANES:(g + 1) * LANES], c, sa, sb).astype(BF16)
        for g in range(kd // LANES):
            for src, dst, rot in ((qd, k_ref, True), (qd + kd, v_ref, False)):
                t = p_ref[:, src + g * LANES:src + (g + 1) * LANES]
                t = _rot(t, c, sa, sb) if rot else t
                sw = pltpu.roll(t, HEAD_DIM, 1)
                dst[:, (2 * g) * LANES:(2 * g + 1) * LANES] = jnp.where(low, t, sw).astype(BF16)
                dst[:, (2 * g + 1) * LANES:(2 * g + 2) * LANES] = jnp.where(low, sw, t).astype(BF16)

    return _pcall(body, name=name, grid=(s // tr,),
                  in_specs=[_row_spec(tr, width)] + [_row_spec(tr, LANES)] * 3,
                  out_specs=[_row_spec(tr, qd), _row_spec(tr, 2 * kd), _row_spec(tr, 2 * kd)],
                  out_shape=[jax.ShapeDtypeStruct((s, qd), BF16), jax.ShapeDtypeStruct((s, 2 * kd), BF16),
                             jax.ShapeDtypeStruct((s, 2 * kd), BF16)],
                  compiler_params=_cp(("parallel",)))(proj, *tabs)


def swa_merge_bwd(dq, dkd, dvd, tabs, name):
    s, qd = dq.shape
    kd = dkd.shape[1] // 2
    width = qd + 2 * kd
    tr = _tile(s, 256)

    def body(dq_ref, dk_ref, dv_ref, c_ref, sa_ref, sb_ref, o_ref):
        c, sa, sb = c_ref[...], sa_ref[...], sb_ref[...]
        low = lax.broadcasted_iota(jnp.int32, (1, LANES), 1) < HEAD_DIM
        for g in range(qd // LANES):
            t = dq_ref[:, g * LANES:(g + 1) * LANES].astype(F32)
            o_ref[:, g * LANES:(g + 1) * LANES] = _rot_t(t, c, sa, sb).astype(BF16)
        for g in range(kd // LANES):
            for dst, src, rot in ((qd, dk_ref, True), (qd + kd, dv_ref, False)):
                e = src[:, (2 * g) * LANES:(2 * g + 1) * LANES]
                o = src[:, (2 * g + 1) * LANES:(2 * g + 2) * LANES]
                t = jnp.where(low, e + pltpu.roll(e, HEAD_DIM, 1), o + pltpu.roll(o, HEAD_DIM, 1))
                t = _rot_t(t, c, sa, sb) if rot else t
                o_ref[:, dst + g * LANES:dst + (g + 1) * LANES] = t.astype(BF16)

    return _pcall(body, name=name, grid=(s // tr,),
                  in_specs=[_row_spec(tr, qd), _row_spec(tr, 2 * kd), _row_spec(tr, 2 * kd)] + [_row_spec(tr, LANES)] * 3,
                  out_specs=_row_spec(tr, width), out_shape=jax.ShapeDtypeStruct((s, width), BF16),
                  compiler_params=_cp(("parallel",)))(dq, dkd, dvd, *tabs)


def _halves():
    lane_half = lax.broadcasted_iota(jnp.int32, (1, LANES), 1) // HEAD_DIM
    return [lane_half == 0, lane_half == 1]


def _nt(a, b):
    return lax.dot_general(a, b, _DIMS["nt"], preferred_element_type=F32)


def _tn(a, b):
    return lax.dot_general(a, b, _DIMS["tn"], preferred_element_type=F32)


def _band_mask(i, tq):
    w = tq + WINDOW
    rel = lax.broadcasted_iota(jnp.int32, (tq, w), 1) - lax.broadcasted_iota(jnp.int32, (tq, w), 0)
    first = lax.broadcasted_iota(jnp.int32, (tq, w), 1) >= jnp.where(i > 0, 0, WINDOW)
    return (rel >= 1) & (rel <= WINDOW) & first


def swa_fwd(q, kd, vd, sink_row, name):
    s, qd = q.shape
    tq = min(256, s)
    r = tq // WINDOW
    pairs = qd // LANES
    group_pairs = pairs // (kd.shape[1] // LANES)
    scale = HEAD_DIM ** -0.5

    def body(q_ref, kp_ref, kc_ref, vp_ref, vc_ref, sk_ref, o_ref, lse_ref):
        i = pl.program_id(1)
        k = jnp.concatenate([kp_ref[...], kc_ref[...]], axis=0)
        v = jnp.concatenate([vp_ref[...], vc_ref[...]], axis=0)
        mask = _band_mask(i, tq)
        q2 = q_ref[...]
        outs, lses = [], []
        for a, hm in enumerate(_halves()):
            sc = _nt(jnp.where(hm, q2, jnp.zeros_like(q2)), k) * scale
            sc = jnp.where(mask, sc, NEG)
            sink = sk_ref[:, a * HEAD_DIM:a * HEAD_DIM + 1]
            m = jnp.maximum(jnp.max(sc, axis=-1, keepdims=True), sink)
            p = jnp.exp(sc - m)
            den = jnp.sum(p, axis=-1, keepdims=True) + jnp.exp(sink - m)
            outs.append(jnp.dot(p.astype(BF16), v, preferred_element_type=F32) / den)
            lses.append(m + jnp.log(den))
        hm0 = _halves()[0]
        o_ref[...] = jnp.where(hm0, outs[0], outs[1]).astype(BF16)
        lse_ref[...] = jnp.where(hm0, lses[0], lses[1])

    prev = lambda p, i: (jnp.maximum(i * r - 1, 0), p // group_pairs)
    cur = lambda p, i: (i, p // group_pairs)
    blk = pl.BlockSpec((tq, LANES), lambda p, i: (i, p))
    return _pcall(body, name=name, grid=(pairs, s // tq),
                  in_specs=[blk, pl.BlockSpec((WINDOW, LANES), prev), pl.BlockSpec((tq, LANES), cur),
                            pl.BlockSpec((WINDOW, LANES), prev), pl.BlockSpec((tq, LANES), cur),
                            pl.BlockSpec((1, LANES), lambda p, i: (0, p))],
                  out_specs=[blk, blk],
                  out_shape=[jax.ShapeDtypeStruct((s, qd), BF16), jax.ShapeDtypeStruct((s, qd), F32)],
                  compiler_params=_cp(("parallel", "parallel")))(q, kd, kd, vd, vd, sink_row)


def swa_bwd(q, kd, vd, sink_row, out, lse, dout, name):
    s, qd = q.shape
    tq = min(256, s)
    r = tq // WINDOW
    n_kv = kd.shape[1] // LANES
    gw = qd // n_kv
    scale = HEAD_DIM ** -0.5

    def body(q_ref, kp_ref, kc_ref, vp_ref, vc_ref, sk_ref, o_ref, lse_ref, do_ref, dq_ref, dk_ref, dv_ref, dsk_ref):
        i = pl.program_id(1)

        @pl.when(i == 0)
        def _():
            dk_ref[...] = jnp.zeros_like(dk_ref)
            dv_ref[...] = jnp.zeros_like(dv_ref)
            dsk_ref[...] = jnp.zeros_like(dsk_ref)

        k = jnp.concatenate([kp_ref[...], kc_ref[...]], axis=0)
        v = jnp.concatenate([vp_ref[...], vc_ref[...]], axis=0)
        mask = _band_mask(i, tq)
        dk = jnp.zeros((tq + WINDOW, LANES), F32)
        dv = jnp.zeros((tq + WINDOW, LANES), F32)
        for pp in range(gw // LANES):
            cols = slice(pp * LANES, (pp + 1) * LANES)
            q2, do2 = q_ref[:, cols], do_ref[:, cols]
            prod = do2.astype(F32) * o_ref[:, cols].astype(F32)
            dq2 = jnp.zeros((tq, LANES), F32)
            dsk = jnp.zeros((1, LANES), F32)
            for a, hm in enumerate(_halves()):
                qa = jnp.where(hm, q2, jnp.zeros_like(q2))
                doa = jnp.where(hm, do2, jnp.zeros_like(do2))
                lse_a = lse_ref[:, pp * LANES + a * HEAD_DIM:pp * LANES + a * HEAD_DIM + 1]
                sc = jnp.where(mask, _nt(qa, k) * scale, NEG)
                p = jnp.exp(sc - lse_a)
                delta = jnp.sum(jnp.where(hm, prod, 0.0), axis=-1, keepdims=True)
                ds = (p * (_nt(doa, v) - delta) * scale).astype(BF16)
                dv = dv + _tn(p.astype(BF16), doa)
                dk = dk + _tn(ds, qa)
                dq2 = dq2 + jnp.where(hm, jnp.dot(ds, k, preferred_element_type=F32), 0.0)
                sink = sk_ref[:, pp * LANES + a * HEAD_DIM:pp * LANES + a * HEAD_DIM + 1]
                dsink = -jnp.sum(jnp.exp(sink - lse_a) * delta, axis=0, keepdims=True)
                dsk = dsk + jnp.where(hm, dsink, 0.0)
            dq_ref[:, cols] = dq2.astype(BF16)
            dsk_ref[0:1, cols] += dsk
        start = pl.multiple_of(i * tq, tq)
        dk_ref[pl.ds(start, tq), :] += dk[WINDOW:, :]
        dv_ref[pl.ds(start, tq), :] += dv[WINDOW:, :]

        @pl.when(i > 0)
        def _():
            before = pl.multiple_of(i * tq - WINDOW, WINDOW)
            dk_ref[pl.ds(before, WINDOW), :] += dk[:WINDOW, :]
            dv_ref[pl.ds(before, WINDOW), :] += dv[:WINDOW, :]

    prev = lambda g, i: (jnp.maximum(i * r - 1, 0), g)
    cur = lambda g, i: (i, g)
    wide = pl.BlockSpec((tq, gw), cur)
    full = pl.BlockSpec((s, LANES), lambda g, i: (0, g))
    return _pcall(body, name=name, grid=(n_kv, s // tq),
                  in_specs=[wide, pl.BlockSpec((WINDOW, LANES), prev), pl.BlockSpec((tq, LANES), cur),
                            pl.BlockSpec((WINDOW, LANES), prev), pl.BlockSpec((tq, LANES), cur),
                            pl.BlockSpec((1, gw), lambda g, i: (0, g)), wide, wide, wide],
                  out_specs=[wide, full, full, pl.BlockSpec((8, gw), lambda g, i: (0, g))],
                  out_shape=[jax.ShapeDtypeStruct((s, qd), BF16), jax.ShapeDtypeStruct(kd.shape, F32),
                             jax.ShapeDtypeStruct(kd.shape, F32), jax.ShapeDtypeStruct((8, qd), F32)],
                  compiler_params=_cp(("parallel", "arbitrary"), VMEM_BIG))(q, kd, kd, vd, vd, sink_row, out, lse, dout)


def forget_cumsum(f_logit, b_row, name):
    s = f_logit.shape[0]

    def body(f_ref, b_ref, c_ref):
        z = f_ref[...] + b_ref[...]
        acc = jnp.minimum(z, 0.0) - jnp.log(1.0 + jnp.exp(-jnp.abs(z)))
        row = lax.broadcasted_iota(jnp.int32, (s, LANES), 0)
        d = 1
        while d < s:
            acc = acc + jnp.where(row >= d, pltpu.roll(acc, d, 0), 0.0)
            d *= 2
        c_ref[...] = acc

    return _pcall(body, name=name, in_specs=[pl.BlockSpec((s, LANES), lambda: (0, 0)), pl.BlockSpec((1, LANES), lambda: (0, 0))],
                  out_specs=pl.BlockSpec((s, LANES), lambda: (0, 0)), out_shape=jax.ShapeDtypeStruct((s, LANES), F32),
                  compiler_params=_cp(None, VMEM_BIG))(f_logit, b_row)


def forget_gate_bwd(dc_key, dc_query, f_logit, b_row, name):
    s = dc_key.shape[0]

    def body(dck_ref, dcq_ref, f_ref, b_ref, df_ref, db_ref):
        acc = dck_ref[...] + dcq_ref[...]
        row = lax.broadcasted_iota(jnp.int32, (s, LANES), 0)
        d = 1
        while d < s:
            acc = acc + jnp.where(row < s - d, pltpu.roll(acc, s - d, 0), 0.0)
            d *= 2
        df = acc * jax.nn.sigmoid(-(f_ref[...] + b_ref[...]))
        df_ref[...] = df.astype(BF16)
        db_ref[...] = jnp.sum(df, axis=0, keepdims=True)

    whole = pl.BlockSpec((s, LANES), lambda: (0, 0))
    vec = pl.BlockSpec((1, LANES), lambda: (0, 0))
    return _pcall(body, name=name, in_specs=[whole, whole, whole, vec], out_specs=[whole, vec],
                  out_shape=[jax.ShapeDtypeStruct((s, LANES), BF16), jax.ShapeDtypeStruct((1, LANES), F32)],
                  compiler_params=_cp(None, VMEM_BIG))(dc_key, dc_query, f_logit, b_row)


def fox_fwd(qkv, crow, n_heads, name):
    s = qkv.shape[0]
    hd = n_heads * HEAD_DIM
    pairs = hd // LANES
    t = min(512, s)
    scale = HEAD_DIM ** -0.5

    def body(q_ref, k_ref, v_ref, c_ref, o_ref, lse_ref, acc_ref, m_ref, l_ref):
        i = pl.program_id(1)
        halves = _halves()
        q2 = q_ref[...] * scale
        qs = [jnp.where(hm, q2, jnp.zeros_like(q2)) for hm in halves]
        acc_ref[...] = jnp.zeros_like(acc_ref)
        m_ref[...] = jnp.full_like(m_ref, NEG)
        l_ref[...] = jnp.zeros_like(l_ref)

        def step(j, diagonal):
            rows = pl.ds(pl.multiple_of(j * t, t), t)
            kc, vc = k_ref[rows, :], v_ref[rows, :]
            alphas, outs, ms, ls = [], [], [], []
            for a in range(2):
                sc = _nt(qs[a], kc) - c_ref[j, a:a + 1, :]
                if diagonal:
                    keep = lax.broadcasted_iota(jnp.int32, (t, t), 1) <= lax.broadcasted_iota(jnp.int32, (t, t), 0)
                    sc = jnp.where(keep, sc, NEG)
                m_old = m_ref[:, a * HEAD_DIM:a * HEAD_DIM + 1]
                m_new = jnp.maximum(m_old, jnp.max(sc, axis=-1, keepdims=True))
                alpha = jnp.exp(m_old - m_new)
                p = jnp.exp(sc - m_new)
                ls.append(alpha * l_ref[:, a * HEAD_DIM:a * HEAD_DIM + 1] + jnp.sum(p, axis=-1, keepdims=True))
                outs.append(jnp.dot(p.astype(BF16), vc, preferred_element_type=F32))
                alphas.append(alpha)
                ms.append(m_new)
            hm0 = halves[0]
            acc_ref[...] = jnp.where(hm0, alphas[0], alphas[1]) * acc_ref[...] + jnp.where(hm0, outs[0], outs[1])
            m_ref[...] = jnp.where(hm0, ms[0], ms[1])
            l_ref[...] = jnp.where(hm0, ls[0], ls[1])

        def loop_body(j, carry):
            step(j, False)
            return carry

        lax.fori_loop(0, i, loop_body, 0)
        step(i, True)
        o_ref[...] = (acc_ref[...] / l_ref[...]).astype(BF16)
        lse_ref[...] = m_ref[...] + jnp.log(l_ref[...])

    blk = pl.BlockSpec((t, LANES), lambda p, i: (i, p))
    return _pcall(body, name=name, grid=(pairs, s // t),
                  in_specs=[blk, pl.BlockSpec((s, LANES), lambda p, i: (0, pairs + p)),
                            pl.BlockSpec((s, LANES), lambda p, i: (0, 2 * pairs + p)),
                            pl.BlockSpec((None, s // t, 8, t), lambda p, i: (p, 0, 0, 0))],
                  out_specs=[blk, blk],
                  out_shape=[jax.ShapeDtypeStruct((s, hd), BF16), jax.ShapeDtypeStruct((s, hd), F32)],
                  scratch_shapes=[pltpu.VMEM((t, LANES), F32)] * 3,
                  compiler_params=_cp(("parallel", "arbitrary"), VMEM_BIG))(qkv, qkv, qkv, crow)


def head_rowdot(a, b, name):
    s, hd = a.shape
    tr = _tile(s, 256)

    def body(a_ref, b_ref, o_ref):
        halves = _halves()
        for g in range(hd // LANES):
            cols = slice(g * LANES, (g + 1) * LANES)
            prod = a_ref[:, cols].astype(F32) * b_ref[:, cols].astype(F32)
            d0 = jnp.sum(jnp.where(halves[0], prod, 0.0), axis=-1, keepdims=True)
            d1 = jnp.sum(jnp.where(halves[1], prod, 0.0), axis=-1, keepdims=True)
            o_ref[:, cols] = jnp.where(halves[0], d0, d1)

    return _pcall(body, name=name, grid=(s // tr,), in_specs=[_row_spec(tr, hd), _row_spec(tr, hd)],
                  out_specs=_row_spec(tr, hd), out_shape=jax.ShapeDtypeStruct((s, hd), F32),
                  compiler_params=_cp(("parallel",)))(a, b)


def fox_bwd(qkv, crow, lse, delta, dout, n_heads, name):
    s = qkv.shape[0]
    hd = n_heads * HEAD_DIM
    pairs = hd // LANES
    t = min(512, s)
    nblk = s // t
    scale = HEAD_DIM ** -0.5

    def body(q_ref, k_ref, v_ref, c_ref, lse_ref, dl_ref, do_ref, dq_ref, dk_ref, dv_ref, dc_ref, dr_ref, dka_ref, dva_ref, dca_ref):
        j = pl.program_id(1)
        halves = _halves()

        @pl.when(j == 0)
        def _():
            dq_ref[...] = jnp.zeros_like(dq_ref)
            dr_ref[...] = jnp.zeros_like(dr_ref)

        kc, vc = k_ref[...], v_ref[...]
        dka_ref[...] = jnp.zeros_like(dka_ref)
        dva_ref[...] = jnp.zeros_like(dva_ref)
        dca_ref[...] = jnp.zeros_like(dca_ref)

        def step(i, diagonal):
            rows = pl.ds(pl.multiple_of(i * t, t), t)
            q2 = q_ref[rows, :] * scale
            do2 = do_ref[rows, :]
            dq2 = jnp.zeros((t, LANES), F32)
            row_sums = []
            for a, hm in enumerate(halves):
                qa = jnp.where(hm, q2, jnp.zeros_like(q2))
                doa = jnp.where(hm, do2, jnp.zeros_like(do2))
                sc = _nt(qa, kc) - c_ref[a:a + 1, :]
                if diagonal:
                    keep = lax.broadcasted_iota(jnp.int32, (t, t), 1) <= lax.broadcasted_iota(jnp.int32, (t, t), 0)
                    sc = jnp.where(keep, sc, NEG)
                p = jnp.exp(sc - lse_ref[rows, a * HEAD_DIM:a * HEAD_DIM + 1])
                ds = p * (_nt(doa, vc) - dl_ref[rows, a * HEAD_DIM:a * HEAD_DIM + 1])
                dca_ref[a:a + 1, :] -= jnp.sum(ds, axis=0, keepdims=True)
                row_sums.append(jnp.sum(ds, axis=1, keepdims=True))
                ds = ds.astype(BF16)
                dva_ref[...] += _tn(p.astype(BF16), doa)
                dka_ref[...] += _tn(ds, qa)
                dq2 = dq2 + jnp.where(hm, jnp.dot(ds, kc, preferred_element_type=F32), 0.0)
            dq_ref[rows, :] += dq2 * scale
            dr_ref[rows, :] += jnp.where(halves[0], row_sums[0], row_sums[1])

        step(j, True)

        def loop_body(i, carry):
            step(i, False)
            return carry

        lax.fori_loop(j + 1, nblk, loop_body, 0)
        dk_ref[...] = dka_ref[...].astype(BF16)
        dv_ref[...] = dva_ref[...].astype(BF16)
        dc_ref[...] = dca_ref[...]

    def whole(off):
        return pl.BlockSpec((s, LANES), lambda p, j: (0, off + p))

    def blk(off):
        return pl.BlockSpec((t, LANES), lambda p, j: (j, off + p))

    cblk = pl.BlockSpec((None, None, 8, t), lambda p, j: (p, j, 0, 0))
    return _pcall(body, name=name, grid=(pairs, nblk),
                  in_specs=[whole(0), blk(pairs), blk(2 * pairs), cblk, whole(0), whole(0), whole(0)],
                  out_specs=[whole(0), blk(0), blk(0), cblk, whole(0)],
                  out_shape=[jax.ShapeDtypeStruct((s, hd), F32), jax.ShapeDtypeStruct((s, hd), BF16),
                             jax.ShapeDtypeStruct((s, hd), BF16), jax.ShapeDtypeStruct(crow.shape, F32),
                             jax.ShapeDtypeStruct((s, hd), F32)],
                  scratch_shapes=[pltpu.VMEM((t, LANES), F32), pltpu.VMEM((t, LANES), F32), pltpu.VMEM((8, t), F32)],
                  compiler_params=_cp(("parallel", "arbitrary"), VMEM_BIG))(qkv, qkv, qkv, crow, lse, delta, dout)


def _w_spec(shape, lead_map=None):
    _, r, c = shape[-3:]
    tr = _tile(r, 128)
    n_lead = len(shape) - 2
    if lead_map is None:
        lead_map = lambda l: (l,)
    return tr, pl.BlockSpec((None,) * n_lead + (tr, c), lambda l, i: (*lead_map(l), i, 0))


def cast_into_slot(w, place, dtype, name):
    tr = _tile(w.shape[1], 128)
    blk = (None, tr, w.shape[2])

    def body(me_ref, c_ref, w_ref, o_ref):
        o_ref[...] = w_ref[...].astype(dtype)

    grid_spec = pltpu.PrefetchScalarGridSpec(
        num_scalar_prefetch=2, grid=(w.shape[0], w.shape[1] // tr),
        in_specs=[pl.BlockSpec(blk, lambda l, i, me, c: (l, i, 0))],
        out_specs=pl.BlockSpec((None,) + blk, lambda l, i, me, c: (me[0], l, i, 0)))
    return _pcall(body, name=name, grid_spec=grid_spec, out_shape=jax.ShapeDtypeStruct((N_CHIPS,) + w.shape, dtype),
                  compiler_params=_cp(("parallel", "parallel")))(*place, w)


def adamw(w, g, m, v, name):
    tr, spec = _w_spec(w.shape)

    def body(w_ref, g_ref, m_ref, v_ref, d_ref, mo_ref, vo_ref):
        gg = g_ref[...]
        mn = ADAM_B1 * m_ref[...] + (1.0 - ADAM_B1) * gg
        vn = ADAM_B2 * v_ref[...] + (1.0 - ADAM_B2) * (gg * gg)
        m_hat = mn / (1.0 - ADAM_B1 ** ADAM_STEP)
        v_hat = vn / (1.0 - ADAM_B2 ** ADAM_STEP)
        d_ref[...] = -ADAM_LR * (m_hat / (jnp.sqrt(v_hat) + ADAM_EPS) + ADAM_WD * w_ref[...])
        mo_ref[...] = mn
        vo_ref[...] = vn

    out = jax.ShapeDtypeStruct(w.shape, F32)
    return _pcall(body, name=name, grid=(w.shape[0], w.shape[1] // tr), in_specs=[spec] * 4, out_specs=[spec] * 3,
                  out_shape=[out] * 3, compiler_params=_cp(("parallel", "parallel")))(w, g, m, v)


def add_core_halves(g, recv, place, name):
    h = recv.shape[1]
    tr = _tile(g.shape[2], 128)
    blk = (None, None, tr, g.shape[3])

    def body(me_ref, c_ref, g_ref, r_ref, o_ref):
        o_ref[...] = (g_ref[...].astype(F32) + r_ref[...].astype(F32)).astype(o_ref.dtype)

    grid_spec = pltpu.PrefetchScalarGridSpec(
        num_scalar_prefetch=2, grid=(N_CHIPS, h, g.shape[2] // tr),
        in_specs=[pl.BlockSpec(blk, lambda q, l, i, me, c: (q, c[0] * h + l, i, 0)),
                  pl.BlockSpec(blk, lambda q, l, i, me, c: (q, l, i, 0))],
        out_specs=pl.BlockSpec(blk, lambda q, l, i, me, c: (q, l, i, 0)))
    return _pcall(body, name=name, grid_spec=grid_spec, out_shape=jax.ShapeDtypeStruct(recv.shape, g.dtype),
                  compiler_params=_cp(("parallel", "parallel", "parallel")))(*place, g, recv)


def add_chips(own, recv, place, name):
    _, h, r, cols = own.shape
    tr = _tile(r, 128)
    blk = (None, None, tr, cols)

    def body(me_ref, c_ref, p0, p1, p2, p3, o_ref):
        o_ref[...] = ((p0[...].astype(F32) + p1[...].astype(F32)) + p2[...].astype(F32)) + p3[...].astype(F32)

    def peer(flip):
        return lambda l, i, me, c: (me[0] ^ flip, l, i, 0)

    grid_spec = pltpu.PrefetchScalarGridSpec(
        num_scalar_prefetch=2, grid=(h, r // tr), in_specs=[pl.BlockSpec(blk, peer(f)) for f in (0, 2, 1, 3)],
        out_specs=pl.BlockSpec((None, tr, cols), lambda l, i, me, c: (c[0] * h + l, i, 0)))
    return _pcall(body, name=name, grid_spec=grid_spec, out_shape=jax.ShapeDtypeStruct((2 * h, r, cols), F32),
                  compiler_params=_cp(("parallel", "parallel")))(*place, own, recv, recv, recv)


def _place():
    x, y, c = lax.axis_index("x"), lax.axis_index("y"), lax.axis_index("c")
    others = [(1 - x, y), (x, 1 - y), (1 - x, 1 - y)]
    return x, y, c, others


def _chip_id(chip):
    return 2 * chip[0] + chip[1]


def _comm_call(body, name, n_in, out_shapes, n_sems, in_place=False):
    return _pcall(body, name=name, in_specs=[ANY] * n_in, out_specs=[ANY] * len(out_shapes), out_shape=out_shapes,
                  scratch_shapes=[pltpu.SemaphoreType.DMA((n_sems,)), pltpu.SemaphoreType.DMA((n_sems,))],
                  input_output_aliases={t: t for t in range(n_in)} if in_place else {},
                  compiler_params=pltpu.CompilerParams(has_side_effects=True))


def all_gather_shards(slots, name):
    n = len(slots)

    def body(*refs):
        bufs = refs[n:2 * n]
        send, recv = refs[2 * n:]
        x, y, c, others = _place()
        me = _chip_id((x, y))

        def remote(blk, k, dev):
            return pltpu.make_async_remote_copy(src_ref=blk, dst_ref=blk, send_sem=send.at[k], recv_sem=recv.at[k],
                                                device_id=dev, device_id_type=MESH)

        first, passed = [], []
        for t in range(n):
            h = bufs[t].shape[1] // 2
            mine = pl.ds(c * h, h)
            for j, chip in enumerate(others):
                cp = remote(bufs[t].at[me, mine], 6 * t + j, (*chip, c))
                cp.start()
                first.append(cp)
        for t in range(n):
            h = bufs[t].shape[1] // 2
            mine = pl.ds(c * h, h)
            for j, chip in enumerate(others):
                got = bufs[t].at[_chip_id(chip), mine]
                remote(got, 6 * t + j, (x, y, c)).wait_recv()
                cp = remote(got, 6 * t + 3 + j, (x, y, 1 - c))
                cp.start()
                passed.append(cp)
        for t in range(n):
            h = bufs[t].shape[1] // 2
            theirs = pl.ds((1 - c) * h, h)
            for j, chip in enumerate(others):
                remote(bufs[t].at[_chip_id(chip), theirs], 6 * t + 3 + j, (x, y, c)).wait_recv()
        for cp in first + passed:
            cp.wait_send()

    outs = [jax.ShapeDtypeStruct(a.shape, a.dtype) for a in slots]
    return _comm_call(body, name, n, outs, 6 * n, in_place=True)(*slots)


def swap_core_halves(grads, name):
    n = len(grads)

    def body(*refs):
        ins, outs = refs[:n], refs[n:2 * n]
        send, recv = refs[2 * n:]
        x, y, c, _ = _place()
        cps = []
        for t in range(n):
            h = ins[t].shape[1] // 2
            cp = pltpu.make_async_remote_copy(src_ref=ins[t].at[:, pl.ds((1 - c) * h, h)], dst_ref=outs[t],
                                              send_sem=send.at[t], recv_sem=recv.at[t],
                                              device_id=(x, y, 1 - c), device_id_type=MESH)
            cp.start()
            cps.append(cp)
        for cp in cps:
            cp.wait()

    outs = [jax.ShapeDtypeStruct((a.shape[0], a.shape[1] // 2) + a.shape[2:], a.dtype) for a in grads]
    return _comm_call(body, name, n, outs, n)(*grads)


def scatter_to_chips(sums, name):
    n = len(sums)

    def body(*refs):
        ins, outs = refs[:n], refs[n:2 * n]
        send, recv = refs[2 * n:]
        x, y, c, others = _place()
        me = _chip_id((x, y))
        cps = []
        for t in range(n):
            for j, chip in enumerate(others):
                cp = pltpu.make_async_remote_copy(src_ref=ins[t].at[_chip_id(chip)], dst_ref=outs[t].at[me],
                                                  send_sem=send.at[3 * t + j], recv_sem=recv.at[3 * t + j],
                                                  device_id=(*chip, c), device_id_type=MESH)
                cp.start()
                cps.append(cp)
        for t in range(n):
            for j, chip in enumerate(others):
                got = outs[t].at[_chip_id(chip)]
                pltpu.make_async_remote_copy(src_ref=got, dst_ref=got, send_sem=send.at[3 * t + j], recv_sem=recv.at[3 * t + j],
                                             device_id=(x, y, c), device_id_type=MESH).wait_recv()
        for cp in cps:
            cp.wait_send()

    outs = [jax.ShapeDtypeStruct(a.shape, a.dtype) for a in sums]
    return _comm_call(body, name, n, outs, 3 * n)(*sums)


def join_core_halves(fulls, name):
    n = len(fulls)

    def body(*refs):
        bufs = refs[n:2 * n]
        send, recv = refs[2 * n:]
        x, y, c, _ = _place()
        cps = []
        for t in range(n):
            h = bufs[t].shape[0] // 2
            mine = bufs[t].at[pl.ds(c * h, h)]
            cp = pltpu.make_async_remote_copy(src_ref=mine, dst_ref=mine, send_sem=send.at[t], recv_sem=recv.at[t],
                                              device_id=(x, y, 1 - c), device_id_type=MESH)
            cp.start()
            cps.append(cp)
        for t in range(n):
            h = bufs[t].shape[0] // 2
            theirs = bufs[t].at[pl.ds((1 - c) * h, h)]
            cps[t].wait_send()
            pltpu.make_async_remote_copy(src_ref=theirs, dst_ref=theirs, send_sem=send.at[t], recv_sem=recv.at[t],
                                         device_id=(x, y, c), device_id_type=MESH).wait_recv()

    outs = [jax.ShapeDtypeStruct(a.shape, a.dtype) for a in fulls]
    return _comm_call(body, name, n, outs, n, in_place=True)(*fulls)


def kernel(x, positions, norm_gains, swa_w_in, swa_sinks, swa_w_out, fox_w_in, fox_b_f, fox_w_out, ffn_w_gate_up, ffn_w_down, loss_target, m_norm_gains, m_swa_w_in, m_swa_sinks, m_swa_w_out, m_fox_w_in, m_fox_b_f, m_fox_w_out, m_ffn_w_gate_up, m_ffn_w_down, v_norm_gains, v_swa_w_in, v_swa_sinks, v_swa_w_out, v_fox_w_in, v_fox_b_f, v_fox_w_out, v_ffn_w_gate_up, v_ffn_w_down):
    s, d = x.shape[1], x.shape[2]
    depth = norm_gains.shape[0]
    n_heads = d // HEAD_DIM
    hd = n_heads * HEAD_DIM
    n_kv = (swa_w_in.shape[2] * N_CHIPS // HEAD_DIM - n_heads) // 2
    ff = ffn_w_down.shape[1] * N_CHIPS
    fox_cols = fox_w_in.shape[2]
    fox_pad = 3 * hd + LANES
    assert fox_cols * N_CHIPS == 3 * hd + n_heads and n_heads <= LANES
    t_fox = min(512, s)
    x0 = x[0]
    target = loss_target[0]
    place = ((2 * lax.axis_index("x") + lax.axis_index("y")).astype(jnp.int32).reshape(1),
             lax.axis_index("c").astype(jnp.int32).reshape(1))

    gains_sh = norm_gains.reshape(2, depth * 2, norm_gains.shape[2])
    w_names = ["swa_w_in", "swa_w_out", "fox_w_in", "fox_w_out", "ffn_w_gate_up", "ffn_w_down"]
    w_f32 = [swa_w_in, swa_w_out, fox_w_in, fox_w_out, ffn_w_gate_up, ffn_w_down]
    slots = [cast_into_slot(w, place, BF16, "cast_" + nm) for w, nm in zip(w_f32, w_names)]
    slots.append(cast_into_slot(gains_sh, place, F32, "cast_norm_gains"))
    gathered = all_gather_shards(slots, "gather_weights")
    g_swa_in, g_swa_out, g_fox_in, g_fox_out, g_gu, g_down, g_gains = gathered
    gains = jnp.transpose(g_gains.reshape(N_CHIPS, depth * 4, -1), (1, 0, 2)).reshape(depth * 4, d)

    def gain(layer, which):
        return gains[layer * 4 + which][None, :]

    def fox_weight(layer):
        parts = [g_fox_in[q, layer] for q in range(N_CHIPS)]
        parts.append(jnp.zeros((d, fox_pad - fox_cols * N_CHIPS), BF16))
        return jnp.concatenate(parts, axis=1)

    inv_freq = ROPE_THETA ** (-jnp.arange(0, ROT_DIM, 2, dtype=F32) / ROT_DIM)
    lane_d = jnp.arange(LANES) % HEAD_DIM
    invf_row = jnp.where(lane_d < ROT_DIM, inv_freq[lane_d % (ROT_DIM // 2)], 0.0)[None, :]
    tabs = rope_tables(positions.reshape(s, 1), invf_row, "rope_tables")

    n_sh_in = swa_w_in.shape[2]
    gu_sh = ffn_w_gate_up.shape[2]
    tn_gu = gu_sh // 2 if (gu_sh // 2) % LANES == 0 else gu_sh
    down_sh = ffn_w_down.shape[1]
    out_sh = swa_w_out.shape[1]
    tm = min(1024, s)

    saved = []
    xin = x0
    h = prenorm(xin, gain(0, 0), "prenorm_first")
    for layer in range(depth):
        j = layer // 2
        rec = {"x_in": xin, "h1": h}
        if layer % 2 == 0:
            proj = mm_nn(h, g_swa_in, lambda n, k: (n, j, k, 0), n_sh_in * N_CHIPS, tm=min(512, s), tn=n_sh_in, tk=d,
                         out_dtype=F32, name=f"swa_proj_{j}")
            q, kd, vd = swa_split(proj, tabs, n_heads, n_kv, f"swa_split_{j}")
            sink_row = jnp.repeat(swa_sinks[j], HEAD_DIM)[None, :]
            attn, lse = swa_fwd(q, kd, vd, sink_row, f"swa_fwd_{j}")
            rec.update(q=q, kd=kd, vd=vd, sink_row=sink_row, lse=lse)
            w_out_g = g_swa_out
        else:
            wf = fox_weight(j)
            tn = 3 * hd // 6 if (3 * hd // 6) % LANES == 0 else LANES
            qkv = mm_nn(h, wf, lambda n, k: (k, n), 3 * hd, tm=tm, tn=tn, tk=d, out_dtype=BF16, name=f"fox_proj_{j}")
            f_off = 3 * hd // LANES
            f_logit = mm_nn(h, wf, lambda n, k: (k, f_off + n), LANES, tm=tm, tn=LANES, tk=d, out_dtype=F32, name=f"fox_gate_{j}")
            b_row = jnp.pad(fox_b_f[j], (0, LANES - n_heads))[None, :]
            c = forget_cumsum(f_logit, b_row, f"fox_cumsum_{j}")
            crow = jnp.transpose(c[:, :n_heads]).reshape(n_heads // 2, 2, s // t_fox, t_fox)
            crow = jnp.pad(jnp.transpose(crow, (0, 2, 1, 3)), ((0, 0), (0, 0), (0, 6), (0, 0)))
            attn, lse = fox_fwd(qkv, crow, n_heads, f"fox_fwd_{j}")
            rec.update(wf=wf, qkv=qkv, f_logit=f_logit, b_row=b_row, crow=crow, lse=lse)
            w_out_g = g_fox_out
        y = mm_nn(attn, w_out_g, lambda n, k: (k, j, 0, n), d, tm=tm, tn=min(1024, d), tk=out_sh, out_dtype=F32, name=f"out_proj_{layer}")
        xmid, h2 = postnorm_residual(xin, y, gain(layer, 1), gain(layer, 2), f"postnorm_mixer_{layer}")
        npb = gu_sh // tn_gu
        gu = mm_nn(h2, g_gu, lambda n, k: (n // npb, layer, k, n % npb), 2 * ff, tm=tm, tn=tn_gu, tk=d, out_dtype=BF16, name=f"ffn_up_{layer}")
        act = swiglu_fwd(gu, f"swiglu_{layer}")
        y2 = mm_nn(act, g_down, lambda n, k: (k, layer, 0, n), d, tm=tm, tn=min(1024, d), tk=down_sh, out_dtype=F32, name=f"ffn_down_{layer}")
        rec.update(attn=attn, y=y, x_mid=xmid, h2=h2, gu=gu, act=act, y2=y2)
        saved.append(rec)
        if layer + 1 < depth:
            xin, h = postnorm_residual(xmid, y2, gain(layer, 3), gain(layer + 1, 0), f"postnorm_ffn_{layer}")
    dx, loss_blk = postnorm_loss(xmid, y2, gain(depth - 1, 3), target, "loss")

    def grad_buf(w):
        return jnp.zeros((N_CHIPS,) + w.shape, BF16)

    gb_swa_in, gb_swa_out, gb_fox_out, gb_gu, gb_down = (grad_buf(w) for w in (swa_w_in, swa_w_out, fox_w_out, ffn_w_gate_up, ffn_w_down))
    fox_in_grads = [None] * (depth // 2)
    dgains = [None] * (depth * 4)
    dsinks = [None] * ((depth + 1) // 2)
    dbf = [None] * (depth // 2)
    tko = min(1024, d)
    for layer in reversed(range(depth)):
        j = layer // 2
        rec = saved[layer]
        dy2, dgains[layer * 4 + 3] = postnorm_bwd(dx, rec["y2"], gain(layer, 3), f"postnorm_ffn_bwd_{layer}")
        gb_down = mm_tn(rec["act"], dy2, lambda i, n: (i, layer, 0, n), tka=down_sh, tn=min(1024, d), tm=min(512, s),
                        out_block=(None, None, down_sh, min(1024, d)), name=f"ffn_down_dw_{layer}", into=gb_down)
        dact = mm_nt(dy2, g_down, lambda o, n: (o, layer, 0, n), ff, tm=tm, tko=down_sh, tn=d, out_dtype=BF16, name=f"ffn_down_dx_{layer}")
        dgu = swiglu_bwd(dact, rec["gu"], f"swiglu_bwd_{layer}")
        npb = gu_sh // tn_gu
        gb_gu = mm_tn(rec["h2"], dgu, lambda i, n: (n // npb, layer, i, n % npb), tka=tko, tn=tn_gu, tm=min(512, s),
                      out_block=(None, None, tko, tn_gu), name=f"ffn_up_dw_{layer}", into=gb_gu)
        dh2 = mm_nt(dgu, g_gu, lambda o, n: (n // npb, layer, o, n % npb), d, tm=tm, tko=tko, tn=tn_gu, out_dtype=F32, name=f"ffn_up_dx_{layer}")
        dxm, dgains[layer * 4 + 2] = prenorm_bwd(rec["x_mid"], dh2, dx, gain(layer, 2), f"prenorm_ffn_bwd_{layer}")
        dy, dgains[layer * 4 + 1] = postnorm_bwd(dxm, rec["y"], gain(layer, 1), f"postnorm_mixer_bwd_{layer}")
        w_out_g = g_swa_out if layer % 2 == 0 else g_fox_out
        gb_out = gb_swa_out if layer % 2 == 0 else gb_fox_out
        gb_out = mm_tn(rec["attn"], dy, lambda i, n: (i, j, 0, n), tka=out_sh, tn=min(1024, d), tm=min(512, s),
                       out_block=(None, None, out_sh, min(1024, d)), name=f"out_proj_dw_{layer}", into=gb_out)
        dattn = mm_nt(dy, w_out_g, lambda o, n: (o, j, 0, n), hd, tm=tm, tko=out_sh, tn=d, out_dtype=BF16, name=f"out_proj_dx_{layer}")
        if layer % 2 == 0:
            gb_swa_out = gb_out
            dq, dkd, dvd, dsk = swa_bwd(rec["q"], rec["kd"], rec["vd"], rec["sink_row"], rec["attn"], rec["lse"], dattn, f"swa_bwd_{j}")
            dsinks[j] = dsk[0].reshape(n_heads, HEAD_DIM)[:, 0]
            dproj = swa_merge_bwd(dq, dkd, dvd, tabs, f"swa_merge_bwd_{j}")
            gb_swa_in = mm_tn(rec["h1"], dproj, lambda i, n: (n, j, i, 0), tka=tko, tn=n_sh_in, tm=min(512, s),
                              out_block=(None, None, tko, n_sh_in), name=f"swa_proj_dw_{j}", into=gb_swa_in)
            dh1 = mm_nt(dproj, g_swa_in, lambda o, n: (n, j, o, 0), d, tm=tm, tko=tko, tn=n_sh_in, out_dtype=F32, name=f"swa_proj_dx_{j}")
        else:
            gb_fox_out = gb_out
            delta = head_rowdot(dattn, rec["attn"], f"fox_delta_{j}")
            dq, dk, dv, dcrow, dcq = fox_bwd(rec["qkv"], rec["crow"], rec["lse"], delta, dattn, n_heads, f"fox_bwd_{j}")
            dc = jnp.transpose(dcrow[:, :, :2, :], (0, 2, 1, 3)).reshape(n_heads, s)
            lane_pad = ((0, 0), (0, LANES - n_heads))
            df, db = forget_gate_bwd(jnp.pad(jnp.transpose(dc), lane_pad), jnp.pad(dcq[:, ::HEAD_DIM], lane_pad),
                                     rec["f_logit"], rec["b_row"], f"fox_gate_bwd_{j}")
            dbf[j] = db[0, :n_heads]
            dproj = jnp.concatenate([dq.astype(BF16), dk, dv, df], axis=1)
            tn_f = LANES * max(k for k in range(1, 9) if (fox_pad // LANES) % k == 0)
            dwf = mm_tn(rec["h1"], dproj, lambda i, n: (i, n), tka=tko, tn=tn_f, tm=min(512, s), out_block=(tko, tn_f),
                        name=f"fox_proj_dw_{j}", out_shape=jax.ShapeDtypeStruct((d, fox_pad), BF16))
            fox_in_grads[j] = dwf
            dh1 = mm_nt(dproj, rec["wf"], lambda o, n: (o, n), d, tm=tm, tko=tko, tn=tn_f, out_dtype=F32, name=f"fox_proj_dx_{j}")
        dx, dgains[layer * 4] = prenorm_bwd(rec["x_in"], dh1, dxm, gain(layer, 0), f"prenorm_mixer_bwd_{layer}")
    grad_x = dx[None]

    gb_fox_in = jnp.stack([jnp.stack([g[:, q * fox_cols:(q + 1) * fox_cols] for g in fox_in_grads]) for q in range(N_CHIPS)])
    dgain_full = jnp.concatenate(dgains, axis=0)
    gb_gains = jnp.transpose(dgain_full.reshape(depth * 4, N_CHIPS, -1), (1, 0, 2)).reshape(N_CHIPS, 2, depth * 2, -1)
    partials = [gb_swa_in, gb_swa_out, gb_fox_in, gb_fox_out, gb_gu, gb_down, gb_gains]
    names = w_names + ["norm_gains"]
    from_sibling = swap_core_halves(partials, "reduce_swap_cores")
    chip_sums = [add_core_halves(g, r, place, "reduce_add_cores_" + nm) for g, r, nm in zip(partials, from_sibling, names)]
    from_chips = scatter_to_chips(chip_sums, "reduce_scatter_chips")
    my_half = [add_chips(o, p, place, "reduce_add_chips_" + nm) for o, p, nm in zip(chip_sums, from_chips, names)]
    full = join_core_halves(my_half, "reduce_join_cores")
    g_swa_in_f, g_swa_out_f, g_fox_in_f, g_fox_out_f, g_gu_f, g_down_f, g_gains_f = full
    g_gains_f = g_gains_f.reshape(norm_gains.shape)

    n_swa, n_fox = len(dsinks), len(dbf)
    small = jnp.concatenate([loss_blk[0, :1]] + dsinks + dbf)
    small = lax.psum(small, ("x", "y", "c"))
    loss = small[0]
    g_sinks = small[1:1 + n_swa * n_heads].reshape(n_swa, n_heads)
    g_bf = small[1 + n_swa * n_heads:].reshape(n_fox, n_heads)

    def pad_small(a):
        return jnp.pad(a, ((0, 8 - a.shape[0]), (0, LANES - a.shape[1])))[None]

    def update(w, g, m, v, nm):
        if w.ndim == 2:
            dl, mn, vn = adamw(pad_small(w), pad_small(g), pad_small(m), pad_small(v), "adamw_" + nm)
            return tuple(a[0, :w.shape[0], :w.shape[1]] for a in (dl, mn, vn))
        return adamw(w, g, m, v, "adamw_" + nm)

    grads = [g_gains_f, g_swa_in_f, g_sinks, g_swa_out_f, g_fox_in_f, g_bf, g_fox_out_f, g_gu_f, g_down_f]
    ws = [norm_gains, swa_w_in, swa_sinks, swa_w_out, fox_w_in, fox_b_f, fox_w_out, ffn_w_gate_up, ffn_w_down]
    ms = [m_norm_gains, m_swa_w_in, m_swa_sinks, m_swa_w_out, m_fox_w_in, m_fox_b_f, m_fox_w_out, m_ffn_w_gate_up, m_ffn_w_down]
    vs = [v_norm_gains, v_swa_w_in, v_swa_sinks, v_swa_w_out, v_fox_w_in, v_fox_b_f, v_fox_w_out, v_ffn_w_gate_up, v_ffn_w_down]
    nms = ["norm_gains", "swa_w_in", "swa_sinks", "swa_w_out", "fox_w_in", "fox_b_f", "fox_w_out", "ffn_w_gate_up", "ffn_w_down"]
    upd = [update(w, g, m, v, nm) for w, g, m, v, nm in zip(ws, grads, ms, vs, nms)]
    return (loss, grad_x, *grads, *[u[0] for u in upd], *[u[1] for u in upd], *[u[2] for u in upd])
```

```python
import jax
import jax.numpy as jnp
from jax import lax
from jax.experimental import pallas as pl
from jax.experimental.pallas import tpu as pltpu

F32 = jnp.float32
BF16 = jnp.bfloat16
HEAD_DIM = 64
LANES = 128
WINDOW = 128
ROPE_THETA = 500000.0
ROT_DIM = HEAD_DIM // 4
RMS_EPS = 1e-6
ADAM_LR, ADAM_B1, ADAM_B2, ADAM_EPS, ADAM_WD, ADAM_STEP = 0.001, 0.9, 0.999, 1e-08, 0.01, 10
NEG = -1e30
VMEM_BIG = 56 * 1024 * 1024
N_CHIPS = 4
MESH = pl.DeviceIdType.MESH
ANY = pl.BlockSpec(memory_space=pl.ANY)


def _pcall(body, **kw):
    return pl.pallas_call(body, **kw)


def _cp(sem=None, vmem=None):
    return pltpu.CompilerParams(dimension_semantics=sem, vmem_limit_bytes=vmem)


def _tile(n, pref):
    if n <= pref:
        return n
    t = pref - pref % 8
    while n % t:
        t -= 8
    return t


_DIMS = {"nn": (((1,), (0,)), ((), ())), "nt": (((1,), (1,)), ((), ())), "tn": (((0,), (0,)), ((), ()))}


def _matmul(kind, a, b, *, grid, a_spec, b_spec, o_spec, out_shape, acc_shape, name, into=None):
    nk = grid[2]
    dims = _DIMS[kind]

    def body(a_ref, b_ref, *rest):
        rest = rest[1:] if into is not None else rest
        o_ref = rest[0]

        def prod():
            return lax.dot_general(a_ref[...], b_ref[...], dims, preferred_element_type=F32)

        if nk == 1:
            o_ref[...] = prod().astype(o_ref.dtype)
        else:
            acc = rest[1]
            k = pl.program_id(2)

            @pl.when(k == 0)
            def _():
                acc[...] = prod()

            @pl.when(k > 0)
            def _():
                acc[...] += prod()

            @pl.when(k == nk - 1)
            def _():
                o_ref[...] = acc[...].astype(o_ref.dtype)

    args, in_specs, aliases = [a, b], [a_spec, b_spec], {}
    if into is not None:
        args.append(into)
        in_specs.append(ANY)
        aliases = {2: 0}
        out_shape = jax.ShapeDtypeStruct(into.shape, into.dtype)
    return _pcall(body, name=name, grid=grid, in_specs=in_specs, out_specs=o_spec, out_shape=out_shape,
                  scratch_shapes=[] if nk == 1 else [pltpu.VMEM(acc_shape, F32)], input_output_aliases=aliases,
                  compiler_params=_cp(("parallel", "parallel", "arbitrary"), VMEM_BIG))(*args)


def mm_nn(a, w, w_map, n_out, *, tm, tn, tk, out_dtype, name):
    m, kdim = a.shape
    tm = min(tm, m)
    lead = (None,) * (w.ndim - 2)
    return _matmul("nn", a, w, grid=(m // tm, n_out // tn, kdim // tk),
                   a_spec=pl.BlockSpec((tm, tk), lambda i, j, k: (i, k)),
                   b_spec=pl.BlockSpec(lead + (tk, tn), lambda i, j, k: w_map(j, k)),
                   o_spec=pl.BlockSpec((tm, tn), lambda i, j, k: (i, j)),
                   out_shape=jax.ShapeDtypeStruct((m, n_out), out_dtype), acc_shape=(tm, tn), name=name)


def mm_nt(a, w, w_map, k_out, *, tm, tko, tn, out_dtype, name):
    m, ndim = a.shape
    tm = min(tm, m)
    lead = (None,) * (w.ndim - 2)
    return _matmul("nt", a, w, grid=(m // tm, k_out // tko, ndim // tn),
                   a_spec=pl.BlockSpec((tm, tn), lambda i, j, n: (i, n)),
                   b_spec=pl.BlockSpec(lead + (tko, tn), lambda i, j, n: w_map(j, n)),
                   o_spec=pl.BlockSpec((tm, tko), lambda i, j, n: (i, j)),
                   out_shape=jax.ShapeDtypeStruct((m, k_out), out_dtype), acc_shape=(tm, tko), name=name)


def mm_tn(a, b, o_map, *, tka, tn, tm, out_block, name, into=None, out_shape=None):
    m, kdim = a.shape
    n = b.shape[1]
    tm = min(tm, m)
    return _matmul("tn", a, b, grid=(kdim // tka, n // tn, m // tm),
                   a_spec=pl.BlockSpec((tm, tka), lambda i, j, mm: (mm, i)),
                   b_spec=pl.BlockSpec((tm, tn), lambda i, j, mm: (mm, j)),
                   o_spec=pl.BlockSpec(out_block, lambda i, j, mm: o_map(i, j)),
                   out_shape=out_shape, acc_shape=(tka, tn), name=name, into=into)


def _rstd(v):
    return lax.rsqrt(jnp.mean(v * v, axis=-1, keepdims=True) + RMS_EPS)


def _row_spec(tr, d):
    return pl.BlockSpec((tr, d), lambda i: (i, 0))


def _vec_spec(d):
    return pl.BlockSpec((1, d), lambda i: (0, 0))


def prenorm(x, g, name):
    s, d = x.shape
    tr = _tile(s, 256)

    def body(x_ref, g_ref, h_ref):
        v = x_ref[...]
        h_ref[...] = (v * _rstd(v) * g_ref[...]).astype(BF16)

    return _pcall(body, name=name, grid=(s // tr,), in_specs=[_row_spec(tr, d), _vec_spec(d)],
                  out_specs=_row_spec(tr, d), out_shape=jax.ShapeDtypeStruct((s, d), BF16),
                  compiler_params=_cp(("parallel",)))(x, g)


def postnorm_residual(x, y, g_post, g_next, name):
    s, d = x.shape
    tr = _tile(s, 256)

    def body(x_ref, y_ref, gp_ref, gn_ref, xo_ref, h_ref):
        v = y_ref[...]
        xn = x_ref[...] + v * _rstd(v) * gp_ref[...]
        xo_ref[...] = xn
        h_ref[...] = (xn * _rstd(xn) * gn_ref[...]).astype(BF16)

    return _pcall(body, name=name, grid=(s // tr,),
                  in_specs=[_row_spec(tr, d), _row_spec(tr, d), _vec_spec(d), _vec_spec(d)],
                  out_specs=[_row_spec(tr, d), _row_spec(tr, d)],
                  out_shape=[jax.ShapeDtypeStruct((s, d), F32), jax.ShapeDtypeStruct((s, d), BF16)],
                  compiler_params=_cp(("parallel",)))(x, y, g_post, g_next)


def postnorm_loss(x, y, g_post, target, name):
    s, d = x.shape
    tr = _tile(s, 256)

    def body(x_ref, y_ref, gp_ref, t_ref, dx_ref, loss_ref):
        v = y_ref[...]
        err = x_ref[...] + v * _rstd(v) * gp_ref[...] - t_ref[...]
        dx_ref[...] = err / d
        part = 0.5 * jnp.sum(jnp.mean(err * err, axis=-1, keepdims=True), axis=0, keepdims=True)

        @pl.when(pl.program_id(0) == 0)
        def _():
            loss_ref[...] = jnp.zeros_like(loss_ref)

        loss_ref[...] += part

    return _pcall(body, name=name, grid=(s // tr,),
                  in_specs=[_row_spec(tr, d), _row_spec(tr, d), _vec_spec(d), _row_spec(tr, d)],
                  out_specs=[_row_spec(tr, d), pl.BlockSpec((8, LANES), lambda i: (0, 0))],
                  out_shape=[jax.ShapeDtypeStruct((s, d), F32), jax.ShapeDtypeStruct((8, LANES), F32)],
                  compiler_params=_cp(("arbitrary",)))(x, y, g_post, target)


def _norm_bwd(v, g, dz):
    r = _rstd(v)
    vhat = v * r
    u = dz * g
    dv = r * (u - vhat * jnp.mean(u * vhat, axis=-1, keepdims=True))
    return dv, jnp.sum(dz * vhat, axis=0, keepdims=True)


def _acc_rows(ref, val):
    @pl.when(pl.program_id(0) == 0)
    def _():
        ref[...] = jnp.zeros_like(ref)

    ref[...] += val


def postnorm_bwd(dz, y, g, name):
    s, d = y.shape
    tr = _tile(s, 256)

    def body(dz_ref, y_ref, g_ref, dy_ref, dg_ref):
        dv, dg = _norm_bwd(y_ref[...], g_ref[...], dz_ref[...])
        dy_ref[...] = dv.astype(BF16)
        _acc_rows(dg_ref, dg)

    return _pcall(body, name=name, grid=(s // tr,), in_specs=[_row_spec(tr, d), _row_spec(tr, d), _vec_spec(d)],
                  out_specs=[_row_spec(tr, d), _vec_spec(d)],
                  out_shape=[jax.ShapeDtypeStruct((s, d), BF16), jax.ShapeDtypeStruct((1, d), F32)],
                  compiler_params=_cp(("arbitrary",)))(dz, y, g)


def prenorm_bwd(x, dh, dskip, g, name):
    s, d = x.shape
    tr = _tile(s, 256)

    def body(x_ref, dh_ref, ds_ref, g_ref, dx_ref, dg_ref):
        dv, dg = _norm_bwd(x_ref[...], g_ref[...], dh_ref[...])
        dx_ref[...] = ds_ref[...] + dv
        _acc_rows(dg_ref, dg)

    return _pcall(body, name=name, grid=(s // tr,),
                  in_specs=[_row_spec(tr, d), _row_spec(tr, d), _row_spec(tr, d), _vec_spec(d)],
                  out_specs=[_row_spec(tr, d), _vec_spec(d)],
                  out_shape=[jax.ShapeDtypeStruct((s, d), F32), jax.ShapeDtypeStruct((1, d), F32)],
                  compiler_params=_cp(("arbitrary",)))(x, dh, dskip, g)


def swiglu_fwd(gu, name):
    s, two_ff = gu.shape
    ff = two_ff // 2
    tr = _tile(s, 128)

    def body(gu_ref, act_ref):
        gate = gu_ref[:, :ff].astype(F32)
        up = gu_ref[:, ff:].astype(F32)
        act_ref[...] = (gate * jax.nn.sigmoid(gate) * up).astype(BF16)

    return _pcall(body, name=name, grid=(s // tr,), in_specs=[_row_spec(tr, two_ff)], out_specs=_row_spec(tr, ff),
                  out_shape=jax.ShapeDtypeStruct((s, ff), BF16), compiler_params=_cp(("parallel",)))(gu)


def swiglu_bwd(dact, gu, name):
    s, two_ff = gu.shape
    ff = two_ff // 2
    tr = _tile(s, 128)

    def body(da_ref, gu_ref, dgu_ref):
        gate = gu_ref[:, :ff].astype(F32)
        up = gu_ref[:, ff:].astype(F32)
        da = da_ref[...].astype(F32)
        sig = jax.nn.sigmoid(gate)
        dgu_ref[:, :ff] = (da * up * sig * (1.0 + gate * (1.0 - sig))).astype(BF16)
        dgu_ref[:, ff:] = (da * gate * sig).astype(BF16)

    return _pcall(body, name=name, grid=(s // tr,), in_specs=[_row_spec(tr, ff), _row_spec(tr, two_ff)],
                  out_specs=_row_spec(tr, two_ff), out_shape=jax.ShapeDtypeStruct((s, two_ff), BF16),
                  compiler_params=_cp(("parallel",)))(dact, gu)


def rope_tables(pos_col, invf_row, name):
    s = pos_col.shape[0]
    tr = _tile(s, 1024)

    def body(p_ref, f_ref, c_ref, sa_ref, sb_ref):
        ang = p_ref[...].astype(F32) * f_ref[...]
        d = lax.broadcasted_iota(jnp.int32, (1, LANES), 1) % HEAD_DIM
        cs, sn = jnp.cos(ang), jnp.sin(ang)
        c_ref[...] = jnp.where(d < ROT_DIM, cs, 1.0)
        sa_ref[...] = jnp.where(d < ROT_DIM // 2, -sn, 0.0)
        sb_ref[...] = jnp.where((d >= ROT_DIM // 2) & (d < ROT_DIM), sn, 0.0)

    tab = jax.ShapeDtypeStruct((s, LANES), F32)
    return _pcall(body, name=name, grid=(s // tr,),
                  in_specs=[pl.BlockSpec((tr, 1), lambda i: (i, 0)), _vec_spec(LANES)],
                  out_specs=[_row_spec(tr, LANES)] * 3, out_shape=[tab] * 3, compiler_params=_cp(("parallel",)))(pos_col, invf_row)


def _rot(t, c, sa, sb):
    half = ROT_DIM // 2
    return t * c + pltpu.roll(t, LANES - half, 1) * sa + pltpu.roll(t, half, 1) * sb


def _rot_t(t, c, sa, sb):
    half = ROT_DIM // 2
    return t * c + pltpu.roll(t * sa, half, 1) + pltpu.roll(t * sb, LANES - half, 1)


def swa_split(proj, tabs, n_heads, n_kv, name):
    s, width = proj.shape
    qd, kd = n_heads * HEAD_DIM, n_kv * HEAD_DIM
    tr = _tile(s, 256)

    def body(p_ref, c_ref, sa_ref, sb_ref, q_ref, k_ref, v_ref):
        c, sa, sb = c_ref[...], sa_ref[...], sb_ref[...]
        low = lax.broadcasted_iota(jnp.int32, (1, LANES), 1) < HEAD_DIM
        for g in range(qd // LANES):
            q_ref[:, g * LANES:(g + 1) * LANES] = _rot(p_ref[:, g * LANES:(g + 1) * LANES], c, sa, sb).astype(BF16)
        for g in range(kd // LANES):
            for src, dst, rot in ((qd, k_ref, True), (qd + kd, v_ref, False)):
                t = p_ref[:, src + g * LANES:src + (g + 1) * LANES]
                t = _rot(t, c, sa, sb) if rot else t
                sw = pltpu.roll(t, HEAD_DIM, 1)
                dst[:, (2 * g) * LANES:(2 * g + 1) * LANES] = jnp.where(low, t, sw).astype(BF16)
                dst[:, (2 * g + 1) * LANES:(2 * g + 2) * LANES] = jnp.where(low, sw, t).astype(BF16)

    return _pcall(body, name=name, grid=(s // tr,),
                  in_specs=[_row_spec(tr, width)] + [_row_spec(tr, LANES)] * 3,
                  out_specs=[_row_spec(tr, qd), _row_spec(tr, 2 * kd), _row_spec(tr, 2 * kd)],
                  out_shape=[jax.ShapeDtypeStruct((s, qd), BF16), jax.ShapeDtypeStruct((s, 2 * kd), BF16),
                             jax.ShapeDtypeStruct((s, 2 * kd), BF16)],
                  compiler_params=_cp(("parallel",)))(proj, *tabs)


def swa_merge_bwd(dq, dkd, dvd, tabs, name):
    s, qd = dq.shape
    kd = dkd.shape[1] // 2
    width = qd + 2 * kd
    tr = _tile(s, 256)

    def body(dq_ref, dk_ref, dv_ref, c_ref, sa_ref, sb_ref, o_ref):
        c, sa, sb = c_ref[...], sa_ref[...], sb_ref[...]
        low = lax.broadcasted_iota(jnp.int32, (1, LANES), 1) < HEAD_DIM
        for g in range(qd // LANES):
            t = dq_ref[:, g * LANES:(g + 1) * LANES].astype(F32)
            o_ref[:, g * LANES:(g + 1) * LANES] = _rot_t(t, c, sa, sb).astype(BF16)
        for g in range(kd // LANES):
            for dst, src, rot in ((qd, dk_ref, True), (qd + kd, dv_ref, False)):
                e = src[:, (2 * g) * LANES:(2 * g + 1) * LANES]
                o = src[:, (2 * g + 1) * LANES:(2 * g + 2) * LANES]
                t = jnp.where(low, e + pltpu.roll(e, HEAD_DIM, 1), o + pltpu.roll(o, HEAD_DIM, 1))
                t = _rot_t(t, c, sa, sb) if rot else t
                o_ref[:, dst + g * LANES:dst + (g + 1) * LANES] = t.astype(BF16)

    return _pcall(body, name=name, grid=(s // tr,),
                  in_specs=[_row_spec(tr, qd), _row_spec(tr, 2 * kd), _row_spec(tr, 2 * kd)] + [_row_spec(tr, LANES)] * 3,
                  out_specs=_row_spec(tr, width), out_shape=jax.ShapeDtypeStruct((s, width), BF16),
                  compiler_params=_cp(("parallel",)))(dq, dkd, dvd, *tabs)


def _halves():
    lane_half = lax.broadcasted_iota(jnp.int32, (1, LANES), 1) // HEAD_DIM
    return [lane_half == 0, lane_half == 1]


def _nt(a, b):
    return lax.dot_general(a, b, _DIMS["nt"], preferred_element_type=F32)


def _tn(a, b):
    return lax.dot_general(a, b, _DIMS["tn"], preferred_element_type=F32)


def _band_mask(i, tq):
    w = tq + WINDOW
    rel = lax.broadcasted_iota(jnp.int32, (tq, w), 1) - lax.broadcasted_iota(jnp.int32, (tq, w), 0)
    first = lax.broadcasted_iota(jnp.int32, (tq, w), 1) >= jnp.where(i > 0, 0, WINDOW)
    return (rel >= 1) & (rel <= WINDOW) & first


def swa_fwd(q, kd, vd, sink_row, name):
    s, qd = q.shape
    tq = min(256, s)
    r = tq // WINDOW
    pairs = qd // LANES
    group_pairs = pairs // (kd.shape[1] // LANES)
    scale = HEAD_DIM ** -0.5

    def body(q_ref, kp_ref, kc_ref, vp_ref, vc_ref, sk_ref, o_ref, lse_ref):
        i = pl.program_id(1)
        k = jnp.concatenate([kp_ref[...], kc_ref[...]], axis=0)
        v = jnp.concatenate([vp_ref[...], vc_ref[...]], axis=0)
        mask = _band_mask(i, tq)
        q2 = q_ref[...]
        outs, lses = [], []
        for a, hm in enumerate(_halves()):
            sc = _nt(jnp.where(hm, q2, jnp.zeros_like(q2)), k) * scale
            sc = jnp.where(mask, sc, NEG)
            sink = sk_ref[:, a * HEAD_DIM:a * HEAD_DIM + 1]
            m = jnp.maximum(jnp.max(sc, axis=-1, keepdims=True), sink)
            p = jnp.exp(sc - m)
            den = jnp.sum(p, axis=-1, keepdims=True) + jnp.exp(sink - m)
            outs.append(jnp.dot(p.astype(BF16), v, preferred_element_type=F32) / den)
            lses.append(m + jnp.log(den))
        hm0 = _halves()[0]
        o_ref[...] = jnp.where(hm0, outs[0], outs[1]).astype(BF16)
        lse_ref[...] = jnp.where(hm0, lses[0], lses[1])

    prev = lambda p, i: (jnp.maximum(i * r - 1, 0), p // group_pairs)
    cur = lambda p, i: (i, p // group_pairs)
    blk = pl.BlockSpec((tq, LANES), lambda p, i: (i, p))
    return _pcall(body, name=name, grid=(pairs, s // tq),
                  in_specs=[blk, pl.BlockSpec((WINDOW, LANES), prev), pl.BlockSpec((tq, LANES), cur),
                            pl.BlockSpec((WINDOW, LANES), prev), pl.BlockSpec((tq, LANES), cur),
                            pl.BlockSpec((1, LANES), lambda p, i: (0, p))],
                  out_specs=[blk, blk],
                  out_shape=[jax.ShapeDtypeStruct((s, qd), BF16), jax.ShapeDtypeStruct((s, qd), F32)],
                  compiler_params=_cp(("parallel", "parallel")))(q, kd, kd, vd, vd, sink_row)


def swa_bwd(q, kd, vd, sink_row, out, lse, dout, name):
    s, qd = q.shape
    tq = min(256, s)
    r = tq // WINDOW
    n_kv = kd.shape[1] // LANES
    gw = qd // n_kv
    scale = HEAD_DIM ** -0.5

    def body(q_ref, kp_ref, kc_ref, vp_ref, vc_ref, sk_ref, o_ref, lse_ref, do_ref, dq_ref, dk_ref, dv_ref, dsk_ref):
        i = pl.program_id(1)

        @pl.when(i == 0)
        def _():
            dk_ref[...] = jnp.zeros_like(dk_ref)
            dv_ref[...] = jnp.zeros_like(dv_ref)
            dsk_ref[...] = jnp.zeros_like(dsk_ref)

        k = jnp.concatenate([kp_ref[...], kc_ref[...]], axis=0)
        v = jnp.concatenate([vp_ref[...], vc_ref[...]], axis=0)
        mask = _band_mask(i, tq)
        dk = jnp.zeros((tq + WINDOW, LANES), F32)
        dv = jnp.zeros((tq + WINDOW, LANES), F32)
        for pp in range(gw // LANES):
            cols = slice(pp * LANES, (pp + 1) * LANES)
            q2, do2 = q_ref[:, cols], do_ref[:, cols]
            prod = do2.astype(F32) * o_ref[:, cols].astype(F32)
            dq2 = jnp.zeros((tq, LANES), F32)
            dsk = jnp.zeros((1, LANES), F32)
            for a, hm in enumerate(_halves()):
                qa = jnp.where(hm, q2, jnp.zeros_like(q2))
                doa = jnp.where(hm, do2, jnp.zeros_like(do2))
                lse_a = lse_ref[:, pp * LANES + a * HEAD_DIM:pp * LANES + a * HEAD_DIM + 1]
                sc = jnp.where(mask, _nt(qa, k) * scale, NEG)
                p = jnp.exp(sc - lse_a)
                delta = jnp.sum(jnp.where(hm, prod, 0.0), axis=-1, keepdims=True)
                ds = (p * (_nt(doa, v) - delta) * scale).astype(BF16)
                dv = dv + _tn(p.astype(BF16), doa)
                dk = dk + _tn(ds, qa)
                dq2 = dq2 + jnp.where(hm, jnp.dot(ds, k, preferred_element_type=F32), 0.0)
                sink = sk_ref[:, pp * LANES + a * HEAD_DIM:pp * LANES + a * HEAD_DIM + 1]
                dsink = -jnp.sum(jnp.exp(sink - lse_a) * delta, axis=0, keepdims=True)
                dsk = dsk + jnp.where(hm, dsink, 0.0)
            dq_ref[:, cols] = dq2.astype(BF16)
            dsk_ref[0:1, cols] += dsk
        start = pl.multiple_of(i * tq, tq)
        dk_ref[pl.ds(start, tq), :] += dk[WINDOW:, :]
        dv_ref[pl.ds(start, tq), :] += dv[WINDOW:, :]

        @pl.when(i > 0)
        def _():
            before = pl.multiple_of(i * tq - WINDOW, WINDOW)
            dk_ref[pl.ds(before, WINDOW), :] += dk[:WINDOW, :]
            dv_ref[pl.ds(before, WINDOW), :] += dv[:WINDOW, :]

    prev = lambda g, i: (jnp.maximum(i * r - 1, 0), g)
    cur = lambda g, i: (i, g)
    wide = pl.BlockSpec((tq, gw), cur)
    full = pl.BlockSpec((s, LANES), lambda g, i: (0, g))
    return _pcall(body, name=name, grid=(n_kv, s // tq),
                  in_specs=[wide, pl.BlockSpec((WINDOW, LANES), prev), pl.BlockSpec((tq, LANES), cur),
                            pl.BlockSpec((WINDOW, LANES), prev), pl.BlockSpec((tq, LANES), cur),
                            pl.BlockSpec((1, gw), lambda g, i: (0, g)), wide, wide, wide],
                  out_specs=[wide, full, full, pl.BlockSpec((8, gw), lambda g, i: (0, g))],
                  out_shape=[jax.ShapeDtypeStruct((s, qd), BF16), jax.ShapeDtypeStruct(kd.shape, F32),
                             jax.ShapeDtypeStruct(kd.shape, F32), jax.ShapeDtypeStruct((8, qd), F32)],
                  compiler_params=_cp(("parallel", "arbitrary"), VMEM_BIG))(q, kd, kd, vd, vd, sink_row, out, lse, dout)


def forget_cumsum(f_logit, b_row, name):
    s = f_logit.shape[0]

    def body(f_ref, b_ref, c_ref):
        z = f_ref[...] + b_ref[...]
        acc = jnp.minimum(z, 0.0) - jnp.log(1.0 + jnp.exp(-jnp.abs(z)))
        row = lax.broadcasted_iota(jnp.int32, (s, LANES), 0)
        d = 1
        while d < s:
            acc = acc + jnp.where(row >= d, pltpu.roll(acc, d, 0), 0.0)
            d *= 2
        c_ref[...] = acc

    return _pcall(body, name=name, in_specs=[pl.BlockSpec((s, LANES), lambda: (0, 0)), pl.BlockSpec((1, LANES), lambda: (0, 0))],
                  out_specs=pl.BlockSpec((s, LANES), lambda: (0, 0)), out_shape=jax.ShapeDtypeStruct((s, LANES), F32),
                  compiler_params=_cp(None, VMEM_BIG))(f_logit, b_row)


def forget_gate_bwd(dc, f_logit, b_row, name):
    s = dc.shape[0]

    def body(dc_ref, f_ref, b_ref, df_ref, db_ref):
        acc = dc_ref[...]
        row = lax.broadcasted_iota(jnp.int32, (s, LANES), 0)
        d = 1
        while d < s:
            acc = acc + jnp.where(row < s - d, pltpu.roll(acc, s - d, 0), 0.0)
            d *= 2
        df = acc * jax.nn.sigmoid(-(f_ref[...] + b_ref[...]))
        df_ref[...] = df.astype(BF16)
        db_ref[...] = jnp.sum(df, axis=0, keepdims=True)

    whole = pl.BlockSpec((s, LANES), lambda: (0, 0))
    vec = pl.BlockSpec((1, LANES), lambda: (0, 0))
    return _pcall(body, name=name, in_specs=[whole, whole, vec], out_specs=[whole, vec],
                  out_shape=[jax.ShapeDtypeStruct((s, LANES), BF16), jax.ShapeDtypeStruct((1, LANES), F32)],
                  compiler_params=_cp(None, VMEM_BIG))(dc, f_logit, b_row)


EXTRA = HEAD_DIM
N_PIECES = 3


def _pieces(v):
    hi = v.astype(BF16).astype(F32)
    mid = (v - hi).astype(BF16).astype(F32)
    return hi, mid, (v - hi - mid).astype(BF16).astype(F32)


def _slot(main, lane, extras=None, ones_at=None):
    out = jnp.where(lane < HEAD_DIM, main, 0.0)
    if extras is not None:
        for r, e in enumerate(extras):
            out = jnp.where(lane == EXTRA + r, e, out)
    if ones_at is not None:
        out = jnp.where((lane >= ones_at) & (lane < ones_at + N_PIECES), 1.0, out)
    return out


def _lane_iota():
    return lax.broadcasted_iota(jnp.int32, (1, LANES), 1)


def fox_split(qkv, c, n_heads, name):
    s = qkv.shape[0]
    hd = n_heads * HEAD_DIM
    tr = _tile(s, 256)
    scale = HEAD_DIM ** -0.5

    def body(x_ref, c_ref, q_ref, k_ref, v_ref):
        lane = _lane_iota()
        cv = c_ref[...]
        for g in range(hd // LANES):
            for part, dst in enumerate((q_ref, k_ref, v_ref)):
                t = x_ref[:, part * hd + g * LANES:part * hd + (g + 1) * LANES].astype(F32)
                for a, main in enumerate((t, pltpu.roll(t, HEAD_DIM, 1))):
                    h = 2 * g + a
                    if part == 0:
                        val = _slot(main * scale, lane, ones_at=EXTRA)
                    elif part == 1:
                        ch = jnp.sum(jnp.where(lane == h, cv, 0.0), axis=1, keepdims=True)
                        val = _slot(main, lane, extras=_pieces(-ch), ones_at=EXTRA + N_PIECES)
                    else:
                        val = _slot(main, lane, ones_at=EXTRA)
                    dst[:, h * LANES:(h + 1) * LANES] = val.astype(BF16)

    slots = jax.ShapeDtypeStruct((s, n_heads * LANES), BF16)
    return _pcall(body, name=name, grid=(s // tr,), in_specs=[_row_spec(tr, 3 * hd), _row_spec(tr, LANES)],
                  out_specs=[_row_spec(tr, n_heads * LANES)] * 3, out_shape=[slots] * 3,
                  compiler_params=_cp(("parallel",), VMEM_BIG))(qkv, c)


def _causal_keep(t):
    return lax.broadcasted_iota(jnp.int32, (t, t), 1) <= lax.broadcasted_iota(jnp.int32, (t, t), 0)


def fox_fwd(qs, ks, vs, n_heads, name):
    s = qs.shape[0]
    pairs = n_heads // 2
    t = min(512, s)
    wide = 2 * LANES

    def body(q_ref, k_ref, v_ref, o_ref, qb_ref, acc_ref, m_ref):
        i = pl.program_id(1)
        lane = _lane_iota()
        acc_ref[...] = jnp.zeros_like(acc_ref)
        m_ref[...] = jnp.full_like(m_ref, NEG)

        def step(j, diagonal):
            rows = pl.ds(pl.multiple_of(j * t, t), t)
            for a in range(2):
                cols = slice(a * LANES, (a + 1) * LANES)
                sc = _nt(q_ref[:, cols], k_ref[rows, cols])
                if diagonal:
                    sc = jnp.where(_causal_keep(t), sc, NEG)
                m_old = m_ref[a]
                m_new = jnp.maximum(m_old, jnp.max(sc, axis=-1, keepdims=True))
                p = jnp.exp(sc - jnp.tile(m_new, (1, t // LANES)))
                acc_ref[a] = jnp.exp(m_old - m_new) * acc_ref[a] + jnp.dot(p.astype(BF16), v_ref[rows, cols],
                                                                              preferred_element_type=F32)
                m_ref[a] = m_new

        def loop_body(j, carry):
            step(j, False)
            return carry

        lax.fori_loop(0, i, loop_body, 0)
        step(i, True)
        outs = []
        for a in range(2):
            cols = slice(a * LANES, (a + 1) * LANES)
            acc = acc_ref[a]
            norm = acc[:, EXTRA:EXTRA + 1]
            outs.append(acc / norm)
            neg_lse = _pieces(-(m_ref[a] + jnp.log(norm)))
            qb = q_ref[:, cols].astype(F32)
            for r in range(N_PIECES):
                qb = jnp.where(lane == EXTRA + N_PIECES + r, neg_lse[r], qb)
            qb_ref[:, cols] = qb.astype(BF16)
        o_ref[...] = jnp.where(lane < HEAD_DIM, outs[0], pltpu.roll(outs[1], HEAD_DIM, 1)).astype(BF16)

    qblk = pl.BlockSpec((t, wide), lambda p, i: (i, p))
    whole = pl.BlockSpec((s, wide), lambda p, i: (0, p))
    return _pcall(body, name=name, grid=(pairs, s // t), in_specs=[qblk, whole, whole],
                  out_specs=[pl.BlockSpec((t, LANES), lambda p, i: (i, p)), qblk],
                  out_shape=[jax.ShapeDtypeStruct((s, n_heads * HEAD_DIM), BF16), jax.ShapeDtypeStruct(qs.shape, BF16)],
                  scratch_shapes=[pltpu.VMEM((2, t, LANES), F32)] * 2,
                  compiler_params=_cp(("parallel", "arbitrary"), VMEM_BIG))(qs, ks, vs)


def fox_dout_slots(dout, out, name):
    s, hd = dout.shape
    tr = _tile(s, 256)

    def body(d_ref, o_ref, s_ref):
        lane = _lane_iota()
        for g in range(hd // LANES):
            cols = slice(g * LANES, (g + 1) * LANES)
            d2 = d_ref[:, cols].astype(F32)
            prod = d2 * o_ref[:, cols].astype(F32)
            for a, main in enumerate((d2, pltpu.roll(d2, HEAD_DIM, 1))):
                delta = jnp.sum(jnp.where((lane // HEAD_DIM) == a, prod, 0.0), axis=1, keepdims=True)
                h = 2 * g + a
                s_ref[:, h * LANES:(h + 1) * LANES] = _slot(main, lane, extras=_pieces(-delta)).astype(BF16)

    return _pcall(body, name=name, grid=(s // tr,), in_specs=[_row_spec(tr, hd), _row_spec(tr, hd)],
                  out_specs=_row_spec(tr, 2 * hd), out_shape=jax.ShapeDtypeStruct((s, 2 * hd), BF16),
                  compiler_params=_cp(("parallel",)))(dout, out)


def fox_bwd(qb, ks, vs, dos, n_heads, name):
    s = qb.shape[0]
    pairs = n_heads // 2
    t = min(512, s)
    nblk = s // t
    wide = 2 * LANES

    def body(q_ref, k_ref, v_ref, do_ref, dq_ref, dk_ref, dv_ref, dka_ref, dva_ref):
        j = pl.program_id(1)

        @pl.when(j == 0)
        def _():
            dq_ref[...] = jnp.zeros_like(dq_ref)

        dka_ref[...] = jnp.zeros_like(dka_ref)
        dva_ref[...] = jnp.zeros_like(dva_ref)

        def step(i, diagonal):
            rows = pl.ds(pl.multiple_of(i * t, t), t)
            for a in range(2):
                cols = slice(a * LANES, (a + 1) * LANES)
                qa, doa, ka = q_ref[rows, cols], do_ref[rows, cols], k_ref[:, cols]
                sc = _nt(qa, ka)
                if diagonal:
                    sc = jnp.where(_causal_keep(t), sc, NEG)
                p = jnp.exp(sc)
                ds = (p * _nt(doa, v_ref[:, cols])).astype(BF16)
                dva_ref[a] += _tn(p.astype(BF16), doa)
                dka_ref[a] += _tn(ds, qa)
                dq_ref[rows, cols] += jnp.dot(ds, ka, preferred_element_type=F32)

        step(j, True)

        def loop_body(i, carry):
            step(i, False)
            return carry

        lax.fori_loop(j + 1, nblk, loop_body, 0)
        for a in range(2):
            cols = slice(a * LANES, (a + 1) * LANES)
            dk_ref[:, cols] = dka_ref[a]
            dv_ref[:, cols] = dva_ref[a].astype(BF16)

    whole = pl.BlockSpec((s, wide), lambda p, j: (0, p))
    blk = pl.BlockSpec((t, wide), lambda p, j: (j, p))
    return _pcall(body, name=name, grid=(pairs, nblk), in_specs=[whole, blk, blk, whole], out_specs=[whole, blk, blk],
                  out_shape=[jax.ShapeDtypeStruct(qb.shape, F32), jax.ShapeDtypeStruct(qb.shape, F32),
                             jax.ShapeDtypeStruct(qb.shape, BF16)],
                  scratch_shapes=[pltpu.VMEM((2, t, LANES), F32)] * 2,
                  compiler_params=_cp(("parallel", "arbitrary"), VMEM_BIG))(qb, ks, vs, dos)


def fox_merge_bwd(dqs, dks, dvs, n_heads, name):
    s = dqs.shape[0]
    hd = n_heads * HEAD_DIM
    tr = _tile(s, 128)
    scale = HEAD_DIM ** -0.5

    def body(dq_ref, dk_ref, dv_ref, o_ref, dc_ref):
        lane = _lane_iota()
        dc = jnp.zeros((tr, LANES), F32)
        for g in range(hd // LANES):
            even = slice(2 * g * LANES, (2 * g + 1) * LANES)
            odd = slice((2 * g + 1) * LANES, (2 * g + 2) * LANES)
            for part, (src, mul) in enumerate(((dq_ref, scale), (dk_ref, 1.0), (dv_ref, 1.0))):
                dense = jnp.where(lane < HEAD_DIM, src[:, even].astype(F32), pltpu.roll(src[:, odd].astype(F32), HEAD_DIM, 1))
                o_ref[:, part * hd + g * LANES:part * hd + (g + 1) * LANES] = (dense * mul).astype(BF16)
            for a, cols in enumerate((even, odd)):
                both = jnp.where(lane == EXTRA + N_PIECES, dq_ref[:, cols], 0.0) - jnp.where(lane == EXTRA, dk_ref[:, cols], 0.0)
                dc = jnp.where(lane == 2 * g + a, jnp.sum(both, axis=1, keepdims=True), dc)
        dc_ref[...] = dc

    wide = n_heads * LANES
    return _pcall(body, name=name, grid=(s // tr,), in_specs=[_row_spec(tr, wide)] * 3,
                  out_specs=[_row_spec(tr, 3 * hd), _row_spec(tr, LANES)],
                  out_shape=[jax.ShapeDtypeStruct((s, 3 * hd), BF16), jax.ShapeDtypeStruct((s, LANES), F32)],
                  compiler_params=_cp(("parallel",), VMEM_BIG))(dqs, dks, dvs)


def _w_spec(shape, lead_map=None):
    _, r, c = shape[-3:]
    tr = _tile(r, 128)
    n_lead = len(shape) - 2
    if lead_map is None:
        lead_map = lambda l: (l,)
    return tr, pl.BlockSpec((None,) * n_lead + (tr, c), lambda l, i: (*lead_map(l), i, 0))


def cast_into_slot(w, place, dtype, name):
    tr = _tile(w.shape[1], 128)
    blk = (None, tr, w.shape[2])

    def body(me_ref, c_ref, w_ref, o_ref):
        o_ref[...] = w_ref[...].astype(dtype)

    grid_spec = pltpu.PrefetchScalarGridSpec(
        num_scalar_prefetch=2, grid=(w.shape[0], w.shape[1] // tr),
        in_specs=[pl.BlockSpec(blk, lambda l, i, me, c: (l, i, 0))],
        out_specs=pl.BlockSpec((None,) + blk, lambda l, i, me, c: (me[0], l, i, 0)))
    return _pcall(body, name=name, grid_spec=grid_spec, out_shape=jax.ShapeDtypeStruct((N_CHIPS,) + w.shape, dtype),
                  compiler_params=_cp(("parallel", "parallel")))(*place, w)


def adamw(w, g, m, v, name):
    tr, spec = _w_spec(w.shape)

    def body(w_ref, g_ref, m_ref, v_ref, d_ref, mo_ref, vo_ref):
        gg = g_ref[...]
        mn = ADAM_B1 * m_ref[...] + (1.0 - ADAM_B1) * gg
        vn = ADAM_B2 * v_ref[...] + (1.0 - ADAM_B2) * (gg * gg)
        m_hat = mn / (1.0 - ADAM_B1 ** ADAM_STEP)
        v_hat = vn / (1.0 - ADAM_B2 ** ADAM_STEP)
        d_ref[...] = -ADAM_LR * (m_hat / (jnp.sqrt(v_hat) + ADAM_EPS) + ADAM_WD * w_ref[...])
        mo_ref[...] = mn
        vo_ref[...] = vn

    out = jax.ShapeDtypeStruct(w.shape, F32)
    return _pcall(body, name=name, grid=(w.shape[0], w.shape[1] // tr), in_specs=[spec] * 4, out_specs=[spec] * 3,
                  out_shape=[out] * 3, compiler_params=_cp(("parallel", "parallel")))(w, g, m, v)


def add_core_halves(g, recv, place, name):
    h = recv.shape[1]
    tr = _tile(g.shape[2], 128)
    blk = (None, None, tr, g.shape[3])

    def body(me_ref, c_ref, g_ref, r_ref, o_ref):
        o_ref[...] = (g_ref[...].astype(F32) + r_ref[...].astype(F32)).astype(o_ref.dtype)

    grid_spec = pltpu.PrefetchScalarGridSpec(
        num_scalar_prefetch=2, grid=(N_CHIPS, h, g.shape[2] // tr),
        in_specs=[pl.BlockSpec(blk, lambda q, l, i, me, c: (q, c[0] * h + l, i, 0)),
                  pl.BlockSpec(blk, lambda q, l, i, me, c: (q, l, i, 0))],
        out_specs=pl.BlockSpec(blk, lambda q, l, i, me, c: (q, l, i, 0)))
    return _pcall(body, name=name, grid_spec=grid_spec, out_shape=jax.ShapeDtypeStruct(recv.shape, g.dtype),
                  compiler_params=_cp(("parallel", "parallel", "parallel")))(*place, g, recv)


def add_chips(own, recv, place, name):
    _, h, r, cols = own.shape
    tr = _tile(r, 128)
    blk = (None, None, tr, cols)

    def body(me_ref, c_ref, p0, p1, p2, p3, o_ref):
        o_ref[...] = ((p0[...].astype(F32) + p1[...].astype(F32)) + p2[...].astype(F32)) + p3[...].astype(F32)

    def peer(flip):
        return lambda l, i, me, c: (me[0] ^ flip, l, i, 0)

    grid_spec = pltpu.PrefetchScalarGridSpec(
        num_scalar_prefetch=2, grid=(h, r // tr), in_specs=[pl.BlockSpec(blk, peer(f)) for f in (0, 2, 1, 3)],
        out_specs=pl.BlockSpec((None, tr, cols), lambda l, i, me, c: (c[0] * h + l, i, 0)))
    return _pcall(body, name=name, grid_spec=grid_spec, out_shape=jax.ShapeDtypeStruct((2 * h, r, cols), F32),
                  compiler_params=_cp(("parallel", "parallel")))(*place, own, recv, recv, recv)


def _place():
    x, y, c = lax.axis_index("x"), lax.axis_index("y"), lax.axis_index("c")
    others = [(1 - x, y), (x, 1 - y), (1 - x, 1 - y)]
    return x, y, c, others


def _chip_id(chip):
    return 2 * chip[0] + chip[1]


def _comm_call(body, name, n_in, out_shapes, n_sems, in_place=False):
    return _pcall(body, name=name, in_specs=[ANY] * n_in, out_specs=[ANY] * len(out_shapes), out_shape=out_shapes,
                  scratch_shapes=[pltpu.SemaphoreType.DMA((n_sems,)), pltpu.SemaphoreType.DMA((n_sems,))],
                  input_output_aliases={t: t for t in range(n_in)} if in_place else {},
                  compiler_params=pltpu.CompilerParams(has_side_effects=True))


def all_gather_shards(slots, name):
    n = len(slots)

    def body(*refs):
        bufs = refs[n:2 * n]
        send, recv = refs[2 * n:]
        x, y, c, others = _place()
        me = _chip_id((x, y))

        def remote(blk, k, dev):
            return pltpu.make_async_remote_copy(src_ref=blk, dst_ref=blk, send_sem=send.at[k], recv_sem=recv.at[k],
                                                device_id=dev, device_id_type=MESH)

        first, passed = [], []
        for t in range(n):
            h = bufs[t].shape[1] // 2
            mine = pl.ds(c * h, h)
            for j, chip in enumerate(others):
                cp = remote(bufs[t].at[me, mine], 6 * t + j, (*chip, c))
                cp.start()
                first.append(cp)
        for t in range(n):
            h = bufs[t].shape[1] // 2
            mine = pl.ds(c * h, h)
            for j, chip in enumerate(others):
                got = bufs[t].at[_chip_id(chip), mine]
                remote(got, 6 * t + j, (x, y, c)).wait_recv()
                cp = remote(got, 6 * t + 3 + j, (x, y, 1 - c))
                cp.start()
                passed.append(cp)
        for t in range(n):
            h = bufs[t].shape[1] // 2
            theirs = pl.ds((1 - c) * h, h)
            for j, chip in enumerate(others):
                remote(bufs[t].at[_chip_id(chip), theirs], 6 * t + 3 + j, (x, y, c)).wait_recv()
        for cp in first + passed:
            cp.wait_send()

    outs = [jax.ShapeDtypeStruct(a.shape, a.dtype) for a in slots]
    return _comm_call(body, name, n, outs, 6 * n, in_place=True)(*slots)


def swap_core_halves(grads, name):
    n = len(grads)

    def body(*refs):
        ins, outs = refs[:n], refs[n:2 * n]
        send, recv = refs[2 * n:]
        x, y, c, _ = _place()
        cps = []
        for t in range(n):
            h = ins[t].shape[1] // 2
            cp = pltpu.make_async_remote_copy(src_ref=ins[t].at[:, pl.ds((1 - c) * h, h)], dst_ref=outs[t],
                                              send_sem=send.at[t], recv_sem=recv.at[t],
                                              device_id=(x, y, 1 - c), device_id_type=MESH)
            cp.start()
            cps.append(cp)
        for cp in cps:
            cp.wait()

    outs = [jax.ShapeDtypeStruct((a.shape[0], a.shape[1] // 2) + a.shape[2:], a.dtype) for a in grads]
    return _comm_call(body, name, n, outs, n)(*grads)


def scatter_to_chips(sums, name):
    n = len(sums)

    def body(*refs):
        ins, outs = refs[:n], refs[n:2 * n]
        send, recv = refs[2 * n:]
        x, y, c, others = _place()
        me = _chip_id((x, y))
        cps = []
        for t in range(n):
            for j, chip in enumerate(others):
                cp = pltpu.make_async_remote_copy(src_ref=ins[t].at[_chip_id(chip)], dst_ref=outs[t].at[me],
                                                  send_sem=send.at[3 * t + j], recv_sem=recv.at[3 * t + j],
                                                  device_id=(*chip, c), device_id_type=MESH)
                cp.start()
                cps.append(cp)
        for t in range(n):
            for j, chip in enumerate(others):
                got = outs[t].at[_chip_id(chip)]
                pltpu.make_async_remote_copy(src_ref=got, dst_ref=got, send_sem=send.at[3 * t + j], recv_sem=recv.at[3 * t + j],
                                             device_id=(x, y, c), device_id_type=MESH).wait_recv()
        for cp in cps:
            cp.wait_send()

    outs = [jax.ShapeDtypeStruct(a.shape, a.dtype) for a in sums]
    return _comm_call(body, name, n, outs, 3 * n)(*sums)


def join_core_halves(fulls, name):
    n = len(fulls)

    def body(*refs):
        bufs = refs[n:2 * n]
        send, recv = refs[2 * n:]
        x, y, c, _ = _place()
        cps = []
        for t in range(n):
            h = bufs[t].shape[0] // 2
            mine = bufs[t].at[pl.ds(c * h, h)]
            cp = pltpu.make_async_remote_copy(src_ref=mine, dst_ref=mine, send_sem=send.at[t], recv_sem=recv.at[t],
                                              device_id=(x, y, 1 - c), device_id_type=MESH)
            cp.start()
            cps.append(cp)
        for t in range(n):
            h = bufs[t].shape[0] // 2
            theirs = bufs[t].at[pl.ds((1 - c) * h, h)]
            cps[t].wait_send()
            pltpu.make_async_remote_copy(src_ref=theirs, dst_ref=theirs, send_sem=send.at[t], recv_sem=recv.at[t],
                                         device_id=(x, y, c), device_id_type=MESH).wait_recv()

    outs = [jax.ShapeDtypeStruct(a.shape, a.dtype) for a in fulls]
    return _comm_call(body, name, n, outs, n, in_place=True)(*fulls)


def kernel(x, positions, norm_gains, swa_w_in, swa_sinks, swa_w_out, fox_w_in, fox_b_f, fox_w_out, ffn_w_gate_up, ffn_w_down, loss_target, m_norm_gains, m_swa_w_in, m_swa_sinks, m_swa_w_out, m_fox_w_in, m_fox_b_f, m_fox_w_out, m_ffn_w_gate_up, m_ffn_w_down, v_norm_gains, v_swa_w_in, v_swa_sinks, v_swa_w_out, v_fox_w_in, v_fox_b_f, v_fox_w_out, v_ffn_w_gate_up, v_ffn_w_down):
    s, d = x.shape[1], x.shape[2]
    depth = norm_gains.shape[0]
    n_heads = d // HEAD_DIM
    hd = n_heads * HEAD_DIM
    n_kv = (swa_w_in.shape[2] * N_CHIPS // HEAD_DIM - n_heads) // 2
    ff = ffn_w_down.shape[1] * N_CHIPS
    fox_cols = fox_w_in.shape[2]
    fox_pad = 3 * hd + LANES
    assert fox_cols * N_CHIPS == 3 * hd + n_heads and n_heads <= LANES
    x0 = x[0]
    target = loss_target[0]
    place = ((2 * lax.axis_index("x") + lax.axis_index("y")).astype(jnp.int32).reshape(1),
             lax.axis_index("c").astype(jnp.int32).reshape(1))

    gains_sh = norm_gains.reshape(2, depth * 2, norm_gains.shape[2])
    w_names = ["swa_w_in", "swa_w_out", "fox_w_in", "fox_w_out", "ffn_w_gate_up", "ffn_w_down"]
    w_f32 = [swa_w_in, swa_w_out, fox_w_in, fox_w_out, ffn_w_gate_up, ffn_w_down]
    slots = [cast_into_slot(w, place, BF16, "cast_" + nm) for w, nm in zip(w_f32, w_names)]
    slots.append(cast_into_slot(gains_sh, place, F32, "cast_norm_gains"))
    gathered = all_gather_shards(slots, "gather_weights")
    g_swa_in, g_swa_out, g_fox_in, g_fox_out, g_gu, g_down, g_gains = gathered
    gains = jnp.transpose(g_gains.reshape(N_CHIPS, depth * 4, -1), (1, 0, 2)).reshape(depth * 4, d)

    def gain(layer, which):
        return gains[layer * 4 + which][None, :]

    def fox_weight(layer):
        parts = [g_fox_in[q, layer] for q in range(N_CHIPS)]
        parts.append(jnp.zeros((d, fox_pad - fox_cols * N_CHIPS), BF16))
        return jnp.concatenate(parts, axis=1)

    inv_freq = ROPE_THETA ** (-jnp.arange(0, ROT_DIM, 2, dtype=F32) / ROT_DIM)
    lane_d = jnp.arange(LANES) % HEAD_DIM
    invf_row = jnp.where(lane_d < ROT_DIM, inv_freq[lane_d % (ROT_DIM // 2)], 0.0)[None, :]
    tabs = rope_tables(positions.reshape(s, 1), invf_row, "rope_tables")

    n_sh_in = swa_w_in.shape[2]
    gu_sh = ffn_w_gate_up.shape[2]
    tn_gu = gu_sh // 2 if (gu_sh // 2) % LANES == 0 else gu_sh
    down_sh = ffn_w_down.shape[1]
    out_sh = swa_w_out.shape[1]
    tm = min(1024, s)

    saved = []
    xin = x0
    h = prenorm(xin, gain(0, 0), "prenorm_first")
    for layer in range(depth):
        j = layer // 2
        rec = {"x_in": xin, "h1": h}
        if layer % 2 == 0:
            proj = mm_nn(h, g_swa_in, lambda n, k: (n, j, k, 0), n_sh_in * N_CHIPS, tm=min(512, s), tn=n_sh_in, tk=d,
                         out_dtype=F32, name=f"swa_proj_{j}")
            q, kd, vd = swa_split(proj, tabs, n_heads, n_kv, f"swa_split_{j}")
            sink_row = jnp.repeat(swa_sinks[j], HEAD_DIM)[None, :]
            attn, lse = swa_fwd(q, kd, vd, sink_row, f"swa_fwd_{j}")
            rec.update(q=q, kd=kd, vd=vd, sink_row=sink_row, lse=lse)
            w_out_g = g_swa_out
        else:
            wf = fox_weight(j)
            tn = 3 * hd // 6 if (3 * hd // 6) % LANES == 0 else LANES
            qkv = mm_nn(h, wf, lambda n, k: (k, n), 3 * hd, tm=tm, tn=tn, tk=d, out_dtype=BF16, name=f"fox_proj_{j}")
            f_off = 3 * hd // LANES
            f_logit = mm_nn(h, wf, lambda n, k: (k, f_off + n), LANES, tm=tm, tn=LANES, tk=d, out_dtype=F32, name=f"fox_gate_{j}")
            b_row = jnp.pad(fox_b_f[j], (0, LANES - n_heads))[None, :]
            c = forget_cumsum(f_logit, b_row, f"fox_cumsum_{j}")
            qs, ks, vs = fox_split(qkv, c, n_heads, f"fox_split_{j}")
            attn, qb = fox_fwd(qs, ks, vs, n_heads, f"fox_fwd_{j}")
            rec.update(wf=wf, f_logit=f_logit, b_row=b_row, qb=qb, ks=ks, vs=vs)
            w_out_g = g_fox_out
        y = mm_nn(attn, w_out_g, lambda n, k: (k, j, 0, n), d, tm=tm, tn=min(1024, d), tk=out_sh, out_dtype=F32, name=f"out_proj_{layer}")
        xmid, h2 = postnorm_residual(xin, y, gain(layer, 1), gain(layer, 2), f"postnorm_mixer_{layer}")
        npb = gu_sh // tn_gu
        gu = mm_nn(h2, g_gu, lambda n, k: (n // npb, layer, k, n % npb), 2 * ff, tm=tm, tn=tn_gu, tk=d, out_dtype=BF16, name=f"ffn_up_{layer}")
        act = swiglu_fwd(gu, f"swiglu_{layer}")
        y2 = mm_nn(act, g_down, lambda n, k: (k, layer, 0, n), d, tm=tm, tn=min(1024, d), tk=down_sh, out_dtype=F32, name=f"ffn_down_{layer}")
        rec.update(attn=attn, y=y, x_mid=xmid, h2=h2, gu=gu, act=act, y2=y2)
        saved.append(rec)
        if layer + 1 < depth:
            xin, h = postnorm_residual(xmid, y2, gain(layer, 3), gain(layer + 1, 0), f"postnorm_ffn_{layer}")
    dx, loss_blk = postnorm_loss(xmid, y2, gain(depth - 1, 3), target, "loss")

    def grad_buf(w):
        return jnp.zeros((N_CHIPS,) + w.shape, BF16)

    gb_swa_in, gb_swa_out, gb_fox_out, gb_gu, gb_down = (grad_buf(w) for w in (swa_w_in, swa_w_out, fox_w_out, ffn_w_gate_up, ffn_w_down))
    fox_in_grads = [None] * (depth // 2)
    dgains = [None] * (depth * 4)
    dsinks = [None] * ((depth + 1) // 2)
    dbf = [None] * (depth // 2)
    tko = min(1024, d)
    for layer in reversed(range(depth)):
        j = layer // 2
        rec = saved[layer]
        dy2, dgains[layer * 4 + 3] = postnorm_bwd(dx, rec["y2"], gain(layer, 3), f"postnorm_ffn_bwd_{layer}")
        gb_down = mm_tn(rec["act"], dy2, lambda i, n: (i, layer, 0, n), tka=down_sh, tn=min(1024, d), tm=min(512, s),
                        out_block=(None, None, down_sh, min(1024, d)), name=f"ffn_down_dw_{layer}", into=gb_down)
        dact = mm_nt(dy2, g_down, lambda o, n: (o, layer, 0, n), ff, tm=tm, tko=down_sh, tn=d, out_dtype=BF16, name=f"ffn_down_dx_{layer}")
        dgu = swiglu_bwd(dact, rec["gu"], f"swiglu_bwd_{layer}")
        npb = gu_sh // tn_gu
        gb_gu = mm_tn(rec["h2"], dgu, lambda i, n: (n // npb, layer, i, n % npb), tka=tko, tn=tn_gu, tm=min(512, s),
                      out_block=(None, None, tko, tn_gu), name=f"ffn_up_dw_{layer}", into=gb_gu)
        dh2 = mm_nt(dgu, g_gu, lambda o, n: (n // npb, layer, o, n % npb), d, tm=tm, tko=tko, tn=tn_gu, out_dtype=F32, name=f"ffn_up_dx_{layer}")
        dxm, dgains[layer * 4 + 2] = prenorm_bwd(rec["x_mid"], dh2, dx, gain(layer, 2), f"prenorm_ffn_bwd_{layer}")
        dy, dgains[layer * 4 + 1] = postnorm_bwd(dxm, rec["y"], gain(layer, 1), f"postnorm_mixer_bwd_{layer}")
        w_out_g = g_swa_out if layer % 2 == 0 else g_fox_out
        gb_out = gb_swa_out if layer % 2 == 0 else gb_fox_out
        gb_out = mm_tn(rec["attn"], dy, lambda i, n: (i, j, 0, n), tka=out_sh, tn=min(1024, d), tm=min(512, s),
                       out_block=(None, None, out_sh, min(1024, d)), name=f"out_proj_dw_{layer}", into=gb_out)
        dattn = mm_nt(dy, w_out_g, lambda o, n: (o, j, 0, n), hd, tm=tm, tko=out_sh, tn=d, out_dtype=BF16, name=f"out_proj_dx_{layer}")
        if layer % 2 == 0:
            gb_swa_out = gb_out
            dq, dkd, dvd, dsk = swa_bwd(rec["q"], rec["kd"], rec["vd"], rec["sink_row"], rec["attn"], rec["lse"], dattn, f"swa_bwd_{j}")
            dsinks[j] = dsk[0].reshape(n_heads, HEAD_DIM)[:, 0]
            dproj = swa_merge_bwd(dq, dkd, dvd, tabs, f"swa_merge_bwd_{j}")
            gb_swa_in = mm_tn(rec["h1"], dproj, lambda i, n: (n, j, i, 0), tka=tko, tn=n_sh_in, tm=min(512, s),
                              out_block=(None, None, tko, n_sh_in), name=f"swa_proj_dw_{j}", into=gb_swa_in)
            dh1 = mm_nt(dproj, g_swa_in, lambda o, n: (n, j, o, 0), d, tm=tm, tko=tko, tn=n_sh_in, out_dtype=F32, name=f"swa_proj_dx_{j}")
        else:
            gb_fox_out = gb_out
            dos = fox_dout_slots(dattn, rec["attn"], f"fox_dout_slots_{j}")
            dqs, dks, dvs = fox_bwd(rec["qb"], rec["ks"], rec["vs"], dos, n_heads, f"fox_bwd_{j}")
            dqkv, dc = fox_merge_bwd(dqs, dks, dvs, n_heads, f"fox_merge_bwd_{j}")
            df, db = forget_gate_bwd(dc, rec["f_logit"], rec["b_row"], f"fox_gate_bwd_{j}")
            dbf[j] = db[0, :n_heads]
            dproj = jnp.concatenate([dqkv, df], axis=1)
            tn_f = LANES * max(k for k in range(1, 9) if (fox_pad // LANES) % k == 0)
            dwf = mm_tn(rec["h1"], dproj, lambda i, n: (i, n), tka=tko, tn=tn_f, tm=min(512, s), out_block=(tko, tn_f),
                        name=f"fox_proj_dw_{j}", out_shape=jax.ShapeDtypeStruct((d, fox_pad), BF16))
            fox_in_grads[j] = dwf
            dh1 = mm_nt(dproj, rec["wf"], lambda o, n: (o, n), d, tm=tm, tko=tko, tn=tn_f, out_dtype=F32, name=f"fox_proj_dx_{j}")
        dx, dgains[layer * 4] = prenorm_bwd(rec["x_in"], dh1, dxm, gain(layer, 0), f"prenorm_mixer_bwd_{layer}")
    grad_x = dx[None]

    gb_fox_in = jnp.stack([jnp.stack([g[:, q * fox_cols:(q + 1) * fox_cols] for g in fox_in_grads]) for q in range(N_CHIPS)])
    dgain_full = jnp.concatenate(dgains, axis=0)
    gb_gains = jnp.transpose(dgain_full.reshape(depth * 4, N_CHIPS, -1), (1, 0, 2)).reshape(N_CHIPS, 2, depth * 2, -1)
    partials = [gb_swa_in, gb_swa_out, gb_fox_in, gb_fox_out, gb_gu, gb_down, gb_gains]
    names = w_names + ["norm_gains"]
    from_sibling = swap_core_halves(partials, "reduce_swap_cores")
    chip_sums = [add_core_halves(g, r, place, "reduce_add_cores_" + nm) for g, r, nm in zip(partials, from_sibling, names)]
    from_chips = scatter_to_chips(chip_sums, "reduce_scatter_chips")
    my_half = [add_chips(o, p, place, "reduce_add_chips_" + nm) for o, p, nm in zip(chip_sums, from_chips, names)]
    full = join_core_halves(my_half, "reduce_join_cores")
    g_swa_in_f, g_swa_out_f, g_fox_in_f, g_fox_out_f, g_gu_f, g_down_f, g_gains_f = full
    g_gains_f = g_gains_f.reshape(norm_gains.shape)

    n_swa, n_fox = len(dsinks), len(dbf)
    small = jnp.concatenate([loss_blk[0, :1]] + dsinks + dbf)
    small = lax.psum(small, ("x", "y", "c"))
    loss = small[0]
    g_sinks = small[1:1 + n_swa * n_heads].reshape(n_swa, n_heads)
    g_bf = small[1 + n_swa * n_heads:].reshape(n_fox, n_heads)

    def pad_small(a):
        return jnp.pad(a, ((0, 8 - a.shape[0]), (0, LANES - a.shape[1])))[None]

    def update(w, g, m, v, nm):
        if w.ndim == 2:
            dl, mn, vn = adamw(pad_small(w), pad_small(g), pad_small(m), pad_small(v), "adamw_" + nm)
            return tuple(a[0, :w.shape[0], :w.shape[1]] for a in (dl, mn, vn))
        return adamw(w, g, m, v, "adamw_" + nm)

    grads = [g_gains_f, g_swa_in_f, g_sinks, g_swa_out_f, g_fox_in_f, g_bf, g_fox_out_f, g_gu_f, g_down_f]
    ws = [norm_gains, swa_w_in, swa_sinks, swa_w_out, fox_w_in, fox_b_f, fox_w_out, ffn_w_gate_up, ffn_w_down]
    ms = [m_norm_gains, m_swa_w_in, m_swa_sinks, m_swa_w_out, m_fox_w_in, m_fox_b_f, m_fox_w_out, m_ffn_w_gate_up, m_ffn_w_down]
    vs = [v_norm_gains, v_swa_w_in, v_swa_sinks, v_swa_w_out, v_fox_w_in, v_fox_b_f, v_fox_w_out, v_ffn_w_gate_up, v_ffn_w_down]
    nms = ["norm_gains", "swa_w_in", "swa_sinks", "swa_w_out", "fox_w_in", "fox_b_f", "fox_w_out", "ffn_w_gate_up", "ffn_w_down"]
    upd = [update(w, g, m, v, nm) for w, g, m, v, nm in zip(ws, grads, ms, vs, nms)]
    return (loss, grad_x, *grads, *[u[0] for u in upd], *[u[1] for u in upd], *[u[2] for u in upd])
```

```python
import jax
import jax.numpy as jnp
from jax import lax
from jax.experimental import pallas as pl
from jax.experimental.pallas import tpu as pltpu

F32 = jnp.float32
BF16 = jnp.bfloat16
HEAD_DIM = 64
LANES = 128
WINDOW = 128
ROPE_THETA = 500000.0
ROT_DIM = HEAD_DIM // 4
RMS_EPS = 1e-6
ADAM_LR, ADAM_B1, ADAM_B2, ADAM_EPS, ADAM_WD, ADAM_STEP = 0.001, 0.9, 0.999, 1e-08, 0.01, 10
NEG = -1e30
VMEM_BIG = 56 * 1024 * 1024
N_CHIPS = 4
MESH = pl.DeviceIdType.MESH
ANY = pl.BlockSpec(memory_space=pl.ANY)


def _pcall(body, **kw):
    return pl.pallas_call(body, **kw)


def _cp(sem=None, vmem=None):
    return pltpu.CompilerParams(dimension_semantics=sem, vmem_limit_bytes=vmem)


def _tile(n, pref):
    if n <= pref:
        return n
    t = pref - pref % 16
    while n % t:
        t -= 16
    return t


_DIMS = {"nn": (((1,), (0,)), ((), ())), "nt": (((1,), (1,)), ((), ())), "tn": (((0,), (0,)), ((), ()))}


def _matmul(kind, a, b, *, grid, a_spec, b_spec, o_spec, out_shape, acc_shape, name, into=None):
    nk = grid[2]
    dims = _DIMS[kind]

    def body(a_ref, b_ref, *rest):
        rest = rest[1:] if into is not None else rest
        o_ref = rest[0]

        def prod():
            return lax.dot_general(a_ref[...], b_ref[...], dims, preferred_element_type=F32)

        if nk == 1:
            o_ref[...] = prod().astype(o_ref.dtype)
        else:
            acc = rest[1]
            k = pl.program_id(2)

            @pl.when(k == 0)
            def _():
                acc[...] = prod()

            @pl.when(k > 0)
            def _():
                acc[...] += prod()

            @pl.when(k == nk - 1)
            def _():
                o_ref[...] = acc[...].astype(o_ref.dtype)

    args, in_specs, aliases = [a, b], [a_spec, b_spec], {}
    if into is not None:
        args.append(into)
        in_specs.append(ANY)
        aliases = {2: 0}
        out_shape = jax.ShapeDtypeStruct(into.shape, into.dtype)
    return _pcall(body, name=name, grid=grid, in_specs=in_specs, out_specs=o_spec, out_shape=out_shape,
                  scratch_shapes=[] if nk == 1 else [pltpu.VMEM(acc_shape, F32)], input_output_aliases=aliases,
                  compiler_params=_cp(("parallel", "parallel", "arbitrary"), VMEM_BIG))(*args)


def mm_nn(a, w, w_map, n_out, *, tm, tn, tk, out_dtype, name):
    m, kdim = a.shape
    tm = min(tm, m)
    lead = (None,) * (w.ndim - 2)
    return _matmul("nn", a, w, grid=(m // tm, n_out // tn, kdim // tk),
                   a_spec=pl.BlockSpec((tm, tk), lambda i, j, k: (i, k)),
                   b_spec=pl.BlockSpec(lead + (tk, tn), lambda i, j, k: w_map(j, k)),
                   o_spec=pl.BlockSpec((tm, tn), lambda i, j, k: (i, j)),
                   out_shape=jax.ShapeDtypeStruct((m, n_out), out_dtype), acc_shape=(tm, tn), name=name)


def mm_nt(a, w, w_map, k_out, *, tm, tko, tn, out_dtype, name):
    m, ndim = a.shape
    tm = min(tm, m)
    lead = (None,) * (w.ndim - 2)
    return _matmul("nt", a, w, grid=(m // tm, k_out // tko, ndim // tn),
                   a_spec=pl.BlockSpec((tm, tn), lambda i, j, n: (i, n)),
                   b_spec=pl.BlockSpec(lead + (tko, tn), lambda i, j, n: w_map(j, n)),
                   o_spec=pl.BlockSpec((tm, tko), lambda i, j, n: (i, j)),
                   out_shape=jax.ShapeDtypeStruct((m, k_out), out_dtype), acc_shape=(tm, tko), name=name)


def mm_tn(a, b, o_map, *, tka, tn, tm, out_block, name, into=None, out_shape=None):
    m, kdim = a.shape
    n = b.shape[1]
    tm = min(tm, m)
    return _matmul("tn", a, b, grid=(kdim // tka, n // tn, m // tm),
                   a_spec=pl.BlockSpec((tm, tka), lambda i, j, mm: (mm, i)),
                   b_spec=pl.BlockSpec((tm, tn), lambda i, j, mm: (mm, j)),
                   o_spec=pl.BlockSpec(out_block, lambda i, j, mm: o_map(i, j)),
                   out_shape=out_shape, acc_shape=(tka, tn), name=name, into=into)


def _rstd(v):
    return lax.rsqrt(jnp.mean(v * v, axis=-1, keepdims=True) + RMS_EPS)


def _row_spec(tr, d):
    return pl.BlockSpec((tr, d), lambda i: (i, 0))


def _vec_spec(d):
    return pl.BlockSpec((1, d), lambda i: (0, 0))


def prenorm(x, g, name):
    s, d = x.shape
    tr = _tile(s, 256)

    def body(x_ref, g_ref, h_ref):
        v = x_ref[...]
        h_ref[...] = (v * _rstd(v) * g_ref[...]).astype(BF16)

    return _pcall(body, name=name, grid=(s // tr,), in_specs=[_row_spec(tr, d), _vec_spec(d)],
                  out_specs=_row_spec(tr, d), out_shape=jax.ShapeDtypeStruct((s, d), BF16),
                  compiler_params=_cp(("parallel",)))(x, g)


def postnorm_residual(x, y, g_post, g_next, name):
    s, d = x.shape
    tr = _tile(s, 256)

    def body(x_ref, y_ref, gp_ref, gn_ref, xo_ref, h_ref):
        v = y_ref[...]
        xn = x_ref[...] + v * _rstd(v) * gp_ref[...]
        xo_ref[...] = xn
        h_ref[...] = (xn * _rstd(xn) * gn_ref[...]).astype(BF16)

    return _pcall(body, name=name, grid=(s // tr,),
                  in_specs=[_row_spec(tr, d), _row_spec(tr, d), _vec_spec(d), _vec_spec(d)],
                  out_specs=[_row_spec(tr, d), _row_spec(tr, d)],
                  out_shape=[jax.ShapeDtypeStruct((s, d), F32), jax.ShapeDtypeStruct((s, d), BF16)],
                  compiler_params=_cp(("parallel",)))(x, y, g_post, g_next)


def postnorm_loss(x, y, g_post, target, name):
    s, d = x.shape
    tr = _tile(s, 256)

    def body(x_ref, y_ref, gp_ref, t_ref, dx_ref, loss_ref):
        v = y_ref[...]
        err = x_ref[...] + v * _rstd(v) * gp_ref[...] - t_ref[...]
        dx_ref[...] = err / d
        part = 0.5 * jnp.sum(jnp.mean(err * err, axis=-1, keepdims=True), axis=0, keepdims=True)

        @pl.when(pl.program_id(0) == 0)
        def _():
            loss_ref[...] = jnp.zeros_like(loss_ref)

        loss_ref[...] += part

    return _pcall(body, name=name, grid=(s // tr,),
                  in_specs=[_row_spec(tr, d), _row_spec(tr, d), _vec_spec(d), _row_spec(tr, d)],
                  out_specs=[_row_spec(tr, d), pl.BlockSpec((8, LANES), lambda i: (0, 0))],
                  out_shape=[jax.ShapeDtypeStruct((s, d), F32), jax.ShapeDtypeStruct((8, LANES), F32)],
                  compiler_params=_cp(("arbitrary",)))(x, y, g_post, target)


def _norm_bwd(v, g, dz):
    r = _rstd(v)
    vhat = v * r
    u = dz * g
    dv = r * (u - vhat * jnp.mean(u * vhat, axis=-1, keepdims=True))
    return dv, jnp.sum(dz * vhat, axis=0, keepdims=True)


def _acc_rows(ref, val):
    @pl.when(pl.program_id(0) == 0)
    def _():
        ref[...] = jnp.zeros_like(ref)

    ref[...] += val


def postnorm_bwd(dz, y, g, name):
    s, d = y.shape
    tr = _tile(s, 256)

    def body(dz_ref, y_ref, g_ref, dy_ref, dg_ref):
        dv, dg = _norm_bwd(y_ref[...], g_ref[...], dz_ref[...])
        dy_ref[...] = dv.astype(BF16)
        _acc_rows(dg_ref, dg)

    return _pcall(body, name=name, grid=(s // tr,), in_specs=[_row_spec(tr, d), _row_spec(tr, d), _vec_spec(d)],
                  out_specs=[_row_spec(tr, d), _vec_spec(d)],
                  out_shape=[jax.ShapeDtypeStruct((s, d), BF16), jax.ShapeDtypeStruct((1, d), F32)],
                  compiler_params=_cp(("arbitrary",)))(dz, y, g)


def prenorm_bwd(x, dh, dskip, g, name):
    s, d = x.shape
    tr = _tile(s, 256)

    def body(x_ref, dh_ref, ds_ref, g_ref, dx_ref, dg_ref):
        dv, dg = _norm_bwd(x_ref[...], g_ref[...], dh_ref[...])
        dx_ref[...] = ds_ref[...] + dv
        _acc_rows(dg_ref, dg)

    return _pcall(body, name=name, grid=(s // tr,),
                  in_specs=[_row_spec(tr, d), _row_spec(tr, d), _row_spec(tr, d), _vec_spec(d)],
                  out_specs=[_row_spec(tr, d), _vec_spec(d)],
                  out_shape=[jax.ShapeDtypeStruct((s, d), F32), jax.ShapeDtypeStruct((1, d), F32)],
                  compiler_params=_cp(("arbitrary",)))(x, dh, dskip, g)


def swiglu_fwd(gu, name):
    s, two_ff = gu.shape
    ff = two_ff // 2
    tr = _tile(s, 128)

    def body(gu_ref, act_ref):
        gate = gu_ref[:, :ff].astype(F32)
        up = gu_ref[:, ff:].astype(F32)
        act_ref[...] = (gate * jax.nn.sigmoid(gate) * up).astype(BF16)

    return _pcall(body, name=name, grid=(s // tr,), in_specs=[_row_spec(tr, two_ff)], out_specs=_row_spec(tr, ff),
                  out_shape=jax.ShapeDtypeStruct((s, ff), BF16), compiler_params=_cp(("parallel",)))(gu)


def swiglu_bwd(dact, gu, name):
    s, two_ff = gu.shape
    ff = two_ff // 2
    tr = _tile(s, 128)

    def body(da_ref, gu_ref, dgu_ref):
        gate = gu_ref[:, :ff].astype(F32)
        up = gu_ref[:, ff:].astype(F32)
        da = da_ref[...].astype(F32)
        sig = jax.nn.sigmoid(gate)
        dgu_ref[:, :ff] = (da * up * sig * (1.0 + gate * (1.0 - sig))).astype(BF16)
        dgu_ref[:, ff:] = (da * gate * sig).astype(BF16)

    return _pcall(body, name=name, grid=(s // tr,), in_specs=[_row_spec(tr, ff), _row_spec(tr, two_ff)],
                  out_specs=_row_spec(tr, two_ff), out_shape=jax.ShapeDtypeStruct((s, two_ff), BF16),
                  compiler_params=_cp(("parallel",)))(dact, gu)


def rope_tables(pos_col, invf_row, name):
    s = pos_col.shape[0]
    tr = _tile(s, 1024)

    def body(p_ref, f_ref, c_ref, sa_ref, sb_ref):
        ang = p_ref[...].astype(F32) * f_ref[...]
        d = lax.broadcasted_iota(jnp.int32, (1, LANES), 1) % HEAD_DIM
        cs, sn = jnp.cos(ang), jnp.sin(ang)
        c_ref[...] = jnp.where(d < ROT_DIM, cs, 1.0)
        sa_ref[...] = jnp.where(d < ROT_DIM // 2, -sn, 0.0)
        sb_ref[...] = jnp.where((d >= ROT_DIM // 2) & (d < ROT_DIM), sn, 0.0)

    tab = jax.ShapeDtypeStruct((s, LANES), F32)
    return _pcall(body, name=name, grid=(s // tr,),
                  in_specs=[pl.BlockSpec((tr, 1), lambda i: (i, 0)), _vec_spec(LANES)],
                  out_specs=[_row_spec(tr, LANES)] * 3, out_shape=[tab] * 3, compiler_params=_cp(("parallel",)))(pos_col, invf_row)


def _rot(t, c, sa, sb):
    half = ROT_DIM // 2
    return t * c + pltpu.roll(t, LANES - half, 1) * sa + pltpu.roll(t, half, 1) * sb


def _rot_t(t, c, sa, sb):
    half = ROT_DIM // 2
    return t * c + pltpu.roll(t * sa, half, 1) + pltpu.roll(t * sb, LANES - half, 1)


def swa_split(proj, tabs, n_heads, n_kv, name):
    s, width = proj.shape
    qd, kd = n_heads * HEAD_DIM, n_kv * HEAD_DIM
    tr = _tile(s, 256)

    def body(p_ref, c_ref, sa_ref, sb_ref, q_ref, k_ref, v_ref):
        c, sa, sb = c_ref[...], sa_ref[...], sb_ref[...]
        low = lax.broadcasted_iota(jnp.int32, (1, LANES), 1) < HEAD_DIM
        for g in range(qd // LANES):
            q_ref[:, g * LANES:(g + 1) * LANES] = _rot(p_ref[:, g * LANES:(g + 1) * LANES], c, sa, sb).astype(BF16)
        for g in range(kd // LANES):
            for src, dst, rot in ((qd, k_ref, True), (qd + kd, v_ref, False)):
                t = p_ref[:, src + g * LANES:src + (g + 1) * LANES]
                t = _rot(t, c, sa, sb) if rot else t
                sw = pltpu.roll(t, HEAD_DIM, 1)
                dst[:, (2 * g) * LANES:(2 * g + 1) * LANES] = jnp.where(low, t, sw).astype(BF16)
                dst[:, (2 * g + 1) * LANES:(2 * g + 2) * LANES] = jnp.where(low, sw, t).astype(BF16)

    return _pcall(body, name=name, grid=(s // tr,),
                  in_specs=[_row_spec(tr, width)] + [_row_spec(tr, LANES)] * 3,
                  out_specs=[_row_spec(tr, qd), _row_spec(tr, 2 * kd), _row_spec(tr, 2 * kd)],
                  out_shape=[jax.ShapeDtypeStruct((s, qd), BF16), jax.ShapeDtypeStruct((s, 2 * kd), BF16),
                             jax.ShapeDtypeStruct((s, 2 * kd), BF16)],
                  compiler_params=_cp(("parallel",)))(proj, *tabs)


def swa_merge_bwd(dq, dkd, dvd, tabs, name):
    s, qd = dq.shape
    kd = dkd.shape[1] // 2
    width = qd + 2 * kd
    tr = _tile(s, 256)

    def body(dq_ref, dk_ref, dv_ref, c_ref, sa_ref, sb_ref, o_ref):
        c, sa, sb = c_ref[...], sa_ref[...], sb_ref[...]
        low = lax.broadcasted_iota(jnp.int32, (1, LANES), 1) < HEAD_DIM
        for g in range(qd // LANES):
            t = dq_ref[:, g * LANES:(g + 1) * LANES].astype(F32)
            o_ref[:, g * LANES:(g + 1) * LANES] = _rot_t(t, c, sa, sb).astype(BF16)
        for g in range(kd // LANES):
            for dst, src, rot in ((qd, dk_ref, True), (qd + kd, dv_ref, False)):
                e = src[:, (2 * g) * LANES:(2 * g + 1) * LANES]
                o = src[:, (2 * g + 1) * LANES:(2 * g + 2) * LANES]
                t = jnp.where(low, e + pltpu.roll(e, HEAD_DIM, 1), o + pltpu.roll(o, HEAD_DIM, 1))
                t = _rot_t(t, c, sa, sb) if rot else t
                o_ref[:, dst + g * LANES:dst + (g + 1) * LANES] = t.astype(BF16)

    return _pcall(body, name=name, grid=(s // tr,),
                  in_specs=[_row_spec(tr, qd), _row_spec(tr, 2 * kd), _row_spec(tr, 2 * kd)] + [_row_spec(tr, LANES)] * 3,
                  out_specs=_row_spec(tr, width), out_shape=jax.ShapeDtypeStruct((s, width), BF16),
                  compiler_params=_cp(("parallel",)))(dq, dkd, dvd, *tabs)


def _halves():
    lane_half = lax.broadcasted_iota(jnp.int32, (1, LANES), 1) // HEAD_DIM
    return [lane_half == 0, lane_half == 1]


def _nt(a, b):
    return lax.dot_general(a, b, _DIMS["nt"], preferred_element_type=F32)


def _tn(a, b):
    return lax.dot_general(a, b, _DIMS["tn"], preferred_element_type=F32)


def _band_mask(i, tq):
    w = tq + WINDOW
    rel = lax.broadcasted_iota(jnp.int32, (tq, w), 1) - lax.broadcasted_iota(jnp.int32, (tq, w), 0)
    first = lax.broadcasted_iota(jnp.int32, (tq, w), 1) >= jnp.where(i > 0, 0, WINDOW)
    return (rel >= 1) & (rel <= WINDOW) & first


def swa_fwd(q, kd, vd, sink_row, name):
    s, qd = q.shape
    tq = min(256, s)
    r = tq // WINDOW
    pairs = qd // LANES
    group_pairs = pairs // (kd.shape[1] // LANES)
    scale = HEAD_DIM ** -0.5

    def body(q_ref, kp_ref, kc_ref, vp_ref, vc_ref, sk_ref, o_ref, lse_ref):
        i = pl.program_id(1)
        k = jnp.concatenate([kp_ref[...], kc_ref[...]], axis=0)
        v = jnp.concatenate([vp_ref[...], vc_ref[...]], axis=0)
        mask = _band_mask(i, tq)
        q2 = q_ref[...]
        outs, lses = [], []
        for a, hm in enumerate(_halves()):
            sc = _nt(jnp.where(hm, q2, jnp.zeros_like(q2)), k) * scale
            sc = jnp.where(mask, sc, NEG)
            sink = sk_ref[:, a * HEAD_DIM:a * HEAD_DIM + 1]
            m = jnp.maximum(jnp.max(sc, axis=-1, keepdims=True), sink)
            p = jnp.exp(sc - m)
            den = jnp.sum(p, axis=-1, keepdims=True) + jnp.exp(sink - m)
            outs.append(jnp.dot(p.astype(BF16), v, preferred_element_type=F32) / den)
            lses.append(m + jnp.log(den))
        hm0 = _halves()[0]
        o_ref[...] = jnp.where(hm0, outs[0], outs[1]).astype(BF16)
        lse_ref[...] = jnp.where(hm0, lses[0], lses[1])

    prev = lambda p, i: (jnp.maximum(i * r - 1, 0), p // group_pairs)
    cur = lambda p, i: (i, p // group_pairs)
    blk = pl.BlockSpec((tq, LANES), lambda p, i: (i, p))
    return _pcall(body, name=name, grid=(pairs, s // tq),
                  in_specs=[blk, pl.BlockSpec((WINDOW, LANES), prev), pl.BlockSpec((tq, LANES), cur),
                            pl.BlockSpec((WINDOW, LANES), prev), pl.BlockSpec((tq, LANES), cur),
                            pl.BlockSpec((1, LANES), lambda p, i: (0, p))],
                  out_specs=[blk, blk],
                  out_shape=[jax.ShapeDtypeStruct((s, qd), BF16), jax.ShapeDtypeStruct((s, qd), F32)],
                  compiler_params=_cp(("parallel", "parallel")))(q, kd, kd, vd, vd, sink_row)


def swa_bwd(q, kd, vd, sink_row, out, lse, dout, name):
    s, qd = q.shape
    tq = min(256, s)
    r = tq // WINDOW
    n_kv = kd.shape[1] // LANES
    gw = qd // n_kv
    scale = HEAD_DIM ** -0.5

    def body(q_ref, kp_ref, kc_ref, vp_ref, vc_ref, sk_ref, o_ref, lse_ref, do_ref, dq_ref, dk_ref, dv_ref, dsk_ref):
        i = pl.program_id(1)

        @pl.when(i == 0)
        def _():
            dk_ref[...] = jnp.zeros_like(dk_ref)
            dv_ref[...] = jnp.zeros_like(dv_ref)
            dsk_ref[...] = jnp.zeros_like(dsk_ref)

        k = jnp.concatenate([kp_ref[...], kc_ref[...]], axis=0)
        v = jnp.concatenate([vp_ref[...], vc_ref[...]], axis=0)
        mask = _band_mask(i, tq)
        dk = jnp.zeros((tq + WINDOW, LANES), F32)
        dv = jnp.zeros((tq + WINDOW, LANES), F32)
        for pp in range(gw // LANES):
            cols = slice(pp * LANES, (pp + 1) * LANES)
            q2, do2 = q_ref[:, cols], do_ref[:, cols]
            prod = do2.astype(F32) * o_ref[:, cols].astype(F32)
            dq2 = jnp.zeros((tq, LANES), F32)
            dsk = jnp.zeros((1, LANES), F32)
            for a, hm in enumerate(_halves()):
                qa = jnp.where(hm, q2, jnp.zeros_like(q2))
                doa = jnp.where(hm, do2, jnp.zeros_like(do2))
                lse_a = lse_ref[:, pp * LANES + a * HEAD_DIM:pp * LANES + a * HEAD_DIM + 1]
                sc = jnp.where(mask, _nt(qa, k) * scale, NEG)
                p = jnp.exp(sc - lse_a)
                delta = jnp.sum(jnp.where(hm, prod, 0.0), axis=-1, keepdims=True)
                ds = (p * (_nt(doa, v) - delta) * scale).astype(BF16)
                dv = dv + _tn(p.astype(BF16), doa)
                dk = dk + _tn(ds, qa)
                dq2 = dq2 + jnp.where(hm, jnp.dot(ds, k, preferred_element_type=F32), 0.0)
                sink = sk_ref[:, pp * LANES + a * HEAD_DIM:pp * LANES + a * HEAD_DIM + 1]
                dsink = -jnp.sum(jnp.exp(sink - lse_a) * delta, axis=0, keepdims=True)
                dsk = dsk + jnp.where(hm, dsink, 0.0)
            dq_ref[:, cols] = dq2.astype(BF16)
            dsk_ref[0:1, cols] += dsk
        start = pl.multiple_of(i * tq, tq)
        dk_ref[pl.ds(start, tq), :] += dk[WINDOW:, :]
        dv_ref[pl.ds(start, tq), :] += dv[WINDOW:, :]

        @pl.when(i > 0)
        def _():
            before = pl.multiple_of(i * tq - WINDOW, WINDOW)
            dk_ref[pl.ds(before, WINDOW), :] += dk[:WINDOW, :]
            dv_ref[pl.ds(before, WINDOW), :] += dv[:WINDOW, :]

    prev = lambda g, i: (jnp.maximum(i * r - 1, 0), g)
    cur = lambda g, i: (i, g)
    wide = pl.BlockSpec((tq, gw), cur)
    full = pl.BlockSpec((s, LANES), lambda g, i: (0, g))
    return _pcall(body, name=name, grid=(n_kv, s // tq),
                  in_specs=[wide, pl.BlockSpec((WINDOW, LANES), prev), pl.BlockSpec((tq, LANES), cur),
                            pl.BlockSpec((WINDOW, LANES), prev), pl.BlockSpec((tq, LANES), cur),
                            pl.BlockSpec((1, gw), lambda g, i: (0, g)), wide, wide, wide],
                  out_specs=[wide, full, full, pl.BlockSpec((8, gw), lambda g, i: (0, g))],
                  out_shape=[jax.ShapeDtypeStruct((s, qd), BF16), jax.ShapeDtypeStruct(kd.shape, F32),
                             jax.ShapeDtypeStruct(kd.shape, F32), jax.ShapeDtypeStruct((8, qd), F32)],
                  compiler_params=_cp(("parallel", "arbitrary"), VMEM_BIG))(q, kd, kd, vd, vd, sink_row, out, lse, dout)


def forget_cumsum(f_logit, b_row, name):
    s = f_logit.shape[0]

    def body(f_ref, b_ref, c_ref):
        z = f_ref[...] + b_ref[...]
        acc = jnp.minimum(z, 0.0) - jnp.log(1.0 + jnp.exp(-jnp.abs(z)))
        row = lax.broadcasted_iota(jnp.int32, (s, LANES), 0)
        d = 1
        while d < s:
            acc = acc + jnp.where(row >= d, pltpu.roll(acc, d, 0), 0.0)
            d *= 2
        c_ref[...] = acc

    return _pcall(body, name=name, in_specs=[pl.BlockSpec((s, LANES), lambda: (0, 0)), pl.BlockSpec((1, LANES), lambda: (0, 0))],
                  out_specs=pl.BlockSpec((s, LANES), lambda: (0, 0)), out_shape=jax.ShapeDtypeStruct((s, LANES), F32),
                  compiler_params=_cp(None, VMEM_BIG))(f_logit, b_row)


def forget_gate_bwd(dc, f_logit, b_row, name):
    s = dc.shape[0]

    def body(dc_ref, f_ref, b_ref, df_ref, db_ref):
        acc = dc_ref[...]
        row = lax.broadcasted_iota(jnp.int32, (s, LANES), 0)
        d = 1
        while d < s:
            acc = acc + jnp.where(row < s - d, pltpu.roll(acc, s - d, 0), 0.0)
            d *= 2
        df = acc * jax.nn.sigmoid(-(f_ref[...] + b_ref[...]))
        df_ref[...] = df.astype(BF16)
        db_ref[...] = jnp.sum(df, axis=0, keepdims=True)

    whole = pl.BlockSpec((s, LANES), lambda: (0, 0))
    vec = pl.BlockSpec((1, LANES), lambda: (0, 0))
    return _pcall(body, name=name, in_specs=[whole, whole, vec], out_specs=[whole, vec],
                  out_shape=[jax.ShapeDtypeStruct((s, LANES), BF16), jax.ShapeDtypeStruct((1, LANES), F32)],
                  compiler_params=_cp(None, VMEM_BIG))(dc, f_logit, b_row)


EXTRA = HEAD_DIM
N_PIECES = 3


def _pieces(v):
    hi = v.astype(BF16).astype(F32)
    mid = (v - hi).astype(BF16).astype(F32)
    return hi, mid, (v - hi - mid).astype(BF16).astype(F32)


def _slot(main, lane, extras=None, ones_at=None):
    out = jnp.where(lane < HEAD_DIM, main, 0.0)
    if extras is not None:
        for r, e in enumerate(extras):
            out = jnp.where(lane == EXTRA + r, e, out)
    if ones_at is not None:
        out = jnp.where((lane >= ones_at) & (lane < ones_at + N_PIECES), 1.0, out)
    return out


def _lane_iota():
    return lax.broadcasted_iota(jnp.int32, (1, LANES), 1)


def fox_split(qkv, c, n_heads, name):
    s = qkv.shape[0]
    hd = n_heads * HEAD_DIM
    tr = _tile(s, 256)
    scale = HEAD_DIM ** -0.5

    def body(x_ref, c_ref, q_ref, k_ref, v_ref):
        lane = _lane_iota()
        cv = c_ref[...]
        for g in range(hd // LANES):
            for part, dst in enumerate((q_ref, k_ref, v_ref)):
                t = x_ref[:, part * hd + g * LANES:part * hd + (g + 1) * LANES].astype(F32)
                for a, main in enumerate((t, pltpu.roll(t, HEAD_DIM, 1))):
                    h = 2 * g + a
                    if part == 0:
                        val = _slot(main * scale, lane, ones_at=EXTRA)
                    elif part == 1:
                        ch = jnp.sum(jnp.where(lane == h, cv, 0.0), axis=1, keepdims=True)
                        val = _slot(main, lane, extras=_pieces(-ch), ones_at=EXTRA + N_PIECES)
                    else:
                        val = _slot(main, lane, ones_at=EXTRA)
                    dst[:, h * LANES:(h + 1) * LANES] = val.astype(BF16)

    slots = jax.ShapeDtypeStruct((s, n_heads * LANES), BF16)
    return _pcall(body, name=name, grid=(s // tr,), in_specs=[_row_spec(tr, 3 * hd), _row_spec(tr, LANES)],
                  out_specs=[_row_spec(tr, n_heads * LANES)] * 3, out_shape=[slots] * 3,
                  compiler_params=_cp(("parallel",), VMEM_BIG))(qkv, c)


def _causal_keep(t):
    return lax.broadcasted_iota(jnp.int32, (t, t), 1) <= lax.broadcasted_iota(jnp.int32, (t, t), 0)


def fox_fwd(qs, ks, vs, n_heads, name):
    s = qs.shape[0]
    pairs = n_heads // 2
    t = min(512, s)
    wide = 2 * LANES

    def body(q_ref, k_ref, v_ref, o_ref, qb_ref, acc_ref, m_ref):
        i = pl.program_id(1)
        lane = _lane_iota()
        acc_ref[...] = jnp.zeros_like(acc_ref)
        m_ref[...] = jnp.full_like(m_ref, NEG)

        def step(j, diagonal):
            rows = pl.ds(pl.multiple_of(j * t, t), t)
            for a in range(2):
                cols = slice(a * LANES, (a + 1) * LANES)
                sc = _nt(q_ref[:, cols], k_ref[rows, cols])
                if diagonal:
                    sc = jnp.where(_causal_keep(t), sc, NEG)
                m_old = m_ref[a]
                m_new = jnp.maximum(m_old, jnp.max(sc, axis=-1, keepdims=True))
                p = jnp.exp(sc - jnp.tile(m_new, (1, t // LANES)))
                acc_ref[a] = jnp.exp(m_old - m_new) * acc_ref[a] + jnp.dot(p.astype(BF16), v_ref[rows, cols],
                                                                              preferred_element_type=F32)
                m_ref[a] = m_new

        def loop_body(j, carry):
            step(j, False)
            return carry

        lax.fori_loop(0, i, loop_body, 0)
        step(i, True)
        outs = []
        for a in range(2):
            cols = slice(a * LANES, (a + 1) * LANES)
            acc = acc_ref[a]
            norm = acc[:, EXTRA:EXTRA + 1]
            outs.append(acc / norm)
            neg_lse = _pieces(-(m_ref[a] + jnp.log(norm)))
            qb = q_ref[:, cols].astype(F32)
            for r in range(N_PIECES):
                qb = jnp.where(lane == EXTRA + N_PIECES + r, neg_lse[r], qb)
            qb_ref[:, cols] = qb.astype(BF16)
        o_ref[...] = jnp.where(lane < HEAD_DIM, outs[0], pltpu.roll(outs[1], HEAD_DIM, 1)).astype(BF16)

    qblk = pl.BlockSpec((t, wide), lambda p, i: (i, p))
    whole = pl.BlockSpec((s, wide), lambda p, i: (0, p))
    return _pcall(body, name=name, grid=(pairs, s // t), in_specs=[qblk, whole, whole],
                  out_specs=[pl.BlockSpec((t, LANES), lambda p, i: (i, p)), qblk],
                  out_shape=[jax.ShapeDtypeStruct((s, n_heads * HEAD_DIM), BF16), jax.ShapeDtypeStruct(qs.shape, BF16)],
                  scratch_shapes=[pltpu.VMEM((2, t, LANES), F32)] * 2,
                  compiler_params=_cp(("parallel", "arbitrary"), VMEM_BIG))(qs, ks, vs)


def fox_dout_slots(dout, out, name):
    s, hd = dout.shape
    tr = _tile(s, 256)

    def body(d_ref, o_ref, s_ref):
        lane = _lane_iota()
        for g in range(hd // LANES):
            cols = slice(g * LANES, (g + 1) * LANES)
            d2 = d_ref[:, cols].astype(F32)
            prod = d2 * o_ref[:, cols].astype(F32)
            for a, main in enumerate((d2, pltpu.roll(d2, HEAD_DIM, 1))):
                delta = jnp.sum(jnp.where((lane // HEAD_DIM) == a, prod, 0.0), axis=1, keepdims=True)
                h = 2 * g + a
                s_ref[:, h * LANES:(h + 1) * LANES] = _slot(main, lane, extras=_pieces(-delta)).astype(BF16)

    return _pcall(body, name=name, grid=(s // tr,), in_specs=[_row_spec(tr, hd), _row_spec(tr, hd)],
                  out_specs=_row_spec(tr, 2 * hd), out_shape=jax.ShapeDtypeStruct((s, 2 * hd), BF16),
                  compiler_params=_cp(("parallel",)))(dout, out)


def fox_bwd(qb, ks, vs, dos, n_heads, name):
    s = qb.shape[0]
    pairs = n_heads // 2
    t = min(512, s)
    nblk = s // t
    wide = 2 * LANES

    def body(q_ref, k_ref, v_ref, do_ref, dq_ref, dk_ref, dv_ref, dka_ref, dva_ref):
        j = pl.program_id(1)

        @pl.when(j == 0)
        def _():
            dq_ref[...] = jnp.zeros_like(dq_ref)

        dka_ref[...] = jnp.zeros_like(dka_ref)
        dva_ref[...] = jnp.zeros_like(dva_ref)

        def step(i, diagonal):
            rows = pl.ds(pl.multiple_of(i * t, t), t)
            for a in range(2):
                cols = slice(a * LANES, (a + 1) * LANES)
                qa, doa, ka = q_ref[rows, cols], do_ref[rows, cols], k_ref[:, cols]
                sc = _nt(qa, ka)
                if diagonal:
                    sc = jnp.where(_causal_keep(t), sc, NEG)
                p = jnp.exp(sc)
                ds = (p * _nt(doa, v_ref[:, cols])).astype(BF16)
                dva_ref[a] += _tn(p.astype(BF16), doa)
                dka_ref[a] += _tn(ds, qa)
                dq_ref[rows, cols] += jnp.dot(ds, ka, preferred_element_type=F32)

        step(j, True)

        def loop_body(i, carry):
            step(i, False)
            return carry

        lax.fori_loop(j + 1, nblk, loop_body, 0)
        for a in range(2):
            cols = slice(a * LANES, (a + 1) * LANES)
            dk_ref[:, cols] = dka_ref[a]
            dv_ref[:, cols] = dva_ref[a].astype(BF16)

    whole = pl.BlockSpec((s, wide), lambda p, j: (0, p))
    blk = pl.BlockSpec((t, wide), lambda p, j: (j, p))
    return _pcall(body, name=name, grid=(pairs, nblk), in_specs=[whole, blk, blk, whole], out_specs=[whole, blk, blk],
                  out_shape=[jax.ShapeDtypeStruct(qb.shape, F32), jax.ShapeDtypeStruct(qb.shape, F32),
                             jax.ShapeDtypeStruct(qb.shape, BF16)],
                  scratch_shapes=[pltpu.VMEM((2, t, LANES), F32)] * 2,
                  compiler_params=_cp(("parallel", "arbitrary"), VMEM_BIG))(qb, ks, vs, dos)


def fox_merge_bwd(dqs, dks, dvs, n_heads, name):
    s = dqs.shape[0]
    hd = n_heads * HEAD_DIM
    tr = _tile(s, 128)
    scale = HEAD_DIM ** -0.5

    def body(dq_ref, dk_ref, dv_ref, o_ref, dc_ref):
        lane = _lane_iota()
        dc = jnp.zeros((tr, LANES), F32)
        for g in range(hd // LANES):
            even = slice(2 * g * LANES, (2 * g + 1) * LANES)
            odd = slice((2 * g + 1) * LANES, (2 * g + 2) * LANES)
            for part, (src, mul) in enumerate(((dq_ref, scale), (dk_ref, 1.0), (dv_ref, 1.0))):
                dense = jnp.where(lane < HEAD_DIM, src[:, even].astype(F32), pltpu.roll(src[:, odd].astype(F32), HEAD_DIM, 1))
                o_ref[:, part * hd + g * LANES:part * hd + (g + 1) * LANES] = (dense * mul).astype(BF16)
            for a, cols in enumerate((even, odd)):
                both = jnp.where(lane == EXTRA + N_PIECES, dq_ref[:, cols], 0.0) - jnp.where(lane == EXTRA, dk_ref[:, cols], 0.0)
                dc = jnp.where(lane == 2 * g + a, jnp.sum(both, axis=1, keepdims=True), dc)
        dc_ref[...] = dc

    wide = n_heads * LANES
    return _pcall(body, name=name, grid=(s // tr,), in_specs=[_row_spec(tr, wide)] * 3,
                  out_specs=[_row_spec(tr, 3 * hd), _row_spec(tr, LANES)],
                  out_shape=[jax.ShapeDtypeStruct((s, 3 * hd), BF16), jax.ShapeDtypeStruct((s, LANES), F32)],
                  compiler_params=_cp(("parallel",), VMEM_BIG))(dqs, dks, dvs)


def _w_spec(shape, lead_map=None):
    _, r, c = shape[-3:]
    tr = _tile(r, 128)
    n_lead = len(shape) - 2
    if lead_map is None:
        lead_map = lambda l: (l,)
    return tr, pl.BlockSpec((None,) * n_lead + (tr, c), lambda l, i: (*lead_map(l), i, 0))


def cast_into_slot(w, layer, place, dtype, name):
    tr = _tile(w.shape[1], 128)
    blk = (None, tr, w.shape[2])

    def body(me_ref, c_ref, w_ref, o_ref):
        o_ref[...] = w_ref[...].astype(dtype)

    grid_spec = pltpu.PrefetchScalarGridSpec(
        num_scalar_prefetch=2, grid=(w.shape[1] // tr,),
        in_specs=[pl.BlockSpec(blk, lambda i, me, c: (layer, i, 0))],
        out_specs=pl.BlockSpec(blk, lambda i, me, c: (me[0], i, 0)))
    return _pcall(body, name=name, grid_spec=grid_spec, out_shape=jax.ShapeDtypeStruct((N_CHIPS,) + w.shape[1:], dtype),
                  compiler_params=_cp(("parallel",)))(*place, w)


def adamw(w, g, m, v, name):
    tr, spec = _w_spec(w.shape)

    def body(w_ref, g_ref, m_ref, v_ref, d_ref, mo_ref, vo_ref):
        gg = g_ref[...]
        mn = ADAM_B1 * m_ref[...] + (1.0 - ADAM_B1) * gg
        vn = ADAM_B2 * v_ref[...] + (1.0 - ADAM_B2) * (gg * gg)
        m_hat = mn / (1.0 - ADAM_B1 ** ADAM_STEP)
        v_hat = vn / (1.0 - ADAM_B2 ** ADAM_STEP)
        d_ref[...] = -ADAM_LR * (m_hat / (jnp.sqrt(v_hat) + ADAM_EPS) + ADAM_WD * w_ref[...])
        mo_ref[...] = mn
        vo_ref[...] = vn

    out = jax.ShapeDtypeStruct(w.shape, F32)
    return _pcall(body, name=name, grid=(w.shape[0], w.shape[1] // tr), in_specs=[spec] * 4, out_specs=[spec] * 3,
                  out_shape=[out] * 3, compiler_params=_cp(("parallel", "parallel")))(w, g, m, v)


def add_core_halves(g, recv, place, name):
    hr = recv.shape[1]
    tr = _tile(hr, 128)
    nb = hr // tr
    blk = (None, tr, g.shape[2])

    def body(me_ref, c_ref, g_ref, r_ref, o_ref):
        o_ref[...] = (g_ref[...].astype(F32) + r_ref[...].astype(F32)).astype(o_ref.dtype)

    grid_spec = pltpu.PrefetchScalarGridSpec(
        num_scalar_prefetch=2, grid=(N_CHIPS, nb),
        in_specs=[pl.BlockSpec(blk, lambda q, i, me, c: (q, c[0] * nb + i, 0)), pl.BlockSpec(blk, lambda q, i, me, c: (q, i, 0))],
        out_specs=pl.BlockSpec(blk, lambda q, i, me, c: (q, i, 0)))
    return _pcall(body, name=name, grid_spec=grid_spec, out_shape=jax.ShapeDtypeStruct(recv.shape, g.dtype),
                  compiler_params=_cp(("parallel", "parallel")))(*place, g, recv)


def add_chips(own, recv, into, layer, place, name):
    _, hr, cols = own.shape
    tr = _tile(hr, 128)
    nb = hr // tr
    blk = (None, tr, cols)

    def body(me_ref, c_ref, p0, p1, p2, p3, _, o_ref):
        o_ref[...] = ((p0[...].astype(F32) + p1[...].astype(F32)) + p2[...].astype(F32)) + p3[...].astype(F32)

    def peer(flip):
        return lambda i, me, c: (me[0] ^ flip, i, 0)

    grid_spec = pltpu.PrefetchScalarGridSpec(
        num_scalar_prefetch=2, grid=(nb,), in_specs=[pl.BlockSpec(blk, peer(f)) for f in (0, 2, 1, 3)] + [ANY],
        out_specs=pl.BlockSpec(blk, lambda i, me, c: (layer, c[0] * nb + i, 0)))
    return _pcall(body, name=name, grid_spec=grid_spec, out_shape=jax.ShapeDtypeStruct(into.shape, F32),
                  input_output_aliases={6: 0}, compiler_params=_cp(("parallel",)))(*place, own, recv, recv, recv, into)


def _place():
    x, y, c = lax.axis_index("x"), lax.axis_index("y"), lax.axis_index("c")
    others = [(1 - x, y), (x, 1 - y), (1 - x, 1 - y)]
    return x, y, c, others


def _chip_id(chip):
    return 2 * chip[0] + chip[1]


def _comm_call(body, name, n_in, out_shapes, n_sems, in_place=False):
    return _pcall(body, name=name, in_specs=[ANY] * n_in, out_specs=[ANY] * len(out_shapes), out_shape=out_shapes,
                  scratch_shapes=[pltpu.SemaphoreType.DMA((n_sems,)), pltpu.SemaphoreType.DMA((n_sems,))],
                  input_output_aliases={t: t for t in range(n_in)} if in_place else {},
                  compiler_params=pltpu.CompilerParams(has_side_effects=True))


def swap_core_halves(grads, name):
    n = len(grads)

    def body(*refs):
        ins, outs = refs[:n], refs[n:2 * n]
        send, recv = refs[2 * n:]
        x, y, c, _ = _place()
        cps = []
        for t in range(n):
            hr = ins[t].shape[1] // 2
            cp = pltpu.make_async_remote_copy(src_ref=ins[t].at[:, pl.ds((1 - c) * hr, hr)], dst_ref=outs[t],
                                              send_sem=send.at[t], recv_sem=recv.at[t],
                                              device_id=(x, y, 1 - c), device_id_type=MESH)
            cp.start()
            cps.append(cp)
        for cp in cps:
            cp.wait()

    outs = [jax.ShapeDtypeStruct((a.shape[0], a.shape[1] // 2) + a.shape[2:], a.dtype) for a in grads]
    return _comm_call(body, name, n, outs, n)(*grads)


HBM = pl.BlockSpec(memory_space=pltpu.HBM)
SEM = pl.BlockSpec(memory_space=pltpu.SEMAPHORE)
DATAFLOW = pltpu.SideEffectType.DATAFLOW_SIDE_EFFECTING


def _in_hbm(a):
    return pltpu.with_memory_space_constraint(a, pltpu.HBM)


def gather_start(slots, name):
    n = len(slots)

    def body(*refs):
        bufs = refs[:n]
        send, recv = refs[n], refs[n + 1]
        token = refs[-1]
        x, y, c, others = _place()
        me = _chip_id((x, y))
        for t in range(n):
            for j, chip in enumerate(others):
                pltpu.make_async_remote_copy(src_ref=bufs[t].at[me], dst_ref=bufs[t].at[me],
                                             send_sem=send.at[3 * t + j], recv_sem=recv.at[3 * t + j],
                                             device_id=(*chip, c), device_id_type=MESH).start()
        token[...] = jnp.zeros_like(token)

    sems = pltpu.SemaphoreType.DMA((3 * n,))
    res = _pcall(body, name=name, in_specs=[HBM] * n,
                 out_shape=(sems, sems, *[pltpu.HBM(a.shape, a.dtype) for a in slots], jax.ShapeDtypeStruct((8, LANES), F32)),
                 out_specs=(SEM, SEM, *[HBM] * n, pl.BlockSpec(memory_space=pltpu.VMEM)),
                 input_output_aliases={i: 2 + i for i in range(n)},
                 compiler_params=pltpu.CompilerParams(has_side_effects=DATAFLOW))(*[_in_hbm(a) for a in slots])
    return res[0], res[1], list(res[2:2 + n]), res[-1]


def gather_wait(send, recv, slot, t, after, name):
    def body(buf, send_ref, recv_ref, after_ref, out):
        x, y, c, others = _place()
        me = _chip_id((x, y))
        for j, chip in enumerate(others):
            pltpu.make_async_remote_copy(src_ref=buf.at[me], dst_ref=buf.at[_chip_id(chip)],
                                         send_sem=send_ref.at[3 * t + j], recv_sem=recv_ref.at[3 * t + j],
                                         device_id=(*chip, c), device_id_type=MESH).wait()

    return _pcall(body, name=name, in_specs=[HBM, SEM, SEM, ANY], out_shape=pltpu.HBM(slot.shape, slot.dtype),
                  out_specs=HBM, input_output_aliases={0: 0},
                  compiler_params=pltpu.CompilerParams(has_side_effects=DATAFLOW))(slot, send, recv, after)


def scatter_start(sums, name):
    n = len(sums)

    def body(*refs):
        ins, lands = refs[:n], refs[n:2 * n]
        send, recv = refs[2 * n], refs[2 * n + 1]
        token = refs[-1]
        x, y, c, others = _place()
        me = _chip_id((x, y))
        for t in range(n):
            for j, chip in enumerate(others):
                pltpu.make_async_remote_copy(src_ref=ins[t].at[_chip_id(chip)], dst_ref=lands[t].at[me],
                                             send_sem=send.at[3 * t + j], recv_sem=recv.at[3 * t + j],
                                             device_id=(*chip, c), device_id_type=MESH).start()
        token[...] = jnp.zeros_like(token)

    bufs = [pltpu.HBM(a.shape, a.dtype) for a in sums]
    sems = pltpu.SemaphoreType.DMA((3 * n,))
    res = _pcall(body, name=name, in_specs=[HBM] * (2 * n),
                 out_shape=(sems, sems, *bufs, *bufs, jax.ShapeDtypeStruct((8, LANES), F32)),
                 out_specs=(SEM, SEM, *[HBM] * (2 * n), pl.BlockSpec(memory_space=pltpu.VMEM)),
                 input_output_aliases={i: 2 + i for i in range(2 * n)},
                 compiler_params=pltpu.CompilerParams(has_side_effects=DATAFLOW))(
        *[_in_hbm(a) for a in sums], *[_in_hbm(lax.empty(a.shape, a.dtype)) for a in sums])
    return res[0], res[1], list(res[2:2 + n]), list(res[2 + n:2 + 2 * n]), res[-1]


def scatter_wait(send, recv, sums, lands, after, name):
    n = len(sums)

    def body(*refs):
        ins, bufs = refs[:n], refs[n:2 * n]
        send_ref, recv_ref = refs[2 * n], refs[2 * n + 1]
        x, y, c, others = _place()
        me = _chip_id((x, y))
        for t in range(n):
            for j, chip in enumerate(others):
                cp = pltpu.make_async_remote_copy(src_ref=ins[t].at[_chip_id(chip)], dst_ref=bufs[t].at[_chip_id(chip)],
                                                  send_sem=send_ref.at[3 * t + j], recv_sem=recv_ref.at[3 * t + j],
                                                  device_id=(*chip, c), device_id_type=MESH)
                cp.wait_send()
                cp.wait_recv()

    shapes = [pltpu.HBM(a.shape, a.dtype) for a in sums]
    res = _pcall(body, name=name, in_specs=[HBM] * (2 * n) + [SEM, SEM, ANY],
                 out_shape=(*shapes, *shapes), out_specs=tuple([HBM] * (2 * n)),
                 input_output_aliases={i: i for i in range(2 * n)},
                 compiler_params=pltpu.CompilerParams(has_side_effects=DATAFLOW))(*sums, *lands, send, recv, after)
    return list(res[:n]), list(res[n:])


def join_core_halves(fulls, name):
    n = len(fulls)

    def body(*refs):
        bufs = refs[n:2 * n]
        send, recv = refs[2 * n:]
        x, y, c, _ = _place()
        cps = []
        for t in range(n):
            hr = bufs[t].shape[1] // 2
            mine = bufs[t].at[:, pl.ds(c * hr, hr)]
            cp = pltpu.make_async_remote_copy(src_ref=mine, dst_ref=mine, send_sem=send.at[t], recv_sem=recv.at[t],
                                              device_id=(x, y, 1 - c), device_id_type=MESH)
            cp.start()
            cps.append(cp)
        for t in range(n):
            hr = bufs[t].shape[1] // 2
            theirs = bufs[t].at[:, pl.ds((1 - c) * hr, hr)]
            cps[t].wait_send()
            pltpu.make_async_remote_copy(src_ref=theirs, dst_ref=theirs, send_sem=send.at[t], recv_sem=recv.at[t],
                                         device_id=(x, y, c), device_id_type=MESH).wait_recv()

    outs = [jax.ShapeDtypeStruct(a.shape, a.dtype) for a in fulls]
    return _comm_call(body, name, n, outs, n, in_place=True)(*fulls)


def kernel(x, positions, norm_gains, swa_w_in, swa_sinks, swa_w_out, fox_w_in, fox_b_f, fox_w_out, ffn_w_gate_up, ffn_w_down, loss_target, m_norm_gains, m_swa_w_in, m_swa_sinks, m_swa_w_out, m_fox_w_in, m_fox_b_f, m_fox_w_out, m_ffn_w_gate_up, m_ffn_w_down, v_norm_gains, v_swa_w_in, v_swa_sinks, v_swa_w_out, v_fox_w_in, v_fox_b_f, v_fox_w_out, v_ffn_w_gate_up, v_ffn_w_down):
    s, d = x.shape[1], x.shape[2]
    depth = norm_gains.shape[0]
    n_heads = d // HEAD_DIM
    hd = n_heads * HEAD_DIM
    n_kv = (swa_w_in.shape[2] * N_CHIPS // HEAD_DIM - n_heads) // 2
    ff = ffn_w_down.shape[1] * N_CHIPS
    fox_cols = fox_w_in.shape[2]
    fox_pad = 3 * hd + LANES
    assert fox_cols * N_CHIPS == 3 * hd + n_heads and n_heads <= LANES
    x0 = x[0]
    target = loss_target[0]
    place = ((2 * lax.axis_index("x") + lax.axis_index("y")).astype(jnp.int32).reshape(1),
             lax.axis_index("c").astype(jnp.int32).reshape(1))

    order = [("norm_gains", norm_gains.reshape(1, depth * 4, norm_gains.shape[2]), 0, F32)]
    for layer in range(depth):
        j = layer // 2
        kind, w_i, w_o = ("swa", swa_w_in, swa_w_out) if layer % 2 == 0 else ("fox", fox_w_in, fox_w_out)
        order += [(f"{kind}_w_in_{j}", w_i, j, BF16), (f"{kind}_w_out_{j}", w_o, j, BF16),
                  (f"ffn_w_gate_up_{layer}", ffn_w_gate_up, layer, BF16), (f"ffn_w_down_{layer}", ffn_w_down, layer, BF16)]
    slots = [cast_into_slot(w, l, place, dt, "cast_" + nm) for nm, w, l, dt in order]
    g_send, g_recv, slots, g_token = gather_start(slots, "gather_start")
    slot_of = {entry[0]: t for t, entry in enumerate(order)}

    def weight(nm, after):
        t = slot_of[nm]
        return gather_wait(g_send, g_recv, slots[t], t, after, "gather_wait_" + nm)

    gains = jnp.transpose(weight("norm_gains", g_token), (1, 0, 2)).reshape(depth * 4, d)

    def gain(layer, which):
        return gains[layer * 4 + which][None, :]

    def fox_weight(w_in):
        parts = [w_in[q] for q in range(N_CHIPS)]
        parts.append(jnp.zeros((d, fox_pad - fox_cols * N_CHIPS), BF16))
        return jnp.concatenate(parts, axis=1)

    inv_freq = ROPE_THETA ** (-jnp.arange(0, ROT_DIM, 2, dtype=F32) / ROT_DIM)
    lane_d = jnp.arange(LANES) % HEAD_DIM
    invf_row = jnp.where(lane_d < ROT_DIM, inv_freq[lane_d % (ROT_DIM // 2)], 0.0)[None, :]
    tabs = rope_tables(positions.reshape(s, 1), invf_row, "rope_tables")

    n_sh_in = swa_w_in.shape[2]
    gu_sh = ffn_w_gate_up.shape[2]
    tn_gu = gu_sh // 2 if (gu_sh // 2) % LANES == 0 else gu_sh
    down_sh = ffn_w_down.shape[1]
    out_sh = swa_w_out.shape[1]
    tm = min(1024, s)

    saved = []
    xin = x0
    h = prenorm(xin, gain(0, 0), "prenorm_first")
    for layer in range(depth):
        j = layer // 2
        rec = {"x_in": xin, "h1": h}
        if layer % 2 == 0:
            w_in = weight(f"swa_w_in_{j}", h)
            proj = mm_nn(h, w_in, lambda n, k: (n, k, 0), n_sh_in * N_CHIPS, tm=min(512, s), tn=n_sh_in, tk=d,
                         out_dtype=F32, name=f"swa_proj_{j}")
            q, kd, vd = swa_split(proj, tabs, n_heads, n_kv, f"swa_split_{j}")
            sink_row = jnp.repeat(swa_sinks[j], HEAD_DIM)[None, :]
            attn, lse = swa_fwd(q, kd, vd, sink_row, f"swa_fwd_{j}")
            rec.update(q=q, kd=kd, vd=vd, sink_row=sink_row, lse=lse, w_in=w_in)
            w_out = weight(f"swa_w_out_{j}", attn)
        else:
            wf = fox_weight(weight(f"fox_w_in_{j}", h))
            tn = 3 * hd // 6 if (3 * hd // 6) % LANES == 0 else LANES
            qkv = mm_nn(h, wf, lambda n, k: (k, n), 3 * hd, tm=tm, tn=tn, tk=d, out_dtype=BF16, name=f"fox_proj_{j}")
            f_off = 3 * hd // LANES
            f_logit = mm_nn(h, wf, lambda n, k: (k, f_off + n), LANES, tm=tm, tn=LANES, tk=d, out_dtype=F32, name=f"fox_gate_{j}")
            b_row = jnp.pad(fox_b_f[j], (0, LANES - n_heads))[None, :]
            c = forget_cumsum(f_logit, b_row, f"fox_cumsum_{j}")
            qs, ks, vs = fox_split(qkv, c, n_heads, f"fox_split_{j}")
            attn, qb = fox_fwd(qs, ks, vs, n_heads, f"fox_fwd_{j}")
            rec.update(wf=wf, f_logit=f_logit, b_row=b_row, qb=qb, ks=ks, vs=vs)
            w_out = weight(f"fox_w_out_{j}", attn)
        y = mm_nn(attn, w_out, lambda n, k: (k, 0, n), d, tm=tm, tn=min(1024, d), tk=out_sh, out_dtype=F32, name=f"out_proj_{layer}")
        xmid, h2 = postnorm_residual(xin, y, gain(layer, 1), gain(layer, 2), f"postnorm_mixer_{layer}")
        npb = gu_sh // tn_gu
        w_gu = weight(f"ffn_w_gate_up_{layer}", h2)
        gu = mm_nn(h2, w_gu, lambda n, k: (n // npb, k, n % npb), 2 * ff, tm=tm, tn=tn_gu, tk=d, out_dtype=BF16, name=f"ffn_up_{layer}")
        act = swiglu_fwd(gu, f"swiglu_{layer}")
        w_down = weight(f"ffn_w_down_{layer}", act)
        y2 = mm_nn(act, w_down, lambda n, k: (k, 0, n), d, tm=tm, tn=min(1024, d), tk=down_sh, out_dtype=F32, name=f"ffn_down_{layer}")
        rec.update(attn=attn, y=y, x_mid=xmid, h2=h2, gu=gu, act=act, y2=y2, w_out=w_out, w_gu=w_gu, w_down=w_down)
        saved.append(rec)
        if layer + 1 < depth:
            xin, h = postnorm_residual(xmid, y2, gain(layer, 3), gain(layer + 1, 0), f"postnorm_ffn_{layer}")
    dx, loss_blk = postnorm_loss(xmid, y2, gain(depth - 1, 3), target, "loss")

    finals = {nm: lax.empty(w.shape, F32) for nm, w in (("swa_w_in", swa_w_in), ("swa_w_out", swa_w_out), ("fox_w_in", fox_w_in),
                                                        ("fox_w_out", fox_w_out), ("ffn_w_gate_up", ffn_w_gate_up),
                                                        ("ffn_w_down", ffn_w_down), ("norm_gains", order[0][1]))}
    in_flight = []

    def finish_reduce(after):
        send_, recv_, sums_, lands_, keys = in_flight.pop()
        sums_, got = scatter_wait(send_, recv_, sums_, lands_, after, "reduce_scatter_wait_" + keys[0][2])
        for own, rcv, (key, l, nm) in zip(sums_, got, keys):
            finals[key] = add_chips(own, rcv, finals[key], l, place, "reduce_add_chips_" + nm)

    def start_reduce(partials, keys):
        if in_flight:
            finish_reduce(partials[0])
        halves = swap_core_halves(partials, "reduce_swap_" + keys[0][2])
        sums_ = [add_core_halves(g, r, place, "reduce_add_cores_" + k[2]) for g, r, k in zip(partials, halves, keys)]
        send_, recv_, sums_, lands_, _ = scatter_start(sums_, "reduce_scatter_start_" + keys[0][2])
        in_flight.append((send_, recv_, sums_, lands_, keys))

    dgains = [None] * (depth * 4)
    dsinks = [None] * ((depth + 1) // 2)
    dbf = [None] * (depth // 2)
    tko = min(1024, d)
    for layer in reversed(range(depth)):
        j = layer // 2
        rec = saved[layer]
        dy2, dgains[layer * 4 + 3] = postnorm_bwd(dx, rec["y2"], gain(layer, 3), f"postnorm_ffn_bwd_{layer}")
        tnd = min(1024, d)
        g_down = mm_tn(rec["act"], dy2, lambda i, n: (i, 0, n), tka=down_sh, tn=tnd, tm=min(512, s), out_block=(None, down_sh, tnd),
                       name=f"ffn_down_dw_{layer}", out_shape=jax.ShapeDtypeStruct((N_CHIPS, down_sh, d), BF16))
        dact = mm_nt(dy2, rec["w_down"], lambda o, n: (o, 0, n), ff, tm=tm, tko=down_sh, tn=d, out_dtype=BF16, name=f"ffn_down_dx_{layer}")
        dgu = swiglu_bwd(dact, rec["gu"], f"swiglu_bwd_{layer}")
        npb = gu_sh // tn_gu
        g_gu = mm_tn(rec["h2"], dgu, lambda i, n: (n // npb, i, n % npb), tka=tko, tn=tn_gu, tm=min(512, s), out_block=(None, tko, tn_gu),
                     name=f"ffn_up_dw_{layer}", out_shape=jax.ShapeDtypeStruct((N_CHIPS, d, gu_sh), BF16))
        start_reduce([g_gu, g_down], [("ffn_w_gate_up", layer, f"ffn_w_gate_up_{layer}"), ("ffn_w_down", layer, f"ffn_w_down_{layer}")])
        dh2 = mm_nt(dgu, rec["w_gu"], lambda o, n: (n // npb, o, n % npb), d, tm=tm, tko=tko, tn=tn_gu, out_dtype=F32, name=f"ffn_up_dx_{layer}")
        dxm, dgains[layer * 4 + 2] = prenorm_bwd(rec["x_mid"], dh2, dx, gain(layer, 2), f"prenorm_ffn_bwd_{layer}")
        dy, dgains[layer * 4 + 1] = postnorm_bwd(dxm, rec["y"], gain(layer, 1), f"postnorm_mixer_bwd_{layer}")
        g_out = mm_tn(rec["attn"], dy, lambda i, n: (i, 0, n), tka=out_sh, tn=tnd, tm=min(512, s), out_block=(None, out_sh, tnd),
                      name=f"out_proj_dw_{layer}", out_shape=jax.ShapeDtypeStruct((N_CHIPS, out_sh, d), BF16))
        dattn = mm_nt(dy, rec["w_out"], lambda o, n: (o, 0, n), hd, tm=tm, tko=out_sh, tn=d, out_dtype=BF16, name=f"out_proj_dx_{layer}")
        if layer % 2 == 0:
            dq, dkd, dvd, dsk = swa_bwd(rec["q"], rec["kd"], rec["vd"], rec["sink_row"], rec["attn"], rec["lse"], dattn, f"swa_bwd_{j}")
            dsinks[j] = dsk[0].reshape(n_heads, HEAD_DIM)[:, 0]
            dproj = swa_merge_bwd(dq, dkd, dvd, tabs, f"swa_merge_bwd_{j}")
            g_in = mm_tn(rec["h1"], dproj, lambda i, n: (n, i, 0), tka=tko, tn=n_sh_in, tm=min(512, s), out_block=(None, tko, n_sh_in),
                         name=f"swa_proj_dw_{j}", out_shape=jax.ShapeDtypeStruct((N_CHIPS, d, n_sh_in), BF16))
            start_reduce([g_in, g_out], [("swa_w_in", j, f"swa_w_in_{j}"), ("swa_w_out", j, f"swa_w_out_{j}")])
            dh1 = mm_nt(dproj, rec["w_in"], lambda o, n: (n, o, 0), d, tm=tm, tko=tko, tn=n_sh_in, out_dtype=F32, name=f"swa_proj_dx_{j}")
        else:
            dos = fox_dout_slots(dattn, rec["attn"], f"fox_dout_slots_{j}")
            dqs, dks, dvs = fox_bwd(rec["qb"], rec["ks"], rec["vs"], dos, n_heads, f"fox_bwd_{j}")
            dqkv, dc = fox_merge_bwd(dqs, dks, dvs, n_heads, f"fox_merge_bwd_{j}")
            df, db = forget_gate_bwd(dc, rec["f_logit"], rec["b_row"], f"fox_gate_bwd_{j}")
            dbf[j] = db[0, :n_heads]
            dproj = jnp.concatenate([dqkv, df], axis=1)
            tn_f = LANES * max(k for k in range(1, 9) if (fox_pad // LANES) % k == 0)
            dwf = mm_tn(rec["h1"], dproj, lambda i, n: (i, n), tka=tko, tn=tn_f, tm=min(512, s), out_block=(tko, tn_f),
                        name=f"fox_proj_dw_{j}", out_shape=jax.ShapeDtypeStruct((d, fox_pad), BF16))
            g_in = jnp.stack([dwf[:, q * fox_cols:(q + 1) * fox_cols] for q in range(N_CHIPS)])
            start_reduce([g_in, g_out], [("fox_w_in", j, f"fox_w_in_{j}"), ("fox_w_out", j, f"fox_w_out_{j}")])
            dh1 = mm_nt(dproj, rec["wf"], lambda o, n: (o, n), d, tm=tm, tko=tko, tn=tn_f, out_dtype=F32, name=f"fox_proj_dx_{j}")
        dx, dgains[layer * 4] = prenorm_bwd(rec["x_in"], dh1, dxm, gain(layer, 0), f"prenorm_mixer_bwd_{layer}")
    grad_x = dx[None]

    dgain_full = jnp.concatenate(dgains, axis=0)
    start_reduce([jnp.transpose(dgain_full.reshape(depth * 4, N_CHIPS, -1), (1, 0, 2))], [("norm_gains", 0, "norm_gains")])
    finish_reduce(dx)
    keys = ["swa_w_in", "swa_w_out", "fox_w_in", "fox_w_out", "ffn_w_gate_up", "ffn_w_down", "norm_gains"]
    full = join_core_halves([finals[k] for k in keys], "reduce_join_cores")
    g_swa_in_f, g_swa_out_f, g_fox_in_f, g_fox_out_f, g_gu_f, g_down_f, g_gains_f = full
    g_gains_f = g_gains_f.reshape(norm_gains.shape)

    n_swa, n_fox = len(dsinks), len(dbf)
    small = jnp.concatenate([loss_blk[0, :1]] + dsinks + dbf)
    small = lax.psum(small, ("x", "y", "c"))
    loss = small[0]
    g_sinks = small[1:1 + n_swa * n_heads].reshape(n_swa, n_heads)
    g_bf = small[1 + n_swa * n_heads:].reshape(n_fox, n_heads)

    def pad_small(a):
        return jnp.pad(a, ((0, 8 - a.shape[0]), (0, LANES - a.shape[1])))[None]

    def update(w, g, m, v, nm):
        if w.ndim == 2:
            dl, mn, vn = adamw(pad_small(w), pad_small(g), pad_small(m), pad_small(v), "adamw_" + nm)
            return tuple(a[0, :w.shape[0], :w.shape[1]] for a in (dl, mn, vn))
        return adamw(w, g, m, v, "adamw_" + nm)

    grads = [g_gains_f, g_swa_in_f, g_sinks, g_swa_out_f, g_fox_in_f, g_bf, g_fox_out_f, g_gu_f, g_down_f]
    ws = [norm_gains, swa_w_in, swa_sinks, swa_w_out, fox_w_in, fox_b_f, fox_w_out, ffn_w_gate_up, ffn_w_down]
    ms = [m_norm_gains, m_swa_w_in, m_swa_sinks, m_swa_w_out, m_fox_w_in, m_fox_b_f, m_fox_w_out, m_ffn_w_gate_up, m_ffn_w_down]
    vs = [v_norm_gains, v_swa_w_in, v_swa_sinks, v_swa_w_out, v_fox_w_in, v_fox_b_f, v_fox_w_out, v_ffn_w_gate_up, v_ffn_w_down]
    nms = ["norm_gains", "swa_w_in", "swa_sinks", "swa_w_out", "fox_w_in", "fox_b_f", "fox_w_out", "ffn_w_gate_up", "ffn_w_down"]
    upd = [update(w, g, m, v, nm) for w, g, m, v, nm in zip(ws, grads, ms, vs, nms)]
    return (loss, grad_x, *grads, *[u[0] for u in upd], *[u[1] for u in upd], *[u[2] for u in upd])
```

```python
import jax
import jax.numpy as jnp
from jax import lax
from jax.experimental import pallas as pl
from jax.experimental.pallas import tpu as pltpu

F32 = jnp.float32
BF16 = jnp.bfloat16
HEAD_DIM = 64
LANES = 128
WINDOW = 128
ROPE_THETA = 500000.0
ROT_DIM = HEAD_DIM // 4
RMS_EPS = 1e-6
ADAM_LR, ADAM_B1, ADAM_B2, ADAM_EPS, ADAM_WD, ADAM_STEP = 0.001, 0.9, 0.999, 1e-08, 0.01, 10
NEG = -1e30
VMEM_BIG = 56 * 1024 * 1024
N_CHIPS = 4
MESH = pl.DeviceIdType.MESH
ANY = pl.BlockSpec(memory_space=pl.ANY)


def _pcall(body, **kw):
    return pl.pallas_call(body, **kw)


def _cp(sem=None, vmem=None):
    return pltpu.CompilerParams(dimension_semantics=sem, vmem_limit_bytes=vmem)


def _tile(n, pref):
    if n <= pref:
        return n
    t = pref - pref % 16
    while n % t:
        t -= 16
    return t


_DIMS = {"nn": (((1,), (0,)), ((), ())), "nt": (((1,), (1,)), ((), ())), "tn": (((0,), (0,)), ((), ()))}


def _matmul(kind, a, b, *, grid, a_spec, b_spec, o_spec, out_shape, acc_shape, name, after=None):
    nk = grid[2]
    dims = _DIMS[kind]

    def body(a_ref, b_ref, *rest):
        rest = rest[1:] if after is not None else rest
        o_ref = rest[0]

        def prod():
            return lax.dot_general(a_ref[...], b_ref[...], dims, preferred_element_type=F32)

        if nk == 1:
            o_ref[...] = prod().astype(o_ref.dtype)
        else:
            acc = rest[1]
            k = pl.program_id(2)

            @pl.when(k == 0)
            def _():
                acc[...] = prod()

            @pl.when(k > 0)
            def _():
                acc[...] += prod()

            @pl.when(k == nk - 1)
            def _():
                o_ref[...] = acc[...].astype(o_ref.dtype)

    args, in_specs = [a, b], [a_spec, b_spec]
    if after is not None:
        args.append(after)
        in_specs.append(ANY)
    return _pcall(body, name=name, grid=grid, in_specs=in_specs, out_specs=o_spec, out_shape=out_shape,
                  scratch_shapes=[] if nk == 1 else [pltpu.VMEM(acc_shape, F32)],
                  compiler_params=_cp(("parallel", "parallel", "arbitrary"), VMEM_BIG))(*args)


def mm_nn(a, w, w_map, n_out, *, tm, tn, tk, out_dtype, name):
    m, kdim = a.shape
    tm = min(tm, m)
    lead = (None,) * (w.ndim - 2)
    return _matmul("nn", a, w, grid=(m // tm, n_out // tn, kdim // tk),
                   a_spec=pl.BlockSpec((tm, tk), lambda i, j, k: (i, k)),
                   b_spec=pl.BlockSpec(lead + (tk, tn), lambda i, j, k: w_map(j, k)),
                   o_spec=pl.BlockSpec((tm, tn), lambda i, j, k: (i, j)),
                   out_shape=jax.ShapeDtypeStruct((m, n_out), out_dtype), acc_shape=(tm, tn), name=name)


def mm_nt(a, w, w_map, k_out, *, tm, tko, tn, out_dtype, name, after=None):
    m, ndim = a.shape
    tm = min(tm, m)
    lead = (None,) * (w.ndim - 2)
    return _matmul("nt", a, w, grid=(m // tm, k_out // tko, ndim // tn),
                   a_spec=pl.BlockSpec((tm, tn), lambda i, j, n: (i, n)),
                   b_spec=pl.BlockSpec(lead + (tko, tn), lambda i, j, n: w_map(j, n)),
                   o_spec=pl.BlockSpec((tm, tko), lambda i, j, n: (i, j)),
                   out_shape=jax.ShapeDtypeStruct((m, k_out), out_dtype), acc_shape=(tm, tko), name=name, after=after)


def mm_tn(a, b, o_map, *, tka, tn, tm, out_block, name, out_shape):
    m, kdim = a.shape
    n = b.shape[1]
    tm = min(tm, m)
    return _matmul("tn", a, b, grid=(kdim // tka, n // tn, m // tm),
                   a_spec=pl.BlockSpec((tm, tka), lambda i, j, mm: (mm, i)),
                   b_spec=pl.BlockSpec((tm, tn), lambda i, j, mm: (mm, j)),
                   o_spec=pl.BlockSpec(out_block, lambda i, j, mm: o_map(i, j)),
                   out_shape=out_shape, acc_shape=(tka, tn), name=name)


def _rstd(v):
    return lax.rsqrt(jnp.mean(v * v, axis=-1, keepdims=True) + RMS_EPS)


def _row_spec(tr, d):
    return pl.BlockSpec((tr, d), lambda i: (i, 0))


def _vec_spec(d):
    return pl.BlockSpec((1, d), lambda i: (0, 0))


def prenorm(x, g, name):
    s, d = x.shape
    tr = _tile(s, 256)

    def body(x_ref, g_ref, h_ref):
        v = x_ref[...]
        h_ref[...] = (v * _rstd(v) * g_ref[...]).astype(BF16)

    return _pcall(body, name=name, grid=(s // tr,), in_specs=[_row_spec(tr, d), _vec_spec(d)],
                  out_specs=_row_spec(tr, d), out_shape=jax.ShapeDtypeStruct((s, d), BF16),
                  compiler_params=_cp(("parallel",)))(x, g)


def postnorm_residual(x, y, g_post, g_next, name):
    s, d = x.shape
    tr = _tile(s, 256)

    def body(x_ref, y_ref, gp_ref, gn_ref, xo_ref, h_ref):
        v = y_ref[...]
        xn = x_ref[...] + v * _rstd(v) * gp_ref[...]
        xo_ref[...] = xn
        h_ref[...] = (xn * _rstd(xn) * gn_ref[...]).astype(BF16)

    return _pcall(body, name=name, grid=(s // tr,),
                  in_specs=[_row_spec(tr, d), _row_spec(tr, d), _vec_spec(d), _vec_spec(d)],
                  out_specs=[_row_spec(tr, d), _row_spec(tr, d)],
                  out_shape=[jax.ShapeDtypeStruct((s, d), F32), jax.ShapeDtypeStruct((s, d), BF16)],
                  compiler_params=_cp(("parallel",)))(x, y, g_post, g_next)


def postnorm_loss(x, y, g_post, target, name):
    s, d = x.shape
    tr = _tile(s, 256)

    def body(x_ref, y_ref, gp_ref, t_ref, dx_ref, loss_ref):
        v = y_ref[...]
        err = x_ref[...] + v * _rstd(v) * gp_ref[...] - t_ref[...]
        dx_ref[...] = err / d
        part = 0.5 * jnp.sum(jnp.mean(err * err, axis=-1, keepdims=True), axis=0, keepdims=True)

        @pl.when(pl.program_id(0) == 0)
        def _():
            loss_ref[...] = jnp.zeros_like(loss_ref)

        loss_ref[...] += part

    return _pcall(body, name=name, grid=(s // tr,),
                  in_specs=[_row_spec(tr, d), _row_spec(tr, d), _vec_spec(d), _row_spec(tr, d)],
                  out_specs=[_row_spec(tr, d), pl.BlockSpec((8, LANES), lambda i: (0, 0))],
                  out_shape=[jax.ShapeDtypeStruct((s, d), F32), jax.ShapeDtypeStruct((8, LANES), F32)],
                  compiler_params=_cp(("arbitrary",)))(x, y, g_post, target)


def _norm_bwd(v, g, dz):
    r = _rstd(v)
    vhat = v * r
    u = dz * g
    dv = r * (u - vhat * jnp.mean(u * vhat, axis=-1, keepdims=True))
    return dv, jnp.sum(dz * vhat, axis=0, keepdims=True)


def _acc_rows(ref, val):
    @pl.when(pl.program_id(0) == 0)
    def _():
        ref[...] = jnp.zeros_like(ref)

    ref[...] += val


def postnorm_bwd(dz, y, g, name):
    s, d = y.shape
    tr = _tile(s, 256)

    def body(dz_ref, y_ref, g_ref, dy_ref, dg_ref):
        dv, dg = _norm_bwd(y_ref[...], g_ref[...], dz_ref[...])
        dy_ref[...] = dv.astype(BF16)
        _acc_rows(dg_ref, dg)

    return _pcall(body, name=name, grid=(s // tr,), in_specs=[_row_spec(tr, d), _row_spec(tr, d), _vec_spec(d)],
                  out_specs=[_row_spec(tr, d), _vec_spec(d)],
                  out_shape=[jax.ShapeDtypeStruct((s, d), BF16), jax.ShapeDtypeStruct((1, d), F32)],
                  compiler_params=_cp(("arbitrary",)))(dz, y, g)


def prenorm_bwd(x, dh, dskip, g, name):
    s, d = x.shape
    tr = _tile(s, 256)

    def body(x_ref, dh_ref, ds_ref, g_ref, dx_ref, dg_ref):
        dv, dg = _norm_bwd(x_ref[...], g_ref[...], dh_ref[...])
        dx_ref[...] = ds_ref[...] + dv
        _acc_rows(dg_ref, dg)

    return _pcall(body, name=name, grid=(s // tr,),
                  in_specs=[_row_spec(tr, d), _row_spec(tr, d), _row_spec(tr, d), _vec_spec(d)],
                  out_specs=[_row_spec(tr, d), _vec_spec(d)],
                  out_shape=[jax.ShapeDtypeStruct((s, d), F32), jax.ShapeDtypeStruct((1, d), F32)],
                  compiler_params=_cp(("arbitrary",)))(x, dh, dskip, g)


def swiglu_fwd(gu, name):
    s, two_ff = gu.shape
    ff = two_ff // 2
    tr = _tile(s, 128)

    def body(gu_ref, act_ref):
        gate = gu_ref[:, :ff].astype(F32)
        up = gu_ref[:, ff:].astype(F32)
        act_ref[...] = (gate * jax.nn.sigmoid(gate) * up).astype(BF16)

    return _pcall(body, name=name, grid=(s // tr,), in_specs=[_row_spec(tr, two_ff)], out_specs=_row_spec(tr, ff),
                  out_shape=jax.ShapeDtypeStruct((s, ff), BF16), compiler_params=_cp(("parallel",)))(gu)


def swiglu_bwd(dact, gu, name):
    s, two_ff = gu.shape
    ff = two_ff // 2
    tr = _tile(s, 128)

    def body(da_ref, gu_ref, dgu_ref):
        gate = gu_ref[:, :ff].astype(F32)
        up = gu_ref[:, ff:].astype(F32)
        da = da_ref[...].astype(F32)
        sig = jax.nn.sigmoid(gate)
        dgu_ref[:, :ff] = (da * up * sig * (1.0 + gate * (1.0 - sig))).astype(BF16)
        dgu_ref[:, ff:] = (da * gate * sig).astype(BF16)

    return _pcall(body, name=name, grid=(s // tr,), in_specs=[_row_spec(tr, ff), _row_spec(tr, two_ff)],
                  out_specs=_row_spec(tr, two_ff), out_shape=jax.ShapeDtypeStruct((s, two_ff), BF16),
                  compiler_params=_cp(("parallel",)))(dact, gu)


def rope_tables(pos_col, invf_row, name):
    s = pos_col.shape[0]
    tr = _tile(s, 1024)

    def body(p_ref, f_ref, c_ref, sa_ref, sb_ref):
        ang = p_ref[...].astype(F32) * f_ref[...]
        d = lax.broadcasted_iota(jnp.int32, (1, LANES), 1) % HEAD_DIM
        cs, sn = jnp.cos(ang), jnp.sin(ang)
        c_ref[...] = jnp.where(d < ROT_DIM, cs, 1.0)
        sa_ref[...] = jnp.where(d < ROT_DIM // 2, -sn, 0.0)
        sb_ref[...] = jnp.where((d >= ROT_DIM // 2) & (d < ROT_DIM), sn, 0.0)

    tab = jax.ShapeDtypeStruct((s, LANES), F32)
    return _pcall(body, name=name, grid=(s // tr,),
                  in_specs=[pl.BlockSpec((tr, 1), lambda i: (i, 0)), _vec_spec(LANES)],
                  out_specs=[_row_spec(tr, LANES)] * 3, out_shape=[tab] * 3, compiler_params=_cp(("parallel",)))(pos_col, invf_row)


def _rot(t, c, sa, sb):
    half = ROT_DIM // 2
    return t * c + pltpu.roll(t, LANES - half, 1) * sa + pltpu.roll(t, half, 1) * sb


def _rot_t(t, c, sa, sb):
    half = ROT_DIM // 2
    return t * c + pltpu.roll(t * sa, half, 1) + pltpu.roll(t * sb, LANES - half, 1)


def swa_split(proj, tabs, n_heads, n_kv, name):
    s, width = proj.shape
    qd, kd = n_heads * HEAD_DIM, n_kv * HEAD_DIM
    tr = _tile(s, 256)

    def body(p_ref, c_ref, sa_ref, sb_ref, q_ref, k_ref, v_ref):
        c, sa, sb = c_ref[...], sa_ref[...], sb_ref[...]
        low = lax.broadcasted_iota(jnp.int32, (1, LANES), 1) < HEAD_DIM
        for g in range(qd // LANES):
            q_ref[:, g * LANES:(g + 1) * LANES] = _rot(p_ref[:, g * LANES:(g + 1) * LANES], c, sa, sb).astype(BF16)
        for g in range(kd // LANES):
            for src, dst, rot in ((qd, k_ref, True), (qd + kd, v_ref, False)):
                t = p_ref[:, src + g * LANES:src + (g + 1) * LANES]
                t = _rot(t, c, sa, sb) if rot else t
                sw = pltpu.roll(t, HEAD_DIM, 1)
                dst[:, (2 * g) * LANES:(2 * g + 1) * LANES] = jnp.where(low, t, sw).astype(BF16)
                dst[:, (2 * g + 1) * LANES:(2 * g + 2) * LANES] = jnp.where(low, sw, t).astype(BF16)

    return _pcall(body, name=name, grid=(s // tr,),
                  in_specs=[_row_spec(tr, width)] + [_row_spec(tr, LANES)] * 3,
                  out_specs=[_row_spec(tr, qd), _row_spec(tr, 2 * kd), _row_spec(tr, 2 * kd)],
                  out_shape=[jax.ShapeDtypeStruct((s, qd), BF16), jax.ShapeDtypeStruct((s, 2 * kd), BF16),
                             jax.ShapeDtypeStruct((s, 2 * kd), BF16)],
                  compiler_params=_cp(("parallel",)))(proj, *tabs)


def swa_merge_bwd(dq, dkd, dvd, tabs, name):
    s, qd = dq.shape
    kd = dkd.shape[1] // 2
    width = qd + 2 * kd
    tr = _tile(s, 256)

    def body(dq_ref, dk_ref, dv_ref, c_ref, sa_ref, sb_ref, o_ref):
        c, sa, sb = c_ref[...], sa_ref[...], sb_ref[...]
        low = lax.broadcasted_iota(jnp.int32, (1, LANES), 1) < HEAD_DIM
        for g in range(qd // LANES):
            t = dq_ref[:, g * LANES:(g + 1) * LANES].astype(F32)
            o_ref[:, g * LANES:(g + 1) * LANES] = _rot_t(t, c, sa, sb).astype(BF16)
        for g in range(kd // LANES):
            for dst, src, rot in ((qd, dk_ref, True), (qd + kd, dv_ref, False)):
                e = src[:, (2 * g) * LANES:(2 * g + 1) * LANES]
                o = src[:, (2 * g + 1) * LANES:(2 * g + 2) * LANES]
                t = jnp.where(low, e + pltpu.roll(e, HEAD_DIM, 1), o + pltpu.roll(o, HEAD_DIM, 1))
                t = _rot_t(t, c, sa, sb) if rot else t
                o_ref[:, dst + g * LANES:dst + (g + 1) * LANES] = t.astype(BF16)

    return _pcall(body, name=name, grid=(s // tr,),
                  in_specs=[_row_spec(tr, qd), _row_spec(tr, 2 * kd), _row_spec(tr, 2 * kd)] + [_row_spec(tr, LANES)] * 3,
                  out_specs=_row_spec(tr, width), out_shape=jax.ShapeDtypeStruct((s, width), BF16),
                  compiler_params=_cp(("parallel",)))(dq, dkd, dvd, *tabs)


def _halves():
    lane_half = lax.broadcasted_iota(jnp.int32, (1, LANES), 1) // HEAD_DIM
    return [lane_half == 0, lane_half == 1]


def _nt(a, b):
    return lax.dot_general(a, b, _DIMS["nt"], preferred_element_type=F32)


def _tn(a, b):
    return lax.dot_general(a, b, _DIMS["tn"], preferred_element_type=F32)


def _band_mask(i, tq):
    w = tq + WINDOW
    rel = lax.broadcasted_iota(jnp.int32, (tq, w), 1) - lax.broadcasted_iota(jnp.int32, (tq, w), 0)
    first = lax.broadcasted_iota(jnp.int32, (tq, w), 1) >= jnp.where(i > 0, 0, WINDOW)
    return (rel >= 1) & (rel <= WINDOW) & first


def swa_fwd(q, kd, vd, sink_row, name):
    s, qd = q.shape
    tq = min(256, s)
    r = tq // WINDOW
    pairs = qd // LANES
    group_pairs = pairs // (kd.shape[1] // LANES)
    scale = HEAD_DIM ** -0.5

    def body(q_ref, kp_ref, kc_ref, vp_ref, vc_ref, sk_ref, o_ref, lse_ref):
        i = pl.program_id(1)
        k = jnp.concatenate([kp_ref[...], kc_ref[...]], axis=0)
        v = jnp.concatenate([vp_ref[...], vc_ref[...]], axis=0)
        mask = _band_mask(i, tq)
        q2 = q_ref[...]
        outs, lses = [], []
        for a, hm in enumerate(_halves()):
            sc = _nt(jnp.where(hm, q2, jnp.zeros_like(q2)), k) * scale
            sc = jnp.where(mask, sc, NEG)
            sink = sk_ref[:, a * HEAD_DIM:a * HEAD_DIM + 1]
            m = jnp.maximum(jnp.max(sc, axis=-1, keepdims=True), sink)
            p = jnp.exp(sc - m)
            den = jnp.sum(p, axis=-1, keepdims=True) + jnp.exp(sink - m)
            outs.append(jnp.dot(p.astype(BF16), v, preferred_element_type=F32) / den)
            lses.append(m + jnp.log(den))
        hm0 = _halves()[0]
        o_ref[...] = jnp.where(hm0, outs[0], outs[1]).astype(BF16)
        lse_ref[...] = jnp.where(hm0, lses[0], lses[1])

    prev = lambda p, i: (jnp.maximum(i * r - 1, 0), p // group_pairs)
    cur = lambda p, i: (i, p // group_pairs)
    blk = pl.BlockSpec((tq, LANES), lambda p, i: (i, p))
    return _pcall(body, name=name, grid=(pairs, s // tq),
                  in_specs=[blk, pl.BlockSpec((WINDOW, LANES), prev), pl.BlockSpec((tq, LANES), cur),
                            pl.BlockSpec((WINDOW, LANES), prev), pl.BlockSpec((tq, LANES), cur),
                            pl.BlockSpec((1, LANES), lambda p, i: (0, p))],
                  out_specs=[blk, blk],
                  out_shape=[jax.ShapeDtypeStruct((s, qd), BF16), jax.ShapeDtypeStruct((s, qd), F32)],
                  compiler_params=_cp(("parallel", "parallel")))(q, kd, kd, vd, vd, sink_row)


def swa_bwd(q, kd, vd, sink_row, out, lse, dout, name):
    s, qd = q.shape
    tq = min(256, s)
    r = tq // WINDOW
    n_kv = kd.shape[1] // LANES
    gw = qd // n_kv
    scale = HEAD_DIM ** -0.5

    def body(q_ref, kp_ref, kc_ref, vp_ref, vc_ref, sk_ref, o_ref, lse_ref, do_ref, dq_ref, dk_ref, dv_ref, dsk_ref):
        i = pl.program_id(1)

        @pl.when(i == 0)
        def _():
            dk_ref[...] = jnp.zeros_like(dk_ref)
            dv_ref[...] = jnp.zeros_like(dv_ref)
            dsk_ref[...] = jnp.zeros_like(dsk_ref)

        k = jnp.concatenate([kp_ref[...], kc_ref[...]], axis=0)
        v = jnp.concatenate([vp_ref[...], vc_ref[...]], axis=0)
        mask = _band_mask(i, tq)
        dk = jnp.zeros((tq + WINDOW, LANES), F32)
        dv = jnp.zeros((tq + WINDOW, LANES), F32)
        for pp in range(gw // LANES):
            cols = slice(pp * LANES, (pp + 1) * LANES)
            q2, do2 = q_ref[:, cols], do_ref[:, cols]
            prod = do2.astype(F32) * o_ref[:, cols].astype(F32)
            dq2 = jnp.zeros((tq, LANES), F32)
            dsk = jnp.zeros((1, LANES), F32)
            for a, hm in enumerate(_halves()):
                qa = jnp.where(hm, q2, jnp.zeros_like(q2))
                doa = jnp.where(hm, do2, jnp.zeros_like(do2))
                lse_a = lse_ref[:, pp * LANES + a * HEAD_DIM:pp * LANES + a * HEAD_DIM + 1]
                sc = jnp.where(mask, _nt(qa, k) * scale, NEG)
                p = jnp.exp(sc - lse_a)
                delta = jnp.sum(jnp.where(hm, prod, 0.0), axis=-1, keepdims=True)
                ds = (p * (_nt(doa, v) - delta) * scale).astype(BF16)
                dv = dv + _tn(p.astype(BF16), doa)
                dk = dk + _tn(ds, qa)
                dq2 = dq2 + jnp.where(hm, jnp.dot(ds, k, preferred_element_type=F32), 0.0)
                sink = sk_ref[:, pp * LANES + a * HEAD_DIM:pp * LANES + a * HEAD_DIM + 1]
                dsink = -jnp.sum(jnp.exp(sink - lse_a) * delta, axis=0, keepdims=True)
                dsk = dsk + jnp.where(hm, dsink, 0.0)
            dq_ref[:, cols] = dq2.astype(BF16)
            dsk_ref[0:1, cols] += dsk
        start = pl.multiple_of(i * tq, tq)
        dk_ref[pl.ds(start, tq), :] += dk[WINDOW:, :]
        dv_ref[pl.ds(start, tq), :] += dv[WINDOW:, :]

        @pl.when(i > 0)
        def _():
            before = pl.multiple_of(i * tq - WINDOW, WINDOW)
            dk_ref[pl.ds(before, WINDOW), :] += dk[:WINDOW, :]
            dv_ref[pl.ds(before, WINDOW), :] += dv[:WINDOW, :]

    prev = lambda g, i: (jnp.maximum(i * r - 1, 0), g)
    cur = lambda g, i: (i, g)
    wide = pl.BlockSpec((tq, gw), cur)
    full = pl.BlockSpec((s, LANES), lambda g, i: (0, g))
    return _pcall(body, name=name, grid=(n_kv, s // tq),
                  in_specs=[wide, pl.BlockSpec((WINDOW, LANES), prev), pl.BlockSpec((tq, LANES), cur),
                            pl.BlockSpec((WINDOW, LANES), prev), pl.BlockSpec((tq, LANES), cur),
                            pl.BlockSpec((1, gw), lambda g, i: (0, g)), wide, wide, wide],
                  out_specs=[wide, full, full, pl.BlockSpec((8, gw), lambda g, i: (0, g))],
                  out_shape=[jax.ShapeDtypeStruct((s, qd), BF16), jax.ShapeDtypeStruct(kd.shape, F32),
                             jax.ShapeDtypeStruct(kd.shape, F32), jax.ShapeDtypeStruct((8, qd), F32)],
                  compiler_params=_cp(("parallel", "arbitrary"), VMEM_BIG))(q, kd, kd, vd, vd, sink_row, out, lse, dout)


def forget_cumsum(f_logit, b_row, name):
    s = f_logit.shape[0]

    def body(f_ref, b_ref, c_ref):
        z = f_ref[...] + b_ref[...]
        acc = jnp.minimum(z, 0.0) - jnp.log(1.0 + jnp.exp(-jnp.abs(z)))
        row = lax.broadcasted_iota(jnp.int32, (s, LANES), 0)
        d = 1
        while d < s:
            acc = acc + jnp.where(row >= d, pltpu.roll(acc, d, 0), 0.0)
            d *= 2
        c_ref[...] = acc

    return _pcall(body, name=name, in_specs=[pl.BlockSpec((s, LANES), lambda: (0, 0)), pl.BlockSpec((1, LANES), lambda: (0, 0))],
                  out_specs=pl.BlockSpec((s, LANES), lambda: (0, 0)), out_shape=jax.ShapeDtypeStruct((s, LANES), F32),
                  compiler_params=_cp(None, VMEM_BIG))(f_logit, b_row)


def forget_gate_bwd(dc, f_logit, b_row, name):
    s = dc.shape[0]

    def body(dc_ref, f_ref, b_ref, df_ref, db_ref):
        acc = dc_ref[...]
        row = lax.broadcasted_iota(jnp.int32, (s, LANES), 0)
        d = 1
        while d < s:
            acc = acc + jnp.where(row < s - d, pltpu.roll(acc, s - d, 0), 0.0)
            d *= 2
        df = acc * jax.nn.sigmoid(-(f_ref[...] + b_ref[...]))
        df_ref[...] = df.astype(BF16)
        db_ref[...] = jnp.sum(df, axis=0, keepdims=True)

    whole = pl.BlockSpec((s, LANES), lambda: (0, 0))
    vec = pl.BlockSpec((1, LANES), lambda: (0, 0))
    return _pcall(body, name=name, in_specs=[whole, whole, vec], out_specs=[whole, vec],
                  out_shape=[jax.ShapeDtypeStruct((s, LANES), BF16), jax.ShapeDtypeStruct((1, LANES), F32)],
                  compiler_params=_cp(None, VMEM_BIG))(dc, f_logit, b_row)


EXTRA = HEAD_DIM
N_PIECES = 3


def _pieces(v):
    hi = v.astype(BF16).astype(F32)
    mid = (v - hi).astype(BF16).astype(F32)
    return hi, mid, (v - hi - mid).astype(BF16).astype(F32)


def _slot(main, lane, extras=None, ones_at=None):
    out = jnp.where(lane < HEAD_DIM, main, 0.0)
    if extras is not None:
        for r, e in enumerate(extras):
            out = jnp.where(lane == EXTRA + r, e, out)
    if ones_at is not None:
        out = jnp.where((lane >= ones_at) & (lane < ones_at + N_PIECES), 1.0, out)
    return out


def _lane_iota():
    return lax.broadcasted_iota(jnp.int32, (1, LANES), 1)


def fox_split(qkv, c, n_heads, name):
    s = qkv.shape[0]
    hd = n_heads * HEAD_DIM
    tr = _tile(s, 256)
    scale = HEAD_DIM ** -0.5

    def body(x_ref, c_ref, q_ref, k_ref, v_ref):
        lane = _lane_iota()
        cv = c_ref[...]
        for g in range(hd // LANES):
            for part, dst in enumerate((q_ref, k_ref, v_ref)):
                t = x_ref[:, part * hd + g * LANES:part * hd + (g + 1) * LANES].astype(F32)
                for a, main in enumerate((t, pltpu.roll(t, HEAD_DIM, 1))):
                    h = 2 * g + a
                    if part == 0:
                        val = _slot(main * scale, lane, ones_at=EXTRA)
                    elif part == 1:
                        ch = jnp.sum(jnp.where(lane == h, cv, 0.0), axis=1, keepdims=True)
                        val = _slot(main, lane, extras=_pieces(-ch), ones_at=EXTRA + N_PIECES)
                    else:
                        val = _slot(main, lane, ones_at=EXTRA)
                    dst[:, h * LANES:(h + 1) * LANES] = val.astype(BF16)

    slots = jax.ShapeDtypeStruct((s, n_heads * LANES), BF16)
    return _pcall(body, name=name, grid=(s // tr,), in_specs=[_row_spec(tr, 3 * hd), _row_spec(tr, LANES)],
                  out_specs=[_row_spec(tr, n_heads * LANES)] * 3, out_shape=[slots] * 3,
                  compiler_params=_cp(("parallel",), VMEM_BIG))(qkv, c)


def _causal_keep(t):
    return lax.broadcasted_iota(jnp.int32, (t, t), 1) <= lax.broadcasted_iota(jnp.int32, (t, t), 0)


def fox_fwd(qs, ks, vs, n_heads, name):
    s = qs.shape[0]
    pairs = n_heads // 2
    t = min(512, s)
    wide = 2 * LANES

    def body(q_ref, k_ref, v_ref, o_ref, qb_ref, acc_ref, m_ref):
        i = pl.program_id(1)
        lane = _lane_iota()
        acc_ref[...] = jnp.zeros_like(acc_ref)
        m_ref[...] = jnp.full_like(m_ref, NEG)

        def step(j, diagonal):
            rows = pl.ds(pl.multiple_of(j * t, t), t)
            for a in range(2):
                cols = slice(a * LANES, (a + 1) * LANES)
                sc = _nt(q_ref[:, cols], k_ref[rows, cols])
                if diagonal:
                    sc = jnp.where(_causal_keep(t), sc, NEG)
                m_old = m_ref[a]
                m_new = jnp.maximum(m_old, jnp.max(sc, axis=-1, keepdims=True))
                p = jnp.exp(sc - jnp.tile(m_new, (1, t // LANES)))
                acc_ref[a] = jnp.exp(m_old - m_new) * acc_ref[a] + jnp.dot(p.astype(BF16), v_ref[rows, cols],
                                                                              preferred_element_type=F32)
                m_ref[a] = m_new

        def loop_body(j, carry):
            step(j, False)
            return carry

        lax.fori_loop(0, i, loop_body, 0)
        step(i, True)
        outs = []
        for a in range(2):
            cols = slice(a * LANES, (a + 1) * LANES)
            acc = acc_ref[a]
            norm = acc[:, EXTRA:EXTRA + 1]
            outs.append(acc / norm)
            neg_lse = _pieces(-(m_ref[a] + jnp.log(norm)))
            qb = q_ref[:, cols].astype(F32)
            for r in range(N_PIECES):
                qb = jnp.where(lane == EXTRA + N_PIECES + r, neg_lse[r], qb)
            qb_ref[:, cols] = qb.astype(BF16)
        o_ref[...] = jnp.where(lane < HEAD_DIM, outs[0], pltpu.roll(outs[1], HEAD_DIM, 1)).astype(BF16)

    qblk = pl.BlockSpec((t, wide), lambda p, i: (i, p))
    whole = pl.BlockSpec((s, wide), lambda p, i: (0, p))
    return _pcall(body, name=name, grid=(pairs, s // t), in_specs=[qblk, whole, whole],
                  out_specs=[pl.BlockSpec((t, LANES), lambda p, i: (i, p)), qblk],
                  out_shape=[jax.ShapeDtypeStruct((s, n_heads * HEAD_DIM), BF16), jax.ShapeDtypeStruct(qs.shape, BF16)],
                  scratch_shapes=[pltpu.VMEM((2, t, LANES), F32)] * 2,
                  compiler_params=_cp(("parallel", "arbitrary"), VMEM_BIG))(qs, ks, vs)


def fox_dout_slots(dout, out, name):
    s, hd = dout.shape
    tr = _tile(s, 256)

    def body(d_ref, o_ref, s_ref):
        lane = _lane_iota()
        for g in range(hd // LANES):
            cols = slice(g * LANES, (g + 1) * LANES)
            d2 = d_ref[:, cols].astype(F32)
            prod = d2 * o_ref[:, cols].astype(F32)
            for a, main in enumerate((d2, pltpu.roll(d2, HEAD_DIM, 1))):
                delta = jnp.sum(jnp.where((lane // HEAD_DIM) == a, prod, 0.0), axis=1, keepdims=True)
                h = 2 * g + a
                s_ref[:, h * LANES:(h + 1) * LANES] = _slot(main, lane, extras=_pieces(-delta)).astype(BF16)

    return _pcall(body, name=name, grid=(s // tr,), in_specs=[_row_spec(tr, hd), _row_spec(tr, hd)],
                  out_specs=_row_spec(tr, 2 * hd), out_shape=jax.ShapeDtypeStruct((s, 2 * hd), BF16),
                  compiler_params=_cp(("parallel",)))(dout, out)


def fox_bwd(qb, ks, vs, dos, n_heads, name):
    s = qb.shape[0]
    pairs = n_heads // 2
    t = min(512, s)
    nblk = s // t
    wide = 2 * LANES

    def body(q_ref, k_ref, v_ref, do_ref, dq_ref, dk_ref, dv_ref, dka_ref, dva_ref):
        j = pl.program_id(1)

        @pl.when(j == 0)
        def _():
            dq_ref[...] = jnp.zeros_like(dq_ref)

        dka_ref[...] = jnp.zeros_like(dka_ref)
        dva_ref[...] = jnp.zeros_like(dva_ref)

        def step(i, diagonal):
            rows = pl.ds(pl.multiple_of(i * t, t), t)
            for a in range(2):
                cols = slice(a * LANES, (a + 1) * LANES)
                qa, doa, ka = q_ref[rows, cols], do_ref[rows, cols], k_ref[:, cols]
                sc = _nt(qa, ka)
                if diagonal:
                    sc = jnp.where(_causal_keep(t), sc, NEG)
                p = jnp.exp(sc)
                ds = (p * _nt(doa, v_ref[:, cols])).astype(BF16)
                dva_ref[a] += _tn(p.astype(BF16), doa)
                dka_ref[a] += _tn(ds, qa)
                dq_ref[rows, cols] += jnp.dot(ds, ka, preferred_element_type=F32)

        step(j, True)

        def loop_body(i, carry):
            step(i, False)
            return carry

        lax.fori_loop(j + 1, nblk, loop_body, 0)
        for a in range(2):
            cols = slice(a * LANES, (a + 1) * LANES)
            dk_ref[:, cols] = dka_ref[a]
            dv_ref[:, cols] = dva_ref[a].astype(BF16)

    whole = pl.BlockSpec((s, wide), lambda p, j: (0, p))
    blk = pl.BlockSpec((t, wide), lambda p, j: (j, p))
    return _pcall(body, name=name, grid=(pairs, nblk), in_specs=[whole, blk, blk, whole], out_specs=[whole, blk, blk],
                  out_shape=[jax.ShapeDtypeStruct(qb.shape, F32), jax.ShapeDtypeStruct(qb.shape, F32),
                             jax.ShapeDtypeStruct(qb.shape, BF16)],
                  scratch_shapes=[pltpu.VMEM((2, t, LANES), F32)] * 2,
                  compiler_params=_cp(("parallel", "arbitrary"), VMEM_BIG))(qb, ks, vs, dos)


def fox_merge_bwd(dqs, dks, dvs, n_heads, name):
    s = dqs.shape[0]
    hd = n_heads * HEAD_DIM
    tr = _tile(s, 128)
    scale = HEAD_DIM ** -0.5

    def body(dq_ref, dk_ref, dv_ref, o_ref, dc_ref):
        lane = _lane_iota()
        dc = jnp.zeros((tr, LANES), F32)
        for g in range(hd // LANES):
            even = slice(2 * g * LANES, (2 * g + 1) * LANES)
            odd = slice((2 * g + 1) * LANES, (2 * g + 2) * LANES)
            for part, (src, mul) in enumerate(((dq_ref, scale), (dk_ref, 1.0), (dv_ref, 1.0))):
                dense = jnp.where(lane < HEAD_DIM, src[:, even].astype(F32), pltpu.roll(src[:, odd].astype(F32), HEAD_DIM, 1))
                o_ref[:, part * hd + g * LANES:part * hd + (g + 1) * LANES] = (dense * mul).astype(BF16)
            for a, cols in enumerate((even, odd)):
                both = jnp.where(lane == EXTRA + N_PIECES, dq_ref[:, cols], 0.0) - jnp.where(lane == EXTRA, dk_ref[:, cols], 0.0)
                dc = jnp.where(lane == 2 * g + a, jnp.sum(both, axis=1, keepdims=True), dc)
        dc_ref[...] = dc

    wide = n_heads * LANES
    return _pcall(body, name=name, grid=(s // tr,), in_specs=[_row_spec(tr, wide)] * 3,
                  out_specs=[_row_spec(tr, 3 * hd), _row_spec(tr, LANES)],
                  out_shape=[jax.ShapeDtypeStruct((s, 3 * hd), BF16), jax.ShapeDtypeStruct((s, LANES), F32)],
                  compiler_params=_cp(("parallel",), VMEM_BIG))(dqs, dks, dvs)


def _w_spec(shape, lead_map=None):
    _, r, c = shape[-3:]
    tr = _tile(r, 128)
    n_lead = len(shape) - 2
    if lead_map is None:
        lead_map = lambda l: (l,)
    return tr, pl.BlockSpec((None,) * n_lead + (tr, c), lambda l, i: (*lead_map(l), i, 0))


def cast_into_slot(w, layer, place, dtype, name):
    tr = _tile(w.shape[1], 128)
    blk = (None, tr, w.shape[2])

    def body(me_ref, c_ref, w_ref, o_ref):
        o_ref[...] = w_ref[...].astype(dtype)

    grid_spec = pltpu.PrefetchScalarGridSpec(
        num_scalar_prefetch=2, grid=(w.shape[1] // tr,),
        in_specs=[pl.BlockSpec(blk, lambda i, me, c: (layer, i, 0))],
        out_specs=pl.BlockSpec(blk, lambda i, me, c: (me[0], i, 0)))
    return _pcall(body, name=name, grid_spec=grid_spec, out_shape=jax.ShapeDtypeStruct((N_CHIPS,) + w.shape[1:], dtype),
                  compiler_params=_cp(("parallel",)))(*place, w)


def adamw(w, g, m, v, name):
    tr, spec = _w_spec(w.shape)

    def body(w_ref, g_ref, m_ref, v_ref, d_ref, mo_ref, vo_ref):
        gg = g_ref[...]
        mn = ADAM_B1 * m_ref[...] + (1.0 - ADAM_B1) * gg
        vn = ADAM_B2 * v_ref[...] + (1.0 - ADAM_B2) * (gg * gg)
        m_hat = mn / (1.0 - ADAM_B1 ** ADAM_STEP)
        v_hat = vn / (1.0 - ADAM_B2 ** ADAM_STEP)
        d_ref[...] = -ADAM_LR * (m_hat / (jnp.sqrt(v_hat) + ADAM_EPS) + ADAM_WD * w_ref[...])
        mo_ref[...] = mn
        vo_ref[...] = vn

    out = jax.ShapeDtypeStruct(w.shape, F32)
    return _pcall(body, name=name, grid=(w.shape[0], w.shape[1] // tr), in_specs=[spec] * 4, out_specs=[spec] * 3,
                  out_shape=[out] * 3, compiler_params=_cp(("parallel", "parallel")))(w, g, m, v)


def add_core_halves(g, recv, place, name):
    hr = recv.shape[1]
    tr = _tile(hr, 128)
    nb = hr // tr
    blk = (None, tr, g.shape[2])

    def body(me_ref, c_ref, g_ref, r_ref, o_ref):
        o_ref[...] = (g_ref[...].astype(F32) + r_ref[...].astype(F32)).astype(o_ref.dtype)

    grid_spec = pltpu.PrefetchScalarGridSpec(
        num_scalar_prefetch=2, grid=(N_CHIPS, nb),
        in_specs=[pl.BlockSpec(blk, lambda q, i, me, c: (q, c[0] * nb + i, 0)), pl.BlockSpec(blk, lambda q, i, me, c: (q, i, 0))],
        out_specs=pl.BlockSpec(blk, lambda q, i, me, c: (q, i, 0)))
    return _pcall(body, name=name, grid_spec=grid_spec, out_shape=jax.ShapeDtypeStruct(recv.shape, g.dtype),
                  compiler_params=_cp(("parallel", "parallel")))(*place, g, recv)


def add_chips(own, recv, into, layer, place, name):
    _, hr, cols = own.shape
    tr = _tile(hr, 128)
    nb = hr // tr
    blk = (None, tr, cols)

    def body(me_ref, c_ref, p0, p1, p2, p3, _, o_ref):
        o_ref[...] = ((p0[...].astype(F32) + p1[...].astype(F32)) + p2[...].astype(F32)) + p3[...].astype(F32)

    def peer(flip):
        return lambda i, me, c: (me[0] ^ flip, i, 0)

    grid_spec = pltpu.PrefetchScalarGridSpec(
        num_scalar_prefetch=2, grid=(nb,), in_specs=[pl.BlockSpec(blk, peer(f)) for f in (0, 2, 1, 3)] + [ANY],
        out_specs=pl.BlockSpec(blk, lambda i, me, c: (layer, c[0] * nb + i, 0)))
    return _pcall(body, name=name, grid_spec=grid_spec, out_shape=jax.ShapeDtypeStruct(into.shape, F32),
                  input_output_aliases={6: 0}, compiler_params=_cp(("parallel",)))(*place, own, recv, recv, recv, into)


def _place():
    x, y, c = lax.axis_index("x"), lax.axis_index("y"), lax.axis_index("c")
    others = [(1 - x, y), (x, 1 - y), (1 - x, 1 - y)]
    return x, y, c, others


def _chip_id(chip):
    return 2 * chip[0] + chip[1]


def _comm_call(body, name, n_in, out_shapes, n_sems, in_place=False):
    return _pcall(body, name=name, in_specs=[ANY] * n_in, out_specs=[ANY] * len(out_shapes), out_shape=out_shapes,
                  scratch_shapes=[pltpu.SemaphoreType.DMA((n_sems,)), pltpu.SemaphoreType.DMA((n_sems,))],
                  input_output_aliases={t: t for t in range(n_in)} if in_place else {},
                  compiler_params=pltpu.CompilerParams(has_side_effects=True))


def swap_core_halves(grads, name):
    n = len(grads)

    def body(*refs):
        ins, outs = refs[:n], refs[n:2 * n]
        send, recv = refs[2 * n:]
        x, y, c, _ = _place()
        cps = []
        for t in range(n):
            hr = ins[t].shape[1] // 2
            cp = pltpu.make_async_remote_copy(src_ref=ins[t].at[:, pl.ds((1 - c) * hr, hr)], dst_ref=outs[t],
                                              send_sem=send.at[t], recv_sem=recv.at[t],
                                              device_id=(x, y, 1 - c), device_id_type=MESH)
            cp.start()
            cps.append(cp)
        for cp in cps:
            cp.wait()

    outs = [jax.ShapeDtypeStruct((a.shape[0], a.shape[1] // 2) + a.shape[2:], a.dtype) for a in grads]
    return _comm_call(body, name, n, outs, n)(*grads)


HBM = pl.BlockSpec(memory_space=pltpu.HBM)
SEM = pl.BlockSpec(memory_space=pltpu.SEMAPHORE)
DATAFLOW = pltpu.SideEffectType.DATAFLOW_SIDE_EFFECTING


def _in_hbm(a):
    return pltpu.with_memory_space_constraint(a, pltpu.HBM)


def gather_start(slots, name):
    n = len(slots)

    def body(*refs):
        bufs = refs[:n]
        send, recv = refs[n], refs[n + 1]
        token = refs[-1]
        x, y, c, others = _place()
        me = _chip_id((x, y))
        for t in range(n):
            for j, chip in enumerate(others):
                pltpu.make_async_remote_copy(src_ref=bufs[t].at[me], dst_ref=bufs[t].at[me],
                                             send_sem=send.at[3 * t + j], recv_sem=recv.at[3 * t + j],
                                             device_id=(*chip, c), device_id_type=MESH).start()
        token[...] = jnp.zeros_like(token)

    sems = pltpu.SemaphoreType.DMA((3 * n,))
    res = _pcall(body, name=name, in_specs=[HBM] * n,
                 out_shape=(sems, sems, *[pltpu.HBM(a.shape, a.dtype) for a in slots], jax.ShapeDtypeStruct((8, LANES), F32)),
                 out_specs=(SEM, SEM, *[HBM] * n, pl.BlockSpec(memory_space=pltpu.VMEM)),
                 input_output_aliases={i: 2 + i for i in range(n)},
                 compiler_params=pltpu.CompilerParams(has_side_effects=DATAFLOW))(*[_in_hbm(a) for a in slots])
    return res[0], res[1], list(res[2:2 + n]), res[-1]


def gather_wait(send, recv, slot, t, after, name):
    def body(buf, send_ref, recv_ref, after_ref, out):
        x, y, c, others = _place()
        me = _chip_id((x, y))
        for j, chip in enumerate(others):
            pltpu.make_async_remote_copy(src_ref=buf.at[me], dst_ref=buf.at[_chip_id(chip)],
                                         send_sem=send_ref.at[3 * t + j], recv_sem=recv_ref.at[3 * t + j],
                                         device_id=(*chip, c), device_id_type=MESH).wait()

    return _pcall(body, name=name, in_specs=[HBM, SEM, SEM, ANY], out_shape=pltpu.HBM(slot.shape, slot.dtype),
                  out_specs=HBM, input_output_aliases={0: 0},
                  compiler_params=pltpu.CompilerParams(has_side_effects=DATAFLOW))(slot, send, recv, after)


def scatter_start(sums, name):
    n = len(sums)

    def body(*refs):
        ins, lands = refs[:n], refs[n:2 * n]
        send, recv = refs[2 * n], refs[2 * n + 1]
        token = refs[-1]
        x, y, c, others = _place()
        me = _chip_id((x, y))
        for t in range(n):
            for j, chip in enumerate(others):
                pltpu.make_async_remote_copy(src_ref=ins[t].at[_chip_id(chip)], dst_ref=lands[t].at[me],
                                             send_sem=send.at[3 * t + j], recv_sem=recv.at[3 * t + j],
                                             device_id=(*chip, c), device_id_type=MESH).start()
        token[...] = jnp.zeros_like(token)

    bufs = [pltpu.HBM(a.shape, a.dtype) for a in sums]
    sems = pltpu.SemaphoreType.DMA((3 * n,))
    res = _pcall(body, name=name, in_specs=[HBM] * (2 * n),
                 out_shape=(sems, sems, *bufs, *bufs, jax.ShapeDtypeStruct((8, LANES), F32)),
                 out_specs=(SEM, SEM, *[HBM] * (2 * n), pl.BlockSpec(memory_space=pltpu.VMEM)),
                 input_output_aliases={i: 2 + i for i in range(2 * n)},
                 compiler_params=pltpu.CompilerParams(has_side_effects=DATAFLOW))(
        *[_in_hbm(a) for a in sums], *[_in_hbm(lax.empty(a.shape, a.dtype)) for a in sums])
    return res[0], res[1], list(res[2:2 + n]), list(res[2 + n:2 + 2 * n]), res[-1]


def scatter_wait(send, recv, sums, lands, after, name):
    n = len(sums)

    def body(*refs):
        ins, bufs = refs[:n], refs[n:2 * n]
        send_ref, recv_ref = refs[2 * n], refs[2 * n + 1]
        x, y, c, others = _place()
        me = _chip_id((x, y))
        for t in range(n):
            for j, chip in enumerate(others):
                cp = pltpu.make_async_remote_copy(src_ref=ins[t].at[_chip_id(chip)], dst_ref=bufs[t].at[_chip_id(chip)],
                                                  send_sem=send_ref.at[3 * t + j], recv_sem=recv_ref.at[3 * t + j],
                                                  device_id=(*chip, c), device_id_type=MESH)
                cp.wait_send()
                cp.wait_recv()

    shapes = [pltpu.HBM(a.shape, a.dtype) for a in sums]
    res = _pcall(body, name=name, in_specs=[HBM] * (2 * n) + [SEM, SEM, ANY],
                 out_shape=(*shapes, *shapes), out_specs=tuple([HBM] * (2 * n)),
                 input_output_aliases={i: i for i in range(2 * n)},
                 compiler_params=pltpu.CompilerParams(has_side_effects=DATAFLOW))(*sums, *lands, send, recv, after)
    return list(res[:n]), list(res[n:])


def join_core_halves(fulls, name):
    n = len(fulls)

    def body(*refs):
        bufs = refs[n:2 * n]
        send, recv = refs[2 * n:]
        x, y, c, _ = _place()
        cps = []
        for t in range(n):
            hr = bufs[t].shape[1] // 2
            mine = bufs[t].at[:, pl.ds(c * hr, hr)]
            cp = pltpu.make_async_remote_copy(src_ref=mine, dst_ref=mine, send_sem=send.at[t], recv_sem=recv.at[t],
                                              device_id=(x, y, 1 - c), device_id_type=MESH)
            cp.start()
            cps.append(cp)
        for t in range(n):
            hr = bufs[t].shape[1] // 2
            theirs = bufs[t].at[:, pl.ds((1 - c) * hr, hr)]
            cps[t].wait_send()
            pltpu.make_async_remote_copy(src_ref=theirs, dst_ref=theirs, send_sem=send.at[t], recv_sem=recv.at[t],
                                         device_id=(x, y, c), device_id_type=MESH).wait_recv()

    outs = [jax.ShapeDtypeStruct(a.shape, a.dtype) for a in fulls]
    return _comm_call(body, name, n, outs, n, in_place=True)(*fulls)


def kernel(x, positions, norm_gains, swa_w_in, swa_sinks, swa_w_out, fox_w_in, fox_b_f, fox_w_out, ffn_w_gate_up, ffn_w_down, loss_target, m_norm_gains, m_swa_w_in, m_swa_sinks, m_swa_w_out, m_fox_w_in, m_fox_b_f, m_fox_w_out, m_ffn_w_gate_up, m_ffn_w_down, v_norm_gains, v_swa_w_in, v_swa_sinks, v_swa_w_out, v_fox_w_in, v_fox_b_f, v_fox_w_out, v_ffn_w_gate_up, v_ffn_w_down):
    s, d = x.shape[1], x.shape[2]
    depth = norm_gains.shape[0]
    n_heads = d // HEAD_DIM
    hd = n_heads * HEAD_DIM
    n_kv = (swa_w_in.shape[2] * N_CHIPS // HEAD_DIM - n_heads) // 2
    ff = ffn_w_down.shape[1] * N_CHIPS
    fox_cols = fox_w_in.shape[2]
    fox_pad = 3 * hd + LANES
    assert fox_cols * N_CHIPS == 3 * hd + n_heads and n_heads <= LANES
    x0 = x[0]
    target = loss_target[0]
    place = ((2 * lax.axis_index("x") + lax.axis_index("y")).astype(jnp.int32).reshape(1),
             lax.axis_index("c").astype(jnp.int32).reshape(1))

    order = [("norm_gains", norm_gains.reshape(1, depth * 4, norm_gains.shape[2]), 0, F32)]
    for layer in range(depth):
        j = layer // 2
        kind, w_i, w_o = ("swa", swa_w_in, swa_w_out) if layer % 2 == 0 else ("fox", fox_w_in, fox_w_out)
        order += [(f"{kind}_w_in_{j}", w_i, j, BF16), (f"{kind}_w_out_{j}", w_o, j, BF16),
                  (f"ffn_w_gate_up_{layer}", ffn_w_gate_up, layer, BF16), (f"ffn_w_down_{layer}", ffn_w_down, layer, BF16)]
    slots = [cast_into_slot(w, l, place, dt, "cast_" + nm) for nm, w, l, dt in order]
    g_send, g_recv, slots, g_token = gather_start(slots, "gather_start")
    slot_of = {entry[0]: t for t, entry in enumerate(order)}

    def weight(nm, after):
        t = slot_of[nm]
        return gather_wait(g_send, g_recv, slots[t], t, after, "gather_wait_" + nm)

    gains = jnp.transpose(weight("norm_gains", g_token), (1, 0, 2)).reshape(depth * 4, d)

    def gain(layer, which):
        return gains[layer * 4 + which][None, :]

    def fox_weight(w_in):
        parts = [w_in[q] for q in range(N_CHIPS)]
        parts.append(jnp.zeros((d, fox_pad - fox_cols * N_CHIPS), BF16))
        return jnp.concatenate(parts, axis=1)

    inv_freq = ROPE_THETA ** (-jnp.arange(0, ROT_DIM, 2, dtype=F32) / ROT_DIM)
    lane_d = jnp.arange(LANES) % HEAD_DIM
    invf_row = jnp.where(lane_d < ROT_DIM, inv_freq[lane_d % (ROT_DIM // 2)], 0.0)[None, :]
    tabs = rope_tables(positions.reshape(s, 1), invf_row, "rope_tables")

    n_sh_in = swa_w_in.shape[2]
    gu_sh = ffn_w_gate_up.shape[2]
    tn_gu = gu_sh // 2 if (gu_sh // 2) % LANES == 0 else gu_sh
    down_sh = ffn_w_down.shape[1]
    out_sh = swa_w_out.shape[1]
    tm = min(1024, s)

    saved = []
    xin = x0
    h = prenorm(xin, gain(0, 0), "prenorm_first")
    for layer in range(depth):
        j = layer // 2
        rec = {"x_in": xin, "h1": h}
        if layer % 2 == 0:
            w_in = weight(f"swa_w_in_{j}", h)
            proj = mm_nn(h, w_in, lambda n, k: (n, k, 0), n_sh_in * N_CHIPS, tm=min(512, s), tn=n_sh_in, tk=d,
                         out_dtype=F32, name=f"swa_proj_{j}")
            q, kd, vd = swa_split(proj, tabs, n_heads, n_kv, f"swa_split_{j}")
            sink_row = jnp.repeat(swa_sinks[j], HEAD_DIM)[None, :]
            attn, lse = swa_fwd(q, kd, vd, sink_row, f"swa_fwd_{j}")
            rec.update(q=q, kd=kd, vd=vd, sink_row=sink_row, lse=lse, w_in=w_in)
            w_out = weight(f"swa_w_out_{j}", attn)
        else:
            wf = fox_weight(weight(f"fox_w_in_{j}", h))
            tn = 3 * hd // 6 if (3 * hd // 6) % LANES == 0 else LANES
            qkv = mm_nn(h, wf, lambda n, k: (k, n), 3 * hd, tm=tm, tn=tn, tk=d, out_dtype=BF16, name=f"fox_proj_{j}")
            f_off = 3 * hd // LANES
            f_logit = mm_nn(h, wf, lambda n, k: (k, f_off + n), LANES, tm=tm, tn=LANES, tk=d, out_dtype=F32, name=f"fox_gate_{j}")
            b_row = jnp.pad(fox_b_f[j], (0, LANES - n_heads))[None, :]
            c = forget_cumsum(f_logit, b_row, f"fox_cumsum_{j}")
            qs, ks, vs = fox_split(qkv, c, n_heads, f"fox_split_{j}")
            attn, qb = fox_fwd(qs, ks, vs, n_heads, f"fox_fwd_{j}")
            rec.update(wf=wf, f_logit=f_logit, b_row=b_row, qb=qb, ks=ks, vs=vs)
            w_out = weight(f"fox_w_out_{j}", attn)
        y = mm_nn(attn, w_out, lambda n, k: (k, 0, n), d, tm=tm, tn=min(1024, d), tk=out_sh, out_dtype=F32, name=f"out_proj_{layer}")
        xmid, h2 = postnorm_residual(xin, y, gain(layer, 1), gain(layer, 2), f"postnorm_mixer_{layer}")
        npb = gu_sh // tn_gu
        w_gu = weight(f"ffn_w_gate_up_{layer}", h2)
        gu = mm_nn(h2, w_gu, lambda n, k: (n // npb, k, n % npb), 2 * ff, tm=tm, tn=tn_gu, tk=d, out_dtype=BF16, name=f"ffn_up_{layer}")
        act = swiglu_fwd(gu, f"swiglu_{layer}")
        w_down = weight(f"ffn_w_down_{layer}", act)
        y2 = mm_nn(act, w_down, lambda n, k: (k, 0, n), d, tm=tm, tn=min(1024, d), tk=down_sh, out_dtype=F32, name=f"ffn_down_{layer}")
        rec.update(attn=attn, y=y, x_mid=xmid, h2=h2, gu=gu, act=act, y2=y2, w_out=w_out, w_gu=w_gu, w_down=w_down)
        saved.append(rec)
        if layer + 1 < depth:
            xin, h = postnorm_residual(xmid, y2, gain(layer, 3), gain(layer + 1, 0), f"postnorm_ffn_{layer}")
    dx, loss_blk = postnorm_loss(xmid, y2, gain(depth - 1, 3), target, "loss")

    finals = {nm: lax.empty(w.shape, F32) for nm, w in (("swa_w_in", swa_w_in), ("swa_w_out", swa_w_out), ("fox_w_in", fox_w_in),
                                                        ("fox_w_out", fox_w_out), ("ffn_w_gate_up", ffn_w_gate_up),
                                                        ("ffn_w_down", ffn_w_down), ("norm_gains", order[0][1]))}
    in_flight = []

    def finish_reduce(after):
        send_, recv_, sums_, lands_, keys = in_flight.pop()
        sums_, got = scatter_wait(send_, recv_, sums_, lands_, after, "reduce_scatter_wait_" + keys[0][2])
        for own, rcv, (key, l, nm) in zip(sums_, got, keys):
            finals[key] = add_chips(own, rcv, finals[key], l, place, "reduce_add_chips_" + nm)

    def start_reduce(partials, keys):
        if in_flight:
            finish_reduce(partials[0])
        halves = swap_core_halves(partials, "reduce_swap_" + keys[0][2])
        sums_ = [add_core_halves(g, r, place, "reduce_add_cores_" + k[2]) for g, r, k in zip(partials, halves, keys)]
        send_, recv_, sums_, lands_, token = scatter_start(sums_, "reduce_scatter_start_" + keys[0][2])
        in_flight.append((send_, recv_, sums_, lands_, keys))
        return token

    dgains = [None] * (depth * 4)
    dsinks = [None] * ((depth + 1) // 2)
    dbf = [None] * (depth // 2)
    tko = min(1024, d)
    for layer in reversed(range(depth)):
        j = layer // 2
        rec = saved[layer]
        dy2, dgains[layer * 4 + 3] = postnorm_bwd(dx, rec["y2"], gain(layer, 3), f"postnorm_ffn_bwd_{layer}")
        tnd = min(1024, d)
        g_down = mm_tn(rec["act"], dy2, lambda i, n: (i, 0, n), tka=down_sh, tn=tnd, tm=min(512, s), out_block=(None, down_sh, tnd),
                       name=f"ffn_down_dw_{layer}", out_shape=jax.ShapeDtypeStruct((N_CHIPS, down_sh, d), BF16))
        dact = mm_nt(dy2, rec["w_down"], lambda o, n: (o, 0, n), ff, tm=tm, tko=down_sh, tn=d, out_dtype=BF16, name=f"ffn_down_dx_{layer}")
        dgu = swiglu_bwd(dact, rec["gu"], f"swiglu_bwd_{layer}")
        npb = gu_sh // tn_gu
        g_gu = mm_tn(rec["h2"], dgu, lambda i, n: (n // npb, i, n % npb), tka=tko, tn=tn_gu, tm=min(512, s), out_block=(None, tko, tn_gu),
                     name=f"ffn_up_dw_{layer}", out_shape=jax.ShapeDtypeStruct((N_CHIPS, d, gu_sh), BF16))
        sent = start_reduce([g_gu, g_down], [("ffn_w_gate_up", layer, f"ffn_w_gate_up_{layer}"), ("ffn_w_down", layer, f"ffn_w_down_{layer}")])
        dh2 = mm_nt(dgu, rec["w_gu"], lambda o, n: (n // npb, o, n % npb), d, tm=tm, tko=tko, tn=tn_gu, out_dtype=F32,
                    name=f"ffn_up_dx_{layer}", after=sent)
        dxm, dgains[layer * 4 + 2] = prenorm_bwd(rec["x_mid"], dh2, dx, gain(layer, 2), f"prenorm_ffn_bwd_{layer}")
        dy, dgains[layer * 4 + 1] = postnorm_bwd(dxm, rec["y"], gain(layer, 1), f"postnorm_mixer_bwd_{layer}")
        g_out = mm_tn(rec["attn"], dy, lambda i, n: (i, 0, n), tka=out_sh, tn=tnd, tm=min(512, s), out_block=(None, out_sh, tnd),
                      name=f"out_proj_dw_{layer}", out_shape=jax.ShapeDtypeStruct((N_CHIPS, out_sh, d), BF16))
        dattn = mm_nt(dy, rec["w_out"], lambda o, n: (o, 0, n), hd, tm=tm, tko=out_sh, tn=d, out_dtype=BF16, name=f"out_proj_dx_{layer}")
        if layer % 2 == 0:
            dq, dkd, dvd, dsk = swa_bwd(rec["q"], rec["kd"], rec["vd"], rec["sink_row"], rec["attn"], rec["lse"], dattn, f"swa_bwd_{j}")
            dsinks[j] = dsk[0].reshape(n_heads, HEAD_DIM)[:, 0]
            dproj = swa_merge_bwd(dq, dkd, dvd, tabs, f"swa_merge_bwd_{j}")
            g_in = mm_tn(rec["h1"], dproj, lambda i, n: (n, i, 0), tka=tko, tn=n_sh_in, tm=min(512, s), out_block=(None, tko, n_sh_in),
                         name=f"swa_proj_dw_{j}", out_shape=jax.ShapeDtypeStruct((N_CHIPS, d, n_sh_in), BF16))
            sent = start_reduce([g_in, g_out], [("swa_w_in", j, f"swa_w_in_{j}"), ("swa_w_out", j, f"swa_w_out_{j}")])
            dh1 = mm_nt(dproj, rec["w_in"], lambda o, n: (n, o, 0), d, tm=tm, tko=tko, tn=n_sh_in, out_dtype=F32,
                        name=f"swa_proj_dx_{j}", after=sent)
        else:
            dos = fox_dout_slots(dattn, rec["attn"], f"fox_dout_slots_{j}")
            dqs, dks, dvs = fox_bwd(rec["qb"], rec["ks"], rec["vs"], dos, n_heads, f"fox_bwd_{j}")
            dqkv, dc = fox_merge_bwd(dqs, dks, dvs, n_heads, f"fox_merge_bwd_{j}")
            df, db = forget_gate_bwd(dc, rec["f_logit"], rec["b_row"], f"fox_gate_bwd_{j}")
            dbf[j] = db[0, :n_heads]
            dproj = jnp.concatenate([dqkv, df], axis=1)
            tn_f = LANES * max(k for k in range(1, 9) if (fox_pad // LANES) % k == 0)
            dwf = mm_tn(rec["h1"], dproj, lambda i, n: (i, n), tka=tko, tn=tn_f, tm=min(512, s), out_block=(tko, tn_f),
                        name=f"fox_proj_dw_{j}", out_shape=jax.ShapeDtypeStruct((d, fox_pad), BF16))
            g_in = jnp.stack([dwf[:, q * fox_cols:(q + 1) * fox_cols] for q in range(N_CHIPS)])
            sent = start_reduce([g_in, g_out], [("fox_w_in", j, f"fox_w_in_{j}"), ("fox_w_out", j, f"fox_w_out_{j}")])
            dh1 = mm_nt(dproj, rec["wf"], lambda o, n: (o, n), d, tm=tm, tko=tko, tn=tn_f, out_dtype=F32,
                        name=f"fox_proj_dx_{j}", after=sent)
        dx, dgains[layer * 4] = prenorm_bwd(rec["x_in"], dh1, dxm, gain(layer, 0), f"prenorm_mixer_bwd_{layer}")
    grad_x = dx[None]

    dgain_full = jnp.concatenate(dgains, axis=0)
    start_reduce([jnp.transpose(dgain_full.reshape(depth * 4, N_CHIPS, -1), (1, 0, 2))], [("norm_gains", 0, "norm_gains")])
    finish_reduce(dx)
    keys = ["swa_w_in", "swa_w_out", "fox_w_in", "fox_w_out", "ffn_w_gate_up", "ffn_w_down", "norm_gains"]
    full = join_core_halves([finals[k] for k in keys], "reduce_join_cores")
    g_swa_in_f, g_swa_out_f, g_fox_in_f, g_fox_out_f, g_gu_f, g_down_f, g_gains_f = full
    g_gains_f = g_gains_f.reshape(norm_gains.shape)

    n_swa, n_fox = len(dsinks), len(dbf)
    small = jnp.concatenate([loss_blk[0, :1]] + dsinks + dbf)
    small = lax.psum(small, ("x", "y", "c"))
    loss = small[0]
    g_sinks = small[1:1 + n_swa * n_heads].reshape(n_swa, n_heads)
    g_bf = small[1 + n_swa * n_heads:].reshape(n_fox, n_heads)

    def pad_small(a):
        return jnp.pad(a, ((0, 8 - a.shape[0]), (0, LANES - a.shape[1])))[None]

    def update(w, g, m, v, nm):
        if w.ndim == 2:
            dl, mn, vn = adamw(pad_small(w), pad_small(g), pad_small(m), pad_small(v), "adamw_" + nm)
            return tuple(a[0, :w.shape[0], :w.shape[1]] for a in (dl, mn, vn))
        return adamw(w, g, m, v, "adamw_" + nm)

    grads = [g_gains_f, g_swa_in_f, g_sinks, g_swa_out_f, g_fox_in_f, g_bf, g_fox_out_f, g_gu_f, g_down_f]
    ws = [norm_gains, swa_w_in, swa_sinks, swa_w_out, fox_w_in, fox_b_f, fox_w_out, ffn_w_gate_up, ffn_w_down]
    ms = [m_norm_gains, m_swa_w_in, m_swa_sinks, m_swa_w_out, m_fox_w_in, m_fox_b_f, m_fox_w_out, m_ffn_w_gate_up, m_ffn_w_down]
    vs = [v_norm_gains, v_swa_w_in, v_swa_sinks, v_swa_w_out, v_fox_w_in, v_fox_b_f, v_fox_w_out, v_ffn_w_gate_up, v_ffn_w_down]
    nms = ["norm_gains", "swa_w_in", "swa_sinks", "swa_w_out", "fox_w_in", "fox_b_f", "fox_w_out", "ffn_w_gate_up", "ffn_w_down"]
    upd = [update(w, g, m, v, nm) for w, g, m, v, nm in zip(ws, grads, ms, vs, nms)]
    return (loss, grad_x, *grads, *[u[0] for u in upd], *[u[1] for u in upd], *[u[2] for u in upd])
```

```python
import jax
import jax.numpy as jnp
from jax import lax
from jax.experimental import pallas as pl
from jax.experimental.pallas import tpu as pltpu

F32 = jnp.float32
BF16 = jnp.bfloat16
HEAD_DIM = 64
LANES = 128
WINDOW = 128
ROPE_THETA = 500000.0
ROT_DIM = HEAD_DIM // 4
RMS_EPS = 1e-6
ADAM_LR, ADAM_B1, ADAM_B2, ADAM_EPS, ADAM_WD, ADAM_STEP = 0.001, 0.9, 0.999, 1e-08, 0.01, 10
NEG = -1e30
VMEM_BIG = 56 * 1024 * 1024
N_CHIPS = 4
MESH = pl.DeviceIdType.MESH
ANY = pl.BlockSpec(memory_space=pl.ANY)


def _pcall(body, **kw):
    return pl.pallas_call(body, **kw)


def _cp(sem=None, vmem=None):
    return pltpu.CompilerParams(dimension_semantics=sem, vmem_limit_bytes=vmem)


def _tile(n, pref):
    if n <= pref:
        return n
    t = pref - pref % 16
    while n % t:
        t -= 16
    return t


_DIMS = {"nn": (((1,), (0,)), ((), ())), "nt": (((1,), (1,)), ((), ())), "tn": (((0,), (0,)), ((), ()))}


def _matmul(kind, a, b, *, grid, a_spec, b_spec, o_spec, out_shape, acc_shape, name, after=None):
    nk = grid[2]
    dims = _DIMS[kind]

    def body(a_ref, b_ref, *rest):
        rest = rest[1:] if after is not None else rest
        o_ref = rest[0]

        def prod():
            return lax.dot_general(a_ref[...], b_ref[...], dims, preferred_element_type=F32)

        if nk == 1:
            o_ref[...] = prod().astype(o_ref.dtype)
        else:
            acc = rest[1]
            k = pl.program_id(2)

            @pl.when(k == 0)
            def _():
                acc[...] = prod()

            @pl.when(k > 0)
            def _():
                acc[...] += prod()

            @pl.when(k == nk - 1)
            def _():
                o_ref[...] = acc[...].astype(o_ref.dtype)

    args, in_specs = [a, b], [a_spec, b_spec]
    if after is not None:
        args.append(after)
        in_specs.append(ANY)
    return _pcall(body, name=name, grid=grid, in_specs=in_specs, out_specs=o_spec, out_shape=out_shape,
                  scratch_shapes=[] if nk == 1 else [pltpu.VMEM(acc_shape, F32)],
                  compiler_params=_cp(("parallel", "parallel", "arbitrary"), VMEM_BIG))(*args)


def mm_nn(a, w, w_map, n_out, *, tm, tn, tk, out_dtype, name):
    m, kdim = a.shape
    tm = min(tm, m)
    lead = (None,) * (w.ndim - 2)
    return _matmul("nn", a, w, grid=(m // tm, n_out // tn, kdim // tk),
                   a_spec=pl.BlockSpec((tm, tk), lambda i, j, k: (i, k)),
                   b_spec=pl.BlockSpec(lead + (tk, tn), lambda i, j, k: w_map(j, k)),
                   o_spec=pl.BlockSpec((tm, tn), lambda i, j, k: (i, j)),
                   out_shape=jax.ShapeDtypeStruct((m, n_out), out_dtype), acc_shape=(tm, tn), name=name)


def mm_nt(a, w, w_map, k_out, *, tm, tko, tn, out_dtype, name, after=None):
    m, ndim = a.shape
    tm = min(tm, m)
    lead = (None,) * (w.ndim - 2)
    return _matmul("nt", a, w, grid=(m // tm, k_out // tko, ndim // tn),
                   a_spec=pl.BlockSpec((tm, tn), lambda i, j, n: (i, n)),
                   b_spec=pl.BlockSpec(lead + (tko, tn), lambda i, j, n: w_map(j, n)),
                   o_spec=pl.BlockSpec((tm, tko), lambda i, j, n: (i, j)),
                   out_shape=jax.ShapeDtypeStruct((m, k_out), out_dtype), acc_shape=(tm, tko), name=name, after=after)


def mm_tn(a, b, o_map, *, tka, tn, tm, out_block, name, out_shape):
    m, kdim = a.shape
    n = b.shape[1]
    tm = min(tm, m)
    return _matmul("tn", a, b, grid=(kdim // tka, n // tn, m // tm),
                   a_spec=pl.BlockSpec((tm, tka), lambda i, j, mm: (mm, i)),
                   b_spec=pl.BlockSpec((tm, tn), lambda i, j, mm: (mm, j)),
                   o_spec=pl.BlockSpec(out_block, lambda i, j, mm: o_map(i, j)),
                   out_shape=out_shape, acc_shape=(tka, tn), name=name)


def _rstd(v):
    return lax.rsqrt(jnp.mean(v * v, axis=-1, keepdims=True) + RMS_EPS)


def _row_spec(tr, d):
    return pl.BlockSpec((tr, d), lambda i: (i, 0))


def _vec_spec(d):
    return pl.BlockSpec((1, d), lambda i: (0, 0))


def prenorm(x, g, name):
    s, d = x.shape
    tr = _tile(s, 256)

    def body(x_ref, g_ref, h_ref):
        v = x_ref[...]
        h_ref[...] = (v * _rstd(v) * g_ref[...]).astype(BF16)

    return _pcall(body, name=name, grid=(s // tr,), in_specs=[_row_spec(tr, d), _vec_spec(d)],
                  out_specs=_row_spec(tr, d), out_shape=jax.ShapeDtypeStruct((s, d), BF16),
                  compiler_params=_cp(("parallel",)))(x, g)


def postnorm_residual(x, y, g_post, g_next, name):
    s, d = x.shape
    tr = _tile(s, 256)

    def body(x_ref, y_ref, gp_ref, gn_ref, xo_ref, h_ref):
        v = y_ref[...]
        xn = x_ref[...] + v * _rstd(v) * gp_ref[...]
        xo_ref[...] = xn
        h_ref[...] = (xn * _rstd(xn) * gn_ref[...]).astype(BF16)

    return _pcall(body, name=name, grid=(s // tr,),
                  in_specs=[_row_spec(tr, d), _row_spec(tr, d), _vec_spec(d), _vec_spec(d)],
                  out_specs=[_row_spec(tr, d), _row_spec(tr, d)],
                  out_shape=[jax.ShapeDtypeStruct((s, d), F32), jax.ShapeDtypeStruct((s, d), BF16)],
                  compiler_params=_cp(("parallel",)))(x, y, g_post, g_next)


def postnorm_loss(x, y, g_post, target, name):
    s, d = x.shape
    tr = _tile(s, 256)

    def body(x_ref, y_ref, gp_ref, t_ref, dx_ref, loss_ref):
        v = y_ref[...]
        err = x_ref[...] + v * _rstd(v) * gp_ref[...] - t_ref[...]
        dx_ref[...] = err / d
        part = 0.5 * jnp.sum(jnp.mean(err * err, axis=-1, keepdims=True), axis=0, keepdims=True)

        @pl.when(pl.program_id(0) == 0)
        def _():
            loss_ref[...] = jnp.zeros_like(loss_ref)

        loss_ref[...] += part

    return _pcall(body, name=name, grid=(s // tr,),
                  in_specs=[_row_spec(tr, d), _row_spec(tr, d), _vec_spec(d), _row_spec(tr, d)],
                  out_specs=[_row_spec(tr, d), pl.BlockSpec((8, LANES), lambda i: (0, 0))],
                  out_shape=[jax.ShapeDtypeStruct((s, d), F32), jax.ShapeDtypeStruct((8, LANES), F32)],
                  compiler_params=_cp(("arbitrary",)))(x, y, g_post, target)


def _norm_bwd(v, g, dz):
    r = _rstd(v)
    vhat = v * r
    u = dz * g
    dv = r * (u - vhat * jnp.mean(u * vhat, axis=-1, keepdims=True))
    return dv, jnp.sum(dz * vhat, axis=0, keepdims=True)


def _acc_rows(ref, val):
    @pl.when(pl.program_id(0) == 0)
    def _():
        ref[...] = jnp.zeros_like(ref)

    ref[...] += val


def postnorm_bwd(dz, y, g, name):
    s, d = y.shape
    tr = _tile(s, 256)

    def body(dz_ref, y_ref, g_ref, dy_ref, dg_ref):
        dv, dg = _norm_bwd(y_ref[...], g_ref[...], dz_ref[...])
        dy_ref[...] = dv.astype(BF16)
        _acc_rows(dg_ref, dg)

    return _pcall(body, name=name, grid=(s // tr,), in_specs=[_row_spec(tr, d), _row_spec(tr, d), _vec_spec(d)],
                  out_specs=[_row_spec(tr, d), _vec_spec(d)],
                  out_shape=[jax.ShapeDtypeStruct((s, d), BF16), jax.ShapeDtypeStruct((1, d), F32)],
                  compiler_params=_cp(("arbitrary",)))(dz, y, g)


def prenorm_bwd(x, dh, dskip, g, name):
    s, d = x.shape
    tr = _tile(s, 256)

    def body(x_ref, dh_ref, ds_ref, g_ref, dx_ref, dg_ref):
        dv, dg = _norm_bwd(x_ref[...], g_ref[...], dh_ref[...])
        dx_ref[...] = ds_ref[...] + dv
        _acc_rows(dg_ref, dg)

    return _pcall(body, name=name, grid=(s // tr,),
                  in_specs=[_row_spec(tr, d), _row_spec(tr, d), _row_spec(tr, d), _vec_spec(d)],
                  out_specs=[_row_spec(tr, d), _vec_spec(d)],
                  out_shape=[jax.ShapeDtypeStruct((s, d), F32), jax.ShapeDtypeStruct((1, d), F32)],
                  compiler_params=_cp(("arbitrary",)))(x, dh, dskip, g)


def swiglu_fwd(gu, name):
    s, two_ff = gu.shape
    ff = two_ff // 2
    tr = _tile(s, 128)

    def body(gu_ref, act_ref):
        gate = gu_ref[:, :ff].astype(F32)
        up = gu_ref[:, ff:].astype(F32)
        act_ref[...] = (gate * jax.nn.sigmoid(gate) * up).astype(BF16)

    return _pcall(body, name=name, grid=(s // tr,), in_specs=[_row_spec(tr, two_ff)], out_specs=_row_spec(tr, ff),
                  out_shape=jax.ShapeDtypeStruct((s, ff), BF16), compiler_params=_cp(("parallel",)))(gu)


def swiglu_bwd(dact, gu, name):
    s, two_ff = gu.shape
    ff = two_ff // 2
    tr = _tile(s, 128)

    def body(da_ref, gu_ref, dgu_ref):
        gate = gu_ref[:, :ff].astype(F32)
        up = gu_ref[:, ff:].astype(F32)
        da = da_ref[...].astype(F32)
        sig = jax.nn.sigmoid(gate)
        dgu_ref[:, :ff] = (da * up * sig * (1.0 + gate * (1.0 - sig))).astype(BF16)
        dgu_ref[:, ff:] = (da * gate * sig).astype(BF16)

    return _pcall(body, name=name, grid=(s // tr,), in_specs=[_row_spec(tr, ff), _row_spec(tr, two_ff)],
                  out_specs=_row_spec(tr, two_ff), out_shape=jax.ShapeDtypeStruct((s, two_ff), BF16),
                  compiler_params=_cp(("parallel",)))(dact, gu)


def rope_tables(pos_col, invf_row, name):
    s = pos_col.shape[0]
    tr = _tile(s, 1024)

    def body(p_ref, f_ref, c_ref, sa_ref, sb_ref):
        ang = p_ref[...].astype(F32) * f_ref[...]
        d = lax.broadcasted_iota(jnp.int32, (1, LANES), 1) % HEAD_DIM
        cs, sn = jnp.cos(ang), jnp.sin(ang)
        c_ref[...] = jnp.where(d < ROT_DIM, cs, 1.0)
        sa_ref[...] = jnp.where(d < ROT_DIM // 2, -sn, 0.0)
        sb_ref[...] = jnp.where((d >= ROT_DIM // 2) & (d < ROT_DIM), sn, 0.0)

    tab = jax.ShapeDtypeStruct((s, LANES), F32)
    return _pcall(body, name=name, grid=(s // tr,),
                  in_specs=[pl.BlockSpec((tr, 1), lambda i: (i, 0)), _vec_spec(LANES)],
                  out_specs=[_row_spec(tr, LANES)] * 3, out_shape=[tab] * 3, compiler_params=_cp(("parallel",)))(pos_col, invf_row)


def _rot(t, c, sa, sb):
    half = ROT_DIM // 2
    return t * c + pltpu.roll(t, LANES - half, 1) * sa + pltpu.roll(t, half, 1) * sb


def _rot_t(t, c, sa, sb):
    half = ROT_DIM // 2
    return t * c + pltpu.roll(t * sa, half, 1) + pltpu.roll(t * sb, LANES - half, 1)


def swa_split(proj, tabs, n_heads, n_kv, name):
    s, width = proj.shape
    qd, kd = n_heads * HEAD_DIM, n_kv * HEAD_DIM
    tr = _tile(s, 256)

    def body(p_ref, c_ref, sa_ref, sb_ref, q_ref, k_ref, v_ref):
        c, sa, sb = c_ref[...], sa_ref[...], sb_ref[...]
        low = lax.broadcasted_iota(jnp.int32, (1, LANES), 1) < HEAD_DIM
        for g in range(qd // LANES):
            q_ref[:, g * LANES:(g + 1) * LANES] = _rot(p_ref[:, g * LANES:(g + 1) * LANES], c, sa, sb).astype(BF16)
        for g in range(kd // LANES):
            for src, dst, rot in ((qd, k_ref, True), (qd + kd, v_ref, False)):
                t = p_ref[:, src + g * LANES:src + (g + 1) * LANES]
                t = _rot(t, c, sa, sb) if rot else t
                sw = pltpu.roll(t, HEAD_DIM, 1)
                dst[:, (2 * g) * LANES:(2 * g + 1) * LANES] = jnp.where(low, t, sw).astype(BF16)
                dst[:, (2 * g + 1) * LANES:(2 * g + 2) * LANES] = jnp.where(low, sw, t).astype(BF16)

    return _pcall(body, name=name, grid=(s // tr,),
                  in_specs=[_row_spec(tr, width)] + [_row_spec(tr, LANES)] * 3,
                  out_specs=[_row_spec(tr, qd), _row_spec(tr, 2 * kd), _row_spec(tr, 2 * kd)],
                  out_shape=[jax.ShapeDtypeStruct((s, qd), BF16), jax.ShapeDtypeStruct((s, 2 * kd), BF16),
                             jax.ShapeDtypeStruct((s, 2 * kd), BF16)],
                  compiler_params=_cp(("parallel",)))(proj, *tabs)


def swa_merge_bwd(dq, dkd, dvd, tabs, name):
    s, qd = dq.shape
    kd = dkd.shape[1] // 2
    width = qd + 2 * kd
    tr = _tile(s, 256)

    def body(dq_ref, dk_ref, dv_ref, c_ref, sa_ref, sb_ref, o_ref):
        c, sa, sb = c_ref[...], sa_ref[...], sb_ref[...]
        low = lax.broadcasted_iota(jnp.int32, (1, LANES), 1) < HEAD_DIM
        for g in range(qd // LANES):
            t = dq_ref[:, g * LANES:(g + 1) * LANES].astype(F32)
            o_ref[:, g * LANES:(g + 1) * LANES] = _rot_t(t, c, sa, sb).astype(BF16)
        for g in range(kd // LANES):
            for dst, src, rot in ((qd, dk_ref, True), (qd + kd, dv_ref, False)):
                e = src[:, (2 * g) * LANES:(2 * g + 1) * LANES]
                o = src[:, (2 * g + 1) * LANES:(2 * g + 2) * LANES]
                t = jnp.where(low, e + pltpu.roll(e, HEAD_DIM, 1), o + pltpu.roll(o, HEAD_DIM, 1))
                t = _rot_t(t, c, sa, sb) if rot else t
                o_ref[:, dst + g * LANES:dst + (g + 1) * LANES] = t.astype(BF16)

    return _pcall(body, name=name, grid=(s // tr,),
                  in_specs=[_row_spec(tr, qd), _row_spec(tr, 2 * kd), _row_spec(tr, 2 * kd)] + [_row_spec(tr, LANES)] * 3,
                  out_specs=_row_spec(tr, width), out_shape=jax.ShapeDtypeStruct((s, width), BF16),
                  compiler_params=_cp(("parallel",)))(dq, dkd, dvd, *tabs)


def _halves():
    lane_half = lax.broadcasted_iota(jnp.int32, (1, LANES), 1) // HEAD_DIM
    return [lane_half == 0, lane_half == 1]


def _nt(a, b):
    return lax.dot_general(a, b, _DIMS["nt"], preferred_element_type=F32)


def _tn(a, b):
    return lax.dot_general(a, b, _DIMS["tn"], preferred_element_type=F32)


def _band_mask(i, tq):
    w = tq + WINDOW
    rel = lax.broadcasted_iota(jnp.int32, (tq, w), 1) - lax.broadcasted_iota(jnp.int32, (tq, w), 0)
    first = lax.broadcasted_iota(jnp.int32, (tq, w), 1) >= jnp.where(i > 0, 0, WINDOW)
    return (rel >= 1) & (rel <= WINDOW) & first


def swa_fwd(q, kd, vd, sink_row, name):
    s, qd = q.shape
    tq = min(256, s)
    r = tq // WINDOW
    pairs = qd // LANES
    group_pairs = pairs // (kd.shape[1] // LANES)
    scale = HEAD_DIM ** -0.5

    def body(q_ref, kp_ref, kc_ref, vp_ref, vc_ref, sk_ref, o_ref, lse_ref):
        i = pl.program_id(1)
        k = jnp.concatenate([kp_ref[...], kc_ref[...]], axis=0)
        v = jnp.concatenate([vp_ref[...], vc_ref[...]], axis=0)
        mask = _band_mask(i, tq)
        q2 = q_ref[...]
        outs, lses = [], []
        for a, hm in enumerate(_halves()):
            sc = _nt(jnp.where(hm, q2, jnp.zeros_like(q2)), k) * scale
            sc = jnp.where(mask, sc, NEG)
            sink = sk_ref[:, a * HEAD_DIM:a * HEAD_DIM + 1]
            m = jnp.maximum(jnp.max(sc, axis=-1, keepdims=True), sink)
            p = jnp.exp(sc - m)
            den = jnp.sum(p, axis=-1, keepdims=True) + jnp.exp(sink - m)
            outs.append(jnp.dot(p.astype(BF16), v, preferred_element_type=F32) / den)
            lses.append(m + jnp.log(den))
        hm0 = _halves()[0]
        o_ref[...] = jnp.where(hm0, outs[0], outs[1]).astype(BF16)
        lse_ref[...] = jnp.where(hm0, lses[0], lses[1])

    prev = lambda p, i: (jnp.maximum(i * r - 1, 0), p // group_pairs)
    cur = lambda p, i: (i, p // group_pairs)
    blk = pl.BlockSpec((tq, LANES), lambda p, i: (i, p))
    return _pcall(body, name=name, grid=(pairs, s // tq),
                  in_specs=[blk, pl.BlockSpec((WINDOW, LANES), prev), pl.BlockSpec((tq, LANES), cur),
                            pl.BlockSpec((WINDOW, LANES), prev), pl.BlockSpec((tq, LANES), cur),
                            pl.BlockSpec((1, LANES), lambda p, i: (0, p))],
                  out_specs=[blk, blk],
                  out_shape=[jax.ShapeDtypeStruct((s, qd), BF16), jax.ShapeDtypeStruct((s, qd), F32)],
                  compiler_params=_cp(("parallel", "parallel")))(q, kd, kd, vd, vd, sink_row)


def swa_bwd(q, kd, vd, sink_row, out, lse, dout, name):
    s, qd = q.shape
    tq = min(256, s)
    r = tq // WINDOW
    n_kv = kd.shape[1] // LANES
    gw = qd // n_kv
    scale = HEAD_DIM ** -0.5

    def body(q_ref, kp_ref, kc_ref, vp_ref, vc_ref, sk_ref, o_ref, lse_ref, do_ref, dq_ref, dk_ref, dv_ref, dsk_ref):
        i = pl.program_id(1)

        @pl.when(i == 0)
        def _():
            dk_ref[...] = jnp.zeros_like(dk_ref)
            dv_ref[...] = jnp.zeros_like(dv_ref)
            dsk_ref[...] = jnp.zeros_like(dsk_ref)

        k = jnp.concatenate([kp_ref[...], kc_ref[...]], axis=0)
        v = jnp.concatenate([vp_ref[...], vc_ref[...]], axis=0)
        mask = _band_mask(i, tq)
        dk = jnp.zeros((tq + WINDOW, LANES), F32)
        dv = jnp.zeros((tq + WINDOW, LANES), F32)
        for pp in range(gw // LANES):
            cols = slice(pp * LANES, (pp + 1) * LANES)
            q2, do2 = q_ref[:, cols], do_ref[:, cols]
            prod = do2.astype(F32) * o_ref[:, cols].astype(F32)
            dq2 = jnp.zeros((tq, LANES), F32)
            dsk = jnp.zeros((1, LANES), F32)
            for a, hm in enumerate(_halves()):
                qa = jnp.where(hm, q2, jnp.zeros_like(q2))
                doa = jnp.where(hm, do2, jnp.zeros_like(do2))
                lse_a = lse_ref[:, pp * LANES + a * HEAD_DIM:pp * LANES + a * HEAD_DIM + 1]
                sc = jnp.where(mask, _nt(qa, k) * scale, NEG)
                p = jnp.exp(sc - lse_a)
                delta = jnp.sum(jnp.where(hm, prod, 0.0), axis=-1, keepdims=True)
                ds = (p * (_nt(doa, v) - delta) * scale).astype(BF16)
                dv = dv + _tn(p.astype(BF16), doa)
                dk = dk + _tn(ds, qa)
                dq2 = dq2 + jnp.where(hm, jnp.dot(ds, k, preferred_element_type=F32), 0.0)
                sink = sk_ref[:, pp * LANES + a * HEAD_DIM:pp * LANES + a * HEAD_DIM + 1]
                dsink = -jnp.sum(jnp.exp(sink - lse_a) * delta, axis=0, keepdims=True)
                dsk = dsk + jnp.where(hm, dsink, 0.0)
            dq_ref[:, cols] = dq2.astype(BF16)
            dsk_ref[0:1, cols] += dsk
        start = pl.multiple_of(i * tq, tq)
        dk_ref[pl.ds(start, tq), :] += dk[WINDOW:, :]
        dv_ref[pl.ds(start, tq), :] += dv[WINDOW:, :]

        @pl.when(i > 0)
        def _():
            before = pl.multiple_of(i * tq - WINDOW, WINDOW)
            dk_ref[pl.ds(before, WINDOW), :] += dk[:WINDOW, :]
            dv_ref[pl.ds(before, WINDOW), :] += dv[:WINDOW, :]

    prev = lambda g, i: (jnp.maximum(i * r - 1, 0), g)
    cur = lambda g, i: (i, g)
    wide = pl.BlockSpec((tq, gw), cur)
    full = pl.BlockSpec((s, LANES), lambda g, i: (0, g))
    return _pcall(body, name=name, grid=(n_kv, s // tq),
                  in_specs=[wide, pl.BlockSpec((WINDOW, LANES), prev), pl.BlockSpec((tq, LANES), cur),
                            pl.BlockSpec((WINDOW, LANES), prev), pl.BlockSpec((tq, LANES), cur),
                            pl.BlockSpec((1, gw), lambda g, i: (0, g)), wide, wide, wide],
                  out_specs=[wide, full, full, pl.BlockSpec((8, gw), lambda g, i: (0, g))],
                  out_shape=[jax.ShapeDtypeStruct((s, qd), BF16), jax.ShapeDtypeStruct(kd.shape, F32),
                             jax.ShapeDtypeStruct(kd.shape, F32), jax.ShapeDtypeStruct((8, qd), F32)],
                  compiler_params=_cp(("parallel", "arbitrary"), VMEM_BIG))(q, kd, kd, vd, vd, sink_row, out, lse, dout)


def forget_cumsum(f_logit, b_row, name):
    s = f_logit.shape[0]

    def body(f_ref, b_ref, c_ref):
        z = f_ref[...] + b_ref[...]
        acc = jnp.minimum(z, 0.0) - jnp.log(1.0 + jnp.exp(-jnp.abs(z)))
        row = lax.broadcasted_iota(jnp.int32, (s, LANES), 0)
        d = 1
        while d < s:
            acc = acc + jnp.where(row >= d, pltpu.roll(acc, d, 0), 0.0)
            d *= 2
        c_ref[...] = acc

    return _pcall(body, name=name, in_specs=[pl.BlockSpec((s, LANES), lambda: (0, 0)), pl.BlockSpec((1, LANES), lambda: (0, 0))],
                  out_specs=pl.BlockSpec((s, LANES), lambda: (0, 0)), out_shape=jax.ShapeDtypeStruct((s, LANES), F32),
                  compiler_params=_cp(None, VMEM_BIG))(f_logit, b_row)


def forget_gate_bwd(dc, f_logit, b_row, name):
    s = dc.shape[0]

    def body(dc_ref, f_ref, b_ref, df_ref, db_ref):
        acc = dc_ref[...]
        row = lax.broadcasted_iota(jnp.int32, (s, LANES), 0)
        d = 1
        while d < s:
            acc = acc + jnp.where(row < s - d, pltpu.roll(acc, s - d, 0), 0.0)
            d *= 2
        df = acc * jax.nn.sigmoid(-(f_ref[...] + b_ref[...]))
        df_ref[...] = df.astype(BF16)
        db_ref[...] = jnp.sum(df, axis=0, keepdims=True)

    whole = pl.BlockSpec((s, LANES), lambda: (0, 0))
    vec = pl.BlockSpec((1, LANES), lambda: (0, 0))
    return _pcall(body, name=name, in_specs=[whole, whole, vec], out_specs=[whole, vec],
                  out_shape=[jax.ShapeDtypeStruct((s, LANES), BF16), jax.ShapeDtypeStruct((1, LANES), F32)],
                  compiler_params=_cp(None, VMEM_BIG))(dc, f_logit, b_row)


EXTRA = HEAD_DIM
N_PIECES = 3


def _pieces(v):
    hi = v.astype(BF16).astype(F32)
    mid = (v - hi).astype(BF16).astype(F32)
    return hi, mid, (v - hi - mid).astype(BF16).astype(F32)


def _slot(main, lane, extras=None, ones_at=None):
    out = jnp.where(lane < HEAD_DIM, main, 0.0)
    if extras is not None:
        for r, e in enumerate(extras):
            out = jnp.where(lane == EXTRA + r, e, out)
    if ones_at is not None:
        out = jnp.where((lane >= ones_at) & (lane < ones_at + N_PIECES), 1.0, out)
    return out


def _lane_iota():
    return lax.broadcasted_iota(jnp.int32, (1, LANES), 1)


def fox_split(qkv, c, n_heads, name):
    s = qkv.shape[0]
    hd = n_heads * HEAD_DIM
    tr = _tile(s, 256)
    scale = HEAD_DIM ** -0.5

    def body(x_ref, c_ref, q_ref, k_ref, v_ref):
        lane = _lane_iota()
        cv = c_ref[...]
        for g in range(hd // LANES):
            for part, dst in enumerate((q_ref, k_ref, v_ref)):
                t = x_ref[:, part * hd + g * LANES:part * hd + (g + 1) * LANES].astype(F32)
                for a, main in enumerate((t, pltpu.roll(t, HEAD_DIM, 1))):
                    h = 2 * g + a
                    if part == 0:
                        val = _slot(main * scale, lane, ones_at=EXTRA)
                    elif part == 1:
                        ch = jnp.sum(jnp.where(lane == h, cv, 0.0), axis=1, keepdims=True)
                        val = _slot(main, lane, extras=_pieces(-ch), ones_at=EXTRA + N_PIECES)
                    else:
                        val = _slot(main, lane, ones_at=EXTRA)
                    dst[:, h * LANES:(h + 1) * LANES] = val.astype(BF16)

    slots = jax.ShapeDtypeStruct((s, n_heads * LANES), BF16)
    return _pcall(body, name=name, grid=(s // tr,), in_specs=[_row_spec(tr, 3 * hd), _row_spec(tr, LANES)],
                  out_specs=[_row_spec(tr, n_heads * LANES)] * 3, out_shape=[slots] * 3,
                  compiler_params=_cp(("parallel",), VMEM_BIG))(qkv, c)


def _causal_keep(t):
    return lax.broadcasted_iota(jnp.int32, (t, t), 1) <= lax.broadcasted_iota(jnp.int32, (t, t), 0)


def fox_fwd(qs, ks, vs, n_heads, name):
    s = qs.shape[0]
    pairs = n_heads // 2
    t = min(512, s)
    wide = 2 * LANES

    def body(q_ref, k_ref, v_ref, o_ref, qb_ref, acc_ref, m_ref):
        i = pl.program_id(1)
        lane = _lane_iota()
        acc_ref[...] = jnp.zeros_like(acc_ref)
        m_ref[...] = jnp.full_like(m_ref, NEG)

        def step(j, diagonal):
            rows = pl.ds(pl.multiple_of(j * t, t), t)
            for a in range(2):
                cols = slice(a * LANES, (a + 1) * LANES)
                sc = _nt(q_ref[:, cols], k_ref[rows, cols])
                if diagonal:
                    sc = jnp.where(_causal_keep(t), sc, NEG)
                m_old = m_ref[a]
                m_new = jnp.maximum(m_old, jnp.max(sc, axis=-1, keepdims=True))
                p = jnp.exp(sc - jnp.tile(m_new, (1, t // LANES)))
                acc_ref[a] = jnp.exp(m_old - m_new) * acc_ref[a] + jnp.dot(p.astype(BF16), v_ref[rows, cols],
                                                                              preferred_element_type=F32)
                m_ref[a] = m_new

        def two_steps(j2, carry):
            step(2 * j2, False)
            step(2 * j2 + 1, False)
            return carry

        lax.fori_loop(0, i // 2, two_steps, 0)

        @pl.when(i % 2 == 1)
        def _():
            step(i - 1, False)

        step(i, True)
        outs = []
        for a in range(2):
            cols = slice(a * LANES, (a + 1) * LANES)
            acc = acc_ref[a]
            norm = acc[:, EXTRA:EXTRA + 1]
            outs.append(acc / norm)
            neg_lse = _pieces(-(m_ref[a] + jnp.log(norm)))
            qb = q_ref[:, cols].astype(F32)
            for r in range(N_PIECES):
                qb = jnp.where(lane == EXTRA + N_PIECES + r, neg_lse[r], qb)
            qb_ref[:, cols] = qb.astype(BF16)
        o_ref[...] = jnp.where(lane < HEAD_DIM, outs[0], pltpu.roll(outs[1], HEAD_DIM, 1)).astype(BF16)

    qblk = pl.BlockSpec((t, wide), lambda p, i: (i, p))
    whole = pl.BlockSpec((s, wide), lambda p, i: (0, p))
    return _pcall(body, name=name, grid=(pairs, s // t), in_specs=[qblk, whole, whole],
                  out_specs=[pl.BlockSpec((t, LANES), lambda p, i: (i, p)), qblk],
                  out_shape=[jax.ShapeDtypeStruct((s, n_heads * HEAD_DIM), BF16), jax.ShapeDtypeStruct(qs.shape, BF16)],
                  scratch_shapes=[pltpu.VMEM((2, t, LANES), F32)] * 2,
                  compiler_params=_cp(("parallel", "arbitrary"), VMEM_BIG))(qs, ks, vs)


def fox_dout_slots(dout, out, name):
    s, hd = dout.shape
    tr = _tile(s, 256)

    def body(d_ref, o_ref, s_ref):
        lane = _lane_iota()
        for g in range(hd // LANES):
            cols = slice(g * LANES, (g + 1) * LANES)
            d2 = d_ref[:, cols].astype(F32)
            prod = d2 * o_ref[:, cols].astype(F32)
            for a, main in enumerate((d2, pltpu.roll(d2, HEAD_DIM, 1))):
                delta = jnp.sum(jnp.where((lane // HEAD_DIM) == a, prod, 0.0), axis=1, keepdims=True)
                h = 2 * g + a
                s_ref[:, h * LANES:(h + 1) * LANES] = _slot(main, lane, extras=_pieces(-delta)).astype(BF16)

    return _pcall(body, name=name, grid=(s // tr,), in_specs=[_row_spec(tr, hd), _row_spec(tr, hd)],
                  out_specs=_row_spec(tr, 2 * hd), out_shape=jax.ShapeDtypeStruct((s, 2 * hd), BF16),
                  compiler_params=_cp(("parallel",)))(dout, out)


def fox_bwd(qb, ks, vs, dos, n_heads, name):
    s = qb.shape[0]
    pairs = n_heads // 2
    t = min(512, s)
    nblk = s // t
    wide = 2 * LANES

    def body(q_ref, k_ref, v_ref, do_ref, dq_ref, dk_ref, dv_ref, dka_ref, dva_ref):
        j = pl.program_id(1)

        @pl.when(j == 0)
        def _():
            dq_ref[...] = jnp.zeros_like(dq_ref)

        dka_ref[...] = jnp.zeros_like(dka_ref)
        dva_ref[...] = jnp.zeros_like(dva_ref)

        def step(i, diagonal):
            rows = pl.ds(pl.multiple_of(i * t, t), t)
            for a in range(2):
                cols = slice(a * LANES, (a + 1) * LANES)
                qa, doa, ka = q_ref[rows, cols], do_ref[rows, cols], k_ref[:, cols]
                sc = _nt(qa, ka)
                if diagonal:
                    sc = jnp.where(_causal_keep(t), sc, NEG)
                p = jnp.exp(sc)
                ds = (p * _nt(doa, v_ref[:, cols])).astype(BF16)
                dva_ref[a] += _tn(p.astype(BF16), doa)
                dka_ref[a] += _tn(ds, qa)
                dq_ref[rows, cols] += jnp.dot(ds, ka, preferred_element_type=F32)

        step(j, True)
        below = nblk - 1 - j

        def two_steps(i2, carry):
            step(j + 1 + 2 * i2, False)
            step(j + 2 + 2 * i2, False)
            return carry

        lax.fori_loop(0, below // 2, two_steps, 0)

        @pl.when(below % 2 == 1)
        def _():
            step(nblk - 1, False)

        for a in range(2):
            cols = slice(a * LANES, (a + 1) * LANES)
            dk_ref[:, cols] = dka_ref[a]
            dv_ref[:, cols] = dva_ref[a].astype(BF16)

    whole = pl.BlockSpec((s, wide), lambda p, j: (0, p))
    blk = pl.BlockSpec((t, wide), lambda p, j: (j, p))
    return _pcall(body, name=name, grid=(pairs, nblk), in_specs=[whole, blk, blk, whole], out_specs=[whole, blk, blk],
                  out_shape=[jax.ShapeDtypeStruct(qb.shape, F32), jax.ShapeDtypeStruct(qb.shape, F32),
                             jax.ShapeDtypeStruct(qb.shape, BF16)],
                  scratch_shapes=[pltpu.VMEM((2, t, LANES), F32)] * 2,
                  compiler_params=_cp(("parallel", "arbitrary"), VMEM_BIG))(qb, ks, vs, dos)


def fox_merge_bwd(dqs, dks, dvs, n_heads, name):
    s = dqs.shape[0]
    hd = n_heads * HEAD_DIM
    tr = _tile(s, 128)
    scale = HEAD_DIM ** -0.5

    def body(dq_ref, dk_ref, dv_ref, o_ref, dc_ref):
        lane = _lane_iota()
        dc = jnp.zeros((tr, LANES), F32)
        for g in range(hd // LANES):
            even = slice(2 * g * LANES, (2 * g + 1) * LANES)
            odd = slice((2 * g + 1) * LANES, (2 * g + 2) * LANES)
            for part, (src, mul) in enumerate(((dq_ref, scale), (dk_ref, 1.0), (dv_ref, 1.0))):
                dense = jnp.where(lane < HEAD_DIM, src[:, even].astype(F32), pltpu.roll(src[:, odd].astype(F32), HEAD_DIM, 1))
                o_ref[:, part * hd + g * LANES:part * hd + (g + 1) * LANES] = (dense * mul).astype(BF16)
            for a, cols in enumerate((even, odd)):
                both = jnp.where(lane == EXTRA + N_PIECES, dq_ref[:, cols], 0.0) - jnp.where(lane == EXTRA, dk_ref[:, cols], 0.0)
                dc = jnp.where(lane == 2 * g + a, jnp.sum(both, axis=1, keepdims=True), dc)
        dc_ref[...] = dc

    wide = n_heads * LANES
    return _pcall(body, name=name, grid=(s // tr,), in_specs=[_row_spec(tr, wide)] * 3,
                  out_specs=[_row_spec(tr, 3 * hd), _row_spec(tr, LANES)],
                  out_shape=[jax.ShapeDtypeStruct((s, 3 * hd), BF16), jax.ShapeDtypeStruct((s, LANES), F32)],
                  compiler_params=_cp(("parallel",), VMEM_BIG))(dqs, dks, dvs)


def _w_spec(shape, lead_map=None):
    _, r, c = shape[-3:]
    tr = _tile(r, 128)
    n_lead = len(shape) - 2
    if lead_map is None:
        lead_map = lambda l: (l,)
    return tr, pl.BlockSpec((None,) * n_lead + (tr, c), lambda l, i: (*lead_map(l), i, 0))


def cast_into_slot(w, layer, place, dtype, name):
    tr = _tile(w.shape[1], 128)
    blk = (None, tr, w.shape[2])

    def body(me_ref, c_ref, w_ref, o_ref):
        o_ref[...] = w_ref[...].astype(dtype)

    grid_spec = pltpu.PrefetchScalarGridSpec(
        num_scalar_prefetch=2, grid=(w.shape[1] // tr,),
        in_specs=[pl.BlockSpec(blk, lambda i, me, c: (layer, i, 0))],
        out_specs=pl.BlockSpec(blk, lambda i, me, c: (me[0], i, 0)))
    return _pcall(body, name=name, grid_spec=grid_spec, out_shape=jax.ShapeDtypeStruct((N_CHIPS,) + w.shape[1:], dtype),
                  compiler_params=_cp(("parallel",)))(*place, w)


def adamw(w, g, m, v, name):
    tr, spec = _w_spec(w.shape)

    def body(w_ref, g_ref, m_ref, v_ref, d_ref, mo_ref, vo_ref):
        gg = g_ref[...]
        mn = ADAM_B1 * m_ref[...] + (1.0 - ADAM_B1) * gg
        vn = ADAM_B2 * v_ref[...] + (1.0 - ADAM_B2) * (gg * gg)
        m_hat = mn / (1.0 - ADAM_B1 ** ADAM_STEP)
        v_hat = vn / (1.0 - ADAM_B2 ** ADAM_STEP)
        d_ref[...] = -ADAM_LR * (m_hat / (jnp.sqrt(v_hat) + ADAM_EPS) + ADAM_WD * w_ref[...])
        mo_ref[...] = mn
        vo_ref[...] = vn

    out = jax.ShapeDtypeStruct(w.shape, F32)
    return _pcall(body, name=name, grid=(w.shape[0], w.shape[1] // tr), in_specs=[spec] * 4, out_specs=[spec] * 3,
                  out_shape=[out] * 3, compiler_params=_cp(("parallel", "parallel")))(w, g, m, v)


def add_core_halves(g, recv, place, name):
    hr = recv.shape[1]
    tr = _tile(hr, 128)
    nb = hr // tr
    blk = (None, tr, g.shape[2])

    def body(me_ref, c_ref, g_ref, r_ref, o_ref):
        o_ref[...] = (g_ref[...].astype(F32) + r_ref[...].astype(F32)).astype(o_ref.dtype)

    grid_spec = pltpu.PrefetchScalarGridSpec(
        num_scalar_prefetch=2, grid=(N_CHIPS, nb),
        in_specs=[pl.BlockSpec(blk, lambda q, i, me, c: (q, c[0] * nb + i, 0)), pl.BlockSpec(blk, lambda q, i, me, c: (q, i, 0))],
        out_specs=pl.BlockSpec(blk, lambda q, i, me, c: (q, i, 0)))
    return _pcall(body, name=name, grid_spec=grid_spec, out_shape=jax.ShapeDtypeStruct(recv.shape, g.dtype),
                  compiler_params=_cp(("parallel", "parallel")))(*place, g, recv)


def add_chips(own, recv, into, layer, place, name):
    _, hr, cols = own.shape
    tr = _tile(hr, 128)
    nb = hr // tr
    blk = (None, tr, cols)

    def body(me_ref, c_ref, p0, p1, p2, p3, _, o_ref):
        o_ref[...] = ((p0[...].astype(F32) + p1[...].astype(F32)) + p2[...].astype(F32)) + p3[...].astype(F32)

    def peer(flip):
        return lambda i, me, c: (me[0] ^ flip, i, 0)

    grid_spec = pltpu.PrefetchScalarGridSpec(
        num_scalar_prefetch=2, grid=(nb,), in_specs=[pl.BlockSpec(blk, peer(f)) for f in (0, 2, 1, 3)] + [ANY],
        out_specs=pl.BlockSpec(blk, lambda i, me, c: (layer, c[0] * nb + i, 0)))
    return _pcall(body, name=name, grid_spec=grid_spec, out_shape=jax.ShapeDtypeStruct(into.shape, F32),
                  input_output_aliases={6: 0}, compiler_params=_cp(("parallel",)))(*place, own, recv, recv, recv, into)


def _place():
    x, y, c = lax.axis_index("x"), lax.axis_index("y"), lax.axis_index("c")
    others = [(1 - x, y), (x, 1 - y), (1 - x, 1 - y)]
    return x, y, c, others


def _chip_id(chip):
    return 2 * chip[0] + chip[1]


def _comm_call(body, name, n_in, out_shapes, n_sems, in_place=False):
    return _pcall(body, name=name, in_specs=[ANY] * n_in, out_specs=[ANY] * len(out_shapes), out_shape=out_shapes,
                  scratch_shapes=[pltpu.SemaphoreType.DMA((n_sems,)), pltpu.SemaphoreType.DMA((n_sems,))],
                  input_output_aliases={t: t for t in range(n_in)} if in_place else {},
                  compiler_params=pltpu.CompilerParams(has_side_effects=True))


def swap_core_halves(grads, name):
    n = len(grads)

    def body(*refs):
        ins, outs = refs[:n], refs[n:2 * n]
        send, recv = refs[2 * n:]
        x, y, c, _ = _place()
        cps = []
        for t in range(n):
            hr = ins[t].shape[1] // 2
            cp = pltpu.make_async_remote_copy(src_ref=ins[t].at[:, pl.ds((1 - c) * hr, hr)], dst_ref=outs[t],
                                              send_sem=send.at[t], recv_sem=recv.at[t],
                                              device_id=(x, y, 1 - c), device_id_type=MESH)
            cp.start()
            cps.append(cp)
        for cp in cps:
            cp.wait()

    outs = [jax.ShapeDtypeStruct((a.shape[0], a.shape[1] // 2) + a.shape[2:], a.dtype) for a in grads]
    return _comm_call(body, name, n, outs, n)(*grads)


HBM = pl.BlockSpec(memory_space=pltpu.HBM)
SEM = pl.BlockSpec(memory_space=pltpu.SEMAPHORE)
DATAFLOW = pltpu.SideEffectType.DATAFLOW_SIDE_EFFECTING


def _in_hbm(a):
    return pltpu.with_memory_space_constraint(a, pltpu.HBM)


def gather_start(slots, name):
    n = len(slots)

    def body(*refs):
        bufs = refs[:n]
        send, recv = refs[n], refs[n + 1]
        token = refs[-1]
        x, y, c, others = _place()
        me = _chip_id((x, y))
        for t in range(n):
            for j, chip in enumerate(others):
                pltpu.make_async_remote_copy(src_ref=bufs[t].at[me], dst_ref=bufs[t].at[me],
                                             send_sem=send.at[3 * t + j], recv_sem=recv.at[3 * t + j],
                                             device_id=(*chip, c), device_id_type=MESH).start()
        token[...] = jnp.zeros_like(token)

    sems = pltpu.SemaphoreType.DMA((3 * n,))
    res = _pcall(body, name=name, in_specs=[HBM] * n,
                 out_shape=(sems, sems, *[pltpu.HBM(a.shape, a.dtype) for a in slots], jax.ShapeDtypeStruct((8, LANES), F32)),
                 out_specs=(SEM, SEM, *[HBM] * n, pl.BlockSpec(memory_space=pltpu.VMEM)),
                 input_output_aliases={i: 2 + i for i in range(n)},
                 compiler_params=pltpu.CompilerParams(has_side_effects=DATAFLOW))(*[_in_hbm(a) for a in slots])
    return res[0], res[1], list(res[2:2 + n]), res[-1]


def gather_wait(send, recv, slot, t, after, name):
    def body(buf, send_ref, recv_ref, after_ref, out):
        x, y, c, others = _place()
        me = _chip_id((x, y))
        for j, chip in enumerate(others):
            pltpu.make_async_remote_copy(src_ref=buf.at[me], dst_ref=buf.at[_chip_id(chip)],
                                         send_sem=send_ref.at[3 * t + j], recv_sem=recv_ref.at[3 * t + j],
                                         device_id=(*chip, c), device_id_type=MESH).wait()

    return _pcall(body, name=name, in_specs=[HBM, SEM, SEM, ANY], out_shape=pltpu.HBM(slot.shape, slot.dtype),
                  out_specs=HBM, input_output_aliases={0: 0},
                  compiler_params=pltpu.CompilerParams(has_side_effects=DATAFLOW))(slot, send, recv, after)


def scatter_start(sums, name):
    n = len(sums)

    def body(*refs):
        ins, lands = refs[:n], refs[n:2 * n]
        send, recv = refs[2 * n], refs[2 * n + 1]
        token = refs[-1]
        x, y, c, others = _place()
        me = _chip_id((x, y))
        for t in range(n):
            for j, chip in enumerate(others):
                pltpu.make_async_remote_copy(src_ref=ins[t].at[_chip_id(chip)], dst_ref=lands[t].at[me],
                                             send_sem=send.at[3 * t + j], recv_sem=recv.at[3 * t + j],
                                             device_id=(*chip, c), device_id_type=MESH).start()
        token[...] = jnp.zeros_like(token)

    bufs = [pltpu.HBM(a.shape, a.dtype) for a in sums]
    sems = pltpu.SemaphoreType.DMA((3 * n,))
    res = _pcall(body, name=name, in_specs=[HBM] * (2 * n),
                 out_shape=(sems, sems, *bufs, *bufs, jax.ShapeDtypeStruct((8, LANES), F32)),
                 out_specs=(SEM, SEM, *[HBM] * (2 * n), pl.BlockSpec(memory_space=pltpu.VMEM)),
                 input_output_aliases={i: 2 + i for i in range(2 * n)},
                 compiler_params=pltpu.CompilerParams(has_side_effects=DATAFLOW))(
        *[_in_hbm(a) for a in sums], *[_in_hbm(lax.empty(a.shape, a.dtype)) for a in sums])
    return res[0], res[1], list(res[2:2 + n]), list(res[2 + n:2 + 2 * n]), res[-1]


def scatter_wait(send, recv, sums, lands, after, name):
    n = len(sums)

    def body(*refs):
        ins, bufs = refs[:n], refs[n:2 * n]
        send_ref, recv_ref = refs[2 * n], refs[2 * n + 1]
        x, y, c, others = _place()
        me = _chip_id((x, y))
        for t in range(n):
            for j, chip in enumerate(others):
                cp = pltpu.make_async_remote_copy(src_ref=ins[t].at[_chip_id(chip)], dst_ref=bufs[t].at[_chip_id(chip)],
                                                  send_sem=send_ref.at[3 * t + j], recv_sem=recv_ref.at[3 * t + j],
                                                  device_id=(*chip, c), device_id_type=MESH)
                cp.wait_send()
                cp.wait_recv()

    shapes = [pltpu.HBM(a.shape, a.dtype) for a in sums]
    res = _pcall(body, name=name, in_specs=[HBM] * (2 * n) + [SEM, SEM, ANY],
                 out_shape=(*shapes, *shapes), out_specs=tuple([HBM] * (2 * n)),
                 input_output_aliases={i: i for i in range(2 * n)},
                 compiler_params=pltpu.CompilerParams(has_side_effects=DATAFLOW))(*sums, *lands, send, recv, after)
    return list(res[:n]), list(res[n:])


def join_core_halves(fulls, name):
    n = len(fulls)

    def body(*refs):
        bufs = refs[n:2 * n]
        send, recv = refs[2 * n:]
        x, y, c, _ = _place()
        cps = []
        for t in range(n):
            hr = bufs[t].shape[1] // 2
            mine = bufs[t].at[:, pl.ds(c * hr, hr)]
            cp = pltpu.make_async_remote_copy(src_ref=mine, dst_ref=mine, send_sem=send.at[t], recv_sem=recv.at[t],
                                              device_id=(x, y, 1 - c), device_id_type=MESH)
            cp.start()
            cps.append(cp)
        for t in range(n):
            hr = bufs[t].shape[1] // 2
            theirs = bufs[t].at[:, pl.ds((1 - c) * hr, hr)]
            cps[t].wait_send()
            pltpu.make_async_remote_copy(src_ref=theirs, dst_ref=theirs, send_sem=send.at[t], recv_sem=recv.at[t],
                                         device_id=(x, y, c), device_id_type=MESH).wait_recv()

    outs = [jax.ShapeDtypeStruct(a.shape, a.dtype) for a in fulls]
    return _comm_call(body, name, n, outs, n, in_place=True)(*fulls)


def kernel(x, positions, norm_gains, swa_w_in, swa_sinks, swa_w_out, fox_w_in, fox_b_f, fox_w_out, ffn_w_gate_up, ffn_w_down, loss_target, m_norm_gains, m_swa_w_in, m_swa_sinks, m_swa_w_out, m_fox_w_in, m_fox_b_f, m_fox_w_out, m_ffn_w_gate_up, m_ffn_w_down, v_norm_gains, v_swa_w_in, v_swa_sinks, v_swa_w_out, v_fox_w_in, v_fox_b_f, v_fox_w_out, v_ffn_w_gate_up, v_ffn_w_down):
    s, d = x.shape[1], x.shape[2]
    depth = norm_gains.shape[0]
    n_heads = d // HEAD_DIM
    hd = n_heads * HEAD_DIM
    n_kv = (swa_w_in.shape[2] * N_CHIPS // HEAD_DIM - n_heads) // 2
    ff = ffn_w_down.shape[1] * N_CHIPS
    fox_cols = fox_w_in.shape[2]
    fox_pad = 3 * hd + LANES
    assert fox_cols * N_CHIPS == 3 * hd + n_heads and n_heads <= LANES
    x0 = x[0]
    target = loss_target[0]
    place = ((2 * lax.axis_index("x") + lax.axis_index("y")).astype(jnp.int32).reshape(1),
             lax.axis_index("c").astype(jnp.int32).reshape(1))

    order = [("norm_gains", norm_gains.reshape(1, depth * 4, norm_gains.shape[2]), 0, F32)]
    for layer in range(depth):
        j = layer // 2
        kind, w_i, w_o = ("swa", swa_w_in, swa_w_out) if layer % 2 == 0 else ("fox", fox_w_in, fox_w_out)
        order += [(f"{kind}_w_in_{j}", w_i, j, BF16), (f"{kind}_w_out_{j}", w_o, j, BF16),
                  (f"ffn_w_gate_up_{layer}", ffn_w_gate_up, layer, BF16), (f"ffn_w_down_{layer}", ffn_w_down, layer, BF16)]
    slots = [cast_into_slot(w, l, place, dt, "cast_" + nm) for nm, w, l, dt in order]
    g_send, g_recv, slots, g_token = gather_start(slots, "gather_start")
    slot_of = {entry[0]: t for t, entry in enumerate(order)}

    def weight(nm, after):
        t = slot_of[nm]
        return gather_wait(g_send, g_recv, slots[t], t, after, "gather_wait_" + nm)

    gains = jnp.transpose(weight("norm_gains", g_token), (1, 0, 2)).reshape(depth * 4, d)

    def gain(layer, which):
        return gains[layer * 4 + which][None, :]

    def fox_weight(w_in):
        parts = [w_in[q] for q in range(N_CHIPS)]
        parts.append(jnp.zeros((d, fox_pad - fox_cols * N_CHIPS), BF16))
        return jnp.concatenate(parts, axis=1)

    inv_freq = ROPE_THETA ** (-jnp.arange(0, ROT_DIM, 2, dtype=F32) / ROT_DIM)
    lane_d = jnp.arange(LANES) % HEAD_DIM
    invf_row = jnp.where(lane_d < ROT_DIM, inv_freq[lane_d % (ROT_DIM // 2)], 0.0)[None, :]
    tabs = rope_tables(positions.reshape(s, 1), invf_row, "rope_tables")

    n_sh_in = swa_w_in.shape[2]
    gu_sh = ffn_w_gate_up.shape[2]
    tn_gu = gu_sh // 2 if (gu_sh // 2) % LANES == 0 else gu_sh
    down_sh = ffn_w_down.shape[1]
    out_sh = swa_w_out.shape[1]
    tm = min(1024, s)

    saved = []
    xin = x0
    h = prenorm(xin, gain(0, 0), "prenorm_first")
    for layer in range(depth):
        j = layer // 2
        rec = {"x_in": xin, "h1": h}
        if layer % 2 == 0:
            w_in = weight(f"swa_w_in_{j}", h)
            proj = mm_nn(h, w_in, lambda n, k: (n, k, 0), n_sh_in * N_CHIPS, tm=tm, tn=n_sh_in, tk=d,
                         out_dtype=F32, name=f"swa_proj_{j}")
            q, kd, vd = swa_split(proj, tabs, n_heads, n_kv, f"swa_split_{j}")
            sink_row = jnp.repeat(swa_sinks[j], HEAD_DIM)[None, :]
            attn, lse = swa_fwd(q, kd, vd, sink_row, f"swa_fwd_{j}")
            rec.update(q=q, kd=kd, vd=vd, sink_row=sink_row, lse=lse, w_in=w_in)
            w_out = weight(f"swa_w_out_{j}", attn)
        else:
            wf = fox_weight(weight(f"fox_w_in_{j}", h))
            tn = 3 * hd // 6 if (3 * hd // 6) % LANES == 0 else LANES
            qkv = mm_nn(h, wf, lambda n, k: (k, n), 3 * hd, tm=tm, tn=tn, tk=d, out_dtype=BF16, name=f"fox_proj_{j}")
            f_off = 3 * hd // LANES
            f_logit = mm_nn(h, wf, lambda n, k: (k, f_off + n), LANES, tm=tm, tn=LANES, tk=d, out_dtype=F32, name=f"fox_gate_{j}")
            b_row = jnp.pad(fox_b_f[j], (0, LANES - n_heads))[None, :]
            c = forget_cumsum(f_logit, b_row, f"fox_cumsum_{j}")
            qs, ks, vs = fox_split(qkv, c, n_heads, f"fox_split_{j}")
            attn, qb = fox_fwd(qs, ks, vs, n_heads, f"fox_fwd_{j}")
            rec.update(wf=wf, f_logit=f_logit, b_row=b_row, qb=qb, ks=ks, vs=vs)
            w_out = weight(f"fox_w_out_{j}", attn)
        y = mm_nn(attn, w_out, lambda n, k: (k, 0, n), d, tm=tm, tn=d, tk=out_sh, out_dtype=F32, name=f"out_proj_{layer}")
        xmid, h2 = postnorm_residual(xin, y, gain(layer, 1), gain(layer, 2), f"postnorm_mixer_{layer}")
        npb = gu_sh // tn_gu
        w_gu = weight(f"ffn_w_gate_up_{layer}", h2)
        gu = mm_nn(h2, w_gu, lambda n, k: (n // npb, k, n % npb), 2 * ff, tm=tm, tn=tn_gu, tk=d, out_dtype=BF16, name=f"ffn_up_{layer}")
        act = swiglu_fwd(gu, f"swiglu_{layer}")
        w_down = weight(f"ffn_w_down_{layer}", act)
        y2 = mm_nn(act, w_down, lambda n, k: (k, 0, n), d, tm=tm, tn=d, tk=down_sh, out_dtype=F32, name=f"ffn_down_{layer}")
        rec.update(attn=attn, y=y, x_mid=xmid, h2=h2, gu=gu, act=act, y2=y2, w_out=w_out, w_gu=w_gu, w_down=w_down)
        saved.append(rec)
        if layer + 1 < depth:
            xin, h = postnorm_residual(xmid, y2, gain(layer, 3), gain(layer + 1, 0), f"postnorm_ffn_{layer}")
    dx, loss_blk = postnorm_loss(xmid, y2, gain(depth - 1, 3), target, "loss")

    finals = {nm: lax.empty(w.shape, F32) for nm, w in (("swa_w_in", swa_w_in), ("swa_w_out", swa_w_out), ("fox_w_in", fox_w_in),
                                                        ("fox_w_out", fox_w_out), ("ffn_w_gate_up", ffn_w_gate_up),
                                                        ("ffn_w_down", ffn_w_down), ("norm_gains", order[0][1]))}
    in_flight = []

    def finish_reduce(after):
        send_, recv_, sums_, lands_, keys = in_flight.pop()
        sums_, got = scatter_wait(send_, recv_, sums_, lands_, after, "reduce_scatter_wait_" + keys[0][2])
        for own, rcv, (key, l, nm) in zip(sums_, got, keys):
            finals[key] = add_chips(own, rcv, finals[key], l, place, "reduce_add_chips_" + nm)

    def start_reduce(partials, keys):
        if in_flight:
            finish_reduce(partials[0])
        halves = swap_core_halves(partials, "reduce_swap_" + keys[0][2])
        sums_ = [add_core_halves(g, r, place, "reduce_add_cores_" + k[2]) for g, r, k in zip(partials, halves, keys)]
        send_, recv_, sums_, lands_, token = scatter_start(sums_, "reduce_scatter_start_" + keys[0][2])
        in_flight.append((send_, recv_, sums_, lands_, keys))
        return token

    dgains = [None] * (depth * 4)
    dsinks = [None] * ((depth + 1) // 2)
    dbf = [None] * (depth // 2)
    tko = d
    for layer in reversed(range(depth)):
        j = layer // 2
        rec = saved[layer]
        dy2, dgains[layer * 4 + 3] = postnorm_bwd(dx, rec["y2"], gain(layer, 3), f"postnorm_ffn_bwd_{layer}")
        tnd = d
        g_down = mm_tn(rec["act"], dy2, lambda i, n: (i, 0, n), tka=down_sh, tn=tnd, tm=tm, out_block=(None, down_sh, tnd),
                       name=f"ffn_down_dw_{layer}", out_shape=jax.ShapeDtypeStruct((N_CHIPS, down_sh, d), BF16))
        dact = mm_nt(dy2, rec["w_down"], lambda o, n: (o, 0, n), ff, tm=tm, tko=down_sh, tn=d, out_dtype=BF16, name=f"ffn_down_dx_{layer}")
        dgu = swiglu_bwd(dact, rec["gu"], f"swiglu_bwd_{layer}")
        npb = gu_sh // tn_gu
        g_gu = mm_tn(rec["h2"], dgu, lambda i, n: (n // npb, i, n % npb), tka=tko, tn=tn_gu, tm=tm, out_block=(None, tko, tn_gu),
                     name=f"ffn_up_dw_{layer}", out_shape=jax.ShapeDtypeStruct((N_CHIPS, d, gu_sh), BF16))
        sent = start_reduce([g_gu, g_down], [("ffn_w_gate_up", layer, f"ffn_w_gate_up_{layer}"), ("ffn_w_down", layer, f"ffn_w_down_{layer}")])
        dh2 = mm_nt(dgu, rec["w_gu"], lambda o, n: (n // npb, o, n % npb), d, tm=tm, tko=tko, tn=tn_gu, out_dtype=F32,
                    name=f"ffn_up_dx_{layer}", after=sent)
        dxm, dgains[layer * 4 + 2] = prenorm_bwd(rec["x_mid"], dh2, dx, gain(layer, 2), f"prenorm_ffn_bwd_{layer}")
        dy, dgains[layer * 4 + 1] = postnorm_bwd(dxm, rec["y"], gain(layer, 1), f"postnorm_mixer_bwd_{layer}")
        g_out = mm_tn(rec["attn"], dy, lambda i, n: (i, 0, n), tka=out_sh, tn=tnd, tm=min(2048, s), out_block=(None, out_sh, tnd),
                      name=f"out_proj_dw_{layer}", out_shape=jax.ShapeDtypeStruct((N_CHIPS, out_sh, d), BF16))
        dattn = mm_nt(dy, rec["w_out"], lambda o, n: (o, 0, n), hd, tm=tm, tko=out_sh, tn=d, out_dtype=BF16, name=f"out_proj_dx_{layer}")
        if layer % 2 == 0:
            dq, dkd, dvd, dsk = swa_bwd(rec["q"], rec["kd"], rec["vd"], rec["sink_row"], rec["attn"], rec["lse"], dattn, f"swa_bwd_{j}")
            dsinks[j] = dsk[0].reshape(n_heads, HEAD_DIM)[:, 0]
            dproj = swa_merge_bwd(dq, dkd, dvd, tabs, f"swa_merge_bwd_{j}")
            g_in = mm_tn(rec["h1"], dproj, lambda i, n: (n, i, 0), tka=tko, tn=n_sh_in, tm=tm, out_block=(None, tko, n_sh_in),
                         name=f"swa_proj_dw_{j}", out_shape=jax.ShapeDtypeStruct((N_CHIPS, d, n_sh_in), BF16))
            sent = start_reduce([g_in, g_out], [("swa_w_in", j, f"swa_w_in_{j}"), ("swa_w_out", j, f"swa_w_out_{j}")])
            dh1 = mm_nt(dproj, rec["w_in"], lambda o, n: (n, o, 0), d, tm=tm, tko=tko, tn=n_sh_in, out_dtype=F32,
                        name=f"swa_proj_dx_{j}", after=sent)
        else:
            dos = fox_dout_slots(dattn, rec["attn"], f"fox_dout_slots_{j}")
            dqs, dks, dvs = fox_bwd(rec["qb"], rec["ks"], rec["vs"], dos, n_heads, f"fox_bwd_{j}")
            dqkv, dc = fox_merge_bwd(dqs, dks, dvs, n_heads, f"fox_merge_bwd_{j}")
            df, db = forget_gate_bwd(dc, rec["f_logit"], rec["b_row"], f"fox_gate_bwd_{j}")
            dbf[j] = db[0, :n_heads]
            dproj = jnp.concatenate([dqkv, df], axis=1)
            tn_f = LANES * max(k for k in range(1, 9) if (fox_pad // LANES) % k == 0)
            dwf = mm_tn(rec["h1"], dproj, lambda i, n: (i, n), tka=tko, tn=tn_f, tm=tm, out_block=(tko, tn_f),
                        name=f"fox_proj_dw_{j}", out_shape=jax.ShapeDtypeStruct((d, fox_pad), BF16))
            g_in = jnp.stack([dwf[:, q * fox_cols:(q + 1) * fox_cols] for q in range(N_CHIPS)])
            sent = start_reduce([g_in, g_out], [("fox_w_in", j, f"fox_w_in_{j}"), ("fox_w_out", j, f"fox_w_out_{j}")])
            dh1 = mm_nt(dproj, rec["wf"], lambda o, n: (o, n), d, tm=tm, tko=tko, tn=tn_f, out_dtype=F32,
                        name=f"fox_proj_dx_{j}", after=sent)
        dx, dgains[layer * 4] = prenorm_bwd(rec["x_in"], dh1, dxm, gain(layer, 0), f"prenorm_mixer_bwd_{layer}")
    grad_x = dx[None]

    dgain_full = jnp.concatenate(dgains, axis=0)
    start_reduce([jnp.transpose(dgain_full.reshape(depth * 4, N_CHIPS, -1), (1, 0, 2))], [("norm_gains", 0, "norm_gains")])
    finish_reduce(dx)
    keys = ["swa_w_in", "swa_w_out", "fox_w_in", "fox_w_out", "ffn_w_gate_up", "ffn_w_down", "norm_gains"]
    full = join_core_halves([finals[k] for k in keys], "reduce_join_cores")
    g_swa_in_f, g_swa_out_f, g_fox_in_f, g_fox_out_f, g_gu_f, g_down_f, g_gains_f = full
    g_gains_f = g_gains_f.reshape(norm_gains.shape)

    n_swa, n_fox = len(dsinks), len(dbf)
    small = jnp.concatenate([loss_blk[0, :1]] + dsinks + dbf)
    small = lax.psum(small, ("x", "y", "c"))
    loss = small[0]
    g_sinks = small[1:1 + n_swa * n_heads].reshape(n_swa, n_heads)
    g_bf = small[1 + n_swa * n_heads:].reshape(n_fox, n_heads)

    def pad_small(a):
        return jnp.pad(a, ((0, 8 - a.shape[0]), (0, LANES - a.shape[1])))[None]

    def update(w, g, m, v, nm):
        if w.ndim == 2:
            dl, mn, vn = adamw(pad_small(w), pad_small(g), pad_small(m), pad_small(v), "adamw_" + nm)
            return tuple(a[0, :w.shape[0], :w.shape[1]] for a in (dl, mn, vn))
        return adamw(w, g, m, v, "adamw_" + nm)

    grads = [g_gains_f, g_swa_in_f, g_sinks, g_swa_out_f, g_fox_in_f, g_bf, g_fox_out_f, g_gu_f, g_down_f]
    ws = [norm_gains, swa_w_in, swa_sinks, swa_w_out, fox_w_in, fox_b_f, fox_w_out, ffn_w_gate_up, ffn_w_down]
    ms = [m_norm_gains, m_swa_w_in, m_swa_sinks, m_swa_w_out, m_fox_w_in, m_fox_b_f, m_fox_w_out, m_ffn_w_gate_up, m_ffn_w_down]
    vs = [v_norm_gains, v_swa_w_in, v_swa_sinks, v_swa_w_out, v_fox_w_in, v_fox_b_f, v_fox_w_out, v_ffn_w_gate_up, v_ffn_w_down]
    nms = ["norm_gains", "swa_w_in", "swa_sinks", "swa_w_out", "fox_w_in", "fox_b_f", "fox_w_out", "ffn_w_gate_up", "ffn_w_down"]
    upd = [update(w, g, m, v, nm) for w, g, m, v, nm in zip(ws, grads, ms, vs, nms)]
    return (loss, grad_x, *grads, *[u[0] for u in upd], *[u[1] for u in upd], *[u[2] for u in upd])
```

```python
import jax
import jax.numpy as jnp
from jax import lax
from jax.experimental import pallas as pl
from jax.experimental.pallas import tpu as pltpu

F32 = jnp.float32
BF16 = jnp.bfloat16
HEAD_DIM = 64
LANES = 128
WINDOW = 128
ROPE_THETA = 500000.0
ROT_DIM = HEAD_DIM // 4
RMS_EPS = 1e-6
ADAM_LR, ADAM_B1, ADAM_B2, ADAM_EPS, ADAM_WD, ADAM_STEP = 0.001, 0.9, 0.999, 1e-08, 0.01, 10
NEG = -1e30
VMEM_BIG = 56 * 1024 * 1024
N_CHIPS = 4
MESH = pl.DeviceIdType.MESH
ANY = pl.BlockSpec(memory_space=pl.ANY)


def _pcall(body, **kw):
    return pl.pallas_call(body, **kw)


def _cp(sem=None, vmem=None):
    return pltpu.CompilerParams(dimension_semantics=sem, vmem_limit_bytes=vmem)


def _tile(n, pref):
    if n <= pref:
        return n
    t = pref - pref % 16
    while n % t:
        t -= 16
    return t


_DIMS = {"nn": (((1,), (0,)), ((), ())), "nt": (((1,), (1,)), ((), ())), "tn": (((0,), (0,)), ((), ()))}


def _matmul(kind, a, b, *, grid, a_spec, b_spec, o_spec, out_shape, acc_shape, name, after=None):
    nk = grid[2]
    dims = _DIMS[kind]

    def body(a_ref, b_ref, *rest):
        rest = rest[1:] if after is not None else rest
        o_ref = rest[0]

        def prod():
            return lax.dot_general(a_ref[...], b_ref[...], dims, preferred_element_type=F32)

        if nk == 1:
            o_ref[...] = prod().astype(o_ref.dtype)
        else:
            acc = rest[1]
            k = pl.program_id(2)

            @pl.when(k == 0)
            def _():
                acc[...] = prod()

            @pl.when(k > 0)
            def _():
                acc[...] += prod()

            @pl.when(k == nk - 1)
            def _():
                o_ref[...] = acc[...].astype(o_ref.dtype)

    args, in_specs = [a, b], [a_spec, b_spec]
    if after is not None:
        args.append(after)
        in_specs.append(ANY)
    return _pcall(body, name=name, grid=grid, in_specs=in_specs, out_specs=o_spec, out_shape=out_shape,
                  scratch_shapes=[] if nk == 1 else [pltpu.VMEM(acc_shape, F32)],
                  compiler_params=_cp(("parallel", "parallel", "arbitrary"), VMEM_BIG))(*args)


def mm_nn(a, w, w_map, n_out, *, tm, tn, tk, out_dtype, name):
    m, kdim = a.shape
    tm = min(tm, m)
    lead = (None,) * (w.ndim - 2)
    return _matmul("nn", a, w, grid=(m // tm, n_out // tn, kdim // tk),
                   a_spec=pl.BlockSpec((tm, tk), lambda i, j, k: (i, k)),
                   b_spec=pl.BlockSpec(lead + (tk, tn), lambda i, j, k: w_map(j, k)),
                   o_spec=pl.BlockSpec((tm, tn), lambda i, j, k: (i, j)),
                   out_shape=jax.ShapeDtypeStruct((m, n_out), out_dtype), acc_shape=(tm, tn), name=name)


def mm_nt(a, w, w_map, k_out, *, tm, tko, tn, out_dtype, name, after=None):
    m, ndim = a.shape
    tm = min(tm, m)
    lead = (None,) * (w.ndim - 2)
    return _matmul("nt", a, w, grid=(m // tm, k_out // tko, ndim // tn),
                   a_spec=pl.BlockSpec((tm, tn), lambda i, j, n: (i, n)),
                   b_spec=pl.BlockSpec(lead + (tko, tn), lambda i, j, n: w_map(j, n)),
                   o_spec=pl.BlockSpec((tm, tko), lambda i, j, n: (i, j)),
                   out_shape=jax.ShapeDtypeStruct((m, k_out), out_dtype), acc_shape=(tm, tko), name=name, after=after)


def mm_tn(a, b, o_map, *, tka, tn, tm, out_block, name, out_shape):
    m, kdim = a.shape
    n = b.shape[1]
    tm = min(tm, m)
    return _matmul("tn", a, b, grid=(kdim // tka, n // tn, m // tm),
                   a_spec=pl.BlockSpec((tm, tka), lambda i, j, mm: (mm, i)),
                   b_spec=pl.BlockSpec((tm, tn), lambda i, j, mm: (mm, j)),
                   o_spec=pl.BlockSpec(out_block, lambda i, j, mm: o_map(i, j)),
                   out_shape=out_shape, acc_shape=(tka, tn), name=name)


def _rstd(v):
    return lax.rsqrt(jnp.mean(v * v, axis=-1, keepdims=True) + RMS_EPS)


def _row_spec(tr, d):
    return pl.BlockSpec((tr, d), lambda i: (i, 0))


def _vec_spec(d):
    return pl.BlockSpec((1, d), lambda i: (0, 0))


def prenorm(x, g, name):
    s, d = x.shape
    tr = _tile(s, 256)

    def body(x_ref, g_ref, h_ref):
        v = x_ref[...]
        h_ref[...] = (v * _rstd(v) * g_ref[...]).astype(BF16)

    return _pcall(body, name=name, grid=(s // tr,), in_specs=[_row_spec(tr, d), _vec_spec(d)],
                  out_specs=_row_spec(tr, d), out_shape=jax.ShapeDtypeStruct((s, d), BF16),
                  compiler_params=_cp(("parallel",)))(x, g)


def postnorm_residual(x, y, g_post, g_next, name):
    s, d = x.shape
    tr = _tile(s, 256)

    def body(x_ref, y_ref, gp_ref, gn_ref, xo_ref, h_ref):
        v = y_ref[...]
        xn = x_ref[...] + v * _rstd(v) * gp_ref[...]
        xo_ref[...] = xn
        h_ref[...] = (xn * _rstd(xn) * gn_ref[...]).astype(BF16)

    return _pcall(body, name=name, grid=(s // tr,),
                  in_specs=[_row_spec(tr, d), _row_spec(tr, d), _vec_spec(d), _vec_spec(d)],
                  out_specs=[_row_spec(tr, d), _row_spec(tr, d)],
                  out_shape=[jax.ShapeDtypeStruct((s, d), F32), jax.ShapeDtypeStruct((s, d), BF16)],
                  compiler_params=_cp(("parallel",)))(x, y, g_post, g_next)


def postnorm_loss(x, y, g_post, target, name):
    s, d = x.shape
    tr = _tile(s, 256)

    def body(x_ref, y_ref, gp_ref, t_ref, dx_ref, loss_ref):
        v = y_ref[...]
        err = x_ref[...] + v * _rstd(v) * gp_ref[...] - t_ref[...]
        dx_ref[...] = err / d
        part = 0.5 * jnp.sum(jnp.mean(err * err, axis=-1, keepdims=True), axis=0, keepdims=True)

        @pl.when(pl.program_id(0) == 0)
        def _():
            loss_ref[...] = jnp.zeros_like(loss_ref)

        loss_ref[...] += part

    return _pcall(body, name=name, grid=(s // tr,),
                  in_specs=[_row_spec(tr, d), _row_spec(tr, d), _vec_spec(d), _row_spec(tr, d)],
                  out_specs=[_row_spec(tr, d), pl.BlockSpec((8, LANES), lambda i: (0, 0))],
                  out_shape=[jax.ShapeDtypeStruct((s, d), F32), jax.ShapeDtypeStruct((8, LANES), F32)],
                  compiler_params=_cp(("arbitrary",)))(x, y, g_post, target)


def _norm_bwd(v, g, dz):
    r = _rstd(v)
    vhat = v * r
    u = dz * g
    dv = r * (u - vhat * jnp.mean(u * vhat, axis=-1, keepdims=True))
    return dv, jnp.sum(dz * vhat, axis=0, keepdims=True)


def _acc_rows(ref, val):
    @pl.when(pl.program_id(0) == 0)
    def _():
        ref[...] = jnp.zeros_like(ref)

    ref[...] += val


def postnorm_bwd(dz, y, g, name):
    s, d = y.shape
    tr = _tile(s, 256)

    def body(dz_ref, y_ref, g_ref, dy_ref, dg_ref):
        dv, dg = _norm_bwd(y_ref[...], g_ref[...], dz_ref[...])
        dy_ref[...] = dv.astype(BF16)
        _acc_rows(dg_ref, dg)

    return _pcall(body, name=name, grid=(s // tr,), in_specs=[_row_spec(tr, d), _row_spec(tr, d), _vec_spec(d)],
                  out_specs=[_row_spec(tr, d), _vec_spec(d)],
                  out_shape=[jax.ShapeDtypeStruct((s, d), BF16), jax.ShapeDtypeStruct((1, d), F32)],
                  compiler_params=_cp(("arbitrary",)))(dz, y, g)


def prenorm_bwd(x, dh, dskip, g, name):
    s, d = x.shape
    tr = _tile(s, 256)

    def body(x_ref, dh_ref, ds_ref, g_ref, dx_ref, dg_ref):
        dv, dg = _norm_bwd(x_ref[...], g_ref[...], dh_ref[...])
        dx_ref[...] = ds_ref[...] + dv
        _acc_rows(dg_ref, dg)

    return _pcall(body, name=name, grid=(s // tr,),
                  in_specs=[_row_spec(tr, d), _row_spec(tr, d), _row_spec(tr, d), _vec_spec(d)],
                  out_specs=[_row_spec(tr, d), _vec_spec(d)],
                  out_shape=[jax.ShapeDtypeStruct((s, d), F32), jax.ShapeDtypeStruct((1, d), F32)],
                  compiler_params=_cp(("arbitrary",)))(x, dh, dskip, g)


def ffn_up_swiglu(h, w, gate_map, up_map, ff, *, tm, tn, name):
    m, kdim = h.shape
    tm = min(tm, m)
    lead = (None,) * (w.ndim - 2)

    def body(h_ref, wg_ref, wu_ref, g_ref, u_ref, act_ref):
        a = h_ref[...]
        gate = jnp.dot(a, wg_ref[...], preferred_element_type=F32)
        up = jnp.dot(a, wu_ref[...], preferred_element_type=F32)
        g_ref[...] = gate.astype(BF16)
        u_ref[...] = up.astype(BF16)
        act_ref[...] = (gate * jax.nn.sigmoid(gate) * up).astype(BF16)

    out = jax.ShapeDtypeStruct((m, ff), BF16)
    oblk = pl.BlockSpec((tm, tn), lambda n, i: (i, n))
    return _pcall(body, name=name, grid=(ff // tn, m // tm),
                  in_specs=[pl.BlockSpec((tm, kdim), lambda n, i: (i, 0)), pl.BlockSpec(lead + (kdim, tn), lambda n, i: gate_map(n)),
                            pl.BlockSpec(lead + (kdim, tn), lambda n, i: up_map(n))],
                  out_specs=[oblk] * 3, out_shape=[out] * 3,
                  compiler_params=_cp(("parallel", "parallel"), VMEM_BIG))(h, w, w)


def swiglu_bwd(dact, gate, up, name):
    s, ff = gate.shape
    tr = _tile(s, 128)

    def body(da_ref, g_ref, u_ref, dgu_ref):
        gt = g_ref[...].astype(F32)
        da = da_ref[...].astype(F32)
        sig = jax.nn.sigmoid(gt)
        dgu_ref[:, :ff] = (da * u_ref[...].astype(F32) * sig * (1.0 + gt * (1.0 - sig))).astype(BF16)
        dgu_ref[:, ff:] = (da * gt * sig).astype(BF16)

    return _pcall(body, name=name, grid=(s // tr,), in_specs=[_row_spec(tr, ff)] * 3,
                  out_specs=_row_spec(tr, 2 * ff), out_shape=jax.ShapeDtypeStruct((s, 2 * ff), BF16),
                  compiler_params=_cp(("parallel",)))(dact, gate, up)


def rope_tables(pos_col, invf_row, name):
    s = pos_col.shape[0]
    tr = _tile(s, 1024)

    def body(p_ref, f_ref, c_ref, sa_ref, sb_ref):
        ang = p_ref[...].astype(F32) * f_ref[...]
        d = lax.broadcasted_iota(jnp.int32, (1, LANES), 1) % HEAD_DIM
        cs, sn = jnp.cos(ang), jnp.sin(ang)
        c_ref[...] = jnp.where(d < ROT_DIM, cs, 1.0)
        sa_ref[...] = jnp.where(d < ROT_DIM // 2, -sn, 0.0)
        sb_ref[...] = jnp.where((d >= ROT_DIM // 2) & (d < ROT_DIM), sn, 0.0)

    tab = jax.ShapeDtypeStruct((s, LANES), F32)
    return _pcall(body, name=name, grid=(s // tr,),
                  in_specs=[pl.BlockSpec((tr, 1), lambda i: (i, 0)), _vec_spec(LANES)],
                  out_specs=[_row_spec(tr, LANES)] * 3, out_shape=[tab] * 3, compiler_params=_cp(("parallel",)))(pos_col, invf_row)


def _rot(t, c, sa, sb):
    half = ROT_DIM // 2
    return t * c + pltpu.roll(t, LANES - half, 1) * sa + pltpu.roll(t, half, 1) * sb


def _rot_t(t, c, sa, sb):
    half = ROT_DIM // 2
    return t * c + pltpu.roll(t * sa, half, 1) + pltpu.roll(t * sb, LANES - half, 1)


def swa_split(proj, tabs, n_heads, n_kv, name):
    s, width = proj.shape
    qd, kd = n_heads * HEAD_DIM, n_kv * HEAD_DIM
    tr = _tile(s, 256)

    def body(p_ref, c_ref, sa_ref, sb_ref, q_ref, k_ref, v_ref):
        c, sa, sb = c_ref[...], sa_ref[...], sb_ref[...]
        low = lax.broadcasted_iota(jnp.int32, (1, LANES), 1) < HEAD_DIM
        for g in range(qd // LANES):
            q_ref[:, g * LANES:(g + 1) * LANES] = _rot(p_ref[:, g * LANES:(g + 1) * LANES], c, sa, sb).astype(BF16)
        for g in range(kd // LANES):
            for src, dst, rot in ((qd, k_ref, True), (qd + kd, v_ref, False)):
                t = p_ref[:, src + g * LANES:src + (g + 1) * LANES]
                t = _rot(t, c, sa, sb) if rot else t
                sw = pltpu.roll(t, HEAD_DIM, 1)
                dst[:, (2 * g) * LANES:(2 * g + 1) * LANES] = jnp.where(low, t, sw).astype(BF16)
                dst[:, (2 * g + 1) * LANES:(2 * g + 2) * LANES] = jnp.where(low, sw, t).astype(BF16)

    return _pcall(body, name=name, grid=(s // tr,),
                  in_specs=[_row_spec(tr, width)] + [_row_spec(tr, LANES)] * 3,
                  out_specs=[_row_spec(tr, qd), _row_spec(tr, 2 * kd), _row_spec(tr, 2 * kd)],
                  out_shape=[jax.ShapeDtypeStruct((s, qd), BF16), jax.ShapeDtypeStruct((s, 2 * kd), BF16),
                             jax.ShapeDtypeStruct((s, 2 * kd), BF16)],
                  compiler_params=_cp(("parallel",)))(proj, *tabs)


def swa_merge_bwd(dq, dkd, dvd, tabs, name):
    s, qd = dq.shape
    kd = dkd.shape[1] // 2
    width = qd + 2 * kd
    tr = _tile(s, 256)

    def body(dq_ref, dk_ref, dv_ref, c_ref, sa_ref, sb_ref, o_ref):
        c, sa, sb = c_ref[...], sa_ref[...], sb_ref[...]
        low = lax.broadcasted_iota(jnp.int32, (1, LANES), 1) < HEAD_DIM
        for g in range(qd // LANES):
            t = dq_ref[:, g * LANES:(g + 1) * LANES].astype(F32)
            o_ref[:, g * LANES:(g + 1) * LANES] = _rot_t(t, c, sa, sb).astype(BF16)
        for g in range(kd // LANES):
            for dst, src, rot in ((qd, dk_ref, True), (qd + kd, dv_ref, False)):
                e = src[:, (2 * g) * LANES:(2 * g + 1) * LANES]
                o = src[:, (2 * g + 1) * LANES:(2 * g + 2) * LANES]
                t = jnp.where(low, e + pltpu.roll(e, HEAD_DIM, 1), o + pltpu.roll(o, HEAD_DIM, 1))
                t = _rot_t(t, c, sa, sb) if rot else t
                o_ref[:, dst + g * LANES:dst + (g + 1) * LANES] = t.astype(BF16)

    return _pcall(body, name=name, grid=(s // tr,),
                  in_specs=[_row_spec(tr, qd), _row_spec(tr, 2 * kd), _row_spec(tr, 2 * kd)] + [_row_spec(tr, LANES)] * 3,
                  out_specs=_row_spec(tr, width), out_shape=jax.ShapeDtypeStruct((s, width), BF16),
                  compiler_params=_cp(("parallel",)))(dq, dkd, dvd, *tabs)


def _halves():
    lane_half = lax.broadcasted_iota(jnp.int32, (1, LANES), 1) // HEAD_DIM
    return [lane_half == 0, lane_half == 1]


def _nt(a, b):
    return lax.dot_general(a, b, _DIMS["nt"], preferred_element_type=F32)


def _tn(a, b):
    return lax.dot_general(a, b, _DIMS["tn"], preferred_element_type=F32)


def _band_mask(i, tq):
    w = tq + WINDOW
    rel = lax.broadcasted_iota(jnp.int32, (tq, w), 1) - lax.broadcasted_iota(jnp.int32, (tq, w), 0)
    first = lax.broadcasted_iota(jnp.int32, (tq, w), 1) >= jnp.where(i > 0, 0, WINDOW)
    return (rel >= 1) & (rel <= WINDOW) & first


def swa_fwd(q, kd, vd, sink_row, name):
    s, qd = q.shape
    tq = min(256, s)
    r = tq // WINDOW
    pairs = qd // LANES
    group_pairs = pairs // (kd.shape[1] // LANES)
    scale = HEAD_DIM ** -0.5

    def body(q_ref, kp_ref, kc_ref, vp_ref, vc_ref, sk_ref, o_ref, lse_ref):
        i = pl.program_id(1)
        k = jnp.concatenate([kp_ref[...], kc_ref[...]], axis=0)
        v = jnp.concatenate([vp_ref[...], vc_ref[...]], axis=0)
        mask = _band_mask(i, tq)
        q2 = q_ref[...]
        outs, lses = [], []
        for a, hm in enumerate(_halves()):
            sc = _nt(jnp.where(hm, q2, jnp.zeros_like(q2)), k) * scale
            sc = jnp.where(mask, sc, NEG)
            sink = sk_ref[:, a * HEAD_DIM:a * HEAD_DIM + 1]
            m = jnp.maximum(jnp.max(sc, axis=-1, keepdims=True), sink)
            p = jnp.exp(sc - m)
            den = jnp.sum(p, axis=-1, keepdims=True) + jnp.exp(sink - m)
            outs.append(jnp.dot(p.astype(BF16), v, preferred_element_type=F32) / den)
            lses.append(m + jnp.log(den))
        hm0 = _halves()[0]
        o_ref[...] = jnp.where(hm0, outs[0], outs[1]).astype(BF16)
        lse_ref[...] = jnp.where(hm0, lses[0], lses[1])

    prev = lambda p, i: (jnp.maximum(i * r - 1, 0), p // group_pairs)
    cur = lambda p, i: (i, p // group_pairs)
    blk = pl.BlockSpec((tq, LANES), lambda p, i: (i, p))
    return _pcall(body, name=name, grid=(pairs, s // tq),
                  in_specs=[blk, pl.BlockSpec((WINDOW, LANES), prev), pl.BlockSpec((tq, LANES), cur),
                            pl.BlockSpec((WINDOW, LANES), prev), pl.BlockSpec((tq, LANES), cur),
                            pl.BlockSpec((1, LANES), lambda p, i: (0, p))],
                  out_specs=[blk, blk],
                  out_shape=[jax.ShapeDtypeStruct((s, qd), BF16), jax.ShapeDtypeStruct((s, qd), F32)],
                  compiler_params=_cp(("parallel", "parallel")))(q, kd, kd, vd, vd, sink_row)


def swa_bwd(q, kd, vd, sink_row, out, lse, dout, name):
    s, qd = q.shape
    tq = min(256, s)
    r = tq // WINDOW
    n_kv = kd.shape[1] // LANES
    gw = qd // n_kv
    scale = HEAD_DIM ** -0.5

    def body(q_ref, kp_ref, kc_ref, vp_ref, vc_ref, sk_ref, o_ref, lse_ref, do_ref, dq_ref, dk_ref, dv_ref, dsk_ref):
        i = pl.program_id(1)

        @pl.when(i == 0)
        def _():
            dk_ref[...] = jnp.zeros_like(dk_ref)
            dv_ref[...] = jnp.zeros_like(dv_ref)
            dsk_ref[...] = jnp.zeros_like(dsk_ref)

        k = jnp.concatenate([kp_ref[...], kc_ref[...]], axis=0)
        v = jnp.concatenate([vp_ref[...], vc_ref[...]], axis=0)
        mask = _band_mask(i, tq)
        dk = jnp.zeros((tq + WINDOW, LANES), F32)
        dv = jnp.zeros((tq + WINDOW, LANES), F32)
        for pp in range(gw // LANES):
            cols = slice(pp * LANES, (pp + 1) * LANES)
            q2, do2 = q_ref[:, cols], do_ref[:, cols]
            prod = do2.astype(F32) * o_ref[:, cols].astype(F32)
            dq2 = jnp.zeros((tq, LANES), F32)
            dsk = jnp.zeros((1, LANES), F32)
            for a, hm in enumerate(_halves()):
                qa = jnp.where(hm, q2, jnp.zeros_like(q2))
                doa = jnp.where(hm, do2, jnp.zeros_like(do2))
                lse_a = lse_ref[:, pp * LANES + a * HEAD_DIM:pp * LANES + a * HEAD_DIM + 1]
                sc = jnp.where(mask, _nt(qa, k) * scale, NEG)
                p = jnp.exp(sc - lse_a)
                delta = jnp.sum(jnp.where(hm, prod, 0.0), axis=-1, keepdims=True)
                ds = (p * (_nt(doa, v) - delta) * scale).astype(BF16)
                dv = dv + _tn(p.astype(BF16), doa)
                dk = dk + _tn(ds, qa)
                dq2 = dq2 + jnp.where(hm, jnp.dot(ds, k, preferred_element_type=F32), 0.0)
                sink = sk_ref[:, pp * LANES + a * HEAD_DIM:pp * LANES + a * HEAD_DIM + 1]
                dsink = -jnp.sum(jnp.exp(sink - lse_a) * delta, axis=0, keepdims=True)
                dsk = dsk + jnp.where(hm, dsink, 0.0)
            dq_ref[:, cols] = dq2.astype(BF16)
            dsk_ref[0:1, cols] += dsk
        start = pl.multiple_of(i * tq, tq)
        dk_ref[pl.ds(start, tq), :] += dk[WINDOW:, :]
        dv_ref[pl.ds(start, tq), :] += dv[WINDOW:, :]

        @pl.when(i > 0)
        def _():
            before = pl.multiple_of(i * tq - WINDOW, WINDOW)
            dk_ref[pl.ds(before, WINDOW), :] += dk[:WINDOW, :]
            dv_ref[pl.ds(before, WINDOW), :] += dv[:WINDOW, :]

    prev = lambda g, i: (jnp.maximum(i * r - 1, 0), g)
    cur = lambda g, i: (i, g)
    wide = pl.BlockSpec((tq, gw), cur)
    full = pl.BlockSpec((s, LANES), lambda g, i: (0, g))
    return _pcall(body, name=name, grid=(n_kv, s // tq),
                  in_specs=[wide, pl.BlockSpec((WINDOW, LANES), prev), pl.BlockSpec((tq, LANES), cur),
                            pl.BlockSpec((WINDOW, LANES), prev), pl.BlockSpec((tq, LANES), cur),
                            pl.BlockSpec((1, gw), lambda g, i: (0, g)), wide, wide, wide],
                  out_specs=[wide, full, full, pl.BlockSpec((8, gw), lambda g, i: (0, g))],
                  out_shape=[jax.ShapeDtypeStruct((s, qd), BF16), jax.ShapeDtypeStruct(kd.shape, F32),
                             jax.ShapeDtypeStruct(kd.shape, F32), jax.ShapeDtypeStruct((8, qd), F32)],
                  compiler_params=_cp(("parallel", "arbitrary"), VMEM_BIG))(q, kd, kd, vd, vd, sink_row, out, lse, dout)


def forget_cumsum(f_logit, b_row, name):
    s = f_logit.shape[0]

    def body(f_ref, b_ref, c_ref):
        z = f_ref[...] + b_ref[...]
        acc = jnp.minimum(z, 0.0) - jnp.log(1.0 + jnp.exp(-jnp.abs(z)))
        row = lax.broadcasted_iota(jnp.int32, (s, LANES), 0)
        d = 1
        while d < s:
            acc = acc + jnp.where(row >= d, pltpu.roll(acc, d, 0), 0.0)
            d *= 2
        c_ref[...] = acc

    return _pcall(body, name=name, in_specs=[pl.BlockSpec((s, LANES), lambda: (0, 0)), pl.BlockSpec((1, LANES), lambda: (0, 0))],
                  out_specs=pl.BlockSpec((s, LANES), lambda: (0, 0)), out_shape=jax.ShapeDtypeStruct((s, LANES), F32),
                  compiler_params=_cp(None, VMEM_BIG))(f_logit, b_row)


def forget_gate_bwd(dc, f_logit, b_row, name):
    s = dc.shape[0]

    def body(dc_ref, f_ref, b_ref, df_ref, db_ref):
        acc = dc_ref[...]
        row = lax.broadcasted_iota(jnp.int32, (s, LANES), 0)
        d = 1
        while d < s:
            acc = acc + jnp.where(row < s - d, pltpu.roll(acc, s - d, 0), 0.0)
            d *= 2
        df = acc * jax.nn.sigmoid(-(f_ref[...] + b_ref[...]))
        df_ref[...] = df.astype(BF16)
        db_ref[...] = jnp.sum(df, axis=0, keepdims=True)

    whole = pl.BlockSpec((s, LANES), lambda: (0, 0))
    vec = pl.BlockSpec((1, LANES), lambda: (0, 0))
    return _pcall(body, name=name, in_specs=[whole, whole, vec], out_specs=[whole, vec],
                  out_shape=[jax.ShapeDtypeStruct((s, LANES), BF16), jax.ShapeDtypeStruct((1, LANES), F32)],
                  compiler_params=_cp(None, VMEM_BIG))(dc, f_logit, b_row)


EXTRA = HEAD_DIM
N_PIECES = 3


def _pieces(v):
    hi = v.astype(BF16).astype(F32)
    mid = (v - hi).astype(BF16).astype(F32)
    return hi, mid, (v - hi - mid).astype(BF16).astype(F32)


def _slot(main, lane, extras=None, ones_at=None):
    out = jnp.where(lane < HEAD_DIM, main, 0.0)
    if extras is not None:
        for r, e in enumerate(extras):
            out = jnp.where(lane == EXTRA + r, e, out)
    if ones_at is not None:
        out = jnp.where((lane >= ones_at) & (lane < ones_at + N_PIECES), 1.0, out)
    return out


def _lane_iota():
    return lax.broadcasted_iota(jnp.int32, (1, LANES), 1)


def fox_split(qkv, c, n_heads, name):
    s = qkv.shape[0]
    hd = n_heads * HEAD_DIM
    tr = _tile(s, 256)
    scale = HEAD_DIM ** -0.5

    def body(x_ref, c_ref, q_ref, k_ref, v_ref):
        lane = _lane_iota()
        cv = c_ref[...]
        for g in range(hd // LANES):
            for part, dst in enumerate((q_ref, k_ref, v_ref)):
                t = x_ref[:, part * hd + g * LANES:part * hd + (g + 1) * LANES].astype(F32)
                for a, main in enumerate((t, pltpu.roll(t, HEAD_DIM, 1))):
                    h = 2 * g + a
                    if part == 0:
                        val = _slot(main * scale, lane, ones_at=EXTRA)
                    elif part == 1:
                        ch = jnp.sum(jnp.where(lane == h, cv, 0.0), axis=1, keepdims=True)
                        val = _slot(main, lane, extras=_pieces(-ch), ones_at=EXTRA + N_PIECES)
                    else:
                        val = _slot(main, lane, ones_at=EXTRA)
                    dst[:, h * LANES:(h + 1) * LANES] = val.astype(BF16)

    slots = jax.ShapeDtypeStruct((s, n_heads * LANES), BF16)
    return _pcall(body, name=name, grid=(s // tr,), in_specs=[_row_spec(tr, 3 * hd), _row_spec(tr, LANES)],
                  out_specs=[_row_spec(tr, n_heads * LANES)] * 3, out_shape=[slots] * 3,
                  compiler_params=_cp(("parallel",), VMEM_BIG))(qkv, c)


def _causal_keep(t):
    return lax.broadcasted_iota(jnp.int32, (t, t), 1) <= lax.broadcasted_iota(jnp.int32, (t, t), 0)


def fox_fwd(qs, ks, vs, n_heads, name):
    s = qs.shape[0]
    pairs = n_heads // 2
    t = min(512, s)
    ratio = 2 if s % (2 * t) == 0 else 1
    tq = ratio * t
    wide = 2 * LANES

    def body(q_ref, k_ref, v_ref, o_ref, qb_ref, acc_ref, m_ref):
        i = pl.program_id(1)
        lane = _lane_iota()
        acc_ref[...] = jnp.zeros_like(acc_ref)
        m_ref[...] = jnp.full_like(m_ref, NEG)

        def step(j, diagonal):
            rows = pl.ds(pl.multiple_of(j * t, t), t)
            for a in range(2):
                cols = slice(a * LANES, (a + 1) * LANES)
                sc = _nt(q_ref[:, cols], k_ref[rows, cols])
                if diagonal is not None:
                    keep = (lax.broadcasted_iota(jnp.int32, (tq, t), 1) + diagonal * t) <= lax.broadcasted_iota(jnp.int32, (tq, t), 0)
                    sc = jnp.where(keep, sc, NEG)
                m_old = m_ref[a]
                m_new = jnp.maximum(m_old, jnp.max(sc, axis=-1, keepdims=True))
                p = jnp.exp(sc - jnp.tile(m_new, (1, t // LANES)))
                acc_ref[a] = jnp.exp(m_old - m_new) * acc_ref[a] + jnp.dot(p.astype(BF16), v_ref[rows, cols],
                                                                              preferred_element_type=F32)
                m_ref[a] = m_new

        def two_steps(j2, carry):
            step(2 * j2, None)
            step(2 * j2 + 1, None)
            return carry

        past = ratio * i
        lax.fori_loop(0, past // 2, two_steps, 0)
        if ratio % 2:
            @pl.when(past % 2 == 1)
            def _():
                step(past - 1, None)

        for u in range(ratio):
            step(past + u, u)
        outs = []
        for a in range(2):
            cols = slice(a * LANES, (a + 1) * LANES)
            acc = acc_ref[a]
            norm = acc[:, EXTRA:EXTRA + 1]
            outs.append(acc / norm)
            neg_lse = _pieces(-(m_ref[a] + jnp.log(norm)))
            qb = q_ref[:, cols].astype(F32)
            for r in range(N_PIECES):
                qb = jnp.where(lane == EXTRA + N_PIECES + r, neg_lse[r], qb)
            qb_ref[:, cols] = qb.astype(BF16)
        o_ref[...] = jnp.where(lane < HEAD_DIM, outs[0], pltpu.roll(outs[1], HEAD_DIM, 1)).astype(BF16)

    qblk = pl.BlockSpec((tq, wide), lambda p, i: (i, p))
    whole = pl.BlockSpec((s, wide), lambda p, i: (0, p))
    return _pcall(body, name=name, grid=(pairs, s // tq), in_specs=[qblk, whole, whole],
                  out_specs=[pl.BlockSpec((tq, LANES), lambda p, i: (i, p)), qblk],
                  out_shape=[jax.ShapeDtypeStruct((s, n_heads * HEAD_DIM), BF16), jax.ShapeDtypeStruct(qs.shape, BF16)],
                  scratch_shapes=[pltpu.VMEM((2, tq, LANES), F32)] * 2,
                  compiler_params=_cp(("parallel", "arbitrary"), VMEM_BIG))(qs, ks, vs)


def fox_dout_slots(dout, out, name):
    s, hd = dout.shape
    tr = _tile(s, 256)

    def body(d_ref, o_ref, s_ref):
        lane = _lane_iota()
        for g in range(hd // LANES):
            cols = slice(g * LANES, (g + 1) * LANES)
            d2 = d_ref[:, cols].astype(F32)
            prod = d2 * o_ref[:, cols].astype(F32)
            for a, main in enumerate((d2, pltpu.roll(d2, HEAD_DIM, 1))):
                delta = jnp.sum(jnp.where((lane // HEAD_DIM) == a, prod, 0.0), axis=1, keepdims=True)
                h = 2 * g + a
                s_ref[:, h * LANES:(h + 1) * LANES] = _slot(main, lane, extras=_pieces(-delta)).astype(BF16)

    return _pcall(body, name=name, grid=(s // tr,), in_specs=[_row_spec(tr, hd), _row_spec(tr, hd)],
                  out_specs=_row_spec(tr, 2 * hd), out_shape=jax.ShapeDtypeStruct((s, 2 * hd), BF16),
                  compiler_params=_cp(("parallel",)))(dout, out)


def fox_bwd(qb, ks, vs, dos, n_heads, name):
    s = qb.shape[0]
    pairs = n_heads // 2
    t = min(512, s)
    nblk = s // t
    wide = 2 * LANES

    def body(q_ref, k_ref, v_ref, do_ref, dq_ref, dk_ref, dv_ref, dka_ref, dva_ref):
        j = pl.program_id(1)

        @pl.when(j == 0)
        def _():
            dq_ref[...] = jnp.zeros_like(dq_ref)

        dka_ref[...] = jnp.zeros_like(dka_ref)
        dva_ref[...] = jnp.zeros_like(dva_ref)

        def step(i, diagonal):
            rows = pl.ds(pl.multiple_of(i * t, t), t)
            for a in range(2):
                cols = slice(a * LANES, (a + 1) * LANES)
                qa, doa, ka = q_ref[rows, cols], do_ref[rows, cols], k_ref[:, cols]
                sc = _nt(qa, ka)
                if diagonal:
                    sc = jnp.where(_causal_keep(t), sc, NEG)
                p = jnp.exp(sc)
                ds = (p * _nt(doa, v_ref[:, cols])).astype(BF16)
                dva_ref[a] += _tn(p.astype(BF16), doa)
                dka_ref[a] += _tn(ds, qa)
                dq_ref[rows, cols] += jnp.dot(ds, ka, preferred_element_type=F32)

        step(j, True)
        below = nblk - 1 - j

        def two_steps(i2, carry):
            step(j + 1 + 2 * i2, False)
            step(j + 2 + 2 * i2, False)
            return carry

        lax.fori_loop(0, below // 2, two_steps, 0)

        @pl.when(below % 2 == 1)
        def _():
            step(nblk - 1, False)

        for a in range(2):
            cols = slice(a * LANES, (a + 1) * LANES)
            dk_ref[:, cols] = dka_ref[a]
            dv_ref[:, cols] = dva_ref[a].astype(BF16)

    whole = pl.BlockSpec((s, wide), lambda p, j: (0, p))
    blk = pl.BlockSpec((t, wide), lambda p, j: (j, p))
    return _pcall(body, name=name, grid=(pairs, nblk), in_specs=[whole, blk, blk, whole], out_specs=[whole, blk, blk],
                  out_shape=[jax.ShapeDtypeStruct(qb.shape, F32), jax.ShapeDtypeStruct(qb.shape, F32),
                             jax.ShapeDtypeStruct(qb.shape, BF16)],
                  scratch_shapes=[pltpu.VMEM((2, t, LANES), F32)] * 2,
                  compiler_params=_cp(("parallel", "arbitrary"), VMEM_BIG))(qb, ks, vs, dos)


def fox_merge_bwd(dqs, dks, dvs, n_heads, name):
    s = dqs.shape[0]
    hd = n_heads * HEAD_DIM
    tr = _tile(s, 128)
    scale = HEAD_DIM ** -0.5

    def body(dq_ref, dk_ref, dv_ref, o_ref, dc_ref):
        lane = _lane_iota()
        dc = jnp.zeros((tr, LANES), F32)
        for g in range(hd // LANES):
            even = slice(2 * g * LANES, (2 * g + 1) * LANES)
            odd = slice((2 * g + 1) * LANES, (2 * g + 2) * LANES)
            for part, (src, mul) in enumerate(((dq_ref, scale), (dk_ref, 1.0), (dv_ref, 1.0))):
                dense = jnp.where(lane < HEAD_DIM, src[:, even].astype(F32), pltpu.roll(src[:, odd].astype(F32), HEAD_DIM, 1))
                o_ref[:, part * hd + g * LANES:part * hd + (g + 1) * LANES] = (dense * mul).astype(BF16)
            for a, cols in enumerate((even, odd)):
                both = jnp.where(lane == EXTRA + N_PIECES, dq_ref[:, cols], 0.0) - jnp.where(lane == EXTRA, dk_ref[:, cols], 0.0)
                dc = jnp.where(lane == 2 * g + a, jnp.sum(both, axis=1, keepdims=True), dc)
        dc_ref[...] = dc

    wide = n_heads * LANES
    return _pcall(body, name=name, grid=(s // tr,), in_specs=[_row_spec(tr, wide)] * 3,
                  out_specs=[_row_spec(tr, 3 * hd), _row_spec(tr, LANES)],
                  out_shape=[jax.ShapeDtypeStruct((s, 3 * hd), BF16), jax.ShapeDtypeStruct((s, LANES), F32)],
                  compiler_params=_cp(("parallel",), VMEM_BIG))(dqs, dks, dvs)


def _w_spec(shape, lead_map=None):
    _, r, c = shape[-3:]
    tr = _tile(r, 128)
    n_lead = len(shape) - 2
    if lead_map is None:
        lead_map = lambda l: (l,)
    return tr, pl.BlockSpec((None,) * n_lead + (tr, c), lambda l, i: (*lead_map(l), i, 0))


def cast_into_slot(w, layer, place, dtype, name):
    tr = _tile(w.shape[1], 128)
    blk = (None, tr, w.shape[2])

    def body(me_ref, c_ref, w_ref, o_ref):
        o_ref[...] = w_ref[...].astype(dtype)

    grid_spec = pltpu.PrefetchScalarGridSpec(
        num_scalar_prefetch=2, grid=(w.shape[1] // tr,),
        in_specs=[pl.BlockSpec(blk, lambda i, me, c: (layer, i, 0))],
        out_specs=pl.BlockSpec(blk, lambda i, me, c: (me[0], i, 0)))
    return _pcall(body, name=name, grid_spec=grid_spec, out_shape=jax.ShapeDtypeStruct((N_CHIPS,) + w.shape[1:], dtype),
                  compiler_params=_cp(("parallel",)))(*place, w)


def adamw(w, g, m, v, name):
    tr, spec = _w_spec(w.shape)

    def body(w_ref, g_ref, m_ref, v_ref, d_ref, mo_ref, vo_ref):
        gg = g_ref[...]
        mn = ADAM_B1 * m_ref[...] + (1.0 - ADAM_B1) * gg
        vn = ADAM_B2 * v_ref[...] + (1.0 - ADAM_B2) * (gg * gg)
        m_hat = mn / (1.0 - ADAM_B1 ** ADAM_STEP)
        v_hat = vn / (1.0 - ADAM_B2 ** ADAM_STEP)
        d_ref[...] = -ADAM_LR * (m_hat / (jnp.sqrt(v_hat) + ADAM_EPS) + ADAM_WD * w_ref[...])
        mo_ref[...] = mn
        vo_ref[...] = vn

    out = jax.ShapeDtypeStruct(w.shape, F32)
    return _pcall(body, name=name, grid=(w.shape[0], w.shape[1] // tr), in_specs=[spec] * 4, out_specs=[spec] * 3,
                  out_shape=[out] * 3, compiler_params=_cp(("parallel", "parallel")))(w, g, m, v)


def add_core_halves(g, recv, place, name):
    hr = recv.shape[1]
    tr = _tile(hr, 128)
    nb = hr // tr
    blk = (None, tr, g.shape[2])

    def body(me_ref, c_ref, g_ref, r_ref, o_ref):
        o_ref[...] = (g_ref[...].astype(F32) + r_ref[...].astype(F32)).astype(o_ref.dtype)

    grid_spec = pltpu.PrefetchScalarGridSpec(
        num_scalar_prefetch=2, grid=(N_CHIPS, nb),
        in_specs=[pl.BlockSpec(blk, lambda q, i, me, c: (q, c[0] * nb + i, 0)), pl.BlockSpec(blk, lambda q, i, me, c: (q, i, 0))],
        out_specs=pl.BlockSpec(blk, lambda q, i, me, c: (q, i, 0)))
    return _pcall(body, name=name, grid_spec=grid_spec, out_shape=jax.ShapeDtypeStruct(recv.shape, g.dtype),
                  compiler_params=_cp(("parallel", "parallel")))(*place, g, recv)


def add_chips(own, recv, into, layer, place, name):
    _, hr, cols = own.shape
    tr = _tile(hr, 128)
    nb = hr // tr
    blk = (None, tr, cols)

    def body(me_ref, c_ref, p0, p1, p2, p3, _, o_ref):
        o_ref[...] = ((p0[...].astype(F32) + p1[...].astype(F32)) + p2[...].astype(F32)) + p3[...].astype(F32)

    def peer(flip):
        return lambda i, me, c: (me[0] ^ flip, i, 0)

    grid_spec = pltpu.PrefetchScalarGridSpec(
        num_scalar_prefetch=2, grid=(nb,), in_specs=[pl.BlockSpec(blk, peer(f)) for f in (0, 2, 1, 3)] + [ANY],
        out_specs=pl.BlockSpec(blk, lambda i, me, c: (layer, c[0] * nb + i, 0)))
    return _pcall(body, name=name, grid_spec=grid_spec, out_shape=jax.ShapeDtypeStruct(into.shape, F32),
                  input_output_aliases={6: 0}, compiler_params=_cp(("parallel",)))(*place, own, recv, recv, recv, into)


def _place():
    x, y, c = lax.axis_index("x"), lax.axis_index("y"), lax.axis_index("c")
    others = [(1 - x, y), (x, 1 - y), (1 - x, 1 - y)]
    return x, y, c, others


def _chip_id(chip):
    return 2 * chip[0] + chip[1]


def _comm_call(body, name, n_in, out_shapes, n_sems, in_place=False):
    return _pcall(body, name=name, in_specs=[ANY] * n_in, out_specs=[ANY] * len(out_shapes), out_shape=out_shapes,
                  scratch_shapes=[pltpu.SemaphoreType.DMA((n_sems,)), pltpu.SemaphoreType.DMA((n_sems,))],
                  input_output_aliases={t: t for t in range(n_in)} if in_place else {},
                  compiler_params=pltpu.CompilerParams(has_side_effects=True))


def swap_core_halves(grads, name):
    n = len(grads)

    def body(*refs):
        ins, outs = refs[:n], refs[n:2 * n]
        send, recv = refs[2 * n:]
        x, y, c, _ = _place()
        cps = []
        for t in range(n):
            hr = ins[t].shape[1] // 2
            cp = pltpu.make_async_remote_copy(src_ref=ins[t].at[:, pl.ds((1 - c) * hr, hr)], dst_ref=outs[t],
                                              send_sem=send.at[t], recv_sem=recv.at[t],
                                              device_id=(x, y, 1 - c), device_id_type=MESH)
            cp.start()
            cps.append(cp)
        for cp in cps:
            cp.wait()

    outs = [jax.ShapeDtypeStruct((a.shape[0], a.shape[1] // 2) + a.shape[2:], a.dtype) for a in grads]
    return _comm_call(body, name, n, outs, n)(*grads)


HBM = pl.BlockSpec(memory_space=pltpu.HBM)
SEM = pl.BlockSpec(memory_space=pltpu.SEMAPHORE)
DATAFLOW = pltpu.SideEffectType.DATAFLOW_SIDE_EFFECTING


def _in_hbm(a):
    return pltpu.with_memory_space_constraint(a, pltpu.HBM)


def gather_start(slots, name):
    n = len(slots)

    def body(*refs):
        bufs = refs[:n]
        send, recv = refs[n], refs[n + 1]
        token = refs[-1]
        x, y, c, others = _place()
        me = _chip_id((x, y))
        for t in range(n):
            for j, chip in enumerate(others):
                pltpu.make_async_remote_copy(src_ref=bufs[t].at[me], dst_ref=bufs[t].at[me],
                                             send_sem=send.at[3 * t + j], recv_sem=recv.at[3 * t + j],
                                             device_id=(*chip, c), device_id_type=MESH).start()
        token[...] = jnp.zeros_like(token)

    sems = pltpu.SemaphoreType.DMA((3 * n,))
    res = _pcall(body, name=name, in_specs=[HBM] * n,
                 out_shape=(sems, sems, *[pltpu.HBM(a.shape, a.dtype) for a in slots], jax.ShapeDtypeStruct((8, LANES), F32)),
                 out_specs=(SEM, SEM, *[HBM] * n, pl.BlockSpec(memory_space=pltpu.VMEM)),
                 input_output_aliases={i: 2 + i for i in range(n)},
                 compiler_params=pltpu.CompilerParams(has_side_effects=DATAFLOW))(*[_in_hbm(a) for a in slots])
    return res[0], res[1], list(res[2:2 + n]), res[-1]


def gather_wait(send, recv, slot, t, after, name):
    def body(buf, send_ref, recv_ref, after_ref, out):
        x, y, c, others = _place()
        me = _chip_id((x, y))
        for j, chip in enumerate(others):
            pltpu.make_async_remote_copy(src_ref=buf.at[me], dst_ref=buf.at[_chip_id(chip)],
                                         send_sem=send_ref.at[3 * t + j], recv_sem=recv_ref.at[3 * t + j],
                                         device_id=(*chip, c), device_id_type=MESH).wait()

    return _pcall(body, name=name, in_specs=[HBM, SEM, SEM, ANY], out_shape=pltpu.HBM(slot.shape, slot.dtype),
                  out_specs=HBM, input_output_aliases={0: 0},
                  compiler_params=pltpu.CompilerParams(has_side_effects=DATAFLOW))(slot, send, recv, after)


def scatter_start(sums, name):
    n = len(sums)

    def body(*refs):
        ins, lands = refs[:n], refs[n:2 * n]
        send, recv = refs[2 * n], refs[2 * n + 1]
        token = refs[-1]
        x, y, c, others = _place()
        me = _chip_id((x, y))
        for t in range(n):
            for j, chip in enumerate(others):
                pltpu.make_async_remote_copy(src_ref=ins[t].at[_chip_id(chip)], dst_ref=lands[t].at[me],
                                             send_sem=send.at[3 * t + j], recv_sem=recv.at[3 * t + j],
                                             device_id=(*chip, c), device_id_type=MESH).start()
        token[...] = jnp.zeros_like(token)

    bufs = [pltpu.HBM(a.shape, a.dtype) for a in sums]
    sems = pltpu.SemaphoreType.DMA((3 * n,))
    res = _pcall(body, name=name, in_specs=[HBM] * (2 * n),
                 out_shape=(sems, sems, *bufs, *bufs, jax.ShapeDtypeStruct((8, LANES), F32)),
                 out_specs=(SEM, SEM, *[HBM] * (2 * n), pl.BlockSpec(memory_space=pltpu.VMEM)),
                 input_output_aliases={i: 2 + i for i in range(2 * n)},
                 compiler_params=pltpu.CompilerParams(has_side_effects=DATAFLOW))(
        *[_in_hbm(a) for a in sums], *[_in_hbm(lax.empty(a.shape, a.dtype)) for a in sums])
    return res[0], res[1], list(res[2:2 + n]), list(res[2 + n:2 + 2 * n]), res[-1]


def scatter_wait(send, recv, sums, lands, after, name):
    n = len(sums)

    def body(*refs):
        ins, bufs = refs[:n], refs[n:2 * n]
        send_ref, recv_ref = refs[2 * n], refs[2 * n + 1]
        x, y, c, others = _place()
        me = _chip_id((x, y))
        for t in range(n):
            for j, chip in enumerate(others):
                cp = pltpu.make_async_remote_copy(src_ref=ins[t].at[_chip_id(chip)], dst_ref=bufs[t].at[_chip_id(chip)],
                                                  send_sem=send_ref.at[3 * t + j], recv_sem=recv_ref.at[3 * t + j],
                                                  device_id=(*chip, c), device_id_type=MESH)
                cp.wait_send()
                cp.wait_recv()

    shapes = [pltpu.HBM(a.shape, a.dtype) for a in sums]
    res = _pcall(body, name=name, in_specs=[HBM] * (2 * n) + [SEM, SEM, ANY],
                 out_shape=(*shapes, *shapes), out_specs=tuple([HBM] * (2 * n)),
                 input_output_aliases={i: i for i in range(2 * n)},
                 compiler_params=pltpu.CompilerParams(has_side_effects=DATAFLOW))(*sums, *lands, send, recv, after)
    return list(res[:n]), list(res[n:])


def join_core_halves(fulls, name):
    n = len(fulls)

    def body(*refs):
        bufs = refs[n:2 * n]
        send, recv = refs[2 * n:]
        x, y, c, _ = _place()
        cps = []
        for t in range(n):
            hr = bufs[t].shape[1] // 2
            mine = bufs[t].at[:, pl.ds(c * hr, hr)]
            cp = pltpu.make_async_remote_copy(src_ref=mine, dst_ref=mine, send_sem=send.at[t], recv_sem=recv.at[t],
                                              device_id=(x, y, 1 - c), device_id_type=MESH)
            cp.start()
            cps.append(cp)
        for t in range(n):
            hr = bufs[t].shape[1] // 2
            theirs = bufs[t].at[:, pl.ds((1 - c) * hr, hr)]
            cps[t].wait_send()
            pltpu.make_async_remote_copy(src_ref=theirs, dst_ref=theirs, send_sem=send.at[t], recv_sem=recv.at[t],
                                         device_id=(x, y, c), device_id_type=MESH).wait_recv()

    outs = [jax.ShapeDtypeStruct(a.shape, a.dtype) for a in fulls]
    return _comm_call(body, name, n, outs, n, in_place=True)(*fulls)


def kernel(x, positions, norm_gains, swa_w_in, swa_sinks, swa_w_out, fox_w_in, fox_b_f, fox_w_out, ffn_w_gate_up, ffn_w_down, loss_target, m_norm_gains, m_swa_w_in, m_swa_sinks, m_swa_w_out, m_fox_w_in, m_fox_b_f, m_fox_w_out, m_ffn_w_gate_up, m_ffn_w_down, v_norm_gains, v_swa_w_in, v_swa_sinks, v_swa_w_out, v_fox_w_in, v_fox_b_f, v_fox_w_out, v_ffn_w_gate_up, v_ffn_w_down):
    s, d = x.shape[1], x.shape[2]
    depth = norm_gains.shape[0]
    n_heads = d // HEAD_DIM
    hd = n_heads * HEAD_DIM
    n_kv = (swa_w_in.shape[2] * N_CHIPS // HEAD_DIM - n_heads) // 2
    ff = ffn_w_down.shape[1] * N_CHIPS
    fox_cols = fox_w_in.shape[2]
    fox_pad = 3 * hd + LANES
    assert fox_cols * N_CHIPS == 3 * hd + n_heads and n_heads <= LANES
    x0 = x[0]
    target = loss_target[0]
    place = ((2 * lax.axis_index("x") + lax.axis_index("y")).astype(jnp.int32).reshape(1),
             lax.axis_index("c").astype(jnp.int32).reshape(1))

    order = [("norm_gains", norm_gains.reshape(1, depth * 4, norm_gains.shape[2]), 0, F32)]
    for layer in range(depth):
        j = layer // 2
        kind, w_i, w_o = ("swa", swa_w_in, swa_w_out) if layer % 2 == 0 else ("fox", fox_w_in, fox_w_out)
        order += [(f"{kind}_w_in_{j}", w_i, j, BF16), (f"{kind}_w_out_{j}", w_o, j, BF16),
                  (f"ffn_w_gate_up_{layer}", ffn_w_gate_up, layer, BF16), (f"ffn_w_down_{layer}", ffn_w_down, layer, BF16)]
    slots = [cast_into_slot(w, l, place, dt, "cast_" + nm) for nm, w, l, dt in order]
    g_send, g_recv, slots, g_token = gather_start(slots, "gather_start")
    slot_of = {entry[0]: t for t, entry in enumerate(order)}

    def weight(nm, after):
        t = slot_of[nm]
        return gather_wait(g_send, g_recv, slots[t], t, after, "gather_wait_" + nm)

    gains = jnp.transpose(weight("norm_gains", g_token), (1, 0, 2)).reshape(depth * 4, d)

    def gain(layer, which):
        return gains[layer * 4 + which][None, :]

    def fox_weight(w_in):
        parts = [w_in[q] for q in range(N_CHIPS)]
        parts.append(jnp.zeros((d, fox_pad - fox_cols * N_CHIPS), BF16))
        return jnp.concatenate(parts, axis=1)

    inv_freq = ROPE_THETA ** (-jnp.arange(0, ROT_DIM, 2, dtype=F32) / ROT_DIM)
    lane_d = jnp.arange(LANES) % HEAD_DIM
    invf_row = jnp.where(lane_d < ROT_DIM, inv_freq[lane_d % (ROT_DIM // 2)], 0.0)[None, :]
    tabs = rope_tables(positions.reshape(s, 1), invf_row, "rope_tables")

    n_sh_in = swa_w_in.shape[2]
    gu_sh = ffn_w_gate_up.shape[2]
    tn_gu = gu_sh // 2 if (gu_sh // 2) % LANES == 0 else gu_sh
    down_sh = ffn_w_down.shape[1]
    out_sh = swa_w_out.shape[1]
    tm = min(1024, s)

    saved = []
    xin = x0
    h = prenorm(xin, gain(0, 0), "prenorm_first")
    for layer in range(depth):
        j = layer // 2
        rec = {"x_in": xin, "h1": h}
        if layer % 2 == 0:
            w_in = weight(f"swa_w_in_{j}", h)
            proj = mm_nn(h, w_in, lambda n, k: (n, k, 0), n_sh_in * N_CHIPS, tm=tm, tn=n_sh_in, tk=d,
                         out_dtype=F32, name=f"swa_proj_{j}")
            q, kd, vd = swa_split(proj, tabs, n_heads, n_kv, f"swa_split_{j}")
            sink_row = jnp.repeat(swa_sinks[j], HEAD_DIM)[None, :]
            attn, lse = swa_fwd(q, kd, vd, sink_row, f"swa_fwd_{j}")
            rec.update(q=q, kd=kd, vd=vd, sink_row=sink_row, lse=lse, w_in=w_in)
            w_out = weight(f"swa_w_out_{j}", attn)
        else:
            wf = fox_weight(weight(f"fox_w_in_{j}", h))
            tn = 3 * hd // 6 if (3 * hd // 6) % LANES == 0 else LANES
            qkv = mm_nn(h, wf, lambda n, k: (k, n), 3 * hd, tm=tm, tn=tn, tk=d, out_dtype=BF16, name=f"fox_proj_{j}")
            f_off = 3 * hd // LANES
            f_logit = mm_nn(h, wf, lambda n, k: (k, f_off + n), LANES, tm=tm, tn=LANES, tk=d, out_dtype=F32, name=f"fox_gate_{j}")
            b_row = jnp.pad(fox_b_f[j], (0, LANES - n_heads))[None, :]
            c = forget_cumsum(f_logit, b_row, f"fox_cumsum_{j}")
            qs, ks, vs = fox_split(qkv, c, n_heads, f"fox_split_{j}")
            attn, qb = fox_fwd(qs, ks, vs, n_heads, f"fox_fwd_{j}")
            rec.update(wf=wf, f_logit=f_logit, b_row=b_row, qb=qb, ks=ks, vs=vs)
            w_out = weight(f"fox_w_out_{j}", attn)
        y = mm_nn(attn, w_out, lambda n, k: (k, 0, n), d, tm=tm, tn=d, tk=out_sh, out_dtype=F32, name=f"out_proj_{layer}")
        xmid, h2 = postnorm_residual(xin, y, gain(layer, 1), gain(layer, 2), f"postnorm_mixer_{layer}")
        npb = gu_sh // tn_gu
        w_gu = weight(f"ffn_w_gate_up_{layer}", h2)
        nbf = ff // tn_gu
        gate, up, act = ffn_up_swiglu(h2, w_gu, lambda n: (n // npb, 0, n % npb), lambda n: ((n + nbf) // npb, 0, (n + nbf) % npb),
                                      ff, tm=min(512, s), tn=tn_gu, name=f"ffn_up_{layer}")
        w_down = weight(f"ffn_w_down_{layer}", act)
        y2 = mm_nn(act, w_down, lambda n, k: (k, 0, n), d, tm=tm, tn=d, tk=down_sh, out_dtype=F32, name=f"ffn_down_{layer}")
        rec.update(attn=attn, y=y, x_mid=xmid, h2=h2, gate=gate, up=up, act=act, y2=y2, w_out=w_out, w_gu=w_gu, w_down=w_down)
        saved.append(rec)
        if layer + 1 < depth:
            xin, h = postnorm_residual(xmid, y2, gain(layer, 3), gain(layer + 1, 0), f"postnorm_ffn_{layer}")
    dx, loss_blk = postnorm_loss(xmid, y2, gain(depth - 1, 3), target, "loss")

    finals = {nm: lax.empty(w.shape, F32) for nm, w in (("swa_w_in", swa_w_in), ("swa_w_out", swa_w_out), ("fox_w_in", fox_w_in),
                                                        ("fox_w_out", fox_w_out), ("ffn_w_gate_up", ffn_w_gate_up),
                                                        ("ffn_w_down", ffn_w_down), ("norm_gains", order[0][1]))}
    in_flight = []

    def finish_reduce(after):
        send_, recv_, sums_, lands_, keys = in_flight.pop()
        sums_, got = scatter_wait(send_, recv_, sums_, lands_, after, "reduce_scatter_wait_" + keys[0][2])
        for own, rcv, (key, l, nm) in zip(sums_, got, keys):
            finals[key] = add_chips(own, rcv, finals[key], l, place, "reduce_add_chips_" + nm)

    def start_reduce(partials, keys):
        if in_flight:
            finish_reduce(partials[0])
        halves = swap_core_halves(partials, "reduce_swap_" + keys[0][2])
        sums_ = [add_core_halves(g, r, place, "reduce_add_cores_" + k[2]) for g, r, k in zip(partials, halves, keys)]
        send_, recv_, sums_, lands_, token = scatter_start(sums_, "reduce_scatter_start_" + keys[0][2])
        in_flight.append((send_, recv_, sums_, lands_, keys))
        return token

    dgains = [None] * (depth * 4)
    dsinks = [None] * ((depth + 1) // 2)
    dbf = [None] * (depth // 2)
    tko = d
    for layer in reversed(range(depth)):
        j = layer // 2
        rec = saved[layer]
        dy2, dgains[layer * 4 + 3] = postnorm_bwd(dx, rec["y2"], gain(layer, 3), f"postnorm_ffn_bwd_{layer}")
        tnd = d
        g_down = mm_tn(rec["act"], dy2, lambda i, n: (i, 0, n), tka=down_sh, tn=tnd, tm=tm, out_block=(None, down_sh, tnd),
                       name=f"ffn_down_dw_{layer}", out_shape=jax.ShapeDtypeStruct((N_CHIPS, down_sh, d), BF16))
        dact = mm_nt(dy2, rec["w_down"], lambda o, n: (o, 0, n), ff, tm=tm, tko=down_sh, tn=d, out_dtype=BF16, name=f"ffn_down_dx_{layer}")
        dgu = swiglu_bwd(dact, rec["gate"], rec["up"], f"swiglu_bwd_{layer}")
        npb = gu_sh // tn_gu
        g_gu = mm_tn(rec["h2"], dgu, lambda i, n: (n // npb, i, n % npb), tka=tko, tn=tn_gu, tm=tm, out_block=(None, tko, tn_gu),
                     name=f"ffn_up_dw_{layer}", out_shape=jax.ShapeDtypeStruct((N_CHIPS, d, gu_sh), BF16))
        sent = start_reduce([g_gu, g_down], [("ffn_w_gate_up", layer, f"ffn_w_gate_up_{layer}"), ("ffn_w_down", layer, f"ffn_w_down_{layer}")])
        dh2 = mm_nt(dgu, rec["w_gu"], lambda o, n: (n // npb, o, n % npb), d, tm=tm, tko=tko, tn=tn_gu, out_dtype=F32,
                    name=f"ffn_up_dx_{layer}", after=sent)
        dxm, dgains[layer * 4 + 2] = prenorm_bwd(rec["x_mid"], dh2, dx, gain(layer, 2), f"prenorm_ffn_bwd_{layer}")
        dy, dgains[layer * 4 + 1] = postnorm_bwd(dxm, rec["y"], gain(layer, 1), f"postnorm_mixer_bwd_{layer}")
        g_out = mm_tn(rec["attn"], dy, lambda i, n: (i, 0, n), tka=out_sh, tn=tnd, tm=min(2048, s), out_block=(None, out_sh, tnd),
                      name=f"out_proj_dw_{layer}", out_shape=jax.ShapeDtypeStruct((N_CHIPS, out_sh, d), BF16))
        dattn = mm_nt(dy, rec["w_out"], lambda o, n: (o, 0, n), hd, tm=tm, tko=out_sh, tn=d, out_dtype=BF16, name=f"out_proj_dx_{layer}")
        if layer % 2 == 0:
            dq, dkd, dvd, dsk = swa_bwd(rec["q"], rec["kd"], rec["vd"], rec["sink_row"], rec["attn"], rec["lse"], dattn, f"swa_bwd_{j}")
            dsinks[j] = dsk[0].reshape(n_heads, HEAD_DIM)[:, 0]
            dproj = swa_merge_bwd(dq, dkd, dvd, tabs, f"swa_merge_bwd_{j}")
            g_in = mm_tn(rec["h1"], dproj, lambda i, n: (n, i, 0), tka=tko, tn=n_sh_in, tm=tm, out_block=(None, tko, n_sh_in),
                         name=f"swa_proj_dw_{j}", out_shape=jax.ShapeDtypeStruct((N_CHIPS, d, n_sh_in), BF16))
            sent = start_reduce([g_in, g_out], [("swa_w_in", j, f"swa_w_in_{j}"), ("swa_w_out", j, f"swa_w_out_{j}")])
            dh1 = mm_nt(dproj, rec["w_in"], lambda o, n: (n, o, 0), d, tm=tm, tko=tko, tn=n_sh_in, out_dtype=F32,
                        name=f"swa_proj_dx_{j}", after=sent)
        else:
            dos = fox_dout_slots(dattn, rec["attn"], f"fox_dout_slots_{j}")
            dqs, dks, dvs = fox_bwd(rec["qb"], rec["ks"], rec["vs"], dos, n_heads, f"fox_bwd_{j}")
            dqkv, dc = fox_merge_bwd(dqs, dks, dvs, n_heads, f"fox_merge_bwd_{j}")
            df, db = forget_gate_bwd(dc, rec["f_logit"], rec["b_row"], f"fox_gate_bwd_{j}")
            dbf[j] = db[0, :n_heads]
            dproj = jnp.concatenate([dqkv, df], axis=1)
            tn_f = LANES * max(k for k in range(1, 9) if (fox_pad // LANES) % k == 0)
            dwf = mm_tn(rec["h1"], dproj, lambda i, n: (i, n), tka=tko, tn=tn_f, tm=tm, out_block=(tko, tn_f),
                        name=f"fox_proj_dw_{j}", out_shape=jax.ShapeDtypeStruct((d, fox_pad), BF16))
            g_in = jnp.stack([dwf[:, q * fox_cols:(q + 1) * fox_cols] for q in range(N_CHIPS)])
            sent = start_reduce([g_in, g_out], [("fox_w_in", j, f"fox_w_in_{j}"), ("fox_w_out", j, f"fox_w_out_{j}")])
            dh1 = mm_nt(dproj, rec["wf"], lambda o, n: (o, n), d, tm=tm, tko=tko, tn=tn_f, out_dtype=F32,
                        name=f"fox_proj_dx_{j}", after=sent)
        dx, dgains[layer * 4] = prenorm_bwd(rec["x_in"], dh1, dxm, gain(layer, 0), f"prenorm_mixer_bwd_{layer}")
    grad_x = dx[None]

    dgain_full = jnp.concatenate(dgains, axis=0)
    start_reduce([jnp.transpose(dgain_full.reshape(depth * 4, N_CHIPS, -1), (1, 0, 2))], [("norm_gains", 0, "norm_gains")])
    finish_reduce(dx)
    keys = ["swa_w_in", "swa_w_out", "fox_w_in", "fox_w_out", "ffn_w_gate_up", "ffn_w_down", "norm_gains"]
    full = join_core_halves([finals[k] for k in keys], "reduce_join_cores")
    g_swa_in_f, g_swa_out_f, g_fox_in_f, g_fox_out_f, g_gu_f, g_down_f, g_gains_f = full
    g_gains_f = g_gains_f.reshape(norm_gains.shape)

    n_swa, n_fox = len(dsinks), len(dbf)
    small = jnp.concatenate([loss_blk[0, :1]] + dsinks + dbf)
    small = lax.psum(small, ("x", "y", "c"))
    loss = small[0]
    g_sinks = small[1:1 + n_swa * n_heads].reshape(n_swa, n_heads)
    g_bf = small[1 + n_swa * n_heads:].reshape(n_fox, n_heads)

    def pad_small(a):
        return jnp.pad(a, ((0, 8 - a.shape[0]), (0, LANES - a.shape[1])))[None]

    def update(w, g, m, v, nm):
        if w.ndim == 2:
            dl, mn, vn = adamw(pad_small(w), pad_small(g), pad_small(m), pad_small(v), "adamw_" + nm)
            return tuple(a[0, :w.shape[0], :w.shape[1]] for a in (dl, mn, vn))
        return adamw(w, g, m, v, "adamw_" + nm)

    grads = [g_gains_f, g_swa_in_f, g_sinks, g_swa_out_f, g_fox_in_f, g_bf, g_fox_out_f, g_gu_f, g_down_f]
    ws = [norm_gains, swa_w_in, swa_sinks, swa_w_out, fox_w_in, fox_b_f, fox_w_out, ffn_w_gate_up, ffn_w_down]
    ms = [m_norm_gains, m_swa_w_in, m_swa_sinks, m_swa_w_out, m_fox_w_in, m_fox_b_f, m_fox_w_out, m_ffn_w_gate_up, m_ffn_w_down]
    vs = [v_norm_gains, v_swa_w_in, v_swa_sinks, v_swa_w_out, v_fox_w_in, v_fox_b_f, v_fox_w_out, v_ffn_w_gate_up, v_ffn_w_down]
    nms = ["norm_gains", "swa_w_in", "swa_sinks", "swa_w_out", "fox_w_in", "fox_b_f", "fox_w_out", "ffn_w_gate_up", "ffn_w_down"]
    upd = [update(w, g, m, v, nm) for w, g, m, v, nm in zip(ws, grads, ms, vs, nms)]
    return (loss, grad_x, *grads, *[u[0] for u in upd], *[u[1] for u in upd], *[u[2] for u in upd])
```

```python
import jax
import jax.numpy as jnp
from jax import lax
from jax.experimental import pallas as pl
from jax.experimental.pallas import tpu as pltpu

F32 = jnp.float32
BF16 = jnp.bfloat16
HEAD_DIM = 64
LANES = 128
WINDOW = 128
ROPE_THETA = 500000.0
ROT_DIM = HEAD_DIM // 4
RMS_EPS = 1e-6
ADAM_LR, ADAM_B1, ADAM_B2, ADAM_EPS, ADAM_WD, ADAM_STEP = 0.001, 0.9, 0.999, 1e-08, 0.01, 10
NEG = -1e30
VMEM_BIG = 56 * 1024 * 1024
N_CHIPS = 4
MESH = pl.DeviceIdType.MESH
ANY = pl.BlockSpec(memory_space=pl.ANY)


def _pcall(body, **kw):
    return pl.pallas_call(body, **kw)


def _cp(sem=None, vmem=None):
    return pltpu.CompilerParams(dimension_semantics=sem, vmem_limit_bytes=vmem)


def _tile(n, pref):
    if n <= pref:
        return n
    t = pref - pref % 16
    while n % t:
        t -= 16
    return t


_DIMS = {"nn": (((1,), (0,)), ((), ())), "nt": (((1,), (1,)), ((), ())), "tn": (((0,), (0,)), ((), ()))}


def _matmul(kind, a, b, *, grid, a_spec, b_spec, o_spec, out_shape, acc_shape, name, after=None):
    nk = grid[2]
    dims = _DIMS[kind]

    def body(a_ref, b_ref, *rest):
        rest = rest[1:] if after is not None else rest
        o_ref = rest[0]

        def prod():
            return lax.dot_general(a_ref[...], b_ref[...], dims, preferred_element_type=F32)

        if nk == 1:
            o_ref[...] = prod().astype(o_ref.dtype)
        else:
            acc = rest[1]
            k = pl.program_id(2)

            @pl.when(k == 0)
            def _():
                acc[...] = prod()

            @pl.when(k > 0)
            def _():
                acc[...] += prod()

            @pl.when(k == nk - 1)
            def _():
                o_ref[...] = acc[...].astype(o_ref.dtype)

    args, in_specs = [a, b], [a_spec, b_spec]
    if after is not None:
        args.append(after)
        in_specs.append(ANY)
    return _pcall(body, name=name, grid=grid, in_specs=in_specs, out_specs=o_spec, out_shape=out_shape,
                  scratch_shapes=[] if nk == 1 else [pltpu.VMEM(acc_shape, F32)],
                  compiler_params=_cp(("parallel", "parallel", "arbitrary"), VMEM_BIG))(*args)


def mm_nn(a, w, w_map, n_out, *, tm, tn, tk, out_dtype, name):
    m, kdim = a.shape
    tm = min(tm, m)
    lead = (None,) * (w.ndim - 2)
    return _matmul("nn", a, w, grid=(m // tm, n_out // tn, kdim // tk),
                   a_spec=pl.BlockSpec((tm, tk), lambda i, j, k: (i, k)),
                   b_spec=pl.BlockSpec(lead + (tk, tn), lambda i, j, k: w_map(j, k)),
                   o_spec=pl.BlockSpec((tm, tn), lambda i, j, k: (i, j)),
                   out_shape=jax.ShapeDtypeStruct((m, n_out), out_dtype), acc_shape=(tm, tn), name=name)


def _split_cols(a, tn):
    if a.ndim == 2:
        return a.shape[0], a.shape[1], (), lambda row, col: (row, col)
    per = a.shape[2] // tn
    return a.shape[1], a.shape[0] * a.shape[2], (None,), lambda row, col: (col // per, row, col % per)


def mm_nt(a, w, w_map, k_out, *, tm, tko, tn, out_dtype, name, after=None):
    m, ndim, a_lead, a_map = _split_cols(a, tn)
    tm = min(tm, m)
    lead = (None,) * (w.ndim - 2)
    return _matmul("nt", a, w, grid=(m // tm, k_out // tko, ndim // tn),
                   a_spec=pl.BlockSpec(a_lead + (tm, tn), lambda i, j, n: a_map(i, n)),
                   b_spec=pl.BlockSpec(lead + (tko, tn), lambda i, j, n: w_map(j, n)),
                   o_spec=pl.BlockSpec((tm, tko), lambda i, j, n: (i, j)),
                   out_shape=jax.ShapeDtypeStruct((m, k_out), out_dtype), acc_shape=(tm, tko), name=name, after=after)


def mm_tn(a, b, o_map, *, tka, tn, tm, out_block, name, out_shape):
    m, kdim = a.shape
    _, n, b_lead, b_map = _split_cols(b, tn)
    tm = min(tm, m)
    return _matmul("tn", a, b, grid=(kdim // tka, n // tn, m // tm),
                   a_spec=pl.BlockSpec((tm, tka), lambda i, j, mm: (mm, i)),
                   b_spec=pl.BlockSpec(b_lead + (tm, tn), lambda i, j, mm: b_map(mm, j)),
                   o_spec=pl.BlockSpec(out_block, lambda i, j, mm: o_map(i, j)),
                   out_shape=out_shape, acc_shape=(tka, tn), name=name)


def _rstd(v):
    return lax.rsqrt(jnp.mean(v * v, axis=-1, keepdims=True) + RMS_EPS)


def _row_spec(tr, d):
    return pl.BlockSpec((tr, d), lambda i: (i, 0))


def _vec_spec(d):
    return pl.BlockSpec((1, d), lambda i: (0, 0))


def prenorm(x, g, name):
    s, d = x.shape
    tr = _tile(s, 256)

    def body(x_ref, g_ref, h_ref):
        v = x_ref[...]
        h_ref[...] = (v * _rstd(v) * g_ref[...]).astype(BF16)

    return _pcall(body, name=name, grid=(s // tr,), in_specs=[_row_spec(tr, d), _vec_spec(d)],
                  out_specs=_row_spec(tr, d), out_shape=jax.ShapeDtypeStruct((s, d), BF16),
                  compiler_params=_cp(("parallel",)))(x, g)


def postnorm_residual(x, y, g_post, g_next, name):
    s, d = x.shape
    tr = _tile(s, 256)

    def body(x_ref, y_ref, gp_ref, gn_ref, xo_ref, h_ref):
        v = y_ref[...]
        xn = x_ref[...] + v * _rstd(v) * gp_ref[...]
        xo_ref[...] = xn
        h_ref[...] = (xn * _rstd(xn) * gn_ref[...]).astype(BF16)

    return _pcall(body, name=name, grid=(s // tr,),
                  in_specs=[_row_spec(tr, d), _row_spec(tr, d), _vec_spec(d), _vec_spec(d)],
                  out_specs=[_row_spec(tr, d), _row_spec(tr, d)],
                  out_shape=[jax.ShapeDtypeStruct((s, d), F32), jax.ShapeDtypeStruct((s, d), BF16)],
                  compiler_params=_cp(("parallel",)))(x, y, g_post, g_next)


def postnorm_loss(x, y, g_post, target, name):
    s, d = x.shape
    tr = _tile(s, 256)

    def body(x_ref, y_ref, gp_ref, t_ref, dx_ref, loss_ref):
        v = y_ref[...]
        err = x_ref[...] + v * _rstd(v) * gp_ref[...] - t_ref[...]
        dx_ref[...] = err / d
        part = 0.5 * jnp.sum(jnp.mean(err * err, axis=-1, keepdims=True), axis=0, keepdims=True)

        @pl.when(pl.program_id(0) == 0)
        def _():
            loss_ref[...] = jnp.zeros_like(loss_ref)

        loss_ref[...] += part

    return _pcall(body, name=name, grid=(s // tr,),
                  in_specs=[_row_spec(tr, d), _row_spec(tr, d), _vec_spec(d), _row_spec(tr, d)],
                  out_specs=[_row_spec(tr, d), pl.BlockSpec((8, LANES), lambda i: (0, 0))],
                  out_shape=[jax.ShapeDtypeStruct((s, d), F32), jax.ShapeDtypeStruct((8, LANES), F32)],
                  compiler_params=_cp(("arbitrary",)))(x, y, g_post, target)


def _norm_bwd(v, g, dz):
    r = _rstd(v)
    vhat = v * r
    u = dz * g
    dv = r * (u - vhat * jnp.mean(u * vhat, axis=-1, keepdims=True))
    return dv, jnp.sum(dz * vhat, axis=0, keepdims=True)


def _acc_rows(ref, val):
    @pl.when(pl.program_id(0) == 0)
    def _():
        ref[...] = jnp.zeros_like(ref)

    ref[...] += val


def postnorm_bwd(dz, y, g, name):
    s, d = y.shape
    tr = _tile(s, 256)

    def body(dz_ref, y_ref, g_ref, dy_ref, dg_ref):
        dv, dg = _norm_bwd(y_ref[...], g_ref[...], dz_ref[...])
        dy_ref[...] = dv.astype(BF16)
        _acc_rows(dg_ref, dg)

    return _pcall(body, name=name, grid=(s // tr,), in_specs=[_row_spec(tr, d), _row_spec(tr, d), _vec_spec(d)],
                  out_specs=[_row_spec(tr, d), _vec_spec(d)],
                  out_shape=[jax.ShapeDtypeStruct((s, d), BF16), jax.ShapeDtypeStruct((1, d), F32)],
                  compiler_params=_cp(("arbitrary",)))(dz, y, g)


def prenorm_bwd(x, dh, dskip, g, name):
    s, d = x.shape
    tr = _tile(s, 256)

    def body(x_ref, dh_ref, ds_ref, g_ref, dx_ref, dg_ref):
        dv, dg = _norm_bwd(x_ref[...], g_ref[...], dh_ref[...])
        dx_ref[...] = ds_ref[...] + dv
        _acc_rows(dg_ref, dg)

    return _pcall(body, name=name, grid=(s // tr,),
                  in_specs=[_row_spec(tr, d), _row_spec(tr, d), _row_spec(tr, d), _vec_spec(d)],
                  out_specs=[_row_spec(tr, d), _vec_spec(d)],
                  out_shape=[jax.ShapeDtypeStruct((s, d), F32), jax.ShapeDtypeStruct((1, d), F32)],
                  compiler_params=_cp(("arbitrary",)))(x, dh, dskip, g)


def ffn_up_swiglu(h, w, gate_map, up_map, ff, *, tm, tn, name):
    m, kdim = h.shape
    tm = min(tm, m)
    lead = (None,) * (w.ndim - 2)

    def body(h_ref, wg_ref, wu_ref, g_ref, u_ref, act_ref):
        a = h_ref[...]
        gate = jnp.dot(a, wg_ref[...], preferred_element_type=F32)
        up = jnp.dot(a, wu_ref[...], preferred_element_type=F32)
        g_ref[...] = gate.astype(BF16)
        u_ref[...] = up.astype(BF16)
        act_ref[...] = (gate * jax.nn.sigmoid(gate) * up).astype(BF16)

    out = jax.ShapeDtypeStruct((m, ff), BF16)
    oblk = pl.BlockSpec((tm, tn), lambda n, i: (i, n))
    return _pcall(body, name=name, grid=(ff // tn, m // tm),
                  in_specs=[pl.BlockSpec((tm, kdim), lambda n, i: (i, 0)), pl.BlockSpec(lead + (kdim, tn), lambda n, i: gate_map(n)),
                            pl.BlockSpec(lead + (kdim, tn), lambda n, i: up_map(n))],
                  out_specs=[oblk] * 3, out_shape=[out] * 3,
                  compiler_params=_cp(("parallel", "parallel"), VMEM_BIG))(h, w, w)


def ffn_down_dx_swiglu(dy, w, w_map, gate, up, *, tm, tn, name):
    m, d = dy.shape
    ff = gate.shape[1]
    tm = min(tm, m)
    lead = (None,) * (w.ndim - 2)

    def body(dy_ref, w_ref, g_ref, u_ref, o_ref):
        da = _nt(dy_ref[...], w_ref[...])
        gt = g_ref[...].astype(F32)
        sig = jax.nn.sigmoid(gt)
        o_ref[0] = (da * u_ref[...].astype(F32) * sig * (1.0 + gt * (1.0 - sig))).astype(BF16)
        o_ref[1] = (da * gt * sig).astype(BF16)

    blk = pl.BlockSpec((tm, tn), lambda n, i: (i, n))
    return _pcall(body, name=name, grid=(ff // tn, m // tm),
                  in_specs=[pl.BlockSpec((tm, d), lambda n, i: (i, 0)), pl.BlockSpec(lead + (tn, d), lambda n, i: w_map(n)), blk, blk],
                  out_specs=pl.BlockSpec((2, tm, tn), lambda n, i: (0, i, n)), out_shape=jax.ShapeDtypeStruct((2, m, ff), BF16),
                  compiler_params=_cp(("parallel", "parallel"), VMEM_BIG))(dy, w, gate, up)


def rope_tables(pos_col, invf_row, name):
    s = pos_col.shape[0]
    tr = _tile(s, 1024)

    def body(p_ref, f_ref, c_ref, sa_ref, sb_ref):
        ang = p_ref[...].astype(F32) * f_ref[...]
        d = lax.broadcasted_iota(jnp.int32, (1, LANES), 1) % HEAD_DIM
        cs, sn = jnp.cos(ang), jnp.sin(ang)
        c_ref[...] = jnp.where(d < ROT_DIM, cs, 1.0)
        sa_ref[...] = jnp.where(d < ROT_DIM // 2, -sn, 0.0)
        sb_ref[...] = jnp.where((d >= ROT_DIM // 2) & (d < ROT_DIM), sn, 0.0)

    tab = jax.ShapeDtypeStruct((s, LANES), F32)
    return _pcall(body, name=name, grid=(s // tr,),
                  in_specs=[pl.BlockSpec((tr, 1), lambda i: (i, 0)), _vec_spec(LANES)],
                  out_specs=[_row_spec(tr, LANES)] * 3, out_shape=[tab] * 3, compiler_params=_cp(("parallel",)))(pos_col, invf_row)


def _rot(t, c, sa, sb):
    half = ROT_DIM // 2
    return t * c + pltpu.roll(t, LANES - half, 1) * sa + pltpu.roll(t, half, 1) * sb


def _rot_t(t, c, sa, sb):
    half = ROT_DIM // 2
    return t * c + pltpu.roll(t * sa, half, 1) + pltpu.roll(t * sb, LANES - half, 1)


def swa_split(proj, tabs, n_heads, n_kv, name):
    s, width = proj.shape
    qd, kd = n_heads * HEAD_DIM, n_kv * HEAD_DIM
    tr = _tile(s, 256)

    def body(p_ref, c_ref, sa_ref, sb_ref, q_ref, k_ref, v_ref):
        c, sa, sb = c_ref[...], sa_ref[...], sb_ref[...]
        low = lax.broadcasted_iota(jnp.int32, (1, LANES), 1) < HEAD_DIM
        for g in range(qd // LANES):
            q_ref[:, g * LANES:(g + 1) * LANES] = _rot(p_ref[:, g * LANES:(g + 1) * LANES], c, sa, sb).astype(BF16)
        for g in range(kd // LANES):
            for src, dst, rot in ((qd, k_ref, True), (qd + kd, v_ref, False)):
                t = p_ref[:, src + g * LANES:src + (g + 1) * LANES]
                t = _rot(t, c, sa, sb) if rot else t
                sw = pltpu.roll(t, HEAD_DIM, 1)
                dst[:, (2 * g) * LANES:(2 * g + 1) * LANES] = jnp.where(low, t, sw).astype(BF16)
                dst[:, (2 * g + 1) * LANES:(2 * g + 2) * LANES] = jnp.where(low, sw, t).astype(BF16)

    return _pcall(body, name=name, grid=(s // tr,),
                  in_specs=[_row_spec(tr, width)] + [_row_spec(tr, LANES)] * 3,
                  out_specs=[_row_spec(tr, qd), _row_spec(tr, 2 * kd), _row_spec(tr, 2 * kd)],
                  out_shape=[jax.ShapeDtypeStruct((s, qd), BF16), jax.ShapeDtypeStruct((s, 2 * kd), BF16),
                             jax.ShapeDtypeStruct((s, 2 * kd), BF16)],
                  compiler_params=_cp(("parallel",)))(proj, *tabs)


def swa_merge_bwd(dq, dkd, dvd, tabs, name):
    s, qd = dq.shape
    kd = dkd.shape[1] // 2
    width = qd + 2 * kd
    tr = _tile(s, 256)

    def body(dq_ref, dk_ref, dv_ref, c_ref, sa_ref, sb_ref, o_ref):
        c, sa, sb = c_ref[...], sa_ref[...], sb_ref[...]
        low = lax.broadcasted_iota(jnp.int32, (1, LANES), 1) < HEAD_DIM
        for g in range(qd // LANES):
            t = dq_ref[:, g * LANES:(g + 1) * LANES].astype(F32)
            o_ref[:, g * LANES:(g + 1) * LANES] = _rot_t(t, c, sa, sb).astype(BF16)
        for g in range(kd // LANES):
            for dst, src, rot in ((qd, dk_ref, True), (qd + kd, dv_ref, False)):
                e = src[:, (2 * g) * LANES:(2 * g + 1) * LANES]
                o = src[:, (2 * g + 1) * LANES:(2 * g + 2) * LANES]
                t = jnp.where(low, e + pltpu.roll(e, HEAD_DIM, 1), o + pltpu.roll(o, HEAD_DIM, 1))
                t = _rot_t(t, c, sa, sb) if rot else t
                o_ref[:, dst + g * LANES:dst + (g + 1) * LANES] = t.astype(BF16)

    return _pcall(body, name=name, grid=(s // tr,),
                  in_specs=[_row_spec(tr, qd), _row_spec(tr, 2 * kd), _row_spec(tr, 2 * kd)] + [_row_spec(tr, LANES)] * 3,
                  out_specs=_row_spec(tr, width), out_shape=jax.ShapeDtypeStruct((s, width), BF16),
                  compiler_params=_cp(("parallel",)))(dq, dkd, dvd, *tabs)


def _halves():
    lane_half = lax.broadcasted_iota(jnp.int32, (1, LANES), 1) // HEAD_DIM
    return [lane_half == 0, lane_half == 1]


def _nt(a, b):
    return lax.dot_general(a, b, _DIMS["nt"], preferred_element_type=F32)


def _tn(a, b):
    return lax.dot_general(a, b, _DIMS["tn"], preferred_element_type=F32)


def _band_mask(i, tq):
    w = tq + WINDOW
    rel = lax.broadcasted_iota(jnp.int32, (tq, w), 1) - lax.broadcasted_iota(jnp.int32, (tq, w), 0)
    first = lax.broadcasted_iota(jnp.int32, (tq, w), 1) >= jnp.where(i > 0, 0, WINDOW)
    return (rel >= 1) & (rel <= WINDOW) & first


def swa_fwd(q, kd, vd, sink_row, name):
    s, qd = q.shape
    tq = min(256, s)
    r = tq // WINDOW
    pairs = qd // LANES
    group_pairs = pairs // (kd.shape[1] // LANES)
    scale = HEAD_DIM ** -0.5

    def body(q_ref, kp_ref, kc_ref, vp_ref, vc_ref, sk_ref, o_ref, lse_ref):
        i = pl.program_id(1)
        k = jnp.concatenate([kp_ref[...], kc_ref[...]], axis=0)
        v = jnp.concatenate([vp_ref[...], vc_ref[...]], axis=0)
        mask = _band_mask(i, tq)
        q2 = q_ref[...]
        outs, lses = [], []
        for a, hm in enumerate(_halves()):
            sc = _nt(jnp.where(hm, q2, jnp.zeros_like(q2)), k) * scale
            sc = jnp.where(mask, sc, NEG)
            sink = sk_ref[:, a * HEAD_DIM:a * HEAD_DIM + 1]
            m = jnp.maximum(jnp.max(sc, axis=-1, keepdims=True), sink)
            p = jnp.exp(sc - m)
            den = jnp.sum(p, axis=-1, keepdims=True) + jnp.exp(sink - m)
            outs.append(jnp.dot(p.astype(BF16), v, preferred_element_type=F32) / den)
            lses.append(m + jnp.log(den))
        hm0 = _halves()[0]
        o_ref[...] = jnp.where(hm0, outs[0], outs[1]).astype(BF16)
        lse_ref[...] = jnp.where(hm0, lses[0], lses[1])

    prev = lambda p, i: (jnp.maximum(i * r - 1, 0), p // group_pairs)
    cur = lambda p, i: (i, p // group_pairs)
    blk = pl.BlockSpec((tq, LANES), lambda p, i: (i, p))
    return _pcall(body, name=name, grid=(pairs, s // tq),
                  in_specs=[blk, pl.BlockSpec((WINDOW, LANES), prev), pl.BlockSpec((tq, LANES), cur),
                            pl.BlockSpec((WINDOW, LANES), prev), pl.BlockSpec((tq, LANES), cur),
                            pl.BlockSpec((1, LANES), lambda p, i: (0, p))],
                  out_specs=[blk, blk],
                  out_shape=[jax.ShapeDtypeStruct((s, qd), BF16), jax.ShapeDtypeStruct((s, qd), F32)],
                  compiler_params=_cp(("parallel", "parallel")))(q, kd, kd, vd, vd, sink_row)


def swa_bwd(q, kd, vd, sink_row, out, lse, dout, name):
    s, qd = q.shape
    tq = min(256, s)
    r = tq // WINDOW
    n_kv = kd.shape[1] // LANES
    gw = qd // n_kv
    scale = HEAD_DIM ** -0.5

    def body(q_ref, kp_ref, kc_ref, vp_ref, vc_ref, sk_ref, o_ref, lse_ref, do_ref, dq_ref, dk_ref, dv_ref, dsk_ref):
        i = pl.program_id(1)

        @pl.when(i == 0)
        def _():
            dk_ref[...] = jnp.zeros_like(dk_ref)
            dv_ref[...] = jnp.zeros_like(dv_ref)
            dsk_ref[...] = jnp.zeros_like(dsk_ref)

        k = jnp.concatenate([kp_ref[...], kc_ref[...]], axis=0)
        v = jnp.concatenate([vp_ref[...], vc_ref[...]], axis=0)
        mask = _band_mask(i, tq)
        dk = jnp.zeros((tq + WINDOW, LANES), F32)
        dv = jnp.zeros((tq + WINDOW, LANES), F32)
        for pp in range(gw // LANES):
            cols = slice(pp * LANES, (pp + 1) * LANES)
            q2, do2 = q_ref[:, cols], do_ref[:, cols]
            prod = do2.astype(F32) * o_ref[:, cols].astype(F32)
            dq2 = jnp.zeros((tq, LANES), F32)
            dsk = jnp.zeros((1, LANES), F32)
            for a, hm in enumerate(_halves()):
                qa = jnp.where(hm, q2, jnp.zeros_like(q2))
                doa = jnp.where(hm, do2, jnp.zeros_like(do2))
                lse_a = lse_ref[:, pp * LANES + a * HEAD_DIM:pp * LANES + a * HEAD_DIM + 1]
                sc = jnp.where(mask, _nt(qa, k) * scale, NEG)
                p = jnp.exp(sc - lse_a)
                delta = jnp.sum(jnp.where(hm, prod, 0.0), axis=-1, keepdims=True)
                ds = (p * (_nt(doa, v) - delta) * scale).astype(BF16)
                dv = dv + _tn(p.astype(BF16), doa)
                dk = dk + _tn(ds, qa)
                dq2 = dq2 + jnp.where(hm, jnp.dot(ds, k, preferred_element_type=F32), 0.0)
                sink = sk_ref[:, pp * LANES + a * HEAD_DIM:pp * LANES + a * HEAD_DIM + 1]
                dsink = -jnp.sum(jnp.exp(sink - lse_a) * delta, axis=0, keepdims=True)
                dsk = dsk + jnp.where(hm, dsink, 0.0)
            dq_ref[:, cols] = dq2.astype(BF16)
            dsk_ref[0:1, cols] += dsk
        start = pl.multiple_of(i * tq, tq)
        dk_ref[pl.ds(start, tq), :] += dk[WINDOW:, :]
        dv_ref[pl.ds(start, tq), :] += dv[WINDOW:, :]

        @pl.when(i > 0)
        def _():
            before = pl.multiple_of(i * tq - WINDOW, WINDOW)
            dk_ref[pl.ds(before, WINDOW), :] += dk[:WINDOW, :]
            dv_ref[pl.ds(before, WINDOW), :] += dv[:WINDOW, :]

    prev = lambda g, i: (jnp.maximum(i * r - 1, 0), g)
    cur = lambda g, i: (i, g)
    wide = pl.BlockSpec((tq, gw), cur)
    full = pl.BlockSpec((s, LANES), lambda g, i: (0, g))
    return _pcall(body, name=name, grid=(n_kv, s // tq),
                  in_specs=[wide, pl.BlockSpec((WINDOW, LANES), prev), pl.BlockSpec((tq, LANES), cur),
                            pl.BlockSpec((WINDOW, LANES), prev), pl.BlockSpec((tq, LANES), cur),
                            pl.BlockSpec((1, gw), lambda g, i: (0, g)), wide, wide, wide],
                  out_specs=[wide, full, full, pl.BlockSpec((8, gw), lambda g, i: (0, g))],
                  out_shape=[jax.ShapeDtypeStruct((s, qd), BF16), jax.ShapeDtypeStruct(kd.shape, F32),
                             jax.ShapeDtypeStruct(kd.shape, F32), jax.ShapeDtypeStruct((8, qd), F32)],
                  compiler_params=_cp(("parallel", "arbitrary"), VMEM_BIG))(q, kd, kd, vd, vd, sink_row, out, lse, dout)


def forget_cumsum(f_logit, b_row, name):
    s = f_logit.shape[0]

    def body(f_ref, b_ref, c_ref):
        z = f_ref[...] + b_ref[...]
        acc = jnp.minimum(z, 0.0) - jnp.log(1.0 + jnp.exp(-jnp.abs(z)))
        row = lax.broadcasted_iota(jnp.int32, (s, LANES), 0)
        d = 1
        while d < s:
            acc = acc + jnp.where(row >= d, pltpu.roll(acc, d, 0), 0.0)
            d *= 2
        c_ref[...] = acc

    return _pcall(body, name=name, in_specs=[pl.BlockSpec((s, LANES), lambda: (0, 0)), pl.BlockSpec((1, LANES), lambda: (0, 0))],
                  out_specs=pl.BlockSpec((s, LANES), lambda: (0, 0)), out_shape=jax.ShapeDtypeStruct((s, LANES), F32),
                  compiler_params=_cp(None, VMEM_BIG))(f_logit, b_row)


def forget_gate_bwd(dc, f_logit, b_row, name):
    s = dc.shape[0]

    def body(dc_ref, f_ref, b_ref, df_ref, db_ref):
        acc = dc_ref[...]
        row = lax.broadcasted_iota(jnp.int32, (s, LANES), 0)
        d = 1
        while d < s:
            acc = acc + jnp.where(row < s - d, pltpu.roll(acc, s - d, 0), 0.0)
            d *= 2
        df = acc * jax.nn.sigmoid(-(f_ref[...] + b_ref[...]))
        df_ref[...] = df.astype(BF16)
        db_ref[...] = jnp.sum(df, axis=0, keepdims=True)

    whole = pl.BlockSpec((s, LANES), lambda: (0, 0))
    vec = pl.BlockSpec((1, LANES), lambda: (0, 0))
    return _pcall(body, name=name, in_specs=[whole, whole, vec], out_specs=[whole, vec],
                  out_shape=[jax.ShapeDtypeStruct((s, LANES), BF16), jax.ShapeDtypeStruct((1, LANES), F32)],
                  compiler_params=_cp(None, VMEM_BIG))(dc, f_logit, b_row)


EXTRA = HEAD_DIM
N_PIECES = 3


def _pieces(v):
    hi = v.astype(BF16).astype(F32)
    mid = (v - hi).astype(BF16).astype(F32)
    return hi, mid, (v - hi - mid).astype(BF16).astype(F32)


def _slot(main, lane, extras=None, ones_at=None):
    out = jnp.where(lane < HEAD_DIM, main, 0.0)
    if extras is not None:
        for r, e in enumerate(extras):
            out = jnp.where(lane == EXTRA + r, e, out)
    if ones_at is not None:
        out = jnp.where((lane >= ones_at) & (lane < ones_at + N_PIECES), 1.0, out)
    return out


def _lane_iota():
    return lax.broadcasted_iota(jnp.int32, (1, LANES), 1)


def fox_split(qkv, c, n_heads, name):
    s = qkv.shape[0]
    hd = n_heads * HEAD_DIM
    tr = _tile(s, 256)
    scale = HEAD_DIM ** -0.5

    def body(x_ref, c_ref, q_ref, k_ref, v_ref):
        lane = _lane_iota()
        cv = c_ref[...]
        for g in range(hd // LANES):
            for part, dst in enumerate((q_ref, k_ref, v_ref)):
                t = x_ref[:, part * hd + g * LANES:part * hd + (g + 1) * LANES].astype(F32)
                for a, main in enumerate((t, pltpu.roll(t, HEAD_DIM, 1))):
                    h = 2 * g + a
                    if part == 0:
                        val = _slot(main * scale, lane, ones_at=EXTRA)
                    elif part == 1:
                        ch = jnp.sum(jnp.where(lane == h, cv, 0.0), axis=1, keepdims=True)
                        val = _slot(main, lane, extras=_pieces(-ch), ones_at=EXTRA + N_PIECES)
                    else:
                        val = _slot(main, lane, ones_at=EXTRA)
                    dst[:, h * LANES:(h + 1) * LANES] = val.astype(BF16)

    slots = jax.ShapeDtypeStruct((s, n_heads * LANES), BF16)
    return _pcall(body, name=name, grid=(s // tr,), in_specs=[_row_spec(tr, 3 * hd), _row_spec(tr, LANES)],
                  out_specs=[_row_spec(tr, n_heads * LANES)] * 3, out_shape=[slots] * 3,
                  compiler_params=_cp(("parallel",), VMEM_BIG))(qkv, c)


def _causal_keep(t):
    return lax.broadcasted_iota(jnp.int32, (t, t), 1) <= lax.broadcasted_iota(jnp.int32, (t, t), 0)


def fox_fwd(qs, ks, vs, n_heads, name):
    s = qs.shape[0]
    pairs = n_heads // 2
    t = min(512, s)
    ratio = 2 if s % (2 * t) == 0 else 1
    tq = ratio * t
    wide = 2 * LANES

    def body(q_ref, k_ref, v_ref, o_ref, qb_ref, acc_ref, m_ref):
        i = pl.program_id(1)
        lane = _lane_iota()
        acc_ref[...] = jnp.zeros_like(acc_ref)
        m_ref[...] = jnp.full_like(m_ref, NEG)

        def step(j, diagonal):
            rows = pl.ds(pl.multiple_of(j * t, t), t)
            for a in range(2):
                cols = slice(a * LANES, (a + 1) * LANES)
                sc = _nt(q_ref[:, cols], k_ref[rows, cols])
                if diagonal is not None:
                    keep = (lax.broadcasted_iota(jnp.int32, (tq, t), 1) + diagonal * t) <= lax.broadcasted_iota(jnp.int32, (tq, t), 0)
                    sc = jnp.where(keep, sc, NEG)
                m_old = m_ref[a]
                m_new = jnp.maximum(m_old, jnp.max(sc, axis=-1, keepdims=True))
                p = jnp.exp(sc - jnp.tile(m_new, (1, t // LANES)))
                acc_ref[a] = jnp.exp(m_old - m_new) * acc_ref[a] + jnp.dot(p.astype(BF16), v_ref[rows, cols],
                                                                              preferred_element_type=F32)
                m_ref[a] = m_new

        def two_steps(j2, carry):
            step(2 * j2, None)
            step(2 * j2 + 1, None)
            return carry

        past = ratio * i
        lax.fori_loop(0, past // 2, two_steps, 0)
        if ratio % 2:
            @pl.when(past % 2 == 1)
            def _():
                step(past - 1, None)

        for u in range(ratio):
            step(past + u, u)
        outs = []
        for a in range(2):
            cols = slice(a * LANES, (a + 1) * LANES)
            acc = acc_ref[a]
            norm = acc[:, EXTRA:EXTRA + 1]
            outs.append(acc / norm)
            neg_lse = _pieces(-(m_ref[a] + jnp.log(norm)))
            qb = q_ref[:, cols].astype(F32)
            for r in range(N_PIECES):
                qb = jnp.where(lane == EXTRA + N_PIECES + r, neg_lse[r], qb)
            qb_ref[:, cols] = qb.astype(BF16)
        o_ref[...] = jnp.where(lane < HEAD_DIM, outs[0], pltpu.roll(outs[1], HEAD_DIM, 1)).astype(BF16)

    qblk = pl.BlockSpec((tq, wide), lambda p, i: (i, p))
    whole = pl.BlockSpec((s, wide), lambda p, i: (0, p))
    return _pcall(body, name=name, grid=(pairs, s // tq), in_specs=[qblk, whole, whole],
                  out_specs=[pl.BlockSpec((tq, LANES), lambda p, i: (i, p)), qblk],
                  out_shape=[jax.ShapeDtypeStruct((s, n_heads * HEAD_DIM), BF16), jax.ShapeDtypeStruct(qs.shape, BF16)],
                  scratch_shapes=[pltpu.VMEM((2, tq, LANES), F32)] * 2,
                  compiler_params=_cp(("parallel", "arbitrary"), VMEM_BIG))(qs, ks, vs)


def fox_dout_slots(dout, out, name):
    s, hd = dout.shape
    tr = _tile(s, 256)

    def body(d_ref, o_ref, s_ref):
        lane = _lane_iota()
        for g in range(hd // LANES):
            cols = slice(g * LANES, (g + 1) * LANES)
            d2 = d_ref[:, cols].astype(F32)
            prod = d2 * o_ref[:, cols].astype(F32)
            for a, main in enumerate((d2, pltpu.roll(d2, HEAD_DIM, 1))):
                delta = jnp.sum(jnp.where((lane // HEAD_DIM) == a, prod, 0.0), axis=1, keepdims=True)
                h = 2 * g + a
                s_ref[:, h * LANES:(h + 1) * LANES] = _slot(main, lane, extras=_pieces(-delta)).astype(BF16)

    return _pcall(body, name=name, grid=(s // tr,), in_specs=[_row_spec(tr, hd), _row_spec(tr, hd)],
                  out_specs=_row_spec(tr, 2 * hd), out_shape=jax.ShapeDtypeStruct((s, 2 * hd), BF16),
                  compiler_params=_cp(("parallel",)))(dout, out)


def fox_bwd(qb, ks, vs, dos, n_heads, name):
    s = qb.shape[0]
    pairs = n_heads // 2
    t = min(512, s)
    nblk = s // t
    wide = 2 * LANES

    def body(q_ref, k_ref, v_ref, do_ref, dq_ref, dk_ref, dv_ref, dka_ref, dva_ref):
        j = pl.program_id(1)

        @pl.when(j == 0)
        def _():
            dq_ref[...] = jnp.zeros_like(dq_ref)

        dka_ref[...] = jnp.zeros_like(dka_ref)
        dva_ref[...] = jnp.zeros_like(dva_ref)

        def step(i, diagonal):
            rows = pl.ds(pl.multiple_of(i * t, t), t)
            for a in range(2):
                cols = slice(a * LANES, (a + 1) * LANES)
                qa, doa, ka = q_ref[rows, cols], do_ref[rows, cols], k_ref[:, cols]
                sc = _nt(qa, ka)
                if diagonal:
                    sc = jnp.where(_causal_keep(t), sc, NEG)
                p = jnp.exp(sc)
                ds = (p * _nt(doa, v_ref[:, cols])).astype(BF16)
                dva_ref[a] += _tn(p.astype(BF16), doa)
                dka_ref[a] += _tn(ds, qa)
                dq_ref[rows, cols] += jnp.dot(ds, ka, preferred_element_type=F32)

        step(j, True)
        below = nblk - 1 - j

        def two_steps(i2, carry):
            step(j + 1 + 2 * i2, False)
            step(j + 2 + 2 * i2, False)
            return carry

        lax.fori_loop(0, below // 2, two_steps, 0)

        @pl.when(below % 2 == 1)
        def _():
            step(nblk - 1, False)

        for a in range(2):
            cols = slice(a * LANES, (a + 1) * LANES)
            dk_ref[:, cols] = dka_ref[a]
            dv_ref[:, cols] = dva_ref[a].astype(BF16)

    whole = pl.BlockSpec((s, wide), lambda p, j: (0, p))
    blk = pl.BlockSpec((t, wide), lambda p, j: (j, p))
    return _pcall(body, name=name, grid=(pairs, nblk), in_specs=[whole, blk, blk, whole], out_specs=[whole, blk, blk],
                  out_shape=[jax.ShapeDtypeStruct(qb.shape, F32), jax.ShapeDtypeStruct(qb.shape, F32),
                             jax.ShapeDtypeStruct(qb.shape, BF16)],
                  scratch_shapes=[pltpu.VMEM((2, t, LANES), F32)] * 2,
                  compiler_params=_cp(("parallel", "arbitrary"), VMEM_BIG))(qb, ks, vs, dos)


def fox_merge_bwd(dqs, dks, dvs, n_heads, name):
    s = dqs.shape[0]
    hd = n_heads * HEAD_DIM
    tr = _tile(s, 128)
    scale = HEAD_DIM ** -0.5

    def body(dq_ref, dk_ref, dv_ref, o_ref, dc_ref):
        lane = _lane_iota()
        dc = jnp.zeros((tr, LANES), F32)
        for g in range(hd // LANES):
            even = slice(2 * g * LANES, (2 * g + 1) * LANES)
            odd = slice((2 * g + 1) * LANES, (2 * g + 2) * LANES)
            for part, (src, mul) in enumerate(((dq_ref, scale), (dk_ref, 1.0), (dv_ref, 1.0))):
                dense = jnp.where(lane < HEAD_DIM, src[:, even].astype(F32), pltpu.roll(src[:, odd].astype(F32), HEAD_DIM, 1))
                o_ref[:, part * hd + g * LANES:part * hd + (g + 1) * LANES] = (dense * mul).astype(BF16)
            for a, cols in enumerate((even, odd)):
                both = jnp.where(lane == EXTRA + N_PIECES, dq_ref[:, cols], 0.0) - jnp.where(lane == EXTRA, dk_ref[:, cols], 0.0)
                dc = jnp.where(lane == 2 * g + a, jnp.sum(both, axis=1, keepdims=True), dc)
        dc_ref[...] = dc

    wide = n_heads * LANES
    return _pcall(body, name=name, grid=(s // tr,), in_specs=[_row_spec(tr, wide)] * 3,
                  out_specs=[_row_spec(tr, 3 * hd), _row_spec(tr, LANES)],
                  out_shape=[jax.ShapeDtypeStruct((s, 3 * hd), BF16), jax.ShapeDtypeStruct((s, LANES), F32)],
                  compiler_params=_cp(("parallel",), VMEM_BIG))(dqs, dks, dvs)


def _w_spec(shape, lead_map=None):
    _, r, c = shape[-3:]
    tr = _tile(r, 128)
    n_lead = len(shape) - 2
    if lead_map is None:
        lead_map = lambda l: (l,)
    return tr, pl.BlockSpec((None,) * n_lead + (tr, c), lambda l, i: (*lead_map(l), i, 0))


def cast_into_slot(w, layer, place, dtype, name):
    tr = _tile(w.shape[1], 128)
    blk = (None, tr, w.shape[2])

    def body(me_ref, c_ref, w_ref, o_ref):
        o_ref[...] = w_ref[...].astype(dtype)

    grid_spec = pltpu.PrefetchScalarGridSpec(
        num_scalar_prefetch=2, grid=(w.shape[1] // tr,),
        in_specs=[pl.BlockSpec(blk, lambda i, me, c: (layer, i, 0))],
        out_specs=pl.BlockSpec(blk, lambda i, me, c: (me[0], i, 0)))
    return _pcall(body, name=name, grid_spec=grid_spec, out_shape=jax.ShapeDtypeStruct((N_CHIPS,) + w.shape[1:], dtype),
                  compiler_params=_cp(("parallel",)))(*place, w)


def adamw(w, g, m, v, name):
    tr, spec = _w_spec(w.shape)

    def body(w_ref, g_ref, m_ref, v_ref, d_ref, mo_ref, vo_ref):
        gg = g_ref[...]
        mn = ADAM_B1 * m_ref[...] + (1.0 - ADAM_B1) * gg
        vn = ADAM_B2 * v_ref[...] + (1.0 - ADAM_B2) * (gg * gg)
        m_hat = mn / (1.0 - ADAM_B1 ** ADAM_STEP)
        v_hat = vn / (1.0 - ADAM_B2 ** ADAM_STEP)
        d_ref[...] = -ADAM_LR * (m_hat / (jnp.sqrt(v_hat) + ADAM_EPS) + ADAM_WD * w_ref[...])
        mo_ref[...] = mn
        vo_ref[...] = vn

    out = jax.ShapeDtypeStruct(w.shape, F32)
    return _pcall(body, name=name, grid=(w.shape[0], w.shape[1] // tr), in_specs=[spec] * 4, out_specs=[spec] * 3,
                  out_shape=[out] * 3, compiler_params=_cp(("parallel", "parallel")))(w, g, m, v)


def add_core_halves(g, recv, place, name):
    hr = recv.shape[1]
    tr = _tile(hr, 128)
    nb = hr // tr
    blk = (None, tr, g.shape[2])

    def body(me_ref, c_ref, g_ref, r_ref, o_ref):
        o_ref[...] = (g_ref[...].astype(F32) + r_ref[...].astype(F32)).astype(o_ref.dtype)

    grid_spec = pltpu.PrefetchScalarGridSpec(
        num_scalar_prefetch=2, grid=(N_CHIPS, nb),
        in_specs=[pl.BlockSpec(blk, lambda q, i, me, c: (q, c[0] * nb + i, 0)), pl.BlockSpec(blk, lambda q, i, me, c: (q, i, 0))],
        out_specs=pl.BlockSpec(blk, lambda q, i, me, c: (q, i, 0)))
    return _pcall(body, name=name, grid_spec=grid_spec, out_shape=jax.ShapeDtypeStruct(recv.shape, g.dtype),
                  compiler_params=_cp(("parallel", "parallel")))(*place, g, recv)


def add_chips(own, recv, into, layer, place, name):
    _, hr, cols = own.shape
    tr = _tile(hr, 128)
    nb = hr // tr
    blk = (None, tr, cols)

    def body(me_ref, c_ref, p0, p1, p2, p3, _, o_ref):
        o_ref[...] = ((p0[...].astype(F32) + p1[...].astype(F32)) + p2[...].astype(F32)) + p3[...].astype(F32)

    def peer(flip):
        return lambda i, me, c: (me[0] ^ flip, i, 0)

    grid_spec = pltpu.PrefetchScalarGridSpec(
        num_scalar_prefetch=2, grid=(nb,), in_specs=[pl.BlockSpec(blk, peer(f)) for f in (0, 2, 1, 3)] + [ANY],
        out_specs=pl.BlockSpec(blk, lambda i, me, c: (layer, c[0] * nb + i, 0)))
    return _pcall(body, name=name, grid_spec=grid_spec, out_shape=jax.ShapeDtypeStruct(into.shape, F32),
                  input_output_aliases={6: 0}, compiler_params=_cp(("parallel",)))(*place, own, recv, recv, recv, into)


def _place():
    x, y, c = lax.axis_index("x"), lax.axis_index("y"), lax.axis_index("c")
    others = [(1 - x, y), (x, 1 - y), (1 - x, 1 - y)]
    return x, y, c, others


def _chip_id(chip):
    return 2 * chip[0] + chip[1]


def _comm_call(body, name, n_in, out_shapes, n_sems, in_place=False):
    return _pcall(body, name=name, in_specs=[ANY] * n_in, out_specs=[ANY] * len(out_shapes), out_shape=out_shapes,
                  scratch_shapes=[pltpu.SemaphoreType.DMA((n_sems,)), pltpu.SemaphoreType.DMA((n_sems,))],
                  input_output_aliases={t: t for t in range(n_in)} if in_place else {},
                  compiler_params=pltpu.CompilerParams(has_side_effects=True))


def swap_core_halves(grads, name):
    n = len(grads)

    def body(*refs):
        ins, outs = refs[:n], refs[n:2 * n]
        send, recv = refs[2 * n:]
        x, y, c, _ = _place()
        cps = []
        for t in range(n):
            hr = ins[t].shape[1] // 2
            cp = pltpu.make_async_remote_copy(src_ref=ins[t].at[:, pl.ds((1 - c) * hr, hr)], dst_ref=outs[t],
                                              send_sem=send.at[t], recv_sem=recv.at[t],
                                              device_id=(x, y, 1 - c), device_id_type=MESH)
            cp.start()
            cps.append(cp)
        for cp in cps:
            cp.wait()

    outs = [jax.ShapeDtypeStruct((a.shape[0], a.shape[1] // 2) + a.shape[2:], a.dtype) for a in grads]
    return _comm_call(body, name, n, outs, n)(*grads)


HBM = pl.BlockSpec(memory_space=pltpu.HBM)
SEM = pl.BlockSpec(memory_space=pltpu.SEMAPHORE)
DATAFLOW = pltpu.SideEffectType.DATAFLOW_SIDE_EFFECTING


def _in_hbm(a):
    return pltpu.with_memory_space_constraint(a, pltpu.HBM)


def gather_start(slots, name):
    n = len(slots)

    def body(*refs):
        bufs = refs[:n]
        send, recv = refs[n], refs[n + 1]
        token = refs[-1]
        x, y, c, others = _place()
        me = _chip_id((x, y))
        for t in range(n):
            for j, chip in enumerate(others):
                pltpu.make_async_remote_copy(src_ref=bufs[t].at[me], dst_ref=bufs[t].at[me],
                                             send_sem=send.at[3 * t + j], recv_sem=recv.at[3 * t + j],
                                             device_id=(*chip, c), device_id_type=MESH).start()
        token[...] = jnp.zeros_like(token)

    sems = pltpu.SemaphoreType.DMA((3 * n,))
    res = _pcall(body, name=name, in_specs=[HBM] * n,
                 out_shape=(sems, sems, *[pltpu.HBM(a.shape, a.dtype) for a in slots], jax.ShapeDtypeStruct((8, LANES), F32)),
                 out_specs=(SEM, SEM, *[HBM] * n, pl.BlockSpec(memory_space=pltpu.VMEM)),
                 input_output_aliases={i: 2 + i for i in range(n)},
                 compiler_params=pltpu.CompilerParams(has_side_effects=DATAFLOW))(*[_in_hbm(a) for a in slots])
    return res[0], res[1], list(res[2:2 + n]), res[-1]


def gather_wait(send, recv, slot, t, after, name):
    def body(buf, send_ref, recv_ref, after_ref, out):
        x, y, c, others = _place()
        me = _chip_id((x, y))
        for j, chip in enumerate(others):
            pltpu.make_async_remote_copy(src_ref=buf.at[me], dst_ref=buf.at[_chip_id(chip)],
                                         send_sem=send_ref.at[3 * t + j], recv_sem=recv_ref.at[3 * t + j],
                                         device_id=(*chip, c), device_id_type=MESH).wait()

    return _pcall(body, name=name, in_specs=[HBM, SEM, SEM, ANY], out_shape=pltpu.HBM(slot.shape, slot.dtype),
                  out_specs=HBM, input_output_aliases={0: 0},
                  compiler_params=pltpu.CompilerParams(has_side_effects=DATAFLOW))(slot, send, recv, after)


def scatter_start(sums, name):
    n = len(sums)

    def body(*refs):
        ins, lands = refs[:n], refs[n:2 * n]
        send, recv = refs[2 * n], refs[2 * n + 1]
        token = refs[-1]
        x, y, c, others = _place()
        me = _chip_id((x, y))
        for t in range(n):
            for j, chip in enumerate(others):
                pltpu.make_async_remote_copy(src_ref=ins[t].at[_chip_id(chip)], dst_ref=lands[t].at[me],
                                             send_sem=send.at[3 * t + j], recv_sem=recv.at[3 * t + j],
                                             device_id=(*chip, c), device_id_type=MESH).start()
        token[...] = jnp.zeros_like(token)

    bufs = [pltpu.HBM(a.shape, a.dtype) for a in sums]
    sems = pltpu.SemaphoreType.DMA((3 * n,))
    res = _pcall(body, name=name, in_specs=[HBM] * (2 * n),
                 out_shape=(sems, sems, *bufs, *bufs, jax.ShapeDtypeStruct((8, LANES), F32)),
                 out_specs=(SEM, SEM, *[HBM] * (2 * n), pl.BlockSpec(memory_space=pltpu.VMEM)),
                 input_output_aliases={i: 2 + i for i in range(2 * n)},
                 compiler_params=pltpu.CompilerParams(has_side_effects=DATAFLOW))(
        *[_in_hbm(a) for a in sums], *[_in_hbm(lax.empty(a.shape, a.dtype)) for a in sums])
    return res[0], res[1], list(res[2:2 + n]), list(res[2 + n:2 + 2 * n]), res[-1]


def scatter_wait(send, recv, sums, lands, after, name):
    n = len(sums)

    def body(*refs):
        ins, bufs = refs[:n], refs[n:2 * n]
        send_ref, recv_ref = refs[2 * n], refs[2 * n + 1]
        x, y, c, others = _place()
        me = _chip_id((x, y))
        for t in range(n):
            for j, chip in enumerate(others):
                cp = pltpu.make_async_remote_copy(src_ref=ins[t].at[_chip_id(chip)], dst_ref=bufs[t].at[_chip_id(chip)],
                                                  send_sem=send_ref.at[3 * t + j], recv_sem=recv_ref.at[3 * t + j],
                                                  device_id=(*chip, c), device_id_type=MESH)
                cp.wait_send()
                cp.wait_recv()

    shapes = [pltpu.HBM(a.shape, a.dtype) for a in sums]
    res = _pcall(body, name=name, in_specs=[HBM] * (2 * n) + [SEM, SEM, ANY],
                 out_shape=(*shapes, *shapes), out_specs=tuple([HBM] * (2 * n)),
                 input_output_aliases={i: i for i in range(2 * n)},
                 compiler_params=pltpu.CompilerParams(has_side_effects=DATAFLOW))(*sums, *lands, send, recv, after)
    return list(res[:n]), list(res[n:])


def join_core_halves(fulls, name):
    n = len(fulls)

    def body(*refs):
        bufs = refs[n:2 * n]
        send, recv = refs[2 * n:]
        x, y, c, _ = _place()
        cps = []
        for t in range(n):
            hr = bufs[t].shape[1] // 2
            mine = bufs[t].at[:, pl.ds(c * hr, hr)]
            cp = pltpu.make_async_remote_copy(src_ref=mine, dst_ref=mine, send_sem=send.at[t], recv_sem=recv.at[t],
                                              device_id=(x, y, 1 - c), device_id_type=MESH)
            cp.start()
            cps.append(cp)
        for t in range(n):
            hr = bufs[t].shape[1] // 2
            theirs = bufs[t].at[:, pl.ds((1 - c) * hr, hr)]
            cps[t].wait_send()
            pltpu.make_async_remote_copy(src_ref=theirs, dst_ref=theirs, send_sem=send.at[t], recv_sem=recv.at[t],
                                         device_id=(x, y, c), device_id_type=MESH).wait_recv()

    outs = [jax.ShapeDtypeStruct(a.shape, a.dtype) for a in fulls]
    return _comm_call(body, name, n, outs, n, in_place=True)(*fulls)


def kernel(x, positions, norm_gains, swa_w_in, swa_sinks, swa_w_out, fox_w_in, fox_b_f, fox_w_out, ffn_w_gate_up, ffn_w_down, loss_target, m_norm_gains, m_swa_w_in, m_swa_sinks, m_swa_w_out, m_fox_w_in, m_fox_b_f, m_fox_w_out, m_ffn_w_gate_up, m_ffn_w_down, v_norm_gains, v_swa_w_in, v_swa_sinks, v_swa_w_out, v_fox_w_in, v_fox_b_f, v_fox_w_out, v_ffn_w_gate_up, v_ffn_w_down):
    s, d = x.shape[1], x.shape[2]
    depth = norm_gains.shape[0]
    n_heads = d // HEAD_DIM
    hd = n_heads * HEAD_DIM
    n_kv = (swa_w_in.shape[2] * N_CHIPS // HEAD_DIM - n_heads) // 2
    ff = ffn_w_down.shape[1] * N_CHIPS
    fox_cols = fox_w_in.shape[2]
    fox_pad = 3 * hd + LANES
    assert fox_cols * N_CHIPS == 3 * hd + n_heads and n_heads <= LANES
    x0 = x[0]
    target = loss_target[0]
    place = ((2 * lax.axis_index("x") + lax.axis_index("y")).astype(jnp.int32).reshape(1),
             lax.axis_index("c").astype(jnp.int32).reshape(1))

    order = [("norm_gains", norm_gains.reshape(1, depth * 4, norm_gains.shape[2]), 0, F32)]
    for layer in range(depth):
        j = layer // 2
        kind, w_i, w_o = ("swa", swa_w_in, swa_w_out) if layer % 2 == 0 else ("fox", fox_w_in, fox_w_out)
        order += [(f"{kind}_w_in_{j}", w_i, j, BF16), (f"{kind}_w_out_{j}", w_o, j, BF16),
                  (f"ffn_w_gate_up_{layer}", ffn_w_gate_up, layer, BF16), (f"ffn_w_down_{layer}", ffn_w_down, layer, BF16)]
    slots = [cast_into_slot(w, l, place, dt, "cast_" + nm) for nm, w, l, dt in order]
    g_send, g_recv, slots, g_token = gather_start(slots, "gather_start")
    slot_of = {entry[0]: t for t, entry in enumerate(order)}

    def weight(nm, after):
        t = slot_of[nm]
        return gather_wait(g_send, g_recv, slots[t], t, after, "gather_wait_" + nm)

    gains = jnp.transpose(weight("norm_gains", g_token), (1, 0, 2)).reshape(depth * 4, d)

    def gain(layer, which):
        return gains[layer * 4 + which][None, :]

    def fox_weight(w_in):
        parts = [w_in[q] for q in range(N_CHIPS)]
        parts.append(jnp.zeros((d, fox_pad - fox_cols * N_CHIPS), BF16))
        return jnp.concatenate(parts, axis=1)

    inv_freq = ROPE_THETA ** (-jnp.arange(0, ROT_DIM, 2, dtype=F32) / ROT_DIM)
    lane_d = jnp.arange(LANES) % HEAD_DIM
    invf_row = jnp.where(lane_d < ROT_DIM, inv_freq[lane_d % (ROT_DIM // 2)], 0.0)[None, :]
    tabs = rope_tables(positions.reshape(s, 1), invf_row, "rope_tables")

    n_sh_in = swa_w_in.shape[2]
    gu_sh = ffn_w_gate_up.shape[2]
    tn_gu = gu_sh // 2 if (gu_sh // 2) % LANES == 0 else gu_sh
    down_sh = ffn_w_down.shape[1]
    out_sh = swa_w_out.shape[1]
    tm = min(1024, s)

    saved = []
    xin = x0
    h = prenorm(xin, gain(0, 0), "prenorm_first")
    for layer in range(depth):
        j = layer // 2
        rec = {"x_in": xin, "h1": h}
        if layer % 2 == 0:
            w_in = weight(f"swa_w_in_{j}", h)
            proj = mm_nn(h, w_in, lambda n, k: (n, k, 0), n_sh_in * N_CHIPS, tm=tm, tn=n_sh_in, tk=d,
                         out_dtype=F32, name=f"swa_proj_{j}")
            q, kd, vd = swa_split(proj, tabs, n_heads, n_kv, f"swa_split_{j}")
            sink_row = jnp.repeat(swa_sinks[j], HEAD_DIM)[None, :]
            attn, lse = swa_fwd(q, kd, vd, sink_row, f"swa_fwd_{j}")
            rec.update(q=q, kd=kd, vd=vd, sink_row=sink_row, lse=lse, w_in=w_in)
            w_out = weight(f"swa_w_out_{j}", attn)
        else:
            wf = fox_weight(weight(f"fox_w_in_{j}", h))
            tn = 3 * hd // 6 if (3 * hd // 6) % LANES == 0 else LANES
            qkv = mm_nn(h, wf, lambda n, k: (k, n), 3 * hd, tm=tm, tn=tn, tk=d, out_dtype=BF16, name=f"fox_proj_{j}")
            f_off = 3 * hd // LANES
            f_logit = mm_nn(h, wf, lambda n, k: (k, f_off + n), LANES, tm=tm, tn=LANES, tk=d, out_dtype=F32, name=f"fox_gate_{j}")
            b_row = jnp.pad(fox_b_f[j], (0, LANES - n_heads))[None, :]
            c = forget_cumsum(f_logit, b_row, f"fox_cumsum_{j}")
            qs, ks, vs = fox_split(qkv, c, n_heads, f"fox_split_{j}")
            attn, qb = fox_fwd(qs, ks, vs, n_heads, f"fox_fwd_{j}")
            rec.update(wf=wf, f_logit=f_logit, b_row=b_row, qb=qb, ks=ks, vs=vs)
            w_out = weight(f"fox_w_out_{j}", attn)
        y = mm_nn(attn, w_out, lambda n, k: (k, 0, n), d, tm=tm, tn=d, tk=out_sh, out_dtype=F32, name=f"out_proj_{layer}")
        xmid, h2 = postnorm_residual(xin, y, gain(layer, 1), gain(layer, 2), f"postnorm_mixer_{layer}")
        npb = gu_sh // tn_gu
        w_gu = weight(f"ffn_w_gate_up_{layer}", h2)
        nbf = ff // tn_gu
        gate, up, act = ffn_up_swiglu(h2, w_gu, lambda n: (n // npb, 0, n % npb), lambda n: ((n + nbf) // npb, 0, (n + nbf) % npb),
                                      ff, tm=min(512, s), tn=tn_gu, name=f"ffn_up_{layer}")
        w_down = weight(f"ffn_w_down_{layer}", act)
        y2 = mm_nn(act, w_down, lambda n, k: (k, 0, n), d, tm=tm, tn=d, tk=down_sh, out_dtype=F32, name=f"ffn_down_{layer}")
        rec.update(attn=attn, y=y, x_mid=xmid, h2=h2, gate=gate, up=up, act=act, y2=y2, w_out=w_out, w_gu=w_gu, w_down=w_down)
        saved.append(rec)
        if layer + 1 < depth:
            xin, h = postnorm_residual(xmid, y2, gain(layer, 3), gain(layer + 1, 0), f"postnorm_ffn_{layer}")
    dx, loss_blk = postnorm_loss(xmid, y2, gain(depth - 1, 3), target, "loss")

    finals = {nm: lax.empty(w.shape, F32) for nm, w in (("swa_w_in", swa_w_in), ("swa_w_out", swa_w_out), ("fox_w_in", fox_w_in),
                                                        ("fox_w_out", fox_w_out), ("ffn_w_gate_up", ffn_w_gate_up),
                                                        ("ffn_w_down", ffn_w_down), ("norm_gains", order[0][1]))}
    in_flight = []

    def finish_reduce(after):
        send_, recv_, sums_, lands_, keys = in_flight.pop()
        sums_, got = scatter_wait(send_, recv_, sums_, lands_, after, "reduce_scatter_wait_" + keys[0][2])
        for own, rcv, (key, l, nm) in zip(sums_, got, keys):
            finals[key] = add_chips(own, rcv, finals[key], l, place, "reduce_add_chips_" + nm)

    def start_reduce(partials, keys):
        if in_flight:
            finish_reduce(partials[0])
        halves = swap_core_halves(partials, "reduce_swap_" + keys[0][2])
        sums_ = [add_core_halves(g, r, place, "reduce_add_cores_" + k[2]) for g, r, k in zip(partials, halves, keys)]
        send_, recv_, sums_, lands_, token = scatter_start(sums_, "reduce_scatter_start_" + keys[0][2])
        in_flight.append((send_, recv_, sums_, lands_, keys))
        return token

    dgains = [None] * (depth * 4)
    dsinks = [None] * ((depth + 1) // 2)
    dbf = [None] * (depth // 2)
    tko = d
    for layer in reversed(range(depth)):
        j = layer // 2
        rec = saved[layer]
        dy2, dgains[layer * 4 + 3] = postnorm_bwd(dx, rec["y2"], gain(layer, 3), f"postnorm_ffn_bwd_{layer}")
        tnd = d
        g_down = mm_tn(rec["act"], dy2, lambda i, n: (i, 0, n), tka=down_sh, tn=tnd, tm=tm, out_block=(None, down_sh, tnd),
                       name=f"ffn_down_dw_{layer}", out_shape=jax.ShapeDtypeStruct((N_CHIPS, down_sh, d), BF16))
        dgu = ffn_down_dx_swiglu(dy2, rec["w_down"], lambda n: (n, 0, 0), rec["gate"], rec["up"], tm=min(512, s), tn=down_sh,
                                 name=f"ffn_down_dx_{layer}")
        npb = gu_sh // tn_gu
        g_gu = mm_tn(rec["h2"], dgu, lambda i, n: (n // npb, i, n % npb), tka=tko, tn=tn_gu, tm=tm, out_block=(None, tko, tn_gu),
                     name=f"ffn_up_dw_{layer}", out_shape=jax.ShapeDtypeStruct((N_CHIPS, d, gu_sh), BF16))
        sent = start_reduce([g_gu, g_down], [("ffn_w_gate_up", layer, f"ffn_w_gate_up_{layer}"), ("ffn_w_down", layer, f"ffn_w_down_{layer}")])
        dh2 = mm_nt(dgu, rec["w_gu"], lambda o, n: (n // npb, o, n % npb), d, tm=tm, tko=tko, tn=tn_gu, out_dtype=F32,
                    name=f"ffn_up_dx_{layer}", after=sent)
        dxm, dgains[layer * 4 + 2] = prenorm_bwd(rec["x_mid"], dh2, dx, gain(layer, 2), f"prenorm_ffn_bwd_{layer}")
        dy, dgains[layer * 4 + 1] = postnorm_bwd(dxm, rec["y"], gain(layer, 1), f"postnorm_mixer_bwd_{layer}")
        g_out = mm_tn(rec["attn"], dy, lambda i, n: (i, 0, n), tka=out_sh, tn=tnd, tm=min(2048, s), out_block=(None, out_sh, tnd),
                      name=f"out_proj_dw_{layer}", out_shape=jax.ShapeDtypeStruct((N_CHIPS, out_sh, d), BF16))
        dattn = mm_nt(dy, rec["w_out"], lambda o, n: (o, 0, n), hd, tm=tm, tko=out_sh, tn=d, out_dtype=BF16, name=f"out_proj_dx_{layer}")
        if layer % 2 == 0:
            dq, dkd, dvd, dsk = swa_bwd(rec["q"], rec["kd"], rec["vd"], rec["sink_row"], rec["attn"], rec["lse"], dattn, f"swa_bwd_{j}")
            dsinks[j] = dsk[0].reshape(n_heads, HEAD_DIM)[:, 0]
            dproj = swa_merge_bwd(dq, dkd, dvd, tabs, f"swa_merge_bwd_{j}")
            g_in = mm_tn(rec["h1"], dproj, lambda i, n: (n, i, 0), tka=tko, tn=n_sh_in, tm=tm, out_block=(None, tko, n_sh_in),
                         name=f"swa_proj_dw_{j}", out_shape=jax.ShapeDtypeStruct((N_CHIPS, d, n_sh_in), BF16))
            sent = start_reduce([g_in, g_out], [("swa_w_in", j, f"swa_w_in_{j}"), ("swa_w_out", j, f"swa_w_out_{j}")])
            dh1 = mm_nt(dproj, rec["w_in"], lambda o, n: (n, o, 0), d, tm=tm, tko=tko, tn=n_sh_in, out_dtype=F32,
                        name=f"swa_proj_dx_{j}", after=sent)
        else:
            dos = fox_dout_slots(dattn, rec["attn"], f"fox_dout_slots_{j}")
            dqs, dks, dvs = fox_bwd(rec["qb"], rec["ks"], rec["vs"], dos, n_heads, f"fox_bwd_{j}")
            dqkv, dc = fox_merge_bwd(dqs, dks, dvs, n_heads, f"fox_merge_bwd_{j}")
            df, db = forget_gate_bwd(dc, rec["f_logit"], rec["b_row"], f"fox_gate_bwd_{j}")
            dbf[j] = db[0, :n_heads]
            dproj = jnp.concatenate([dqkv, df], axis=1)
            tn_f = LANES * max(k for k in range(1, 9) if (fox_pad // LANES) % k == 0)
            dwf = mm_tn(rec["h1"], dproj, lambda i, n: (i, n), tka=tko, tn=tn_f, tm=tm, out_block=(tko, tn_f),
                        name=f"fox_proj_dw_{j}", out_shape=jax.ShapeDtypeStruct((d, fox_pad), BF16))
            g_in = jnp.stack([dwf[:, q * fox_cols:(q + 1) * fox_cols] for q in range(N_CHIPS)])
            sent = start_reduce([g_in, g_out], [("fox_w_in", j, f"fox_w_in_{j}"), ("fox_w_out", j, f"fox_w_out_{j}")])
            dh1 = mm_nt(dproj, rec["wf"], lambda o, n: (o, n), d, tm=tm, tko=tko, tn=tn_f, out_dtype=F32,
                        name=f"fox_proj_dx_{j}", after=sent)
        dx, dgains[layer * 4] = prenorm_bwd(rec["x_in"], dh1, dxm, gain(layer, 0), f"prenorm_mixer_bwd_{layer}")
    grad_x = dx[None]

    dgain_full = jnp.concatenate(dgains, axis=0)
    start_reduce([jnp.transpose(dgain_full.reshape(depth * 4, N_CHIPS, -1), (1, 0, 2))], [("norm_gains", 0, "norm_gains")])
    finish_reduce(dx)
    keys = ["swa_w_in", "swa_w_out", "fox_w_in", "fox_w_out", "ffn_w_gate_up", "ffn_w_down", "norm_gains"]
    full = join_core_halves([finals[k] for k in keys], "reduce_join_cores")
    g_swa_in_f, g_swa_out_f, g_fox_in_f, g_fox_out_f, g_gu_f, g_down_f, g_gains_f = full
    g_gains_f = g_gains_f.reshape(norm_gains.shape)

    n_swa, n_fox = len(dsinks), len(dbf)
    small = jnp.concatenate([loss_blk[0, :1]] + dsinks + dbf)
    small = lax.psum(small, ("x", "y", "c"))
    loss = small[0]
    g_sinks = small[1:1 + n_swa * n_heads].reshape(n_swa, n_heads)
    g_bf = small[1 + n_swa * n_heads:].reshape(n_fox, n_heads)

    def pad_small(a):
        return jnp.pad(a, ((0, 8 - a.shape[0]), (0, LANES - a.shape[1])))[None]

    def update(w, g, m, v, nm):
        if w.ndim == 2:
            dl, mn, vn = adamw(pad_small(w), pad_small(g), pad_small(m), pad_small(v), "adamw_" + nm)
            return tuple(a[0, :w.shape[0], :w.shape[1]] for a in (dl, mn, vn))
        return adamw(w, g, m, v, "adamw_" + nm)

    grads = [g_gains_f, g_swa_in_f, g_sinks, g_swa_out_f, g_fox_in_f, g_bf, g_fox_out_f, g_gu_f, g_down_f]
    ws = [norm_gains, swa_w_in, swa_sinks, swa_w_out, fox_w_in, fox_b_f, fox_w_out, ffn_w_gate_up, ffn_w_down]
    ms = [m_norm_gains, m_swa_w_in, m_swa_sinks, m_swa_w_out, m_fox_w_in, m_fox_b_f, m_fox_w_out, m_ffn_w_gate_up, m_ffn_w_down]
    vs = [v_norm_gains, v_swa_w_in, v_swa_sinks, v_swa_w_out, v_fox_w_in, v_fox_b_f, v_fox_w_out, v_ffn_w_gate_up, v_ffn_w_down]
    nms = ["norm_gains", "swa_w_in", "swa_sinks", "swa_w_out", "fox_w_in", "fox_b_f", "fox_w_out", "ffn_w_gate_up", "ffn_w_down"]
    upd = [update(w, g, m, v, nm) for w, g, m, v, nm in zip(ws, grads, ms, vs, nms)]
    return (loss, grad_x, *grads, *[u[0] for u in upd], *[u[1] for u in upd], *[u[2] for u in upd])
```

```python
import jax
import jax.numpy as jnp
from jax import lax
from jax.experimental import pallas as pl
from jax.experimental.pallas import tpu as pltpu

F32 = jnp.float32
BF16 = jnp.bfloat16
HEAD_DIM = 64
LANES = 128
WINDOW = 128
ROPE_THETA = 500000.0
ROT_DIM = HEAD_DIM // 4
RMS_EPS = 1e-6
ADAM_LR, ADAM_B1, ADAM_B2, ADAM_EPS, ADAM_WD, ADAM_STEP = 0.001, 0.9, 0.999, 1e-08, 0.01, 10
NEG = -1e30
VMEM_BIG = 56 * 1024 * 1024
N_CHIPS = 4
MESH = pl.DeviceIdType.MESH
ANY = pl.BlockSpec(memory_space=pl.ANY)


def _pcall(body, **kw):
    return pl.pallas_call(body, **kw)


def _cp(sem=None, vmem=None):
    return pltpu.CompilerParams(dimension_semantics=sem, vmem_limit_bytes=vmem)


def _tile(n, pref):
    if n <= pref:
        return n
    t = pref - pref % 16
    while n % t:
        t -= 16
    return t


_DIMS = {"nn": (((1,), (0,)), ((), ())), "nt": (((1,), (1,)), ((), ())), "tn": (((0,), (0,)), ((), ()))}


def _matmul(kind, a, b, *, grid, a_spec, b_spec, o_spec, out_shape, acc_shape, name, after=None):
    nk = grid[2]
    dims = _DIMS[kind]

    def body(a_ref, b_ref, *rest):
        rest = rest[1:] if after is not None else rest
        o_ref = rest[0]

        def prod():
            return lax.dot_general(a_ref[...], b_ref[...], dims, preferred_element_type=F32)

        if nk == 1:
            o_ref[...] = prod().astype(o_ref.dtype)
        else:
            acc = rest[1]
            k = pl.program_id(2)

            @pl.when(k == 0)
            def _():
                acc[...] = prod()

            @pl.when(k > 0)
            def _():
                acc[...] += prod()

            @pl.when(k == nk - 1)
            def _():
                o_ref[...] = acc[...].astype(o_ref.dtype)

    args, in_specs = [a, b], [a_spec, b_spec]
    if after is not None:
        args.append(after)
        in_specs.append(ANY)
    return _pcall(body, name=name, grid=grid, in_specs=in_specs, out_specs=o_spec, out_shape=out_shape,
                  scratch_shapes=[] if nk == 1 else [pltpu.VMEM(acc_shape, F32)],
                  compiler_params=_cp(("parallel", "parallel", "arbitrary"), VMEM_BIG))(*args)


def mm_nn(a, w, w_map, n_out, *, tm, tn, tk, out_dtype, name):
    m, kdim = a.shape
    tm = min(tm, m)
    lead = (None,) * (w.ndim - 2)
    return _matmul("nn", a, w, grid=(m // tm, n_out // tn, kdim // tk),
                   a_spec=pl.BlockSpec((tm, tk), lambda i, j, k: (i, k)),
                   b_spec=pl.BlockSpec(lead + (tk, tn), lambda i, j, k: w_map(j, k)),
                   o_spec=pl.BlockSpec((tm, tn), lambda i, j, k: (i, j)),
                   out_shape=jax.ShapeDtypeStruct((m, n_out), out_dtype), acc_shape=(tm, tn), name=name)


def _split_cols(a, tn):
    if a.ndim == 2:
        return a.shape[0], a.shape[1], (), lambda row, col: (row, col)
    per = a.shape[2] // tn
    return a.shape[1], a.shape[0] * a.shape[2], (None,), lambda row, col: (col // per, row, col % per)


def mm_nt(a, w, w_map, k_out, *, tm, tko, tn, out_dtype, name, after=None):
    m, ndim, a_lead, a_map = _split_cols(a, tn)
    tm = min(tm, m)
    lead = (None,) * (w.ndim - 2)
    return _matmul("nt", a, w, grid=(m // tm, k_out // tko, ndim // tn),
                   a_spec=pl.BlockSpec(a_lead + (tm, tn), lambda i, j, n: a_map(i, n)),
                   b_spec=pl.BlockSpec(lead + (tko, tn), lambda i, j, n: w_map(j, n)),
                   o_spec=pl.BlockSpec((tm, tko), lambda i, j, n: (i, j)),
                   out_shape=jax.ShapeDtypeStruct((m, k_out), out_dtype), acc_shape=(tm, tko), name=name, after=after)


def mm_tn(a, b, o_map, *, tka, tn, tm, out_block, name, out_shape):
    m, kdim = a.shape
    _, n, b_lead, b_map = _split_cols(b, tn)
    tm = min(tm, m)
    return _matmul("tn", a, b, grid=(kdim // tka, n // tn, m // tm),
                   a_spec=pl.BlockSpec((tm, tka), lambda i, j, mm: (mm, i)),
                   b_spec=pl.BlockSpec(b_lead + (tm, tn), lambda i, j, mm: b_map(mm, j)),
                   o_spec=pl.BlockSpec(out_block, lambda i, j, mm: o_map(i, j)),
                   out_shape=out_shape, acc_shape=(tka, tn), name=name)


def _rstd(v):
    return lax.rsqrt(jnp.mean(v * v, axis=-1, keepdims=True) + RMS_EPS)


def _row_spec(tr, d):
    return pl.BlockSpec((tr, d), lambda i: (i, 0))


def _vec_spec(d):
    return pl.BlockSpec((1, d), lambda i: (0, 0))


def prenorm(x, g, name):
    s, d = x.shape
    tr = _tile(s, 256)

    def body(x_ref, g_ref, h_ref):
        v = x_ref[...]
        h_ref[...] = (v * _rstd(v) * g_ref[...]).astype(BF16)

    return _pcall(body, name=name, grid=(s // tr,), in_specs=[_row_spec(tr, d), _vec_spec(d)],
                  out_specs=_row_spec(tr, d), out_shape=jax.ShapeDtypeStruct((s, d), BF16),
                  compiler_params=_cp(("parallel",)))(x, g)


def postnorm_residual(x, y, g_post, g_next, name):
    s, d = x.shape
    tr = _tile(s, 256)

    def body(x_ref, y_ref, gp_ref, gn_ref, xo_ref, h_ref):
        v = y_ref[...]
        xn = x_ref[...] + v * _rstd(v) * gp_ref[...]
        xo_ref[...] = xn
        h_ref[...] = (xn * _rstd(xn) * gn_ref[...]).astype(BF16)

    return _pcall(body, name=name, grid=(s // tr,),
                  in_specs=[_row_spec(tr, d), _row_spec(tr, d), _vec_spec(d), _vec_spec(d)],
                  out_specs=[_row_spec(tr, d), _row_spec(tr, d)],
                  out_shape=[jax.ShapeDtypeStruct((s, d), F32), jax.ShapeDtypeStruct((s, d), BF16)],
                  compiler_params=_cp(("parallel",)))(x, y, g_post, g_next)


def postnorm_loss(x, y, g_post, target, name):
    s, d = x.shape
    tr = _tile(s, 256)

    def body(x_ref, y_ref, gp_ref, t_ref, dx_ref, loss_ref):
        v = y_ref[...]
        err = x_ref[...] + v * _rstd(v) * gp_ref[...] - t_ref[...]
        dx_ref[...] = err / d
        part = 0.5 * jnp.sum(jnp.mean(err * err, axis=-1, keepdims=True), axis=0, keepdims=True)

        @pl.when(pl.program_id(0) == 0)
        def _():
            loss_ref[...] = jnp.zeros_like(loss_ref)

        loss_ref[...] += part

    return _pcall(body, name=name, grid=(s // tr,),
                  in_specs=[_row_spec(tr, d), _row_spec(tr, d), _vec_spec(d), _row_spec(tr, d)],
                  out_specs=[_row_spec(tr, d), pl.BlockSpec((8, LANES), lambda i: (0, 0))],
                  out_shape=[jax.ShapeDtypeStruct((s, d), F32), jax.ShapeDtypeStruct((8, LANES), F32)],
                  compiler_params=_cp(("arbitrary",)))(x, y, g_post, target)


def _norm_bwd(v, g, dz):
    r = _rstd(v)
    vhat = v * r
    u = dz * g
    dv = r * (u - vhat * jnp.mean(u * vhat, axis=-1, keepdims=True))
    return dv, jnp.sum(dz * vhat, axis=0, keepdims=True)


def _acc_rows(ref, val):
    @pl.when(pl.program_id(0) == 0)
    def _():
        ref[...] = jnp.zeros_like(ref)

    ref[...] += val


def postnorm_bwd(dz, y, g, name):
    s, d = y.shape
    tr = _tile(s, 256)

    def body(dz_ref, y_ref, g_ref, dy_ref, dg_ref):
        dv, dg = _norm_bwd(y_ref[...], g_ref[...], dz_ref[...])
        dy_ref[...] = dv.astype(BF16)
        _acc_rows(dg_ref, dg)

    return _pcall(body, name=name, grid=(s // tr,), in_specs=[_row_spec(tr, d), _row_spec(tr, d), _vec_spec(d)],
                  out_specs=[_row_spec(tr, d), _vec_spec(d)],
                  out_shape=[jax.ShapeDtypeStruct((s, d), BF16), jax.ShapeDtypeStruct((1, d), F32)],
                  compiler_params=_cp(("arbitrary",)))(dz, y, g)


def prenorm_bwd(x, dh, dskip, g, name):
    s, d = x.shape
    tr = _tile(s, 256)

    def body(x_ref, dh_ref, ds_ref, g_ref, dx_ref, dg_ref):
        dv, dg = _norm_bwd(x_ref[...], g_ref[...], dh_ref[...])
        dx_ref[...] = ds_ref[...] + dv
        _acc_rows(dg_ref, dg)

    return _pcall(body, name=name, grid=(s // tr,),
                  in_specs=[_row_spec(tr, d), _row_spec(tr, d), _row_spec(tr, d), _vec_spec(d)],
                  out_specs=[_row_spec(tr, d), _vec_spec(d)],
                  out_shape=[jax.ShapeDtypeStruct((s, d), F32), jax.ShapeDtypeStruct((1, d), F32)],
                  compiler_params=_cp(("arbitrary",)))(x, dh, dskip, g)


def ffn_up_swiglu(h, w, gate_map, up_map, ff, *, tm, tn, name):
    m, kdim = h.shape
    tm = min(tm, m)
    lead = (None,) * (w.ndim - 2)

    def body(h_ref, wg_ref, wu_ref, g_ref, u_ref, act_ref):
        a = h_ref[...]
        gate = jnp.dot(a, wg_ref[...], preferred_element_type=F32)
        up = jnp.dot(a, wu_ref[...], preferred_element_type=F32)
        g_ref[...] = gate.astype(BF16)
        u_ref[...] = up.astype(BF16)
        act_ref[...] = (gate * jax.nn.sigmoid(gate) * up).astype(BF16)

    out = jax.ShapeDtypeStruct((m, ff), BF16)
    oblk = pl.BlockSpec((tm, tn), lambda n, i: (i, n))
    return _pcall(body, name=name, grid=(ff // tn, m // tm),
                  in_specs=[pl.BlockSpec((tm, kdim), lambda n, i: (i, 0)), pl.BlockSpec(lead + (kdim, tn), lambda n, i: gate_map(n)),
                            pl.BlockSpec(lead + (kdim, tn), lambda n, i: up_map(n))],
                  out_specs=[oblk] * 3, out_shape=[out] * 3,
                  compiler_params=_cp(("parallel", "parallel"), VMEM_BIG))(h, w, w)


def ffn_down_dx_swiglu(dy, w, w_map, gate, up, *, tm, tn, name):
    m, d = dy.shape
    ff = gate.shape[1]
    tm = min(tm, m)
    lead = (None,) * (w.ndim - 2)

    def body(dy_ref, w_ref, g_ref, u_ref, o_ref):
        da = _nt(dy_ref[...], w_ref[...])
        gt = g_ref[...].astype(F32)
        sig = jax.nn.sigmoid(gt)
        o_ref[0] = (da * u_ref[...].astype(F32) * sig * (1.0 + gt * (1.0 - sig))).astype(BF16)
        o_ref[1] = (da * gt * sig).astype(BF16)

    blk = pl.BlockSpec((tm, tn), lambda n, i: (i, n))
    return _pcall(body, name=name, grid=(ff // tn, m // tm),
                  in_specs=[pl.BlockSpec((tm, d), lambda n, i: (i, 0)), pl.BlockSpec(lead + (tn, d), lambda n, i: w_map(n)), blk, blk],
                  out_specs=pl.BlockSpec((2, tm, tn), lambda n, i: (0, i, n)), out_shape=jax.ShapeDtypeStruct((2, m, ff), BF16),
                  compiler_params=_cp(("parallel", "parallel"), VMEM_BIG))(dy, w, gate, up)


def rope_tables(pos_col, invf_row, name):
    s = pos_col.shape[0]
    tr = _tile(s, 1024)

    def body(p_ref, f_ref, c_ref, sa_ref, sb_ref):
        ang = p_ref[...].astype(F32) * f_ref[...]
        d = lax.broadcasted_iota(jnp.int32, (1, LANES), 1) % HEAD_DIM
        cs, sn = jnp.cos(ang), jnp.sin(ang)
        c_ref[...] = jnp.where(d < ROT_DIM, cs, 1.0)
        sa_ref[...] = jnp.where(d < ROT_DIM // 2, -sn, 0.0)
        sb_ref[...] = jnp.where((d >= ROT_DIM // 2) & (d < ROT_DIM), sn, 0.0)

    tab = jax.ShapeDtypeStruct((s, LANES), F32)
    return _pcall(body, name=name, grid=(s // tr,),
                  in_specs=[pl.BlockSpec((tr, 1), lambda i: (i, 0)), _vec_spec(LANES)],
                  out_specs=[_row_spec(tr, LANES)] * 3, out_shape=[tab] * 3, compiler_params=_cp(("parallel",)))(pos_col, invf_row)


def _rot(t, c, sa, sb):
    half = ROT_DIM // 2
    return t * c + pltpu.roll(t, LANES - half, 1) * sa + pltpu.roll(t, half, 1) * sb


def _rot_t(t, c, sa, sb):
    half = ROT_DIM // 2
    return t * c + pltpu.roll(t * sa, half, 1) + pltpu.roll(t * sb, LANES - half, 1)


def swa_split(proj, tabs, n_heads, n_kv, name):
    s, width = proj.shape
    qd, kd = n_heads * HEAD_DIM, n_kv * HEAD_DIM
    tr = _tile(s, 256)

    def body(p_ref, c_ref, sa_ref, sb_ref, q_ref, k_ref, v_ref):
        c, sa, sb = c_ref[...], sa_ref[...], sb_ref[...]
        low = lax.broadcasted_iota(jnp.int32, (1, LANES), 1) < HEAD_DIM
        for g in range(qd // LANES):
            q_ref[:, g * LANES:(g + 1) * LANES] = _rot(p_ref[:, g * LANES:(g + 1) * LANES], c, sa, sb).astype(BF16)
        for g in range(kd // LANES):
            for src, dst, rot in ((qd, k_ref, True), (qd + kd, v_ref, False)):
                t = p_ref[:, src + g * LANES:src + (g + 1) * LANES]
                t = _rot(t, c, sa, sb) if rot else t
                sw = pltpu.roll(t, HEAD_DIM, 1)
                dst[:, (2 * g) * LANES:(2 * g + 1) * LANES] = jnp.where(low, t, sw).astype(BF16)
                dst[:, (2 * g + 1) * LANES:(2 * g + 2) * LANES] = jnp.where(low, sw, t).astype(BF16)

    return _pcall(body, name=name, grid=(s // tr,),
                  in_specs=[_row_spec(tr, width)] + [_row_spec(tr, LANES)] * 3,
                  out_specs=[_row_spec(tr, qd), _row_spec(tr, 2 * kd), _row_spec(tr, 2 * kd)],
                  out_shape=[jax.ShapeDtypeStruct((s, qd), BF16), jax.ShapeDtypeStruct((s, 2 * kd), BF16),
                             jax.ShapeDtypeStruct((s, 2 * kd), BF16)],
                  compiler_params=_cp(("parallel",)))(proj, *tabs)


def swa_merge_bwd(dq, dkd, dvd, tabs, name):
    s, qd = dq.shape
    kd = dkd.shape[1] // 2
    width = qd + 2 * kd
    tr = _tile(s, 256)

    def body(dq_ref, dk_ref, dv_ref, c_ref, sa_ref, sb_ref, o_ref):
        c, sa, sb = c_ref[...], sa_ref[...], sb_ref[...]
        low = lax.broadcasted_iota(jnp.int32, (1, LANES), 1) < HEAD_DIM
        for g in range(qd // LANES):
            t = dq_ref[:, g * LANES:(g + 1) * LANES].astype(F32)
            o_ref[:, g * LANES:(g + 1) * LANES] = _rot_t(t, c, sa, sb).astype(BF16)
        for g in range(kd // LANES):
            for dst, src, rot in ((qd, dk_ref, True), (qd + kd, dv_ref, False)):
                e = src[:, (2 * g) * LANES:(2 * g + 1) * LANES]
                o = src[:, (2 * g + 1) * LANES:(2 * g + 2) * LANES]
                t = jnp.where(low, e + pltpu.roll(e, HEAD_DIM, 1), o + pltpu.roll(o, HEAD_DIM, 1))
                t = _rot_t(t, c, sa, sb) if rot else t
                o_ref[:, dst + g * LANES:dst + (g + 1) * LANES] = t.astype(BF16)

    return _pcall(body, name=name, grid=(s // tr,),
                  in_specs=[_row_spec(tr, qd), _row_spec(tr, 2 * kd), _row_spec(tr, 2 * kd)] + [_row_spec(tr, LANES)] * 3,
                  out_specs=_row_spec(tr, width), out_shape=jax.ShapeDtypeStruct((s, width), BF16),
                  compiler_params=_cp(("parallel",)))(dq, dkd, dvd, *tabs)


def _halves():
    lane_half = lax.broadcasted_iota(jnp.int32, (1, LANES), 1) // HEAD_DIM
    return [lane_half == 0, lane_half == 1]


def _nt(a, b):
    return lax.dot_general(a, b, _DIMS["nt"], preferred_element_type=F32)


def _tn(a, b):
    return lax.dot_general(a, b, _DIMS["tn"], preferred_element_type=F32)


def _band_mask(i, tq):
    w = tq + WINDOW
    rel = lax.broadcasted_iota(jnp.int32, (tq, w), 1) - lax.broadcasted_iota(jnp.int32, (tq, w), 0)
    first = lax.broadcasted_iota(jnp.int32, (tq, w), 1) >= jnp.where(i > 0, 0, WINDOW)
    return (rel >= 1) & (rel <= WINDOW) & first


def swa_fwd(q, kd, vd, sink_row, name):
    s, qd = q.shape
    tq = min(256, s)
    r = tq // WINDOW
    pairs = qd // LANES
    group_pairs = pairs // (kd.shape[1] // LANES)
    scale = HEAD_DIM ** -0.5

    def body(q_ref, kp_ref, kc_ref, vp_ref, vc_ref, sk_ref, o_ref, lse_ref):
        i = pl.program_id(1)
        k = jnp.concatenate([kp_ref[...], kc_ref[...]], axis=0)
        v = jnp.concatenate([vp_ref[...], vc_ref[...]], axis=0)
        mask = _band_mask(i, tq)
        q2 = q_ref[...]
        outs, lses = [], []
        for a, hm in enumerate(_halves()):
            sc = _nt(jnp.where(hm, q2, jnp.zeros_like(q2)), k) * scale
            sc = jnp.where(mask, sc, NEG)
            sink = sk_ref[:, a * HEAD_DIM:a * HEAD_DIM + 1]
            m = jnp.maximum(jnp.max(sc, axis=-1, keepdims=True), sink)
            p = jnp.exp(sc - m)
            den = jnp.sum(p, axis=-1, keepdims=True) + jnp.exp(sink - m)
            outs.append(jnp.dot(p.astype(BF16), v, preferred_element_type=F32) / den)
            lses.append(m + jnp.log(den))
        hm0 = _halves()[0]
        o_ref[...] = jnp.where(hm0, outs[0], outs[1]).astype(BF16)
        lse_ref[...] = jnp.where(hm0, lses[0], lses[1])

    prev = lambda p, i: (jnp.maximum(i * r - 1, 0), p // group_pairs)
    cur = lambda p, i: (i, p // group_pairs)
    blk = pl.BlockSpec((tq, LANES), lambda p, i: (i, p))
    return _pcall(body, name=name, grid=(pairs, s // tq),
                  in_specs=[blk, pl.BlockSpec((WINDOW, LANES), prev), pl.BlockSpec((tq, LANES), cur),
                            pl.BlockSpec((WINDOW, LANES), prev), pl.BlockSpec((tq, LANES), cur),
                            pl.BlockSpec((1, LANES), lambda p, i: (0, p))],
                  out_specs=[blk, blk],
                  out_shape=[jax.ShapeDtypeStruct((s, qd), BF16), jax.ShapeDtypeStruct((s, qd), F32)],
                  compiler_params=_cp(("parallel", "parallel")))(q, kd, kd, vd, vd, sink_row)


def swa_bwd(q, kd, vd, sink_row, out, lse, dout, name):
    s, qd = q.shape
    tq = min(256, s)
    r = tq // WINDOW
    n_kv = kd.shape[1] // LANES
    gw = qd // n_kv
    scale = HEAD_DIM ** -0.5

    def body(q_ref, kp_ref, kc_ref, vp_ref, vc_ref, sk_ref, o_ref, lse_ref, do_ref, dq_ref, dk_ref, dv_ref, dsk_ref):
        i = pl.program_id(1)

        @pl.when(i == 0)
        def _():
            dk_ref[...] = jnp.zeros_like(dk_ref)
            dv_ref[...] = jnp.zeros_like(dv_ref)
            dsk_ref[...] = jnp.zeros_like(dsk_ref)

        k = jnp.concatenate([kp_ref[...], kc_ref[...]], axis=0)
        v = jnp.concatenate([vp_ref[...], vc_ref[...]], axis=0)
        mask = _band_mask(i, tq)
        dk = jnp.zeros((tq + WINDOW, LANES), F32)
        dv = jnp.zeros((tq + WINDOW, LANES), F32)
        for pp in range(gw // LANES):
            cols = slice(pp * LANES, (pp + 1) * LANES)
            q2, do2 = q_ref[:, cols], do_ref[:, cols]
            prod = do2.astype(F32) * o_ref[:, cols].astype(F32)
            dq2 = jnp.zeros((tq, LANES), F32)
            dsk = jnp.zeros((1, LANES), F32)
            for a, hm in enumerate(_halves()):
                qa = jnp.where(hm, q2, jnp.zeros_like(q2))
                doa = jnp.where(hm, do2, jnp.zeros_like(do2))
                lse_a = lse_ref[:, pp * LANES + a * HEAD_DIM:pp * LANES + a * HEAD_DIM + 1]
                sc = jnp.where(mask, _nt(qa, k) * scale, NEG)
                p = jnp.exp(sc - lse_a)
                delta = jnp.sum(jnp.where(hm, prod, 0.0), axis=-1, keepdims=True)
                ds = (p * (_nt(doa, v) - delta) * scale).astype(BF16)
                dv = dv + _tn(p.astype(BF16), doa)
                dk = dk + _tn(ds, qa)
                dq2 = dq2 + jnp.where(hm, jnp.dot(ds, k, preferred_element_type=F32), 0.0)
                sink = sk_ref[:, pp * LANES + a * HEAD_DIM:pp * LANES + a * HEAD_DIM + 1]
                dsink = -jnp.sum(jnp.exp(sink - lse_a) * delta, axis=0, keepdims=True)
                dsk = dsk + jnp.where(hm, dsink, 0.0)
            dq_ref[:, cols] = dq2.astype(BF16)
            dsk_ref[0:1, cols] += dsk
        start = pl.multiple_of(i * tq, tq)
        dk_ref[pl.ds(start, tq), :] += dk[WINDOW:, :]
        dv_ref[pl.ds(start, tq), :] += dv[WINDOW:, :]

        @pl.when(i > 0)
        def _():
            before = pl.multiple_of(i * tq - WINDOW, WINDOW)
            dk_ref[pl.ds(before, WINDOW), :] += dk[:WINDOW, :]
            dv_ref[pl.ds(before, WINDOW), :] += dv[:WINDOW, :]

    prev = lambda g, i: (jnp.maximum(i * r - 1, 0), g)
    cur = lambda g, i: (i, g)
    wide = pl.BlockSpec((tq, gw), cur)
    full = pl.BlockSpec((s, LANES), lambda g, i: (0, g))
    return _pcall(body, name=name, grid=(n_kv, s // tq),
                  in_specs=[wide, pl.BlockSpec((WINDOW, LANES), prev), pl.BlockSpec((tq, LANES), cur),
                            pl.BlockSpec((WINDOW, LANES), prev), pl.BlockSpec((tq, LANES), cur),
                            pl.BlockSpec((1, gw), lambda g, i: (0, g)), wide, wide, wide],
                  out_specs=[wide, full, full, pl.BlockSpec((8, gw), lambda g, i: (0, g))],
                  out_shape=[jax.ShapeDtypeStruct((s, qd), BF16), jax.ShapeDtypeStruct(kd.shape, F32),
                             jax.ShapeDtypeStruct(kd.shape, F32), jax.ShapeDtypeStruct((8, qd), F32)],
                  compiler_params=_cp(("parallel", "arbitrary"), VMEM_BIG))(q, kd, kd, vd, vd, sink_row, out, lse, dout)


def forget_cumsum(f_logit, b_row, name):
    s = f_logit.shape[0]

    def body(f_ref, b_ref, c_ref):
        z = f_ref[...] + b_ref[...]
        acc = jnp.minimum(z, 0.0) - jnp.log(1.0 + jnp.exp(-jnp.abs(z)))
        row = lax.broadcasted_iota(jnp.int32, (s, LANES), 0)
        d = 1
        while d < s:
            acc = acc + jnp.where(row >= d, pltpu.roll(acc, d, 0), 0.0)
            d *= 2
        c_ref[...] = acc

    return _pcall(body, name=name, in_specs=[pl.BlockSpec((s, LANES), lambda: (0, 0)), pl.BlockSpec((1, LANES), lambda: (0, 0))],
                  out_specs=pl.BlockSpec((s, LANES), lambda: (0, 0)), out_shape=jax.ShapeDtypeStruct((s, LANES), F32),
                  compiler_params=_cp(None, VMEM_BIG))(f_logit, b_row)


def forget_gate_bwd(dc, f_logit, b_row, name):
    s = dc.shape[0]

    def body(dc_ref, f_ref, b_ref, df_ref, db_ref):
        acc = dc_ref[...]
        row = lax.broadcasted_iota(jnp.int32, (s, LANES), 0)
        d = 1
        while d < s:
            acc = acc + jnp.where(row < s - d, pltpu.roll(acc, s - d, 0), 0.0)
            d *= 2
        df = acc * jax.nn.sigmoid(-(f_ref[...] + b_ref[...]))
        df_ref[...] = df.astype(BF16)
        db_ref[...] = jnp.sum(df, axis=0, keepdims=True)

    whole = pl.BlockSpec((s, LANES), lambda: (0, 0))
    vec = pl.BlockSpec((1, LANES), lambda: (0, 0))
    return _pcall(body, name=name, in_specs=[whole, whole, vec], out_specs=[whole, vec],
                  out_shape=[jax.ShapeDtypeStruct((s, LANES), BF16), jax.ShapeDtypeStruct((1, LANES), F32)],
                  compiler_params=_cp(None, VMEM_BIG))(dc, f_logit, b_row)


EXTRA = HEAD_DIM
N_PIECES = 3


def _pieces(v):
    hi = v.astype(BF16).astype(F32)
    mid = (v - hi).astype(BF16).astype(F32)
    return hi, mid, (v - hi - mid).astype(BF16).astype(F32)


def _slot(main, lane, extras=None, ones_at=None):
    out = jnp.where(lane < HEAD_DIM, main, 0.0)
    if extras is not None:
        for r, e in enumerate(extras):
            out = jnp.where(lane == EXTRA + r, e, out)
    if ones_at is not None:
        out = jnp.where((lane >= ones_at) & (lane < ones_at + N_PIECES), 1.0, out)
    return out


def _lane_iota():
    return lax.broadcasted_iota(jnp.int32, (1, LANES), 1)


def fox_split(qkv, c, n_heads, name):
    s = qkv.shape[0]
    hd = n_heads * HEAD_DIM
    tr = _tile(s, 256)
    scale = HEAD_DIM ** -0.5

    def body(x_ref, c_ref, q_ref, k_ref, v_ref):
        lane = _lane_iota()
        cv = c_ref[...]
        for g in range(hd // LANES):
            for part, dst in enumerate((q_ref, k_ref, v_ref)):
                t = x_ref[:, part * hd + g * LANES:part * hd + (g + 1) * LANES].astype(F32)
                for a, main in enumerate((t, pltpu.roll(t, HEAD_DIM, 1))):
                    h = 2 * g + a
                    if part == 0:
                        val = _slot(main * scale, lane, ones_at=EXTRA)
                    elif part == 1:
                        ch = jnp.sum(jnp.where(lane == h, cv, 0.0), axis=1, keepdims=True)
                        val = _slot(main, lane, extras=_pieces(-ch), ones_at=EXTRA + N_PIECES)
                    else:
                        val = _slot(main, lane, ones_at=EXTRA)
                    dst[:, h * LANES:(h + 1) * LANES] = val.astype(BF16)

    slots = jax.ShapeDtypeStruct((s, n_heads * LANES), BF16)
    return _pcall(body, name=name, grid=(s // tr,), in_specs=[_row_spec(tr, 3 * hd), _row_spec(tr, LANES)],
                  out_specs=[_row_spec(tr, n_heads * LANES)] * 3, out_shape=[slots] * 3,
                  compiler_params=_cp(("parallel",), VMEM_BIG))(qkv, c)


def _causal_keep(t):
    return lax.broadcasted_iota(jnp.int32, (t, t), 1) <= lax.broadcasted_iota(jnp.int32, (t, t), 0)


def fox_fwd(qs, ks, vs, n_heads, name):
    s = qs.shape[0]
    pairs = n_heads // 2
    t = min(512, s)
    ratio = 2 if s % (2 * t) == 0 else 1
    tq = ratio * t
    wide = 2 * LANES

    def body(q_ref, k_ref, v_ref, o_ref, qb_ref, acc_ref, m_ref):
        i = pl.program_id(1)
        lane = _lane_iota()
        acc_ref[...] = jnp.zeros_like(acc_ref)
        m_ref[...] = jnp.full_like(m_ref, NEG)

        def step(j, diagonal):
            rows = pl.ds(pl.multiple_of(j * t, t), t)
            first = 0 if diagonal is None else diagonal * t
            live = slice(first, tq)
            for a in range(2):
                cols = slice(a * LANES, (a + 1) * LANES)
                sc = _nt(q_ref[live, cols], k_ref[rows, cols])
                if diagonal is not None:
                    shape = (tq - first, t)
                    sc = jnp.where(lax.broadcasted_iota(jnp.int32, shape, 1) <= lax.broadcasted_iota(jnp.int32, shape, 0), sc, NEG)
                m_old = m_ref[a, live, :]
                m_new = jnp.maximum(m_old, jnp.max(sc, axis=-1, keepdims=True))
                p = jnp.exp(sc - jnp.tile(m_new, (1, t // LANES)))
                acc_ref[a, live, :] = jnp.exp(m_old - m_new) * acc_ref[a, live, :] + jnp.dot(p.astype(BF16), v_ref[rows, cols],
                                                                                             preferred_element_type=F32)
                m_ref[a, live, :] = m_new

        def two_steps(j2, carry):
            step(2 * j2, None)
            step(2 * j2 + 1, None)
            return carry

        past = ratio * i
        lax.fori_loop(0, past // 2, two_steps, 0)
        if ratio % 2:
            @pl.when(past % 2 == 1)
            def _():
                step(past - 1, None)

        for u in range(ratio):
            step(past + u, u)
        outs = []
        for a in range(2):
            cols = slice(a * LANES, (a + 1) * LANES)
            acc = acc_ref[a]
            norm = acc[:, EXTRA:EXTRA + 1]
            outs.append(acc / norm)
            neg_lse = _pieces(-(m_ref[a] + jnp.log(norm)))
            qb = q_ref[:, cols].astype(F32)
            for r in range(N_PIECES):
                qb = jnp.where(lane == EXTRA + N_PIECES + r, neg_lse[r], qb)
            qb_ref[:, cols] = qb.astype(BF16)
        o_ref[...] = jnp.where(lane < HEAD_DIM, outs[0], pltpu.roll(outs[1], HEAD_DIM, 1)).astype(BF16)

    qblk = pl.BlockSpec((tq, wide), lambda p, i: (i, p))
    whole = pl.BlockSpec((s, wide), lambda p, i: (0, p))
    return _pcall(body, name=name, grid=(pairs, s // tq), in_specs=[qblk, whole, whole],
                  out_specs=[pl.BlockSpec((tq, LANES), lambda p, i: (i, p)), qblk],
                  out_shape=[jax.ShapeDtypeStruct((s, n_heads * HEAD_DIM), BF16), jax.ShapeDtypeStruct(qs.shape, BF16)],
                  scratch_shapes=[pltpu.VMEM((2, tq, LANES), F32)] * 2,
                  compiler_params=_cp(("parallel", "arbitrary"), VMEM_BIG))(qs, ks, vs)


def fox_dout_slots(dout, out, name):
    s, hd = dout.shape
    tr = _tile(s, 256)

    def body(d_ref, o_ref, s_ref):
        lane = _lane_iota()
        for g in range(hd // LANES):
            cols = slice(g * LANES, (g + 1) * LANES)
            d2 = d_ref[:, cols].astype(F32)
            prod = d2 * o_ref[:, cols].astype(F32)
            for a, main in enumerate((d2, pltpu.roll(d2, HEAD_DIM, 1))):
                delta = jnp.sum(jnp.where((lane // HEAD_DIM) == a, prod, 0.0), axis=1, keepdims=True)
                h = 2 * g + a
                s_ref[:, h * LANES:(h + 1) * LANES] = _slot(main, lane, extras=_pieces(-delta)).astype(BF16)

    return _pcall(body, name=name, grid=(s // tr,), in_specs=[_row_spec(tr, hd), _row_spec(tr, hd)],
                  out_specs=_row_spec(tr, 2 * hd), out_shape=jax.ShapeDtypeStruct((s, 2 * hd), BF16),
                  compiler_params=_cp(("parallel",)))(dout, out)


def fox_bwd(qb, ks, vs, dos, n_heads, name):
    s = qb.shape[0]
    pairs = n_heads // 2
    t = min(512, s)
    nblk = s // t
    wide = 2 * LANES

    def body(q_ref, k_ref, v_ref, do_ref, dq_ref, dk_ref, dv_ref, dka_ref, dva_ref):
        j = pl.program_id(1)

        @pl.when(j == 0)
        def _():
            dq_ref[...] = jnp.zeros_like(dq_ref)

        dka_ref[...] = jnp.zeros_like(dka_ref)
        dva_ref[...] = jnp.zeros_like(dva_ref)

        def step(i, diagonal):
            rows = pl.ds(pl.multiple_of(i * t, t), t)
            for a in range(2):
                cols = slice(a * LANES, (a + 1) * LANES)
                qa, doa, ka = q_ref[rows, cols], do_ref[rows, cols], k_ref[:, cols]
                sc = _nt(qa, ka)
                if diagonal:
                    sc = jnp.where(_causal_keep(t), sc, NEG)
                p = jnp.exp(sc)
                ds = (p * _nt(doa, v_ref[:, cols])).astype(BF16)
                dva_ref[a] += _tn(p.astype(BF16), doa)
                dka_ref[a] += _tn(ds, qa)
                dq_ref[rows, cols] += jnp.dot(ds, ka, preferred_element_type=F32)

        step(j, True)
        below = nblk - 1 - j

        def two_steps(i2, carry):
            step(j + 1 + 2 * i2, False)
            step(j + 2 + 2 * i2, False)
            return carry

        lax.fori_loop(0, below // 2, two_steps, 0)

        @pl.when(below % 2 == 1)
        def _():
            step(nblk - 1, False)

        for a in range(2):
            cols = slice(a * LANES, (a + 1) * LANES)
            dk_ref[:, cols] = dka_ref[a]
            dv_ref[:, cols] = dva_ref[a].astype(BF16)

    whole = pl.BlockSpec((s, wide), lambda p, j: (0, p))
    blk = pl.BlockSpec((t, wide), lambda p, j: (j, p))
    return _pcall(body, name=name, grid=(pairs, nblk), in_specs=[whole, blk, blk, whole], out_specs=[whole, blk, blk],
                  out_shape=[jax.ShapeDtypeStruct(qb.shape, F32), jax.ShapeDtypeStruct(qb.shape, F32),
                             jax.ShapeDtypeStruct(qb.shape, BF16)],
                  scratch_shapes=[pltpu.VMEM((2, t, LANES), F32)] * 2,
                  compiler_params=_cp(("parallel", "arbitrary"), VMEM_BIG))(qb, ks, vs, dos)


def fox_merge_bwd(dqs, dks, dvs, n_heads, name):
    s = dqs.shape[0]
    hd = n_heads * HEAD_DIM
    tr = _tile(s, 128)
    scale = HEAD_DIM ** -0.5

    def body(dq_ref, dk_ref, dv_ref, o_ref, dc_ref):
        lane = _lane_iota()
        dc = jnp.zeros((tr, LANES), F32)
        for g in range(hd // LANES):
            even = slice(2 * g * LANES, (2 * g + 1) * LANES)
            odd = slice((2 * g + 1) * LANES, (2 * g + 2) * LANES)
            for part, (src, mul) in enumerate(((dq_ref, scale), (dk_ref, 1.0), (dv_ref, 1.0))):
                dense = jnp.where(lane < HEAD_DIM, src[:, even].astype(F32), pltpu.roll(src[:, odd].astype(F32), HEAD_DIM, 1))
                o_ref[:, part * hd + g * LANES:part * hd + (g + 1) * LANES] = (dense * mul).astype(BF16)
            for a, cols in enumerate((even, odd)):
                both = jnp.where(lane == EXTRA + N_PIECES, dq_ref[:, cols], 0.0) - jnp.where(lane == EXTRA, dk_ref[:, cols], 0.0)
                dc = jnp.where(lane == 2 * g + a, jnp.sum(both, axis=1, keepdims=True), dc)
        dc_ref[...] = dc

    wide = n_heads * LANES
    return _pcall(body, name=name, grid=(s // tr,), in_specs=[_row_spec(tr, wide)] * 3,
                  out_specs=[_row_spec(tr, 3 * hd), _row_spec(tr, LANES)],
                  out_shape=[jax.ShapeDtypeStruct((s, 3 * hd), BF16), jax.ShapeDtypeStruct((s, LANES), F32)],
                  compiler_params=_cp(("parallel",), VMEM_BIG))(dqs, dks, dvs)


def _w_spec(shape):
    _, r, c = shape
    if r % 16 == 0 or r <= 128:
        tr = _tile(r, 128)
        return r // tr, pl.BlockSpec((None, tr, c), lambda l, i: (l, i, 0))
    tc = 2 * LANES
    return c // tc, pl.BlockSpec((None, r, tc), lambda l, i: (l, 0, i))


def cast_into_slot(w, layer, place, dtype, name):
    tr = _tile(w.shape[1], 128)
    blk = (None, tr, w.shape[2])

    def body(me_ref, c_ref, w_ref, o_ref):
        o_ref[...] = w_ref[...].astype(dtype)

    grid_spec = pltpu.PrefetchScalarGridSpec(
        num_scalar_prefetch=2, grid=(w.shape[1] // tr,),
        in_specs=[pl.BlockSpec(blk, lambda i, me, c: (layer, i, 0))],
        out_specs=pl.BlockSpec(blk, lambda i, me, c: (me[0], i, 0)))
    return _pcall(body, name=name, grid_spec=grid_spec, out_shape=jax.ShapeDtypeStruct((N_CHIPS,) + w.shape[1:], dtype),
                  compiler_params=_cp(("parallel",)))(*place, w)


def adamw(w, g, m, v, name):
    steps, spec = _w_spec(w.shape)

    def body(w_ref, g_ref, m_ref, v_ref, d_ref, mo_ref, vo_ref):
        gg = g_ref[...]
        mn = ADAM_B1 * m_ref[...] + (1.0 - ADAM_B1) * gg
        vn = ADAM_B2 * v_ref[...] + (1.0 - ADAM_B2) * (gg * gg)
        m_hat = mn / (1.0 - ADAM_B1 ** ADAM_STEP)
        v_hat = vn / (1.0 - ADAM_B2 ** ADAM_STEP)
        d_ref[...] = -ADAM_LR * (m_hat / (jnp.sqrt(v_hat) + ADAM_EPS) + ADAM_WD * w_ref[...])
        mo_ref[...] = mn
        vo_ref[...] = vn

    out = jax.ShapeDtypeStruct(w.shape, F32)
    return _pcall(body, name=name, grid=(w.shape[0], steps), in_specs=[spec] * 4, out_specs=[spec] * 3,
                  out_shape=[out] * 3, compiler_params=_cp(("parallel", "parallel"), VMEM_BIG))(w, g, m, v)


def add_core_halves(g, recv, place, name):
    hr = recv.shape[1]
    tr = _tile(hr, 128)
    nb = hr // tr
    blk = (None, tr, g.shape[2])

    def body(me_ref, c_ref, g_ref, r_ref, o_ref):
        o_ref[...] = (g_ref[...].astype(F32) + r_ref[...].astype(F32)).astype(o_ref.dtype)

    grid_spec = pltpu.PrefetchScalarGridSpec(
        num_scalar_prefetch=2, grid=(N_CHIPS, nb),
        in_specs=[pl.BlockSpec(blk, lambda q, i, me, c: (q, c[0] * nb + i, 0)), pl.BlockSpec(blk, lambda q, i, me, c: (q, i, 0))],
        out_specs=pl.BlockSpec(blk, lambda q, i, me, c: (q, i, 0)))
    return _pcall(body, name=name, grid_spec=grid_spec, out_shape=jax.ShapeDtypeStruct(recv.shape, g.dtype),
                  compiler_params=_cp(("parallel", "parallel")))(*place, g, recv)


def add_chips(own, recv, into, layer, place, name):
    _, hr, cols = own.shape
    tr = _tile(hr, 128)
    nb = hr // tr
    blk = (None, tr, cols)

    def body(me_ref, c_ref, p0, p1, p2, p3, _, o_ref):
        o_ref[...] = ((p0[...].astype(F32) + p1[...].astype(F32)) + p2[...].astype(F32)) + p3[...].astype(F32)

    def peer(flip):
        return lambda i, me, c: (me[0] ^ flip, i, 0)

    grid_spec = pltpu.PrefetchScalarGridSpec(
        num_scalar_prefetch=2, grid=(nb,), in_specs=[pl.BlockSpec(blk, peer(f)) for f in (0, 2, 1, 3)] + [ANY],
        out_specs=pl.BlockSpec(blk, lambda i, me, c: (layer, c[0] * nb + i, 0)))
    return _pcall(body, name=name, grid_spec=grid_spec, out_shape=jax.ShapeDtypeStruct(into.shape, F32),
                  input_output_aliases={6: 0}, compiler_params=_cp(("parallel",)))(*place, own, recv, recv, recv, into)


def _place():
    x, y, c = lax.axis_index("x"), lax.axis_index("y"), lax.axis_index("c")
    others = [(1 - x, y), (x, 1 - y), (1 - x, 1 - y)]
    return x, y, c, others


def _chip_id(chip):
    return 2 * chip[0] + chip[1]


def _comm_call(body, name, n_in, out_shapes, n_sems, in_place=False):
    return _pcall(body, name=name, in_specs=[ANY] * n_in, out_specs=[ANY] * len(out_shapes), out_shape=out_shapes,
                  scratch_shapes=[pltpu.SemaphoreType.DMA((n_sems,)), pltpu.SemaphoreType.DMA((n_sems,))],
                  input_output_aliases={t: t for t in range(n_in)} if in_place else {},
                  compiler_params=pltpu.CompilerParams(has_side_effects=True))


def swap_core_halves(grads, name):
    n = len(grads)

    def body(*refs):
        ins, outs = refs[:n], refs[n:2 * n]
        send, recv = refs[2 * n:]
        x, y, c, _ = _place()
        cps = []
        for t in range(n):
            hr = ins[t].shape[1] // 2
            cp = pltpu.make_async_remote_copy(src_ref=ins[t].at[:, pl.ds((1 - c) * hr, hr)], dst_ref=outs[t],
                                              send_sem=send.at[t], recv_sem=recv.at[t],
                                              device_id=(x, y, 1 - c), device_id_type=MESH)
            cp.start()
            cps.append(cp)
        for cp in cps:
            cp.wait()

    outs = [jax.ShapeDtypeStruct((a.shape[0], a.shape[1] // 2) + a.shape[2:], a.dtype) for a in grads]
    return _comm_call(body, name, n, outs, n)(*grads)


HBM = pl.BlockSpec(memory_space=pltpu.HBM)
SEM = pl.BlockSpec(memory_space=pltpu.SEMAPHORE)
DATAFLOW = pltpu.SideEffectType.DATAFLOW_SIDE_EFFECTING


def _in_hbm(a):
    return pltpu.with_memory_space_constraint(a, pltpu.HBM)


def gather_start(slots, name):
    n = len(slots)

    def body(*refs):
        bufs = refs[:n]
        send, recv = refs[n], refs[n + 1]
        token = refs[-1]
        x, y, c, others = _place()
        me = _chip_id((x, y))
        for t in range(n):
            for j, chip in enumerate(others):
                pltpu.make_async_remote_copy(src_ref=bufs[t].at[me], dst_ref=bufs[t].at[me],
                                             send_sem=send.at[3 * t + j], recv_sem=recv.at[3 * t + j],
                                             device_id=(*chip, c), device_id_type=MESH).start()
        token[...] = jnp.zeros_like(token)

    sems = pltpu.SemaphoreType.DMA((3 * n,))
    res = _pcall(body, name=name, in_specs=[HBM] * n,
                 out_shape=(sems, sems, *[pltpu.HBM(a.shape, a.dtype) for a in slots], jax.ShapeDtypeStruct((8, LANES), F32)),
                 out_specs=(SEM, SEM, *[HBM] * n, pl.BlockSpec(memory_space=pltpu.VMEM)),
                 input_output_aliases={i: 2 + i for i in range(n)},
                 compiler_params=pltpu.CompilerParams(has_side_effects=DATAFLOW))(*[_in_hbm(a) for a in slots])
    return res[0], res[1], list(res[2:2 + n]), res[-1]


def gather_wait(send, recv, slot, t, after, name):
    def body(buf, send_ref, recv_ref, after_ref, out):
        x, y, c, others = _place()
        me = _chip_id((x, y))
        for j, chip in enumerate(others):
            pltpu.make_async_remote_copy(src_ref=buf.at[me], dst_ref=buf.at[_chip_id(chip)],
                                         send_sem=send_ref.at[3 * t + j], recv_sem=recv_ref.at[3 * t + j],
                                         device_id=(*chip, c), device_id_type=MESH).wait()

    return _pcall(body, name=name, in_specs=[HBM, SEM, SEM, ANY], out_shape=pltpu.HBM(slot.shape, slot.dtype),
                  out_specs=HBM, input_output_aliases={0: 0},
                  compiler_params=pltpu.CompilerParams(has_side_effects=DATAFLOW))(slot, send, recv, after)


def scatter_start(sums, name):
    n = len(sums)

    def body(*refs):
        ins, lands = refs[:n], refs[n:2 * n]
        send, recv = refs[2 * n], refs[2 * n + 1]
        token = refs[-1]
        x, y, c, others = _place()
        me = _chip_id((x, y))
        for t in range(n):
            for j, chip in enumerate(others):
                pltpu.make_async_remote_copy(src_ref=ins[t].at[_chip_id(chip)], dst_ref=lands[t].at[me],
                                             send_sem=send.at[3 * t + j], recv_sem=recv.at[3 * t + j],
                                             device_id=(*chip, c), device_id_type=MESH).start()
        token[...] = jnp.zeros_like(token)

    bufs = [pltpu.HBM(a.shape, a.dtype) for a in sums]
    sems = pltpu.SemaphoreType.DMA((3 * n,))
    res = _pcall(body, name=name, in_specs=[HBM] * (2 * n),
                 out_shape=(sems, sems, *bufs, *bufs, jax.ShapeDtypeStruct((8, LANES), F32)),
                 out_specs=(SEM, SEM, *[HBM] * (2 * n), pl.BlockSpec(memory_space=pltpu.VMEM)),
                 input_output_aliases={i: 2 + i for i in range(2 * n)},
                 compiler_params=pltpu.CompilerParams(has_side_effects=DATAFLOW))(
        *[_in_hbm(a) for a in sums], *[_in_hbm(lax.empty(a.shape, a.dtype)) for a in sums])
    return res[0], res[1], list(res[2:2 + n]), list(res[2 + n:2 + 2 * n]), res[-1]


def scatter_wait(send, recv, sums, lands, after, name):
    n = len(sums)

    def body(*refs):
        ins, bufs = refs[:n], refs[n:2 * n]
        send_ref, recv_ref = refs[2 * n], refs[2 * n + 1]
        x, y, c, others = _place()
        me = _chip_id((x, y))
        for t in range(n):
            for j, chip in enumerate(others):
                cp = pltpu.make_async_remote_copy(src_ref=ins[t].at[_chip_id(chip)], dst_ref=bufs[t].at[_chip_id(chip)],
                                                  send_sem=send_ref.at[3 * t + j], recv_sem=recv_ref.at[3 * t + j],
                                                  device_id=(*chip, c), device_id_type=MESH)
                cp.wait_send()
                cp.wait_recv()

    shapes = [pltpu.HBM(a.shape, a.dtype) for a in sums]
    res = _pcall(body, name=name, in_specs=[HBM] * (2 * n) + [SEM, SEM, ANY],
                 out_shape=(*shapes, *shapes), out_specs=tuple([HBM] * (2 * n)),
                 input_output_aliases={i: i for i in range(2 * n)},
                 compiler_params=pltpu.CompilerParams(has_side_effects=DATAFLOW))(*sums, *lands, send, recv, after)
    return list(res[:n]), list(res[n:])


def join_core_halves(fulls, name):
    n = len(fulls)

    def body(*refs):
        bufs = refs[n:2 * n]
        send, recv = refs[2 * n:]
        x, y, c, _ = _place()
        cps = []
        for t in range(n):
            hr = bufs[t].shape[1] // 2
            mine = bufs[t].at[:, pl.ds(c * hr, hr)]
            cp = pltpu.make_async_remote_copy(src_ref=mine, dst_ref=mine, send_sem=send.at[t], recv_sem=recv.at[t],
                                              device_id=(x, y, 1 - c), device_id_type=MESH)
            cp.start()
            cps.append(cp)
        for t in range(n):
            hr = bufs[t].shape[1] // 2
            theirs = bufs[t].at[:, pl.ds((1 - c) * hr, hr)]
            cps[t].wait_send()
            pltpu.make_async_remote_copy(src_ref=theirs, dst_ref=theirs, send_sem=send.at[t], recv_sem=recv.at[t],
                                         device_id=(x, y, c), device_id_type=MESH).wait_recv()

    outs = [jax.ShapeDtypeStruct(a.shape, a.dtype) for a in fulls]
    return _comm_call(body, name, n, outs, n, in_place=True)(*fulls)


def kernel(x, positions, norm_gains, swa_w_in, swa_sinks, swa_w_out, fox_w_in, fox_b_f, fox_w_out, ffn_w_gate_up, ffn_w_down, loss_target, m_norm_gains, m_swa_w_in, m_swa_sinks, m_swa_w_out, m_fox_w_in, m_fox_b_f, m_fox_w_out, m_ffn_w_gate_up, m_ffn_w_down, v_norm_gains, v_swa_w_in, v_swa_sinks, v_swa_w_out, v_fox_w_in, v_fox_b_f, v_fox_w_out, v_ffn_w_gate_up, v_ffn_w_down):
    s, d = x.shape[1], x.shape[2]
    depth = norm_gains.shape[0]
    n_heads = d // HEAD_DIM
    hd = n_heads * HEAD_DIM
    n_kv = (swa_w_in.shape[2] * N_CHIPS // HEAD_DIM - n_heads) // 2
    ff = ffn_w_down.shape[1] * N_CHIPS
    fox_cols = fox_w_in.shape[2]
    fox_pad = 3 * hd + LANES
    assert fox_cols * N_CHIPS == 3 * hd + n_heads and n_heads <= LANES
    x0 = x[0]
    target = loss_target[0]
    place = ((2 * lax.axis_index("x") + lax.axis_index("y")).astype(jnp.int32).reshape(1),
             lax.axis_index("c").astype(jnp.int32).reshape(1))

    order = [("norm_gains", norm_gains.reshape(1, depth * 4, norm_gains.shape[2]), 0, F32)]
    for layer in range(depth):
        j = layer // 2
        kind, w_i, w_o = ("swa", swa_w_in, swa_w_out) if layer % 2 == 0 else ("fox", fox_w_in, fox_w_out)
        order += [(f"{kind}_w_in_{j}", w_i, j, BF16), (f"{kind}_w_out_{j}", w_o, j, BF16),
                  (f"ffn_w_gate_up_{layer}", ffn_w_gate_up, layer, BF16), (f"ffn_w_down_{layer}", ffn_w_down, layer, BF16)]
    slots = [cast_into_slot(w, l, place, dt, "cast_" + nm) for nm, w, l, dt in order]
    g_send, g_recv, slots, g_token = gather_start(slots, "gather_start")
    slot_of = {entry[0]: t for t, entry in enumerate(order)}

    def weight(nm, after):
        t = slot_of[nm]
        return gather_wait(g_send, g_recv, slots[t], t, after, "gather_wait_" + nm)

    gains = jnp.transpose(weight("norm_gains", g_token), (1, 0, 2)).reshape(depth * 4, d)

    def gain(layer, which):
        return gains[layer * 4 + which][None, :]

    def fox_weight(w_in):
        parts = [w_in[q] for q in range(N_CHIPS)]
        parts.append(jnp.zeros((d, fox_pad - fox_cols * N_CHIPS), BF16))
        return jnp.concatenate(parts, axis=1)

    inv_freq = ROPE_THETA ** (-jnp.arange(0, ROT_DIM, 2, dtype=F32) / ROT_DIM)
    lane_d = jnp.arange(LANES) % HEAD_DIM
    invf_row = jnp.where(lane_d < ROT_DIM, inv_freq[lane_d % (ROT_DIM // 2)], 0.0)[None, :]
    tabs = rope_tables(positions.reshape(s, 1), invf_row, "rope_tables")

    n_sh_in = swa_w_in.shape[2]
    gu_sh = ffn_w_gate_up.shape[2]
    tn_gu = gu_sh // 2 if (gu_sh // 2) % LANES == 0 else gu_sh
    down_sh = ffn_w_down.shape[1]
    out_sh = swa_w_out.shape[1]
    tm = min(1024, s)

    saved = []
    xin = x0
    h = prenorm(xin, gain(0, 0), "prenorm_first")
    for layer in range(depth):
        j = layer // 2
        rec = {"x_in": xin, "h1": h}
        if layer % 2 == 0:
            w_in = weight(f"swa_w_in_{j}", h)
            proj = mm_nn(h, w_in, lambda n, k: (n, k, 0), n_sh_in * N_CHIPS, tm=tm, tn=n_sh_in, tk=d,
                         out_dtype=F32, name=f"swa_proj_{j}")
            q, kd, vd = swa_split(proj, tabs, n_heads, n_kv, f"swa_split_{j}")
            sink_row = jnp.repeat(swa_sinks[j], HEAD_DIM)[None, :]
            attn, lse = swa_fwd(q, kd, vd, sink_row, f"swa_fwd_{j}")
            rec.update(q=q, kd=kd, vd=vd, sink_row=sink_row, lse=lse, w_in=w_in)
            w_out = weight(f"swa_w_out_{j}", attn)
        else:
            wf = fox_weight(weight(f"fox_w_in_{j}", h))
            tn = 3 * hd // 6 if (3 * hd // 6) % LANES == 0 else LANES
            qkv = mm_nn(h, wf, lambda n, k: (k, n), 3 * hd, tm=tm, tn=tn, tk=d, out_dtype=BF16, name=f"fox_proj_{j}")
            f_off = 3 * hd // LANES
            f_logit = mm_nn(h, wf, lambda n, k: (k, f_off + n), LANES, tm=tm, tn=LANES, tk=d, out_dtype=F32, name=f"fox_gate_{j}")
            b_row = jnp.pad(fox_b_f[j], (0, LANES - n_heads))[None, :]
            c = forget_cumsum(f_logit, b_row, f"fox_cumsum_{j}")
            qs, ks, vs = fox_split(qkv, c, n_heads, f"fox_split_{j}")
            attn, qb = fox_fwd(qs, ks, vs, n_heads, f"fox_fwd_{j}")
            rec.update(wf=wf, f_logit=f_logit, b_row=b_row, qb=qb, ks=ks, vs=vs)
            w_out = weight(f"fox_w_out_{j}", attn)
        y = mm_nn(attn, w_out, lambda n, k: (k, 0, n), d, tm=tm, tn=d, tk=out_sh, out_dtype=F32, name=f"out_proj_{layer}")
        xmid, h2 = postnorm_residual(xin, y, gain(layer, 1), gain(layer, 2), f"postnorm_mixer_{layer}")
        npb = gu_sh // tn_gu
        w_gu = weight(f"ffn_w_gate_up_{layer}", h2)
        nbf = ff // tn_gu
        gate, up, act = ffn_up_swiglu(h2, w_gu, lambda n: (n // npb, 0, n % npb), lambda n: ((n + nbf) // npb, 0, (n + nbf) % npb),
                                      ff, tm=min(512, s), tn=tn_gu, name=f"ffn_up_{layer}")
        w_down = weight(f"ffn_w_down_{layer}", act)
        y2 = mm_nn(act, w_down, lambda n, k: (k, 0, n), d, tm=tm, tn=d, tk=down_sh, out_dtype=F32, name=f"ffn_down_{layer}")
        rec.update(attn=attn, y=y, x_mid=xmid, h2=h2, gate=gate, up=up, act=act, y2=y2, w_out=w_out, w_gu=w_gu, w_down=w_down)
        saved.append(rec)
        if layer + 1 < depth:
            xin, h = postnorm_residual(xmid, y2, gain(layer, 3), gain(layer + 1, 0), f"postnorm_ffn_{layer}")
    dx, loss_blk = postnorm_loss(xmid, y2, gain(depth - 1, 3), target, "loss")

    finals = {nm: lax.empty(w.shape, F32) for nm, w in (("swa_w_in", swa_w_in), ("swa_w_out", swa_w_out), ("fox_w_in", fox_w_in),
                                                        ("fox_w_out", fox_w_out), ("ffn_w_gate_up", ffn_w_gate_up),
                                                        ("ffn_w_down", ffn_w_down), ("norm_gains", order[0][1]))}
    in_flight = []

    def finish_reduce(after):
        send_, recv_, sums_, lands_, keys = in_flight.pop()
        sums_, got = scatter_wait(send_, recv_, sums_, lands_, after, "reduce_scatter_wait_" + keys[0][2])
        for own, rcv, (key, l, nm) in zip(sums_, got, keys):
            finals[key] = add_chips(own, rcv, finals[key], l, place, "reduce_add_chips_" + nm)

    def start_reduce(partials, keys):
        if in_flight:
            finish_reduce(partials[0])
        halves = swap_core_halves(partials, "reduce_swap_" + keys[0][2])
        sums_ = [add_core_halves(g, r, place, "reduce_add_cores_" + k[2]) for g, r, k in zip(partials, halves, keys)]
        send_, recv_, sums_, lands_, token = scatter_start(sums_, "reduce_scatter_start_" + keys[0][2])
        in_flight.append((send_, recv_, sums_, lands_, keys))
        return token

    dgains = [None] * (depth * 4)
    dsinks = [None] * ((depth + 1) // 2)
    dbf = [None] * (depth // 2)
    tko = d
    for layer in reversed(range(depth)):
        j = layer // 2
        rec = saved[layer]
        dy2, dgains[layer * 4 + 3] = postnorm_bwd(dx, rec["y2"], gain(layer, 3), f"postnorm_ffn_bwd_{layer}")
        tnd = d
        g_down = mm_tn(rec["act"], dy2, lambda i, n: (i, 0, n), tka=down_sh, tn=tnd, tm=tm, out_block=(None, down_sh, tnd),
                       name=f"ffn_down_dw_{layer}", out_shape=jax.ShapeDtypeStruct((N_CHIPS, down_sh, d), BF16))
        dgu = ffn_down_dx_swiglu(dy2, rec["w_down"], lambda n: (n, 0, 0), rec["gate"], rec["up"], tm=min(512, s), tn=down_sh,
                                 name=f"ffn_down_dx_{layer}")
        npb = gu_sh // tn_gu
        g_gu = mm_tn(rec["h2"], dgu, lambda i, n: (n // npb, i, n % npb), tka=tko, tn=tn_gu, tm=tm, out_block=(None, tko, tn_gu),
                     name=f"ffn_up_dw_{layer}", out_shape=jax.ShapeDtypeStruct((N_CHIPS, d, gu_sh), BF16))
        sent = start_reduce([g_gu, g_down], [("ffn_w_gate_up", layer, f"ffn_w_gate_up_{layer}"), ("ffn_w_down", layer, f"ffn_w_down_{layer}")])
        dh2 = mm_nt(dgu, rec["w_gu"], lambda o, n: (n // npb, o, n % npb), d, tm=tm, tko=tko, tn=tn_gu, out_dtype=F32,
                    name=f"ffn_up_dx_{layer}", after=sent)
        dxm, dgains[layer * 4 + 2] = prenorm_bwd(rec["x_mid"], dh2, dx, gain(layer, 2), f"prenorm_ffn_bwd_{layer}")
        dy, dgains[layer * 4 + 1] = postnorm_bwd(dxm, rec["y"], gain(layer, 1), f"postnorm_mixer_bwd_{layer}")
        g_out = mm_tn(rec["attn"], dy, lambda i, n: (i, 0, n), tka=out_sh, tn=tnd, tm=min(2048, s), out_block=(None, out_sh, tnd),
                      name=f"out_proj_dw_{layer}", out_shape=jax.ShapeDtypeStruct((N_CHIPS, out_sh, d), BF16))
        dattn = mm_nt(dy, rec["w_out"], lambda o, n: (o, 0, n), hd, tm=tm, tko=out_sh, tn=d, out_dtype=BF16, name=f"out_proj_dx_{layer}")
        if layer % 2 == 0:
            dq, dkd, dvd, dsk = swa_bwd(rec["q"], rec["kd"], rec["vd"], rec["sink_row"], rec["attn"], rec["lse"], dattn, f"swa_bwd_{j}")
            dsinks[j] = dsk[0].reshape(n_heads, HEAD_DIM)[:, 0]
            dproj = swa_merge_bwd(dq, dkd, dvd, tabs, f"swa_merge_bwd_{j}")
            g_in = mm_tn(rec["h1"], dproj, lambda i, n: (n, i, 0), tka=tko, tn=n_sh_in, tm=tm, out_block=(None, tko, n_sh_in),
                         name=f"swa_proj_dw_{j}", out_shape=jax.ShapeDtypeStruct((N_CHIPS, d, n_sh_in), BF16))
            sent = start_reduce([g_in, g_out], [("swa_w_in", j, f"swa_w_in_{j}"), ("swa_w_out", j, f"swa_w_out_{j}")])
            dh1 = mm_nt(dproj, rec["w_in"], lambda o, n: (n, o, 0), d, tm=tm, tko=tko, tn=n_sh_in, out_dtype=F32,
                        name=f"swa_proj_dx_{j}", after=sent)
        else:
            dos = fox_dout_slots(dattn, rec["attn"], f"fox_dout_slots_{j}")
            dqs, dks, dvs = fox_bwd(rec["qb"], rec["ks"], rec["vs"], dos, n_heads, f"fox_bwd_{j}")
            dqkv, dc = fox_merge_bwd(dqs, dks, dvs, n_heads, f"fox_merge_bwd_{j}")
            df, db = forget_gate_bwd(dc, rec["f_logit"], rec["b_row"], f"fox_gate_bwd_{j}")
            dbf[j] = db[0, :n_heads]
            dproj = jnp.concatenate([dqkv, df], axis=1)
            tn_f = LANES * max(k for k in range(1, 9) if (fox_pad // LANES) % k == 0)
            dwf = mm_tn(rec["h1"], dproj, lambda i, n: (i, n), tka=tko, tn=tn_f, tm=tm, out_block=(tko, tn_f),
                        name=f"fox_proj_dw_{j}", out_shape=jax.ShapeDtypeStruct((d, fox_pad), BF16))
            g_in = jnp.stack([dwf[:, q * fox_cols:(q + 1) * fox_cols] for q in range(N_CHIPS)])
            sent = start_reduce([g_in, g_out], [("fox_w_in", j, f"fox_w_in_{j}"), ("fox_w_out", j, f"fox_w_out_{j}")])
            dh1 = mm_nt(dproj, rec["wf"], lambda o, n: (o, n), d, tm=tm, tko=tko, tn=tn_f, out_dtype=F32,
                        name=f"fox_proj_dx_{j}", after=sent)
        dx, dgains[layer * 4] = prenorm_bwd(rec["x_in"], dh1, dxm, gain(layer, 0), f"prenorm_mixer_bwd_{layer}")
    grad_x = dx[None]

    dgain_full = jnp.concatenate(dgains, axis=0)
    start_reduce([jnp.transpose(dgain_full.reshape(depth * 4, N_CHIPS, -1), (1, 0, 2))], [("norm_gains", 0, "norm_gains")])
    finish_reduce(dx)
    keys = ["swa_w_in", "swa_w_out", "fox_w_in", "fox_w_out", "ffn_w_gate_up", "ffn_w_down", "norm_gains"]
    full = join_core_halves([finals[k] for k in keys], "reduce_join_cores")
    g_swa_in_f, g_swa_out_f, g_fox_in_f, g_fox_out_f, g_gu_f, g_down_f, g_gains_f = full
    g_gains_f = g_gains_f.reshape(norm_gains.shape)

    n_swa, n_fox = len(dsinks), len(dbf)
    small = jnp.concatenate([loss_blk[0, :1]] + dsinks + dbf)
    small = lax.psum(small, ("x", "y", "c"))
    loss = small[0]
    g_sinks = small[1:1 + n_swa * n_heads].reshape(n_swa, n_heads)
    g_bf = small[1 + n_swa * n_heads:].reshape(n_fox, n_heads)

    def pad_small(a):
        return jnp.pad(a, ((0, 8 - a.shape[0]), (0, LANES - a.shape[1])))[None]

    def update(w, g, m, v, nm):
        if w.ndim == 2:
            dl, mn, vn = adamw(pad_small(w), pad_small(g), pad_small(m), pad_small(v), "adamw_" + nm)
            return tuple(a[0, :w.shape[0], :w.shape[1]] for a in (dl, mn, vn))
        if w.shape[2] % LANES and w.shape[1] % LANES == 0:
            turn = lambda a: jnp.swapaxes(a, 1, 2)
            return tuple(turn(a) for a in adamw(turn(w), turn(g), turn(m), turn(v), "adamw_" + nm))
        return adamw(w, g, m, v, "adamw_" + nm)

    grads = [g_gains_f, g_swa_in_f, g_sinks, g_swa_out_f, g_fox_in_f, g_bf, g_fox_out_f, g_gu_f, g_down_f]
    ws = [norm_gains, swa_w_in, swa_sinks, swa_w_out, fox_w_in, fox_b_f, fox_w_out, ffn_w_gate_up, ffn_w_down]
    ms = [m_norm_gains, m_swa_w_in, m_swa_sinks, m_swa_w_out, m_fox_w_in, m_fox_b_f, m_fox_w_out, m_ffn_w_gate_up, m_ffn_w_down]
    vs = [v_norm_gains, v_swa_w_in, v_swa_sinks, v_swa_w_out, v_fox_w_in, v_fox_b_f, v_fox_w_out, v_ffn_w_gate_up, v_ffn_w_down]
    nms = ["norm_gains", "swa_w_in", "swa_sinks", "swa_w_out", "fox_w_in", "fox_b_f", "fox_w_out", "ffn_w_gate_up", "ffn_w_down"]
    upd = [update(w, g, m, v, nm) for w, g, m, v, nm in zip(ws, grads, ms, vs, nms)]
    return (loss, grad_x, *grads, *[u[0] for u in upd], *[u[1] for u in upd], *[u[2] for u in upd])
```

```python
import jax
import jax.numpy as jnp
from jax import lax
from jax.experimental import pallas as pl
from jax.experimental.pallas import tpu as pltpu

F32 = jnp.float32
BF16 = jnp.bfloat16
HEAD_DIM = 64
LANES = 128
WINDOW = 128
ROPE_THETA = 500000.0
ROT_DIM = HEAD_DIM // 4
RMS_EPS = 1e-6
ADAM_LR, ADAM_B1, ADAM_B2, ADAM_EPS, ADAM_WD, ADAM_STEP = 0.001, 0.9, 0.999, 1e-08, 0.01, 10
NEG = -1e30
VMEM_BIG = 56 * 1024 * 1024
N_CHIPS = 4
MESH = pl.DeviceIdType.MESH
ANY = pl.BlockSpec(memory_space=pl.ANY)


def _pcall(body, **kw):
    return pl.pallas_call(body, **kw)


def _cp(sem=None, vmem=None):
    return pltpu.CompilerParams(dimension_semantics=sem, vmem_limit_bytes=vmem)


def _tile(n, pref):
    if n <= pref:
        return n
    t = pref - pref % 16
    while n % t:
        t -= 16
    return t


_DIMS = {"nn": (((1,), (0,)), ((), ())), "nt": (((1,), (1,)), ((), ())), "tn": (((0,), (0,)), ((), ()))}


def _matmul(kind, a, b, *, grid, a_spec, b_spec, o_spec, out_shape, acc_shape, name, after=None):
    nk = grid[2]
    dims = _DIMS[kind]

    def body(a_ref, b_ref, *rest):
        rest = rest[1:] if after is not None else rest
        o_ref = rest[0]

        def prod():
            return lax.dot_general(a_ref[...], b_ref[...], dims, preferred_element_type=F32)

        if nk == 1:
            o_ref[...] = prod().astype(o_ref.dtype)
        else:
            acc = rest[1]
            k = pl.program_id(2)

            @pl.when(k == 0)
            def _():
                acc[...] = prod()

            @pl.when(k > 0)
            def _():
                acc[...] += prod()

            @pl.when(k == nk - 1)
            def _():
                o_ref[...] = acc[...].astype(o_ref.dtype)

    args, in_specs = [a, b], [a_spec, b_spec]
    if after is not None:
        args.append(after)
        in_specs.append(ANY)
    return _pcall(body, name=name, grid=grid, in_specs=in_specs, out_specs=o_spec, out_shape=out_shape,
                  scratch_shapes=[] if nk == 1 else [pltpu.VMEM(acc_shape, F32)],
                  compiler_params=_cp(("parallel", "parallel", "arbitrary"), VMEM_BIG))(*args)


def mm_nn(a, w, w_map, n_out, *, tm, tn, tk, out_dtype, name):
    m, kdim = a.shape
    tm = min(tm, m)
    lead = (None,) * (w.ndim - 2)
    return _matmul("nn", a, w, grid=(m // tm, n_out // tn, kdim // tk),
                   a_spec=pl.BlockSpec((tm, tk), lambda i, j, k: (i, k)),
                   b_spec=pl.BlockSpec(lead + (tk, tn), lambda i, j, k: w_map(j, k)),
                   o_spec=pl.BlockSpec((tm, tn), lambda i, j, k: (i, j)),
                   out_shape=jax.ShapeDtypeStruct((m, n_out), out_dtype), acc_shape=(tm, tn), name=name)


def _split_cols(a, tn):
    if a.ndim == 2:
        return a.shape[0], a.shape[1], (), lambda row, col: (row, col)
    per = a.shape[2] // tn
    return a.shape[1], a.shape[0] * a.shape[2], (None,), lambda row, col: (col // per, row, col % per)


def mm_nt(a, w, w_map, k_out, *, tm, tko, tn, out_dtype, name, after=None):
    m, ndim, a_lead, a_map = _split_cols(a, tn)
    tm = min(tm, m)
    lead = (None,) * (w.ndim - 2)
    return _matmul("nt", a, w, grid=(m // tm, k_out // tko, ndim // tn),
                   a_spec=pl.BlockSpec(a_lead + (tm, tn), lambda i, j, n: a_map(i, n)),
                   b_spec=pl.BlockSpec(lead + (tko, tn), lambda i, j, n: w_map(j, n)),
                   o_spec=pl.BlockSpec((tm, tko), lambda i, j, n: (i, j)),
                   out_shape=jax.ShapeDtypeStruct((m, k_out), out_dtype), acc_shape=(tm, tko), name=name, after=after)


def mm_tn(a, b, o_map, *, tka, tn, tm, out_block, name, out_shape):
    m, kdim = a.shape
    _, n, b_lead, b_map = _split_cols(b, tn)
    tm = min(tm, m)
    return _matmul("tn", a, b, grid=(kdim // tka, n // tn, m // tm),
                   a_spec=pl.BlockSpec((tm, tka), lambda i, j, mm: (mm, i)),
                   b_spec=pl.BlockSpec(b_lead + (tm, tn), lambda i, j, mm: b_map(mm, j)),
                   o_spec=pl.BlockSpec(out_block, lambda i, j, mm: o_map(i, j)),
                   out_shape=out_shape, acc_shape=(tka, tn), name=name)


def _rstd(v):
    return lax.rsqrt(jnp.mean(v * v, axis=-1, keepdims=True) + RMS_EPS)


def _row_spec(tr, d):
    return pl.BlockSpec((tr, d), lambda i: (i, 0))


def _vec_spec(d):
    return pl.BlockSpec((1, d), lambda i: (0, 0))


def prenorm(x, g, name):
    s, d = x.shape
    tr = _tile(s, 256)

    def body(x_ref, g_ref, h_ref):
        v = x_ref[...]
        h_ref[...] = (v * _rstd(v) * g_ref[...]).astype(BF16)

    return _pcall(body, name=name, grid=(s // tr,), in_specs=[_row_spec(tr, d), _vec_spec(d)],
                  out_specs=_row_spec(tr, d), out_shape=jax.ShapeDtypeStruct((s, d), BF16),
                  compiler_params=_cp(("parallel",)))(x, g)


def postnorm_residual(x, y, g_post, g_next, name):
    s, d = x.shape
    tr = _tile(s, 256)

    def body(x_ref, y_ref, gp_ref, gn_ref, xo_ref, h_ref):
        v = y_ref[...]
        xn = x_ref[...] + v * _rstd(v) * gp_ref[...]
        xo_ref[...] = xn
        h_ref[...] = (xn * _rstd(xn) * gn_ref[...]).astype(BF16)

    return _pcall(body, name=name, grid=(s // tr,),
                  in_specs=[_row_spec(tr, d), _row_spec(tr, d), _vec_spec(d), _vec_spec(d)],
                  out_specs=[_row_spec(tr, d), _row_spec(tr, d)],
                  out_shape=[jax.ShapeDtypeStruct((s, d), F32), jax.ShapeDtypeStruct((s, d), BF16)],
                  compiler_params=_cp(("parallel",)))(x, y, g_post, g_next)


def postnorm_loss(x, y, g_post, target, name):
    s, d = x.shape
    tr = _tile(s, 256)

    def body(x_ref, y_ref, gp_ref, t_ref, dx_ref, loss_ref):
        v = y_ref[...]
        err = x_ref[...] + v * _rstd(v) * gp_ref[...] - t_ref[...]
        dx_ref[...] = err / d
        part = 0.5 * jnp.sum(jnp.mean(err * err, axis=-1, keepdims=True), axis=0, keepdims=True)

        @pl.when(pl.program_id(0) == 0)
        def _():
            loss_ref[...] = jnp.zeros_like(loss_ref)

        loss_ref[...] += part

    return _pcall(body, name=name, grid=(s // tr,),
                  in_specs=[_row_spec(tr, d), _row_spec(tr, d), _vec_spec(d), _row_spec(tr, d)],
                  out_specs=[_row_spec(tr, d), pl.BlockSpec((8, LANES), lambda i: (0, 0))],
                  out_shape=[jax.ShapeDtypeStruct((s, d), F32), jax.ShapeDtypeStruct((8, LANES), F32)],
                  compiler_params=_cp(("arbitrary",)))(x, y, g_post, target)


def _norm_bwd(v, g, dz):
    r = _rstd(v)
    vhat = v * r
    u = dz * g
    dv = r * (u - vhat * jnp.mean(u * vhat, axis=-1, keepdims=True))
    return dv, jnp.sum(dz * vhat, axis=0, keepdims=True)


def _acc_rows(ref, val):
    @pl.when(pl.program_id(0) == 0)
    def _():
        ref[...] = jnp.zeros_like(ref)

    ref[...] += val


def postnorm_bwd(dz, y, g, name):
    s, d = y.shape
    tr = _tile(s, 256)

    def body(dz_ref, y_ref, g_ref, dy_ref, dg_ref):
        dv, dg = _norm_bwd(y_ref[...], g_ref[...], dz_ref[...])
        dy_ref[...] = dv.astype(BF16)
        _acc_rows(dg_ref, dg)

    return _pcall(body, name=name, grid=(s // tr,), in_specs=[_row_spec(tr, d), _row_spec(tr, d), _vec_spec(d)],
                  out_specs=[_row_spec(tr, d), _vec_spec(d)],
                  out_shape=[jax.ShapeDtypeStruct((s, d), BF16), jax.ShapeDtypeStruct((1, d), F32)],
                  compiler_params=_cp(("arbitrary",)))(dz, y, g)


def prenorm_bwd(x, dh, dskip, g, name):
    s, d = x.shape
    tr = _tile(s, 256)

    def body(x_ref, dh_ref, ds_ref, g_ref, dx_ref, dg_ref):
        dv, dg = _norm_bwd(x_ref[...], g_ref[...], dh_ref[...])
        dx_ref[...] = ds_ref[...] + dv
        _acc_rows(dg_ref, dg)

    return _pcall(body, name=name, grid=(s // tr,),
                  in_specs=[_row_spec(tr, d), _row_spec(tr, d), _row_spec(tr, d), _vec_spec(d)],
                  out_specs=[_row_spec(tr, d), _vec_spec(d)],
                  out_shape=[jax.ShapeDtypeStruct((s, d), F32), jax.ShapeDtypeStruct((1, d), F32)],
                  compiler_params=_cp(("arbitrary",)))(x, dh, dskip, g)


def ffn_up_swiglu(h, w, gate_map, up_map, ff, *, tm, tn, name):
    m, kdim = h.shape
    tm = min(tm, m)
    lead = (None,) * (w.ndim - 2)

    def body(h_ref, wg_ref, wu_ref, g_ref, u_ref, act_ref):
        a = h_ref[...]
        gate = jnp.dot(a, wg_ref[...], preferred_element_type=F32)
        up = jnp.dot(a, wu_ref[...], preferred_element_type=F32)
        g_ref[...] = gate.astype(BF16)
        u_ref[...] = up.astype(BF16)
        act_ref[...] = (gate * jax.nn.sigmoid(gate) * up).astype(BF16)

    out = jax.ShapeDtypeStruct((m, ff), BF16)
    oblk = pl.BlockSpec((tm, tn), lambda n, i: (i, n))
    return _pcall(body, name=name, grid=(ff // tn, m // tm),
                  in_specs=[pl.BlockSpec((tm, kdim), lambda n, i: (i, 0)), pl.BlockSpec(lead + (kdim, tn), lambda n, i: gate_map(n)),
                            pl.BlockSpec(lead + (kdim, tn), lambda n, i: up_map(n))],
                  out_specs=[oblk] * 3, out_shape=[out] * 3,
                  compiler_params=_cp(("parallel", "parallel"), VMEM_BIG))(h, w, w)


def ffn_down_dx_swiglu(dy, w, w_map, gate, up, *, tm, tn, name):
    m, d = dy.shape
    ff = gate.shape[1]
    tm = min(tm, m)
    lead = (None,) * (w.ndim - 2)

    def body(dy_ref, w_ref, g_ref, u_ref, o_ref):
        da = _nt(dy_ref[...], w_ref[...])
        gt = g_ref[...].astype(F32)
        sig = jax.nn.sigmoid(gt)
        o_ref[0] = (da * u_ref[...].astype(F32) * sig * (1.0 + gt * (1.0 - sig))).astype(BF16)
        o_ref[1] = (da * gt * sig).astype(BF16)

    blk = pl.BlockSpec((tm, tn), lambda n, i: (i, n))
    return _pcall(body, name=name, grid=(ff // tn, m // tm),
                  in_specs=[pl.BlockSpec((tm, d), lambda n, i: (i, 0)), pl.BlockSpec(lead + (tn, d), lambda n, i: w_map(n)), blk, blk],
                  out_specs=pl.BlockSpec((2, tm, tn), lambda n, i: (0, i, n)), out_shape=jax.ShapeDtypeStruct((2, m, ff), BF16),
                  compiler_params=_cp(("parallel", "parallel"), VMEM_BIG))(dy, w, gate, up)


def rope_tables(pos_col, invf_row, name):
    s = pos_col.shape[0]
    tr = _tile(s, 1024)

    def body(p_ref, f_ref, c_ref, sa_ref, sb_ref):
        ang = p_ref[...].astype(F32) * f_ref[...]
        d = lax.broadcasted_iota(jnp.int32, (1, LANES), 1) % HEAD_DIM
        cs, sn = jnp.cos(ang), jnp.sin(ang)
        c_ref[...] = jnp.where(d < ROT_DIM, cs, 1.0)
        sa_ref[...] = jnp.where(d < ROT_DIM // 2, -sn, 0.0)
        sb_ref[...] = jnp.where((d >= ROT_DIM // 2) & (d < ROT_DIM), sn, 0.0)

    tab = jax.ShapeDtypeStruct((s, LANES), F32)
    return _pcall(body, name=name, grid=(s // tr,),
                  in_specs=[pl.BlockSpec((tr, 1), lambda i: (i, 0)), _vec_spec(LANES)],
                  out_specs=[_row_spec(tr, LANES)] * 3, out_shape=[tab] * 3, compiler_params=_cp(("parallel",)))(pos_col, invf_row)


def _rot(t, c, sa, sb):
    half = ROT_DIM // 2
    return t * c + pltpu.roll(t, LANES - half, 1) * sa + pltpu.roll(t, half, 1) * sb


def _rot_t(t, c, sa, sb):
    half = ROT_DIM // 2
    return t * c + pltpu.roll(t * sa, half, 1) + pltpu.roll(t * sb, LANES - half, 1)


def swa_split(proj, tabs, n_heads, n_kv, name):
    s, width = proj.shape
    qd, kd = n_heads * HEAD_DIM, n_kv * HEAD_DIM
    tr = _tile(s, 256)

    def body(p_ref, c_ref, sa_ref, sb_ref, q_ref, k_ref, v_ref):
        c, sa, sb = c_ref[...], sa_ref[...], sb_ref[...]
        low = lax.broadcasted_iota(jnp.int32, (1, LANES), 1) < HEAD_DIM
        for g in range(qd // LANES):
            q_ref[:, g * LANES:(g + 1) * LANES] = _rot(p_ref[:, g * LANES:(g + 1) * LANES], c, sa, sb).astype(BF16)
        for g in range(kd // LANES):
            for src, dst, rot in ((qd, k_ref, True), (qd + kd, v_ref, False)):
                t = p_ref[:, src + g * LANES:src + (g + 1) * LANES]
                t = _rot(t, c, sa, sb) if rot else t
                sw = pltpu.roll(t, HEAD_DIM, 1)
                dst[:, (2 * g) * LANES:(2 * g + 1) * LANES] = jnp.where(low, t, sw).astype(BF16)
                dst[:, (2 * g + 1) * LANES:(2 * g + 2) * LANES] = jnp.where(low, sw, t).astype(BF16)

    return _pcall(body, name=name, grid=(s // tr,),
                  in_specs=[_row_spec(tr, width)] + [_row_spec(tr, LANES)] * 3,
                  out_specs=[_row_spec(tr, qd), _row_spec(tr, 2 * kd), _row_spec(tr, 2 * kd)],
                  out_shape=[jax.ShapeDtypeStruct((s, qd), BF16), jax.ShapeDtypeStruct((s, 2 * kd), BF16),
                             jax.ShapeDtypeStruct((s, 2 * kd), BF16)],
                  compiler_params=_cp(("parallel",)))(proj, *tabs)


def swa_merge_bwd(dq, dkd, dvd, tabs, name):
    s, qd = dq.shape
    kd = dkd.shape[1] // 2
    width = qd + 2 * kd
    tr = _tile(s, 256)

    def body(dq_ref, dk_ref, dv_ref, c_ref, sa_ref, sb_ref, o_ref):
        c, sa, sb = c_ref[...], sa_ref[...], sb_ref[...]
        low = lax.broadcasted_iota(jnp.int32, (1, LANES), 1) < HEAD_DIM
        for g in range(qd // LANES):
            t = dq_ref[:, g * LANES:(g + 1) * LANES].astype(F32)
            o_ref[:, g * LANES:(g + 1) * LANES] = _rot_t(t, c, sa, sb).astype(BF16)
        for g in range(kd // LANES):
            for dst, src, rot in ((qd, dk_ref, True), (qd + kd, dv_ref, False)):
                e = src[:, (2 * g) * LANES:(2 * g + 1) * LANES]
                o = src[:, (2 * g + 1) * LANES:(2 * g + 2) * LANES]
                t = jnp.where(low, e + pltpu.roll(e, HEAD_DIM, 1), o + pltpu.roll(o, HEAD_DIM, 1))
                t = _rot_t(t, c, sa, sb) if rot else t
                o_ref[:, dst + g * LANES:dst + (g + 1) * LANES] = t.astype(BF16)

    return _pcall(body, name=name, grid=(s // tr,),
                  in_specs=[_row_spec(tr, qd), _row_spec(tr, 2 * kd), _row_spec(tr, 2 * kd)] + [_row_spec(tr, LANES)] * 3,
                  out_specs=_row_spec(tr, width), out_shape=jax.ShapeDtypeStruct((s, width), BF16),
                  compiler_params=_cp(("parallel",)))(dq, dkd, dvd, *tabs)


def _halves():
    lane_half = lax.broadcasted_iota(jnp.int32, (1, LANES), 1) // HEAD_DIM
    return [lane_half == 0, lane_half == 1]


def _nt(a, b):
    return lax.dot_general(a, b, _DIMS["nt"], preferred_element_type=F32)


def _tn(a, b):
    return lax.dot_general(a, b, _DIMS["tn"], preferred_element_type=F32)


def _band_mask(i, tq):
    w = tq + WINDOW
    rel = lax.broadcasted_iota(jnp.int32, (tq, w), 1) - lax.broadcasted_iota(jnp.int32, (tq, w), 0)
    first = lax.broadcasted_iota(jnp.int32, (tq, w), 1) >= jnp.where(i > 0, 0, WINDOW)
    return (rel >= 1) & (rel <= WINDOW) & first


def swa_fwd(q, kd, vd, sink_row, name):
    s, qd = q.shape
    tq = min(256, s)
    r = tq // WINDOW
    pairs = qd // LANES
    group_pairs = pairs // (kd.shape[1] // LANES)
    scale = HEAD_DIM ** -0.5

    def body(q_ref, kp_ref, kc_ref, vp_ref, vc_ref, sk_ref, o_ref, lse_ref):
        i = pl.program_id(1)
        k = jnp.concatenate([kp_ref[...], kc_ref[...]], axis=0)
        v = jnp.concatenate([vp_ref[...], vc_ref[...]], axis=0)
        mask = _band_mask(i, tq)
        q2 = q_ref[...]
        outs, lses = [], []
        for a, hm in enumerate(_halves()):
            sc = _nt(jnp.where(hm, q2, jnp.zeros_like(q2)), k) * scale
            sc = jnp.where(mask, sc, NEG)
            sink = sk_ref[:, a * HEAD_DIM:a * HEAD_DIM + 1]
            m = jnp.maximum(jnp.max(sc, axis=-1, keepdims=True), sink)
            p = jnp.exp(sc - m)
            den = jnp.sum(p, axis=-1, keepdims=True) + jnp.exp(sink - m)
            outs.append(jnp.dot(p.astype(BF16), v, preferred_element_type=F32) / den)
            lses.append(m + jnp.log(den))
        hm0 = _halves()[0]
        o_ref[...] = jnp.where(hm0, outs[0], outs[1]).astype(BF16)
        lse_ref[...] = jnp.where(hm0, lses[0], lses[1])

    prev = lambda p, i: (jnp.maximum(i * r - 1, 0), p // group_pairs)
    cur = lambda p, i: (i, p // group_pairs)
    blk = pl.BlockSpec((tq, LANES), lambda p, i: (i, p))
    return _pcall(body, name=name, grid=(pairs, s // tq),
                  in_specs=[blk, pl.BlockSpec((WINDOW, LANES), prev), pl.BlockSpec((tq, LANES), cur),
                            pl.BlockSpec((WINDOW, LANES), prev), pl.BlockSpec((tq, LANES), cur),
                            pl.BlockSpec((1, LANES), lambda p, i: (0, p))],
                  out_specs=[blk, blk],
                  out_shape=[jax.ShapeDtypeStruct((s, qd), BF16), jax.ShapeDtypeStruct((s, qd), F32)],
                  compiler_params=_cp(("parallel", "parallel")))(q, kd, kd, vd, vd, sink_row)


def swa_bwd(q, kd, vd, sink_row, out, lse, dout, name):
    s, qd = q.shape
    tq = min(256, s)
    r = tq // WINDOW
    n_kv = kd.shape[1] // LANES
    gw = qd // n_kv
    scale = HEAD_DIM ** -0.5

    def body(q_ref, kp_ref, kc_ref, vp_ref, vc_ref, sk_ref, o_ref, lse_ref, do_ref, dq_ref, dk_ref, dv_ref, dsk_ref):
        i = pl.program_id(1)

        @pl.when(i == 0)
        def _():
            dk_ref[...] = jnp.zeros_like(dk_ref)
            dv_ref[...] = jnp.zeros_like(dv_ref)
            dsk_ref[...] = jnp.zeros_like(dsk_ref)

        k = jnp.concatenate([kp_ref[...], kc_ref[...]], axis=0)
        v = jnp.concatenate([vp_ref[...], vc_ref[...]], axis=0)
        mask = _band_mask(i, tq)
        dk = jnp.zeros((tq + WINDOW, LANES), F32)
        dv = jnp.zeros((tq + WINDOW, LANES), F32)
        for pp in range(gw // LANES):
            cols = slice(pp * LANES, (pp + 1) * LANES)
            q2, do2 = q_ref[:, cols], do_ref[:, cols]
            prod = do2.astype(F32) * o_ref[:, cols].astype(F32)
            dq2 = jnp.zeros((tq, LANES), F32)
            dsk = jnp.zeros((1, LANES), F32)
            for a, hm in enumerate(_halves()):
                qa = jnp.where(hm, q2, jnp.zeros_like(q2))
                doa = jnp.where(hm, do2, jnp.zeros_like(do2))
                lse_a = lse_ref[:, pp * LANES + a * HEAD_DIM:pp * LANES + a * HEAD_DIM + 1]
                sc = jnp.where(mask, _nt(qa, k) * scale, NEG)
                p = jnp.exp(sc - lse_a)
                delta = jnp.sum(jnp.where(hm, prod, 0.0), axis=-1, keepdims=True)
                ds = (p * (_nt(doa, v) - delta) * scale).astype(BF16)
                dv = dv + _tn(p.astype(BF16), doa)
                dk = dk + _tn(ds, qa)
                dq2 = dq2 + jnp.where(hm, jnp.dot(ds, k, preferred_element_type=F32), 0.0)
                sink = sk_ref[:, pp * LANES + a * HEAD_DIM:pp * LANES + a * HEAD_DIM + 1]
                dsink = -jnp.sum(jnp.exp(sink - lse_a) * delta, axis=0, keepdims=True)
                dsk = dsk + jnp.where(hm, dsink, 0.0)
            dq_ref[:, cols] = dq2.astype(BF16)
            dsk_ref[0:1, cols] += dsk
        start = pl.multiple_of(i * tq, tq)
        dk_ref[pl.ds(start, tq), :] += dk[WINDOW:, :]
        dv_ref[pl.ds(start, tq), :] += dv[WINDOW:, :]

        @pl.when(i > 0)
        def _():
            before = pl.multiple_of(i * tq - WINDOW, WINDOW)
            dk_ref[pl.ds(before, WINDOW), :] += dk[:WINDOW, :]
            dv_ref[pl.ds(before, WINDOW), :] += dv[:WINDOW, :]

    prev = lambda g, i: (jnp.maximum(i * r - 1, 0), g)
    cur = lambda g, i: (i, g)
    wide = pl.BlockSpec((tq, gw), cur)
    full = pl.BlockSpec((s, LANES), lambda g, i: (0, g))
    return _pcall(body, name=name, grid=(n_kv, s // tq),
                  in_specs=[wide, pl.BlockSpec((WINDOW, LANES), prev), pl.BlockSpec((tq, LANES), cur),
                            pl.BlockSpec((WINDOW, LANES), prev), pl.BlockSpec((tq, LANES), cur),
                            pl.BlockSpec((1, gw), lambda g, i: (0, g)), wide, wide, wide],
                  out_specs=[wide, full, full, pl.BlockSpec((8, gw), lambda g, i: (0, g))],
                  out_shape=[jax.ShapeDtypeStruct((s, qd), BF16), jax.ShapeDtypeStruct(kd.shape, F32),
                             jax.ShapeDtypeStruct(kd.shape, F32), jax.ShapeDtypeStruct((8, qd), F32)],
                  compiler_params=_cp(("parallel", "arbitrary"), VMEM_BIG))(q, kd, kd, vd, vd, sink_row, out, lse, dout)


def forget_cumsum(f_logit, b_row, name):
    s = f_logit.shape[0]

    def body(f_ref, b_ref, c_ref):
        z = f_ref[...] + b_ref[...]
        acc = jnp.minimum(z, 0.0) - jnp.log(1.0 + jnp.exp(-jnp.abs(z)))
        row = lax.broadcasted_iota(jnp.int32, (s, LANES), 0)
        d = 1
        while d < s:
            acc = acc + jnp.where(row >= d, pltpu.roll(acc, d, 0), 0.0)
            d *= 2
        c_ref[...] = acc

    return _pcall(body, name=name, in_specs=[pl.BlockSpec((s, LANES), lambda: (0, 0)), pl.BlockSpec((1, LANES), lambda: (0, 0))],
                  out_specs=pl.BlockSpec((s, LANES), lambda: (0, 0)), out_shape=jax.ShapeDtypeStruct((s, LANES), F32),
                  compiler_params=_cp(None, VMEM_BIG))(f_logit, b_row)


def forget_gate_bwd(dc, f_logit, b_row, name):
    s = dc.shape[0]

    def body(dc_ref, f_ref, b_ref, df_ref, db_ref):
        acc = dc_ref[...]
        row = lax.broadcasted_iota(jnp.int32, (s, LANES), 0)
        d = 1
        while d < s:
            acc = acc + jnp.where(row < s - d, pltpu.roll(acc, s - d, 0), 0.0)
            d *= 2
        df = acc * jax.nn.sigmoid(-(f_ref[...] + b_ref[...]))
        df_ref[...] = df.astype(BF16)
        db_ref[...] = jnp.sum(df, axis=0, keepdims=True)

    whole = pl.BlockSpec((s, LANES), lambda: (0, 0))
    vec = pl.BlockSpec((1, LANES), lambda: (0, 0))
    return _pcall(body, name=name, in_specs=[whole, whole, vec], out_specs=[whole, vec],
                  out_shape=[jax.ShapeDtypeStruct((s, LANES), BF16), jax.ShapeDtypeStruct((1, LANES), F32)],
                  compiler_params=_cp(None, VMEM_BIG))(dc, f_logit, b_row)


EXTRA = HEAD_DIM
N_PIECES = 3


def _pieces(v):
    hi = v.astype(BF16).astype(F32)
    mid = (v - hi).astype(BF16).astype(F32)
    return hi, mid, (v - hi - mid).astype(BF16).astype(F32)


def _slot(main, lane, extras=None, ones_at=None):
    out = jnp.where(lane < HEAD_DIM, main, 0.0)
    if extras is not None:
        for r, e in enumerate(extras):
            out = jnp.where(lane == EXTRA + r, e, out)
    if ones_at is not None:
        out = jnp.where((lane >= ones_at) & (lane < ones_at + N_PIECES), 1.0, out)
    return out


def _lane_iota():
    return lax.broadcasted_iota(jnp.int32, (1, LANES), 1)


def fox_split(qkv, c, n_heads, name):
    s = qkv.shape[0]
    hd = n_heads * HEAD_DIM
    tr = _tile(s, 256)
    scale = HEAD_DIM ** -0.5

    def body(x_ref, c_ref, q_ref, k_ref, v_ref):
        lane = _lane_iota()
        cv = c_ref[...]
        for g in range(hd // LANES):
            for part, dst in enumerate((q_ref, k_ref, v_ref)):
                t = x_ref[:, part * hd + g * LANES:part * hd + (g + 1) * LANES].astype(F32)
                for a, main in enumerate((t, pltpu.roll(t, HEAD_DIM, 1))):
                    h = 2 * g + a
                    if part == 0:
                        val = _slot(main * scale, lane, ones_at=EXTRA)
                    elif part == 1:
                        ch = jnp.sum(jnp.where(lane == h, cv, 0.0), axis=1, keepdims=True)
                        val = _slot(main, lane, extras=_pieces(-ch), ones_at=EXTRA + N_PIECES)
                    else:
                        val = _slot(main, lane, ones_at=EXTRA)
                    dst[:, h * LANES:(h + 1) * LANES] = val.astype(BF16)

    slots = jax.ShapeDtypeStruct((s, n_heads * LANES), BF16)
    return _pcall(body, name=name, grid=(s // tr,), in_specs=[_row_spec(tr, 3 * hd), _row_spec(tr, LANES)],
                  out_specs=[_row_spec(tr, n_heads * LANES)] * 3, out_shape=[slots] * 3,
                  compiler_params=_cp(("parallel",), VMEM_BIG))(qkv, c)


def _causal_keep(t):
    return lax.broadcasted_iota(jnp.int32, (t, t), 1) <= lax.broadcasted_iota(jnp.int32, (t, t), 0)


def fox_fwd(qs, ks, vs, n_heads, name):
    s = qs.shape[0]
    pairs = n_heads // 2
    t = min(512, s)
    ratio = 2 if s % (2 * t) == 0 else 1
    tq = ratio * t
    wide = 2 * LANES

    def body(q_ref, k_ref, v_ref, o_ref, qb_ref, acc_ref, m_ref):
        i = pl.program_id(1)
        lane = _lane_iota()
        acc_ref[...] = jnp.zeros_like(acc_ref)
        m_ref[...] = jnp.full_like(m_ref, NEG)

        def step(j, diagonal):
            rows = pl.ds(pl.multiple_of(j * t, t), t)
            first = 0 if diagonal is None else diagonal * t
            live = slice(first, tq)
            for a in range(2):
                cols = slice(a * LANES, (a + 1) * LANES)
                sc = _nt(q_ref[live, cols], k_ref[rows, cols])
                if diagonal is not None:
                    shape = (tq - first, t)
                    sc = jnp.where(lax.broadcasted_iota(jnp.int32, shape, 1) <= lax.broadcasted_iota(jnp.int32, shape, 0), sc, NEG)
                m_old = m_ref[a, live, :]
                m_new = jnp.maximum(m_old, jnp.max(sc, axis=-1, keepdims=True))
                p = jnp.exp(sc - jnp.tile(m_new, (1, t // LANES)))
                acc_ref[a, live, :] = jnp.exp(m_old - m_new) * acc_ref[a, live, :] + jnp.dot(p.astype(BF16), v_ref[rows, cols],
                                                                                             preferred_element_type=F32)
                m_ref[a, live, :] = m_new

        def two_steps(j2, carry):
            step(2 * j2, None)
            step(2 * j2 + 1, None)
            return carry

        past = ratio * i
        lax.fori_loop(0, past // 2, two_steps, 0)
        if ratio % 2:
            @pl.when(past % 2 == 1)
            def _():
                step(past - 1, None)

        for u in range(ratio):
            step(past + u, u)
        outs = []
        for a in range(2):
            cols = slice(a * LANES, (a + 1) * LANES)
            acc = acc_ref[a]
            norm = acc[:, EXTRA:EXTRA + 1]
            outs.append(acc / norm)
            neg_lse = _pieces(-(m_ref[a] + jnp.log(norm)))
            qb = q_ref[:, cols].astype(F32)
            for r in range(N_PIECES):
                qb = jnp.where(lane == EXTRA + N_PIECES + r, neg_lse[r], qb)
            qb_ref[:, cols] = qb.astype(BF16)
        o_ref[...] = jnp.where(lane < HEAD_DIM, outs[0], pltpu.roll(outs[1], HEAD_DIM, 1)).astype(BF16)

    qblk = pl.BlockSpec((tq, wide), lambda p, i: (i, p))
    whole = pl.BlockSpec((s, wide), lambda p, i: (0, p))
    return _pcall(body, name=name, grid=(pairs, s // tq), in_specs=[qblk, whole, whole],
                  out_specs=[pl.BlockSpec((tq, LANES), lambda p, i: (i, p)), qblk],
                  out_shape=[jax.ShapeDtypeStruct((s, n_heads * HEAD_DIM), BF16), jax.ShapeDtypeStruct(qs.shape, BF16)],
                  scratch_shapes=[pltpu.VMEM((2, tq, LANES), F32)] * 2,
                  compiler_params=_cp(("parallel", "arbitrary"), VMEM_BIG))(qs, ks, vs)


def fox_dout_slots(dout, out, name):
    s, hd = dout.shape
    tr = _tile(s, 256)

    def body(d_ref, o_ref, s_ref):
        lane = _lane_iota()
        for g in range(hd // LANES):
            cols = slice(g * LANES, (g + 1) * LANES)
            d2 = d_ref[:, cols].astype(F32)
            prod = d2 * o_ref[:, cols].astype(F32)
            for a, main in enumerate((d2, pltpu.roll(d2, HEAD_DIM, 1))):
                delta = jnp.sum(jnp.where((lane // HEAD_DIM) == a, prod, 0.0), axis=1, keepdims=True)
                h = 2 * g + a
                s_ref[:, h * LANES:(h + 1) * LANES] = _slot(main, lane, extras=_pieces(-delta)).astype(BF16)

    return _pcall(body, name=name, grid=(s // tr,), in_specs=[_row_spec(tr, hd), _row_spec(tr, hd)],
                  out_specs=_row_spec(tr, 2 * hd), out_shape=jax.ShapeDtypeStruct((s, 2 * hd), BF16),
                  compiler_params=_cp(("parallel",)))(dout, out)


def fox_bwd(qb, ks, vs, dos, n_heads, name):
    s = qb.shape[0]
    pairs = n_heads // 2
    t = min(512, s)
    nblk = s // t
    wide = 2 * LANES

    def body(q_ref, k_ref, v_ref, do_ref, dq_ref, dk_ref, dv_ref, dka_ref, dva_ref):
        j = pl.program_id(1)

        @pl.when(j == 0)
        def _():
            dq_ref[...] = jnp.zeros_like(dq_ref)

        dka_ref[...] = jnp.zeros_like(dka_ref)
        dva_ref[...] = jnp.zeros_like(dva_ref)

        def tile(rows, keys, causal):
            for a in range(2):
                cols = slice(a * LANES, (a + 1) * LANES)
                qa, doa, ka = q_ref[rows, cols], do_ref[rows, cols], k_ref[keys, cols]
                sc = _nt(qa, ka)
                if causal:
                    sc = jnp.where(_causal_keep(sc.shape[0]), sc, NEG)
                p = jnp.exp(sc)
                ds = (p * _nt(doa, v_ref[keys, cols])).astype(BF16)
                dva_ref[a, keys, :] += _tn(p.astype(BF16), doa)
                dka_ref[a, keys, :] += _tn(ds, qa)
                dq_ref[rows, cols] += jnp.dot(ds, ka, preferred_element_type=F32)

        def step(i):
            tile(pl.ds(pl.multiple_of(i * t, t), t), slice(0, t), False)

        half = t // 2
        for rh, ch in ((0, 0), (1, 0), (1, 1)):
            tile(pl.ds(pl.multiple_of(j * t + rh * half, half), half), slice(ch * half, (ch + 1) * half), rh == ch)
        below = nblk - 1 - j

        def two_steps(i2, carry):
            step(j + 1 + 2 * i2)
            step(j + 2 + 2 * i2)
            return carry

        lax.fori_loop(0, below // 2, two_steps, 0)

        @pl.when(below % 2 == 1)
        def _():
            step(nblk - 1)

        for a in range(2):
            cols = slice(a * LANES, (a + 1) * LANES)
            dk_ref[:, cols] = dka_ref[a]
            dv_ref[:, cols] = dva_ref[a].astype(BF16)

    whole = pl.BlockSpec((s, wide), lambda p, j: (0, p))
    blk = pl.BlockSpec((t, wide), lambda p, j: (j, p))
    return _pcall(body, name=name, grid=(pairs, nblk), in_specs=[whole, blk, blk, whole], out_specs=[whole, blk, blk],
                  out_shape=[jax.ShapeDtypeStruct(qb.shape, F32), jax.ShapeDtypeStruct(qb.shape, F32),
                             jax.ShapeDtypeStruct(qb.shape, BF16)],
                  scratch_shapes=[pltpu.VMEM((2, t, LANES), F32)] * 2,
                  compiler_params=_cp(("parallel", "arbitrary"), VMEM_BIG))(qb, ks, vs, dos)


def fox_merge_bwd(dqs, dks, dvs, n_heads, name):
    s = dqs.shape[0]
    hd = n_heads * HEAD_DIM
    tr = _tile(s, 128)
    scale = HEAD_DIM ** -0.5

    def body(dq_ref, dk_ref, dv_ref, o_ref, dc_ref):
        lane = _lane_iota()
        dc = jnp.zeros((tr, LANES), F32)
        for g in range(hd // LANES):
            even = slice(2 * g * LANES, (2 * g + 1) * LANES)
            odd = slice((2 * g + 1) * LANES, (2 * g + 2) * LANES)
            for part, (src, mul) in enumerate(((dq_ref, scale), (dk_ref, 1.0), (dv_ref, 1.0))):
                dense = jnp.where(lane < HEAD_DIM, src[:, even].astype(F32), pltpu.roll(src[:, odd].astype(F32), HEAD_DIM, 1))
                o_ref[:, part * hd + g * LANES:part * hd + (g + 1) * LANES] = (dense * mul).astype(BF16)
            for a, cols in enumerate((even, odd)):
                both = jnp.where(lane == EXTRA + N_PIECES, dq_ref[:, cols], 0.0) - jnp.where(lane == EXTRA, dk_ref[:, cols], 0.0)
                dc = jnp.where(lane == 2 * g + a, jnp.sum(both, axis=1, keepdims=True), dc)
        dc_ref[...] = dc

    wide = n_heads * LANES
    return _pcall(body, name=name, grid=(s // tr,), in_specs=[_row_spec(tr, wide)] * 3,
                  out_specs=[_row_spec(tr, 3 * hd), _row_spec(tr, LANES)],
                  out_shape=[jax.ShapeDtypeStruct((s, 3 * hd), BF16), jax.ShapeDtypeStruct((s, LANES), F32)],
                  compiler_params=_cp(("parallel",), VMEM_BIG))(dqs, dks, dvs)


def _w_spec(shape):
    _, r, c = shape
    if r % 16 == 0 or r <= 128:
        tr = _tile(r, 128)
        return r // tr, pl.BlockSpec((None, tr, c), lambda l, i: (l, i, 0))
    tc = 2 * LANES
    return c // tc, pl.BlockSpec((None, r, tc), lambda l, i: (l, 0, i))


def cast_into_slot(w, layer, place, dtype, name):
    tr = _tile(w.shape[1], 128)
    blk = (None, tr, w.shape[2])

    def body(me_ref, c_ref, w_ref, o_ref):
        o_ref[...] = w_ref[...].astype(dtype)

    grid_spec = pltpu.PrefetchScalarGridSpec(
        num_scalar_prefetch=2, grid=(w.shape[1] // tr,),
        in_specs=[pl.BlockSpec(blk, lambda i, me, c: (layer, i, 0))],
        out_specs=pl.BlockSpec(blk, lambda i, me, c: (me[0], i, 0)))
    return _pcall(body, name=name, grid_spec=grid_spec, out_shape=jax.ShapeDtypeStruct((N_CHIPS,) + w.shape[1:], dtype),
                  compiler_params=_cp(("parallel",)))(*place, w)


def adamw(w, g, m, v, name):
    steps, spec = _w_spec(w.shape)

    def body(w_ref, g_ref, m_ref, v_ref, d_ref, mo_ref, vo_ref):
        gg = g_ref[...]
        mn = ADAM_B1 * m_ref[...] + (1.0 - ADAM_B1) * gg
        vn = ADAM_B2 * v_ref[...] + (1.0 - ADAM_B2) * (gg * gg)
        m_hat = mn / (1.0 - ADAM_B1 ** ADAM_STEP)
        v_hat = vn / (1.0 - ADAM_B2 ** ADAM_STEP)
        d_ref[...] = -ADAM_LR * (m_hat / (jnp.sqrt(v_hat) + ADAM_EPS) + ADAM_WD * w_ref[...])
        mo_ref[...] = mn
        vo_ref[...] = vn

    out = jax.ShapeDtypeStruct(w.shape, F32)
    return _pcall(body, name=name, grid=(w.shape[0], steps), in_specs=[spec] * 4, out_specs=[spec] * 3,
                  out_shape=[out] * 3, compiler_params=_cp(("parallel", "parallel"), VMEM_BIG))(w, g, m, v)


def add_core_halves(g, recv, place, name):
    hr = recv.shape[1]
    tr = _tile(hr, 128)
    nb = hr // tr
    blk = (None, tr, g.shape[2])

    def body(me_ref, c_ref, g_ref, r_ref, o_ref):
        o_ref[...] = (g_ref[...].astype(F32) + r_ref[...].astype(F32)).astype(o_ref.dtype)

    grid_spec = pltpu.PrefetchScalarGridSpec(
        num_scalar_prefetch=2, grid=(N_CHIPS, nb),
        in_specs=[pl.BlockSpec(blk, lambda q, i, me, c: (q, c[0] * nb + i, 0)), pl.BlockSpec(blk, lambda q, i, me, c: (q, i, 0))],
        out_specs=pl.BlockSpec(blk, lambda q, i, me, c: (q, i, 0)))
    return _pcall(body, name=name, grid_spec=grid_spec, out_shape=jax.ShapeDtypeStruct(recv.shape, g.dtype),
                  compiler_params=_cp(("parallel", "parallel")))(*place, g, recv)


def add_chips(own, recv, into, layer, place, name):
    _, hr, cols = own.shape
    tr = _tile(hr, 128)
    nb = hr // tr
    blk = (None, tr, cols)

    def body(me_ref, c_ref, p0, p1, p2, p3, _, o_ref):
        o_ref[...] = ((p0[...].astype(F32) + p1[...].astype(F32)) + p2[...].astype(F32)) + p3[...].astype(F32)

    def peer(flip):
        return lambda i, me, c: (me[0] ^ flip, i, 0)

    grid_spec = pltpu.PrefetchScalarGridSpec(
        num_scalar_prefetch=2, grid=(nb,), in_specs=[pl.BlockSpec(blk, peer(f)) for f in (0, 2, 1, 3)] + [ANY],
        out_specs=pl.BlockSpec(blk, lambda i, me, c: (layer, c[0] * nb + i, 0)))
    return _pcall(body, name=name, grid_spec=grid_spec, out_shape=jax.ShapeDtypeStruct(into.shape, F32),
                  input_output_aliases={6: 0}, compiler_params=_cp(("parallel",)))(*place, own, recv, recv, recv, into)


def _place():
    x, y, c = lax.axis_index("x"), lax.axis_index("y"), lax.axis_index("c")
    others = [(1 - x, y), (x, 1 - y), (1 - x, 1 - y)]
    return x, y, c, others


def _chip_id(chip):
    return 2 * chip[0] + chip[1]


def _comm_call(body, name, n_in, out_shapes, n_sems, in_place=False):
    return _pcall(body, name=name, in_specs=[ANY] * n_in, out_specs=[ANY] * len(out_shapes), out_shape=out_shapes,
                  scratch_shapes=[pltpu.SemaphoreType.DMA((n_sems,)), pltpu.SemaphoreType.DMA((n_sems,))],
                  input_output_aliases={t: t for t in range(n_in)} if in_place else {},
                  compiler_params=pltpu.CompilerParams(has_side_effects=True))


def swap_core_halves(grads, name):
    n = len(grads)

    def body(*refs):
        ins, outs = refs[:n], refs[n:2 * n]
        send, recv = refs[2 * n:]
        x, y, c, _ = _place()
        cps = []
        for t in range(n):
            hr = ins[t].shape[1] // 2
            cp = pltpu.make_async_remote_copy(src_ref=ins[t].at[:, pl.ds((1 - c) * hr, hr)], dst_ref=outs[t],
                                              send_sem=send.at[t], recv_sem=recv.at[t],
                                              device_id=(x, y, 1 - c), device_id_type=MESH)
            cp.start()
            cps.append(cp)
        for cp in cps:
            cp.wait()

    outs = [jax.ShapeDtypeStruct((a.shape[0], a.shape[1] // 2) + a.shape[2:], a.dtype) for a in grads]
    return _comm_call(body, name, n, outs, n)(*grads)


HBM = pl.BlockSpec(memory_space=pltpu.HBM)
SEM = pl.BlockSpec(memory_space=pltpu.SEMAPHORE)
DATAFLOW = pltpu.SideEffectType.DATAFLOW_SIDE_EFFECTING


def _in_hbm(a):
    return pltpu.with_memory_space_constraint(a, pltpu.HBM)


def gather_start(slots, name):
    n = len(slots)

    def body(*refs):
        bufs = refs[:n]
        send, recv = refs[n], refs[n + 1]
        token = refs[-1]
        x, y, c, others = _place()
        me = _chip_id((x, y))
        for t in range(n):
            for j, chip in enumerate(others):
                pltpu.make_async_remote_copy(src_ref=bufs[t].at[me], dst_ref=bufs[t].at[me],
                                             send_sem=send.at[3 * t + j], recv_sem=recv.at[3 * t + j],
                                             device_id=(*chip, c), device_id_type=MESH).start()
        token[...] = jnp.zeros_like(token)

    sems = pltpu.SemaphoreType.DMA((3 * n,))
    res = _pcall(body, name=name, in_specs=[HBM] * n,
                 out_shape=(sems, sems, *[pltpu.HBM(a.shape, a.dtype) for a in slots], jax.ShapeDtypeStruct((8, LANES), F32)),
                 out_specs=(SEM, SEM, *[HBM] * n, pl.BlockSpec(memory_space=pltpu.VMEM)),
                 input_output_aliases={i: 2 + i for i in range(n)},
                 compiler_params=pltpu.CompilerParams(has_side_effects=DATAFLOW))(*[_in_hbm(a) for a in slots])
    return res[0], res[1], list(res[2:2 + n]), res[-1]


def gather_wait(send, recv, slot, t, after, name):
    def body(buf, send_ref, recv_ref, after_ref, out):
        x, y, c, others = _place()
        me = _chip_id((x, y))
        for j, chip in enumerate(others):
            pltpu.make_async_remote_copy(src_ref=buf.at[me], dst_ref=buf.at[_chip_id(chip)],
                                         send_sem=send_ref.at[3 * t + j], recv_sem=recv_ref.at[3 * t + j],
                                         device_id=(*chip, c), device_id_type=MESH).wait()

    return _pcall(body, name=name, in_specs=[HBM, SEM, SEM, ANY], out_shape=pltpu.HBM(slot.shape, slot.dtype),
                  out_specs=HBM, input_output_aliases={0: 0},
                  compiler_params=pltpu.CompilerParams(has_side_effects=DATAFLOW))(slot, send, recv, after)


def scatter_start(sums, name):
    n = len(sums)

    def body(*refs):
        ins, lands = refs[:n], refs[n:2 * n]
        send, recv = refs[2 * n], refs[2 * n + 1]
        token = refs[-1]
        x, y, c, others = _place()
        me = _chip_id((x, y))
        for t in range(n):
            for j, chip in enumerate(others):
                pltpu.make_async_remote_copy(src_ref=ins[t].at[_chip_id(chip)], dst_ref=lands[t].at[me],
                                             send_sem=send.at[3 * t + j], recv_sem=recv.at[3 * t + j],
                                             device_id=(*chip, c), device_id_type=MESH).start()
        token[...] = jnp.zeros_like(token)

    bufs = [pltpu.HBM(a.shape, a.dtype) for a in sums]
    sems = pltpu.SemaphoreType.DMA((3 * n,))
    res = _pcall(body, name=name, in_specs=[HBM] * (2 * n),
                 out_shape=(sems, sems, *bufs, *bufs, jax.ShapeDtypeStruct((8, LANES), F32)),
                 out_specs=(SEM, SEM, *[HBM] * (2 * n), pl.BlockSpec(memory_space=pltpu.VMEM)),
                 input_output_aliases={i: 2 + i for i in range(2 * n)},
                 compiler_params=pltpu.CompilerParams(has_side_effects=DATAFLOW))(
        *[_in_hbm(a) for a in sums], *[_in_hbm(lax.empty(a.shape, a.dtype)) for a in sums])
    return res[0], res[1], list(res[2:2 + n]), list(res[2 + n:2 + 2 * n]), res[-1]


def scatter_wait(send, recv, sums, lands, after, name):
    n = len(sums)

    def body(*refs):
        ins, bufs = refs[:n], refs[n:2 * n]
        send_ref, recv_ref = refs[2 * n], refs[2 * n + 1]
        x, y, c, others = _place()
        me = _chip_id((x, y))
        for t in range(n):
            for j, chip in enumerate(others):
                cp = pltpu.make_async_remote_copy(src_ref=ins[t].at[_chip_id(chip)], dst_ref=bufs[t].at[_chip_id(chip)],
                                                  send_sem=send_ref.at[3 * t + j], recv_sem=recv_ref.at[3 * t + j],
                                                  device_id=(*chip, c), device_id_type=MESH)
                cp.wait_send()
                cp.wait_recv()

    shapes = [pltpu.HBM(a.shape, a.dtype) for a in sums]
    res = _pcall(body, name=name, in_specs=[HBM] * (2 * n) + [SEM, SEM, ANY],
                 out_shape=(*shapes, *shapes), out_specs=tuple([HBM] * (2 * n)),
                 input_output_aliases={i: i for i in range(2 * n)},
                 compiler_params=pltpu.CompilerParams(has_side_effects=DATAFLOW))(*sums, *lands, send, recv, after)
    return list(res[:n]), list(res[n:])


def join_core_halves(fulls, name):
    n = len(fulls)

    def body(*refs):
        bufs = refs[n:2 * n]
        send, recv = refs[2 * n:]
        x, y, c, _ = _place()
        cps = []
        for t in range(n):
            hr = bufs[t].shape[1] // 2
            mine = bufs[t].at[:, pl.ds(c * hr, hr)]
            cp = pltpu.make_async_remote_copy(src_ref=mine, dst_ref=mine, send_sem=send.at[t], recv_sem=recv.at[t],
                                              device_id=(x, y, 1 - c), device_id_type=MESH)
            cp.start()
            cps.append(cp)
        for t in range(n):
            hr = bufs[t].shape[1] // 2
            theirs = bufs[t].at[:, pl.ds((1 - c) * hr, hr)]
            cps[t].wait_send()
            pltpu.make_async_remote_copy(src_ref=theirs, dst_ref=theirs, send_sem=send.at[t], recv_sem=recv.at[t],
                                         device_id=(x, y, c), device_id_type=MESH).wait_recv()

    outs = [jax.ShapeDtypeStruct(a.shape, a.dtype) for a in fulls]
    return _comm_call(body, name, n, outs, n, in_place=True)(*fulls)


def kernel(x, positions, norm_gains, swa_w_in, swa_sinks, swa_w_out, fox_w_in, fox_b_f, fox_w_out, ffn_w_gate_up, ffn_w_down, loss_target, m_norm_gains, m_swa_w_in, m_swa_sinks, m_swa_w_out, m_fox_w_in, m_fox_b_f, m_fox_w_out, m_ffn_w_gate_up, m_ffn_w_down, v_norm_gains, v_swa_w_in, v_swa_sinks, v_swa_w_out, v_fox_w_in, v_fox_b_f, v_fox_w_out, v_ffn_w_gate_up, v_ffn_w_down):
    s, d = x.shape[1], x.shape[2]
    depth = norm_gains.shape[0]
    n_heads = d // HEAD_DIM
    hd = n_heads * HEAD_DIM
    n_kv = (swa_w_in.shape[2] * N_CHIPS // HEAD_DIM - n_heads) // 2
    ff = ffn_w_down.shape[1] * N_CHIPS
    fox_cols = fox_w_in.shape[2]
    fox_pad = 3 * hd + LANES
    assert fox_cols * N_CHIPS == 3 * hd + n_heads and n_heads <= LANES
    x0 = x[0]
    target = loss_target[0]
    place = ((2 * lax.axis_index("x") + lax.axis_index("y")).astype(jnp.int32).reshape(1),
             lax.axis_index("c").astype(jnp.int32).reshape(1))

    order = [("norm_gains", norm_gains.reshape(1, depth * 4, norm_gains.shape[2]), 0, F32)]
    for layer in range(depth):
        j = layer // 2
        kind, w_i, w_o = ("swa", swa_w_in, swa_w_out) if layer % 2 == 0 else ("fox", fox_w_in, fox_w_out)
        order += [(f"{kind}_w_in_{j}", w_i, j, BF16), (f"{kind}_w_out_{j}", w_o, j, BF16),
                  (f"ffn_w_gate_up_{layer}", ffn_w_gate_up, layer, BF16), (f"ffn_w_down_{layer}", ffn_w_down, layer, BF16)]
    slots = [cast_into_slot(w, l, place, dt, "cast_" + nm) for nm, w, l, dt in order]
    g_send, g_recv, slots, g_token = gather_start(slots, "gather_start")
    slot_of = {entry[0]: t for t, entry in enumerate(order)}

    def weight(nm, after):
        t = slot_of[nm]
        return gather_wait(g_send, g_recv, slots[t], t, after, "gather_wait_" + nm)

    gains = jnp.transpose(weight("norm_gains", g_token), (1, 0, 2)).reshape(depth * 4, d)

    def gain(layer, which):
        return gains[layer * 4 + which][None, :]

    def fox_weight(w_in):
        parts = [w_in[q] for q in range(N_CHIPS)]
        parts.append(jnp.zeros((d, fox_pad - fox_cols * N_CHIPS), BF16))
        return jnp.concatenate(parts, axis=1)

    inv_freq = ROPE_THETA ** (-jnp.arange(0, ROT_DIM, 2, dtype=F32) / ROT_DIM)
    lane_d = jnp.arange(LANES) % HEAD_DIM
    invf_row = jnp.where(lane_d < ROT_DIM, inv_freq[lane_d % (ROT_DIM // 2)], 0.0)[None, :]
    tabs = rope_tables(positions.reshape(s, 1), invf_row, "rope_tables")

    n_sh_in = swa_w_in.shape[2]
    gu_sh = ffn_w_gate_up.shape[2]
    tn_gu = gu_sh // 2 if (gu_sh // 2) % LANES == 0 else gu_sh
    down_sh = ffn_w_down.shape[1]
    out_sh = swa_w_out.shape[1]
    tm = min(1024, s)

    saved = []
    xin = x0
    h = prenorm(xin, gain(0, 0), "prenorm_first")
    for layer in range(depth):
        j = layer // 2
        rec = {"x_in": xin, "h1": h}
        if layer % 2 == 0:
            w_in = weight(f"swa_w_in_{j}", h)
            proj = mm_nn(h, w_in, lambda n, k: (n, k, 0), n_sh_in * N_CHIPS, tm=tm, tn=n_sh_in, tk=d,
                         out_dtype=F32, name=f"swa_proj_{j}")
            q, kd, vd = swa_split(proj, tabs, n_heads, n_kv, f"swa_split_{j}")
            sink_row = jnp.repeat(swa_sinks[j], HEAD_DIM)[None, :]
            attn, lse = swa_fwd(q, kd, vd, sink_row, f"swa_fwd_{j}")
            rec.update(q=q, kd=kd, vd=vd, sink_row=sink_row, lse=lse, w_in=w_in)
            w_out = weight(f"swa_w_out_{j}", attn)
        else:
            wf = fox_weight(weight(f"fox_w_in_{j}", h))
            tn = 3 * hd // 6 if (3 * hd // 6) % LANES == 0 else LANES
            qkv = mm_nn(h, wf, lambda n, k: (k, n), 3 * hd, tm=tm, tn=tn, tk=d, out_dtype=BF16, name=f"fox_proj_{j}")
            f_off = 3 * hd // LANES
            f_logit = mm_nn(h, wf, lambda n, k: (k, f_off + n), LANES, tm=tm, tn=LANES, tk=d, out_dtype=F32, name=f"fox_gate_{j}")
            b_row = jnp.pad(fox_b_f[j], (0, LANES - n_heads))[None, :]
            c = forget_cumsum(f_logit, b_row, f"fox_cumsum_{j}")
            qs, ks, vs = fox_split(qkv, c, n_heads, f"fox_split_{j}")
            attn, qb = fox_fwd(qs, ks, vs, n_heads, f"fox_fwd_{j}")
            rec.update(wf=wf, f_logit=f_logit, b_row=b_row, qb=qb, ks=ks, vs=vs)
            w_out = weight(f"fox_w_out_{j}", attn)
        y = mm_nn(attn, w_out, lambda n, k: (k, 0, n), d, tm=tm, tn=d, tk=out_sh, out_dtype=F32, name=f"out_proj_{layer}")
        xmid, h2 = postnorm_residual(xin, y, gain(layer, 1), gain(layer, 2), f"postnorm_mixer_{layer}")
        npb = gu_sh // tn_gu
        w_gu = weight(f"ffn_w_gate_up_{layer}", h2)
        nbf = ff // tn_gu
        gate, up, act = ffn_up_swiglu(h2, w_gu, lambda n: (n // npb, 0, n % npb), lambda n: ((n + nbf) // npb, 0, (n + nbf) % npb),
                                      ff, tm=min(512, s), tn=tn_gu, name=f"ffn_up_{layer}")
        w_down = weight(f"ffn_w_down_{layer}", act)
        y2 = mm_nn(act, w_down, lambda n, k: (k, 0, n), d, tm=tm, tn=d, tk=down_sh, out_dtype=F32, name=f"ffn_down_{layer}")
        rec.update(attn=attn, y=y, x_mid=xmid, h2=h2, gate=gate, up=up, act=act, y2=y2, w_out=w_out, w_gu=w_gu, w_down=w_down)
        saved.append(rec)
        if layer + 1 < depth:
            xin, h = postnorm_residual(xmid, y2, gain(layer, 3), gain(layer + 1, 0), f"postnorm_ffn_{layer}")
    dx, loss_blk = postnorm_loss(xmid, y2, gain(depth - 1, 3), target, "loss")

    finals = {nm: lax.empty(w.shape, F32) for nm, w in (("swa_w_in", swa_w_in), ("swa_w_out", swa_w_out), ("fox_w_in", fox_w_in),
                                                        ("fox_w_out", fox_w_out), ("ffn_w_gate_up", ffn_w_gate_up),
                                                        ("ffn_w_down", ffn_w_down), ("norm_gains", order[0][1]))}
    in_flight = []

    def finish_reduce(after):
        send_, recv_, sums_, lands_, keys = in_flight.pop()
        sums_, got = scatter_wait(send_, recv_, sums_, lands_, after, "reduce_scatter_wait_" + keys[0][2])
        for own, rcv, (key, l, nm) in zip(sums_, got, keys):
            finals[key] = add_chips(own, rcv, finals[key], l, place, "reduce_add_chips_" + nm)

    def start_reduce(partials, keys):
        if in_flight:
            finish_reduce(partials[0])
        halves = swap_core_halves(partials, "reduce_swap_" + keys[0][2])
        sums_ = [add_core_halves(g, r, place, "reduce_add_cores_" + k[2]) for g, r, k in zip(partials, halves, keys)]
        send_, recv_, sums_, lands_, token = scatter_start(sums_, "reduce_scatter_start_" + keys[0][2])
        in_flight.append((send_, recv_, sums_, lands_, keys))
        return token

    dgains = [None] * (depth * 4)
    dsinks = [None] * ((depth + 1) // 2)
    dbf = [None] * (depth // 2)
    tko = d
    for layer in reversed(range(depth)):
        j = layer // 2
        rec = saved[layer]
        dy2, dgains[layer * 4 + 3] = postnorm_bwd(dx, rec["y2"], gain(layer, 3), f"postnorm_ffn_bwd_{layer}")
        tnd = d
        g_down = mm_tn(rec["act"], dy2, lambda i, n: (i, 0, n), tka=down_sh, tn=tnd, tm=tm, out_block=(None, down_sh, tnd),
                       name=f"ffn_down_dw_{layer}", out_shape=jax.ShapeDtypeStruct((N_CHIPS, down_sh, d), BF16))
        dgu = ffn_down_dx_swiglu(dy2, rec["w_down"], lambda n: (n, 0, 0), rec["gate"], rec["up"], tm=min(512, s), tn=down_sh,
                                 name=f"ffn_down_dx_{layer}")
        npb = gu_sh // tn_gu
        g_gu = mm_tn(rec["h2"], dgu, lambda i, n: (n // npb, i, n % npb), tka=tko, tn=tn_gu, tm=tm, out_block=(None, tko, tn_gu),
                     name=f"ffn_up_dw_{layer}", out_shape=jax.ShapeDtypeStruct((N_CHIPS, d, gu_sh), BF16))
        sent = start_reduce([g_gu, g_down], [("ffn_w_gate_up", layer, f"ffn_w_gate_up_{layer}"), ("ffn_w_down", layer, f"ffn_w_down_{layer}")])
        dh2 = mm_nt(dgu, rec["w_gu"], lambda o, n: (n // npb, o, n % npb), d, tm=tm, tko=tko, tn=tn_gu, out_dtype=F32,
                    name=f"ffn_up_dx_{layer}", after=sent)
        dxm, dgains[layer * 4 + 2] = prenorm_bwd(rec["x_mid"], dh2, dx, gain(layer, 2), f"prenorm_ffn_bwd_{layer}")
        dy, dgains[layer * 4 + 1] = postnorm_bwd(dxm, rec["y"], gain(layer, 1), f"postnorm_mixer_bwd_{layer}")
        g_out = mm_tn(rec["attn"], dy, lambda i, n: (i, 0, n), tka=out_sh, tn=tnd, tm=min(2048, s), out_block=(None, out_sh, tnd),
                      name=f"out_proj_dw_{layer}", out_shape=jax.ShapeDtypeStruct((N_CHIPS, out_sh, d), BF16))
        dattn = mm_nt(dy, rec["w_out"], lambda o, n: (o, 0, n), hd, tm=tm, tko=out_sh, tn=d, out_dtype=BF16, name=f"out_proj_dx_{layer}")
        if layer % 2 == 0:
            dq, dkd, dvd, dsk = swa_bwd(rec["q"], rec["kd"], rec["vd"], rec["sink_row"], rec["attn"], rec["lse"], dattn, f"swa_bwd_{j}")
            dsinks[j] = dsk[0].reshape(n_heads, HEAD_DIM)[:, 0]
            dproj = swa_merge_bwd(dq, dkd, dvd, tabs, f"swa_merge_bwd_{j}")
            g_in = mm_tn(rec["h1"], dproj, lambda i, n: (n, i, 0), tka=tko, tn=n_sh_in, tm=tm, out_block=(None, tko, n_sh_in),
                         name=f"swa_proj_dw_{j}", out_shape=jax.ShapeDtypeStruct((N_CHIPS, d, n_sh_in), BF16))
            sent = start_reduce([g_in, g_out], [("swa_w_in", j, f"swa_w_in_{j}"), ("swa_w_out", j, f"swa_w_out_{j}")])
            dh1 = mm_nt(dproj, rec["w_in"], lambda o, n: (n, o, 0), d, tm=tm, tko=tko, tn=n_sh_in, out_dtype=F32,
                        name=f"swa_proj_dx_{j}", after=sent)
        else:
            dos = fox_dout_slots(dattn, rec["attn"], f"fox_dout_slots_{j}")
            dqs, dks, dvs = fox_bwd(rec["qb"], rec["ks"], rec["vs"], dos, n_heads, f"fox_bwd_{j}")
            dqkv, dc = fox_merge_bwd(dqs, dks, dvs, n_heads, f"fox_merge_bwd_{j}")
            df, db = forget_gate_bwd(dc, rec["f_logit"], rec["b_row"], f"fox_gate_bwd_{j}")
            dbf[j] = db[0, :n_heads]
            dproj = jnp.concatenate([dqkv, df], axis=1)
            tn_f = LANES * max(k for k in range(1, 9) if (fox_pad // LANES) % k == 0)
            dwf = mm_tn(rec["h1"], dproj, lambda i, n: (i, n), tka=tko, tn=tn_f, tm=tm, out_block=(tko, tn_f),
                        name=f"fox_proj_dw_{j}", out_shape=jax.ShapeDtypeStruct((d, fox_pad), BF16))
            g_in = jnp.stack([dwf[:, q * fox_cols:(q + 1) * fox_cols] for q in range(N_CHIPS)])
            sent = start_reduce([g_in, g_out], [("fox_w_in", j, f"fox_w_in_{j}"), ("fox_w_out", j, f"fox_w_out_{j}")])
            dh1 = mm_nt(dproj, rec["wf"], lambda o, n: (o, n), d, tm=tm, tko=tko, tn=tn_f, out_dtype=F32,
                        name=f"fox_proj_dx_{j}", after=sent)
        dx, dgains[layer * 4] = prenorm_bwd(rec["x_in"], dh1, dxm, gain(layer, 0), f"prenorm_mixer_bwd_{layer}")
    grad_x = dx[None]

    dgain_full = jnp.concatenate(dgains, axis=0)
    start_reduce([jnp.transpose(dgain_full.reshape(depth * 4, N_CHIPS, -1), (1, 0, 2))], [("norm_gains", 0, "norm_gains")])
    finish_reduce(dx)
    keys = ["swa_w_in", "swa_w_out", "fox_w_in", "fox_w_out", "ffn_w_gate_up", "ffn_w_down", "norm_gains"]
    full = join_core_halves([finals[k] for k in keys], "reduce_join_cores")
    g_swa_in_f, g_swa_out_f, g_fox_in_f, g_fox_out_f, g_gu_f, g_down_f, g_gains_f = full
    g_gains_f = g_gains_f.reshape(norm_gains.shape)

    n_swa, n_fox = len(dsinks), len(dbf)
    small = jnp.concatenate([loss_blk[0, :1]] + dsinks + dbf)
    small = lax.psum(small, ("x", "y", "c"))
    loss = small[0]
    g_sinks = small[1:1 + n_swa * n_heads].reshape(n_swa, n_heads)
    g_bf = small[1 + n_swa * n_heads:].reshape(n_fox, n_heads)

    def pad_small(a):
        return jnp.pad(a, ((0, 8 - a.shape[0]), (0, LANES - a.shape[1])))[None]

    def update(w, g, m, v, nm):
        if w.ndim == 2:
            dl, mn, vn = adamw(pad_small(w), pad_small(g), pad_small(m), pad_small(v), "adamw_" + nm)
            return tuple(a[0, :w.shape[0], :w.shape[1]] for a in (dl, mn, vn))
        if w.shape[2] % LANES and w.shape[1] % LANES == 0:
            turn = lambda a: jnp.swapaxes(a, 1, 2)
            return tuple(turn(a) for a in adamw(turn(w), turn(g), turn(m), turn(v), "adamw_" + nm))
        return adamw(w, g, m, v, "adamw_" + nm)

    grads = [g_gains_f, g_swa_in_f, g_sinks, g_swa_out_f, g_fox_in_f, g_bf, g_fox_out_f, g_gu_f, g_down_f]
    ws = [norm_gains, swa_w_in, swa_sinks, swa_w_out, fox_w_in, fox_b_f, fox_w_out, ffn_w_gate_up, ffn_w_down]
    ms = [m_norm_gains, m_swa_w_in, m_swa_sinks, m_swa_w_out, m_fox_w_in, m_fox_b_f, m_fox_w_out, m_ffn_w_gate_up, m_ffn_w_down]
    vs = [v_norm_gains, v_swa_w_in, v_swa_sinks, v_swa_w_out, v_fox_w_in, v_fox_b_f, v_fox_w_out, v_ffn_w_gate_up, v_ffn_w_down]
    nms = ["norm_gains", "swa_w_in", "swa_sinks", "swa_w_out", "fox_w_in", "fox_b_f", "fox_w_out", "ffn_w_gate_up", "ffn_w_down"]
    upd = [update(w, g, m, v, nm) for w, g, m, v, nm in zip(ws, grads, ms, vs, nms)]
    return (loss, grad_x, *grads, *[u[0] for u in upd], *[u[1] for u in upd], *[u[2] for u in upd])
```

```python
import jax
import jax.numpy as jnp
from jax import lax
from jax.experimental import pallas as pl
from jax.experimental.pallas import tpu as pltpu

F32 = jnp.float32
BF16 = jnp.bfloat16
HEAD_DIM = 64
LANES = 128
WINDOW = 128
ROPE_THETA = 500000.0
ROT_DIM = HEAD_DIM // 4
RMS_EPS = 1e-6
ADAM_LR, ADAM_B1, ADAM_B2, ADAM_EPS, ADAM_WD, ADAM_STEP = 0.001, 0.9, 0.999, 1e-08, 0.01, 10
NEG = -1e30
VMEM_BIG = 56 * 1024 * 1024
N_CHIPS = 4
MESH = pl.DeviceIdType.MESH
ANY = pl.BlockSpec(memory_space=pl.ANY)


def _pcall(body, **kw):
    return pl.pallas_call(body, **kw)


def _cp(sem=None, vmem=None):
    return pltpu.CompilerParams(dimension_semantics=sem, vmem_limit_bytes=vmem)


def _tile(n, pref):
    if n <= pref:
        return n
    t = pref - pref % 16
    while n % t:
        t -= 16
    return t


_DIMS = {"nn": (((1,), (0,)), ((), ())), "nt": (((1,), (1,)), ((), ())), "tn": (((0,), (0,)), ((), ()))}


def _matmul(kind, a, b, *, grid, a_spec, b_spec, o_spec, out_shape, acc_shape, name, after=None):
    nk = grid[2]
    dims = _DIMS[kind]

    def body(a_ref, b_ref, *rest):
        rest = rest[1:] if after is not None else rest
        o_ref = rest[0]

        def prod():
            return lax.dot_general(a_ref[...], b_ref[...], dims, preferred_element_type=F32)

        if nk == 1:
            o_ref[...] = prod().astype(o_ref.dtype)
        else:
            acc = rest[1]
            k = pl.program_id(2)

            @pl.when(k == 0)
            def _():
                acc[...] = prod()

            @pl.when(k > 0)
            def _():
                acc[...] += prod()

            @pl.when(k == nk - 1)
            def _():
                o_ref[...] = acc[...].astype(o_ref.dtype)

    args, in_specs = [a, b], [a_spec, b_spec]
    if after is not None:
        args.append(after)
        in_specs.append(ANY)
    return _pcall(body, name=name, grid=grid, in_specs=in_specs, out_specs=o_spec, out_shape=out_shape,
                  scratch_shapes=[] if nk == 1 else [pltpu.VMEM(acc_shape, F32)],
                  compiler_params=_cp(("parallel", "parallel", "arbitrary"), VMEM_BIG))(*args)


def mm_nn(a, w, w_map, n_out, *, tm, tn, tk, out_dtype, name):
    m, kdim = a.shape
    tm = min(tm, m)
    lead = (None,) * (w.ndim - 2)
    return _matmul("nn", a, w, grid=(m // tm, n_out // tn, kdim // tk),
                   a_spec=pl.BlockSpec((tm, tk), lambda i, j, k: (i, k)),
                   b_spec=pl.BlockSpec(lead + (tk, tn), lambda i, j, k: w_map(j, k)),
                   o_spec=pl.BlockSpec((tm, tn), lambda i, j, k: (i, j)),
                   out_shape=jax.ShapeDtypeStruct((m, n_out), out_dtype), acc_shape=(tm, tn), name=name)


def _split_cols(a, tn):
    if a.ndim == 2:
        return a.shape[0], a.shape[1], (), lambda row, col: (row, col)
    per = a.shape[2] // tn
    return a.shape[1], a.shape[0] * a.shape[2], (None,), lambda row, col: (col // per, row, col % per)


def mm_nt(a, w, w_map, k_out, *, tm, tko, tn, out_dtype, name, after=None):
    m, ndim, a_lead, a_map = _split_cols(a, tn)
    tm = min(tm, m)
    lead = (None,) * (w.ndim - 2)
    return _matmul("nt", a, w, grid=(m // tm, k_out // tko, ndim // tn),
                   a_spec=pl.BlockSpec(a_lead + (tm, tn), lambda i, j, n: a_map(i, n)),
                   b_spec=pl.BlockSpec(lead + (tko, tn), lambda i, j, n: w_map(j, n)),
                   o_spec=pl.BlockSpec((tm, tko), lambda i, j, n: (i, j)),
                   out_shape=jax.ShapeDtypeStruct((m, k_out), out_dtype), acc_shape=(tm, tko), name=name, after=after)


def mm_tn(a, b, o_map, *, tka, tn, tm, out_block, name, out_shape):
    m, kdim = a.shape
    _, n, b_lead, b_map = _split_cols(b, tn)
    tm = min(tm, m)
    return _matmul("tn", a, b, grid=(kdim // tka, n // tn, m // tm),
                   a_spec=pl.BlockSpec((tm, tka), lambda i, j, mm: (mm, i)),
                   b_spec=pl.BlockSpec(b_lead + (tm, tn), lambda i, j, mm: b_map(mm, j)),
                   o_spec=pl.BlockSpec(out_block, lambda i, j, mm: o_map(i, j)),
                   out_shape=out_shape, acc_shape=(tka, tn), name=name)


def _rstd(v):
    return lax.rsqrt(jnp.mean(v * v, axis=-1, keepdims=True) + RMS_EPS)


def _row_spec(tr, d):
    return pl.BlockSpec((tr, d), lambda i: (i, 0))


def _vec_spec(d):
    return pl.BlockSpec((1, d), lambda i: (0, 0))


def prenorm(x, g, name):
    s, d = x.shape
    tr = _tile(s, 256)

    def body(x_ref, g_ref, h_ref):
        v = x_ref[...]
        h_ref[...] = (v * _rstd(v) * g_ref[...]).astype(BF16)

    return _pcall(body, name=name, grid=(s // tr,), in_specs=[_row_spec(tr, d), _vec_spec(d)],
                  out_specs=_row_spec(tr, d), out_shape=jax.ShapeDtypeStruct((s, d), BF16),
                  compiler_params=_cp(("parallel",)))(x, g)


def postnorm_residual(x, y, g_post, g_next, name):
    s, d = x.shape
    tr = _tile(s, 256)

    def body(x_ref, y_ref, gp_ref, gn_ref, xo_ref, h_ref):
        v = y_ref[...]
        xn = x_ref[...] + v * _rstd(v) * gp_ref[...]
        xo_ref[...] = xn
        h_ref[...] = (xn * _rstd(xn) * gn_ref[...]).astype(BF16)

    return _pcall(body, name=name, grid=(s // tr,),
                  in_specs=[_row_spec(tr, d), _row_spec(tr, d), _vec_spec(d), _vec_spec(d)],
                  out_specs=[_row_spec(tr, d), _row_spec(tr, d)],
                  out_shape=[jax.ShapeDtypeStruct((s, d), F32), jax.ShapeDtypeStruct((s, d), BF16)],
                  compiler_params=_cp(("parallel",)))(x, y, g_post, g_next)


def postnorm_loss(x, y, g_post, target, name):
    s, d = x.shape
    tr = _tile(s, 256)

    def body(x_ref, y_ref, gp_ref, t_ref, dx_ref, loss_ref):
        v = y_ref[...]
        err = x_ref[...] + v * _rstd(v) * gp_ref[...] - t_ref[...]
        dx_ref[...] = err / d
        part = 0.5 * jnp.sum(jnp.mean(err * err, axis=-1, keepdims=True), axis=0, keepdims=True)

        @pl.when(pl.program_id(0) == 0)
        def _():
            loss_ref[...] = jnp.zeros_like(loss_ref)

        loss_ref[...] += part

    return _pcall(body, name=name, grid=(s // tr,),
                  in_specs=[_row_spec(tr, d), _row_spec(tr, d), _vec_spec(d), _row_spec(tr, d)],
                  out_specs=[_row_spec(tr, d), pl.BlockSpec((8, LANES), lambda i: (0, 0))],
                  out_shape=[jax.ShapeDtypeStruct((s, d), F32), jax.ShapeDtypeStruct((8, LANES), F32)],
                  compiler_params=_cp(("arbitrary",)))(x, y, g_post, target)


def _norm_bwd(v, g, dz):
    r = _rstd(v)
    vhat = v * r
    u = dz * g
    dv = r * (u - vhat * jnp.mean(u * vhat, axis=-1, keepdims=True))
    return dv, jnp.sum(dz * vhat, axis=0, keepdims=True)


def _acc_rows(ref, val):
    @pl.when(pl.program_id(0) == 0)
    def _():
        ref[...] = jnp.zeros_like(ref)

    ref[...] += val


def postnorm_bwd(dz, y, g, name):
    s, d = y.shape
    tr = _tile(s, 256)

    def body(dz_ref, y_ref, g_ref, dy_ref, dg_ref):
        dv, dg = _norm_bwd(y_ref[...], g_ref[...], dz_ref[...])
        dy_ref[...] = dv.astype(BF16)
        _acc_rows(dg_ref, dg)

    return _pcall(body, name=name, grid=(s // tr,), in_specs=[_row_spec(tr, d), _row_spec(tr, d), _vec_spec(d)],
                  out_specs=[_row_spec(tr, d), _vec_spec(d)],
                  out_shape=[jax.ShapeDtypeStruct((s, d), BF16), jax.ShapeDtypeStruct((1, d), F32)],
                  compiler_params=_cp(("arbitrary",)))(dz, y, g)


def prenorm_bwd(x, dh, dskip, g, name):
    s, d = x.shape
    tr = _tile(s, 256)

    def body(x_ref, dh_ref, ds_ref, g_ref, dx_ref, dg_ref):
        dv, dg = _norm_bwd(x_ref[...], g_ref[...], dh_ref[...])
        dx_ref[...] = ds_ref[...] + dv
        _acc_rows(dg_ref, dg)

    return _pcall(body, name=name, grid=(s // tr,),
                  in_specs=[_row_spec(tr, d), _row_spec(tr, d), _row_spec(tr, d), _vec_spec(d)],
                  out_specs=[_row_spec(tr, d), _vec_spec(d)],
                  out_shape=[jax.ShapeDtypeStruct((s, d), F32), jax.ShapeDtypeStruct((1, d), F32)],
                  compiler_params=_cp(("arbitrary",)))(x, dh, dskip, g)


def ffn_up_swiglu(h, w, gate_map, up_map, ff, *, tm, tn, name):
    m, kdim = h.shape
    tm = min(tm, m)
    lead = (None,) * (w.ndim - 2)

    def body(h_ref, wg_ref, wu_ref, g_ref, u_ref, act_ref):
        a = h_ref[...]
        gate = jnp.dot(a, wg_ref[...], preferred_element_type=F32)
        up = jnp.dot(a, wu_ref[...], preferred_element_type=F32)
        g_ref[...] = gate.astype(BF16)
        u_ref[...] = up.astype(BF16)
        act_ref[...] = (gate * jax.nn.sigmoid(gate) * up).astype(BF16)

    out = jax.ShapeDtypeStruct((m, ff), BF16)
    oblk = pl.BlockSpec((tm, tn), lambda n, i: (i, n))
    return _pcall(body, name=name, grid=(ff // tn, m // tm),
                  in_specs=[pl.BlockSpec((tm, kdim), lambda n, i: (i, 0)), pl.BlockSpec(lead + (kdim, tn), lambda n, i: gate_map(n)),
                            pl.BlockSpec(lead + (kdim, tn), lambda n, i: up_map(n))],
                  out_specs=[oblk] * 3, out_shape=[out] * 3,
                  compiler_params=_cp(("parallel", "parallel"), VMEM_BIG))(h, w, w)


def ffn_down_dx_swiglu(dy, w, w_map, gate, up, *, tm, tn, name):
    m, d = dy.shape
    ff = gate.shape[1]
    tm = min(tm, m)
    lead = (None,) * (w.ndim - 2)

    def body(dy_ref, w_ref, g_ref, u_ref, o_ref):
        da = _nt(dy_ref[...], w_ref[...])
        gt = g_ref[...].astype(F32)
        sig = jax.nn.sigmoid(gt)
        o_ref[0] = (da * u_ref[...].astype(F32) * sig * (1.0 + gt * (1.0 - sig))).astype(BF16)
        o_ref[1] = (da * gt * sig).astype(BF16)

    blk = pl.BlockSpec((tm, tn), lambda n, i: (i, n))
    return _pcall(body, name=name, grid=(ff // tn, m // tm),
                  in_specs=[pl.BlockSpec((tm, d), lambda n, i: (i, 0)), pl.BlockSpec(lead + (tn, d), lambda n, i: w_map(n)), blk, blk],
                  out_specs=pl.BlockSpec((2, tm, tn), lambda n, i: (0, i, n)), out_shape=jax.ShapeDtypeStruct((2, m, ff), BF16),
                  compiler_params=_cp(("parallel", "parallel"), VMEM_BIG))(dy, w, gate, up)


def rope_tables(pos_col, invf_row, name):
    s = pos_col.shape[0]
    tr = _tile(s, 1024)

    def body(p_ref, f_ref, c_ref, sa_ref, sb_ref):
        ang = p_ref[...].astype(F32) * f_ref[...]
        d = lax.broadcasted_iota(jnp.int32, (1, LANES), 1) % HEAD_DIM
        cs, sn = jnp.cos(ang), jnp.sin(ang)
        c_ref[...] = jnp.where(d < ROT_DIM, cs, 1.0)
        sa_ref[...] = jnp.where(d < ROT_DIM // 2, -sn, 0.0)
        sb_ref[...] = jnp.where((d >= ROT_DIM // 2) & (d < ROT_DIM), sn, 0.0)

    tab = jax.ShapeDtypeStruct((s, LANES), F32)
    return _pcall(body, name=name, grid=(s // tr,),
                  in_specs=[pl.BlockSpec((tr, 1), lambda i: (i, 0)), _vec_spec(LANES)],
                  out_specs=[_row_spec(tr, LANES)] * 3, out_shape=[tab] * 3, compiler_params=_cp(("parallel",)))(pos_col, invf_row)


def _rot(t, c, sa, sb):
    half = ROT_DIM // 2
    return t * c + pltpu.roll(t, LANES - half, 1) * sa + pltpu.roll(t, half, 1) * sb


def _rot_t(t, c, sa, sb):
    half = ROT_DIM // 2
    return t * c + pltpu.roll(t * sa, half, 1) + pltpu.roll(t * sb, LANES - half, 1)


def swa_split(proj, tabs, n_heads, n_kv, name):
    s, width = proj.shape
    qd, kd = n_heads * HEAD_DIM, n_kv * HEAD_DIM
    tr = _tile(s, 256)

    def body(p_ref, c_ref, sa_ref, sb_ref, q_ref, k_ref, v_ref):
        c, sa, sb = c_ref[...], sa_ref[...], sb_ref[...]
        low = lax.broadcasted_iota(jnp.int32, (1, LANES), 1) < HEAD_DIM
        for g in range(qd // LANES):
            q_ref[:, g * LANES:(g + 1) * LANES] = _rot(p_ref[:, g * LANES:(g + 1) * LANES], c, sa, sb).astype(BF16)
        for g in range(kd // LANES):
            for src, dst, rot in ((qd, k_ref, True), (qd + kd, v_ref, False)):
                t = p_ref[:, src + g * LANES:src + (g + 1) * LANES]
                t = _rot(t, c, sa, sb) if rot else t
                sw = pltpu.roll(t, HEAD_DIM, 1)
                dst[:, (2 * g) * LANES:(2 * g + 1) * LANES] = jnp.where(low, t, sw).astype(BF16)
                dst[:, (2 * g + 1) * LANES:(2 * g + 2) * LANES] = jnp.where(low, sw, t).astype(BF16)

    return _pcall(body, name=name, grid=(s // tr,),
                  in_specs=[_row_spec(tr, width)] + [_row_spec(tr, LANES)] * 3,
                  out_specs=[_row_spec(tr, qd), _row_spec(tr, 2 * kd), _row_spec(tr, 2 * kd)],
                  out_shape=[jax.ShapeDtypeStruct((s, qd), BF16), jax.ShapeDtypeStruct((s, 2 * kd), BF16),
                             jax.ShapeDtypeStruct((s, 2 * kd), BF16)],
                  compiler_params=_cp(("parallel",)))(proj, *tabs)


def swa_merge_bwd(dq, dkd, dvd, tabs, name):
    s, qd = dq.shape
    kd = dkd.shape[1] // 2
    width = qd + 2 * kd
    tr = _tile(s, 256)

    def body(dq_ref, dk_ref, dv_ref, c_ref, sa_ref, sb_ref, o_ref):
        c, sa, sb = c_ref[...], sa_ref[...], sb_ref[...]
        low = lax.broadcasted_iota(jnp.int32, (1, LANES), 1) < HEAD_DIM
        for g in range(qd // LANES):
            t = dq_ref[:, g * LANES:(g + 1) * LANES].astype(F32)
            o_ref[:, g * LANES:(g + 1) * LANES] = _rot_t(t, c, sa, sb).astype(BF16)
        for g in range(kd // LANES):
            for dst, src, rot in ((qd, dk_ref, True), (qd + kd, dv_ref, False)):
                e = src[:, (2 * g) * LANES:(2 * g + 1) * LANES]
                o = src[:, (2 * g + 1) * LANES:(2 * g + 2) * LANES]
                t = jnp.where(low, e + pltpu.roll(e, HEAD_DIM, 1), o + pltpu.roll(o, HEAD_DIM, 1))
                t = _rot_t(t, c, sa, sb) if rot else t
                o_ref[:, dst + g * LANES:dst + (g + 1) * LANES] = t.astype(BF16)

    return _pcall(body, name=name, grid=(s // tr,),
                  in_specs=[_row_spec(tr, qd), _row_spec(tr, 2 * kd), _row_spec(tr, 2 * kd)] + [_row_spec(tr, LANES)] * 3,
                  out_specs=_row_spec(tr, width), out_shape=jax.ShapeDtypeStruct((s, width), BF16),
                  compiler_params=_cp(("parallel",)))(dq, dkd, dvd, *tabs)


def _halves():
    lane_half = lax.broadcasted_iota(jnp.int32, (1, LANES), 1) // HEAD_DIM
    return [lane_half == 0, lane_half == 1]


def _nt(a, b):
    return lax.dot_general(a, b, _DIMS["nt"], preferred_element_type=F32)


def _tn(a, b):
    return lax.dot_general(a, b, _DIMS["tn"], preferred_element_type=F32)


def _band_mask(i, tq):
    w = tq + WINDOW
    rel = lax.broadcasted_iota(jnp.int32, (tq, w), 1) - lax.broadcasted_iota(jnp.int32, (tq, w), 0)
    first = lax.broadcasted_iota(jnp.int32, (tq, w), 1) >= jnp.where(i > 0, 0, WINDOW)
    return (rel >= 1) & (rel <= WINDOW) & first


def swa_fwd(q, kd, vd, sink_row, name):
    s, qd = q.shape
    tq = min(256, s)
    r = tq // WINDOW
    pairs = qd // LANES
    group_pairs = pairs // (kd.shape[1] // LANES)
    scale = HEAD_DIM ** -0.5

    def body(q_ref, kp_ref, kc_ref, vp_ref, vc_ref, sk_ref, o_ref, lse_ref):
        i = pl.program_id(1)
        k = jnp.concatenate([kp_ref[...], kc_ref[...]], axis=0)
        v = jnp.concatenate([vp_ref[...], vc_ref[...]], axis=0)
        mask = _band_mask(i, tq)
        q2 = q_ref[...]
        outs, lses = [], []
        for a, hm in enumerate(_halves()):
            sc = _nt(jnp.where(hm, q2, jnp.zeros_like(q2)), k) * scale
            sc = jnp.where(mask, sc, NEG)
            sink = sk_ref[:, a * HEAD_DIM:a * HEAD_DIM + 1]
            m = jnp.maximum(jnp.max(sc, axis=-1, keepdims=True), sink)
            p = jnp.exp(sc - m)
            den = jnp.sum(p, axis=-1, keepdims=True) + jnp.exp(sink - m)
            outs.append(jnp.dot(p.astype(BF16), v, preferred_element_type=F32) / den)
            lses.append(m + jnp.log(den))
        hm0 = _halves()[0]
        o_ref[...] = jnp.where(hm0, outs[0], outs[1]).astype(BF16)
        lse_ref[...] = jnp.where(hm0, lses[0], lses[1])

    prev = lambda p, i: (jnp.maximum(i * r - 1, 0), p // group_pairs)
    cur = lambda p, i: (i, p // group_pairs)
    blk = pl.BlockSpec((tq, LANES), lambda p, i: (i, p))
    return _pcall(body, name=name, grid=(pairs, s // tq),
                  in_specs=[blk, pl.BlockSpec((WINDOW, LANES), prev), pl.BlockSpec((tq, LANES), cur),
                            pl.BlockSpec((WINDOW, LANES), prev), pl.BlockSpec((tq, LANES), cur),
                            pl.BlockSpec((1, LANES), lambda p, i: (0, p))],
                  out_specs=[blk, blk],
                  out_shape=[jax.ShapeDtypeStruct((s, qd), BF16), jax.ShapeDtypeStruct((s, qd), F32)],
                  compiler_params=_cp(("parallel", "parallel")))(q, kd, kd, vd, vd, sink_row)


def swa_bwd(q, kd, vd, sink_row, out, lse, dout, name):
    s, qd = q.shape
    tq = min(256, s)
    r = tq // WINDOW
    n_kv = kd.shape[1] // LANES
    gw = qd // n_kv
    scale = HEAD_DIM ** -0.5

    def body(q_ref, kp_ref, kc_ref, vp_ref, vc_ref, sk_ref, o_ref, lse_ref, do_ref, dq_ref, dk_ref, dv_ref, dsk_ref):
        i = pl.program_id(1)

        @pl.when(i == 0)
        def _():
            dk_ref[...] = jnp.zeros_like(dk_ref)
            dv_ref[...] = jnp.zeros_like(dv_ref)
            dsk_ref[...] = jnp.zeros_like(dsk_ref)

        k = jnp.concatenate([kp_ref[...], kc_ref[...]], axis=0)
        v = jnp.concatenate([vp_ref[...], vc_ref[...]], axis=0)
        mask = _band_mask(i, tq)
        dk = jnp.zeros((tq + WINDOW, LANES), F32)
        dv = jnp.zeros((tq + WINDOW, LANES), F32)
        for pp in range(gw // LANES):
            cols = slice(pp * LANES, (pp + 1) * LANES)
            q2, do2 = q_ref[:, cols], do_ref[:, cols]
            prod = do2.astype(F32) * o_ref[:, cols].astype(F32)
            dq2 = jnp.zeros((tq, LANES), F32)
            dsk = jnp.zeros((1, LANES), F32)
            for a, hm in enumerate(_halves()):
                qa = jnp.where(hm, q2, jnp.zeros_like(q2))
                doa = jnp.where(hm, do2, jnp.zeros_like(do2))
                lse_a = lse_ref[:, pp * LANES + a * HEAD_DIM:pp * LANES + a * HEAD_DIM + 1]
                sc = jnp.where(mask, _nt(qa, k) * scale, NEG)
                p = jnp.exp(sc - lse_a)
                delta = jnp.sum(jnp.where(hm, prod, 0.0), axis=-1, keepdims=True)
                ds = (p * (_nt(doa, v) - delta) * scale).astype(BF16)
                dv = dv + _tn(p.astype(BF16), doa)
                dk = dk + _tn(ds, qa)
                dq2 = dq2 + jnp.where(hm, jnp.dot(ds, k, preferred_element_type=F32), 0.0)
                sink = sk_ref[:, pp * LANES + a * HEAD_DIM:pp * LANES + a * HEAD_DIM + 1]
                dsink = -jnp.sum(jnp.exp(sink - lse_a) * delta, axis=0, keepdims=True)
                dsk = dsk + jnp.where(hm, dsink, 0.0)
            dq_ref[:, cols] = dq2.astype(BF16)
            dsk_ref[0:1, cols] += dsk
        start = pl.multiple_of(i * tq, tq)
        dk_ref[pl.ds(start, tq), :] += dk[WINDOW:, :]
        dv_ref[pl.ds(start, tq), :] += dv[WINDOW:, :]

        @pl.when(i > 0)
        def _():
            before = pl.multiple_of(i * tq - WINDOW, WINDOW)
            dk_ref[pl.ds(before, WINDOW), :] += dk[:WINDOW, :]
            dv_ref[pl.ds(before, WINDOW), :] += dv[:WINDOW, :]

    prev = lambda g, i: (jnp.maximum(i * r - 1, 0), g)
    cur = lambda g, i: (i, g)
    wide = pl.BlockSpec((tq, gw), cur)
    full = pl.BlockSpec((s, LANES), lambda g, i: (0, g))
    return _pcall(body, name=name, grid=(n_kv, s // tq),
                  in_specs=[wide, pl.BlockSpec((WINDOW, LANES), prev), pl.BlockSpec((tq, LANES), cur),
                            pl.BlockSpec((WINDOW, LANES), prev), pl.BlockSpec((tq, LANES), cur),
                            pl.BlockSpec((1, gw), lambda g, i: (0, g)), wide, wide, wide],
                  out_specs=[wide, full, full, pl.BlockSpec((8, gw), lambda g, i: (0, g))],
                  out_shape=[jax.ShapeDtypeStruct((s, qd), BF16), jax.ShapeDtypeStruct(kd.shape, F32),
                             jax.ShapeDtypeStruct(kd.shape, F32), jax.ShapeDtypeStruct((8, qd), F32)],
                  compiler_params=_cp(("parallel", "arbitrary"), VMEM_BIG))(q, kd, kd, vd, vd, sink_row, out, lse, dout)


def forget_cumsum(f_logit, b_row, name):
    s = f_logit.shape[0]

    def body(f_ref, b_ref, c_ref):
        z = f_ref[...] + b_ref[...]
        acc = jnp.minimum(z, 0.0) - jnp.log(1.0 + jnp.exp(-jnp.abs(z)))
        row = lax.broadcasted_iota(jnp.int32, (s, LANES), 0)
        d = 1
        while d < s:
            acc = acc + jnp.where(row >= d, pltpu.roll(acc, d, 0), 0.0)
            d *= 2
        c_ref[...] = acc

    return _pcall(body, name=name, in_specs=[pl.BlockSpec((s, LANES), lambda: (0, 0)), pl.BlockSpec((1, LANES), lambda: (0, 0))],
                  out_specs=pl.BlockSpec((s, LANES), lambda: (0, 0)), out_shape=jax.ShapeDtypeStruct((s, LANES), F32),
                  compiler_params=_cp(None, VMEM_BIG))(f_logit, b_row)


def forget_gate_bwd(dc, f_logit, b_row, name):
    s = dc.shape[0]

    def body(dc_ref, f_ref, b_ref, df_ref, db_ref):
        acc = dc_ref[...]
        row = lax.broadcasted_iota(jnp.int32, (s, LANES), 0)
        d = 1
        while d < s:
            acc = acc + jnp.where(row < s - d, pltpu.roll(acc, s - d, 0), 0.0)
            d *= 2
        df = acc * jax.nn.sigmoid(-(f_ref[...] + b_ref[...]))
        df_ref[...] = df.astype(BF16)
        db_ref[...] = jnp.sum(df, axis=0, keepdims=True)

    whole = pl.BlockSpec((s, LANES), lambda: (0, 0))
    vec = pl.BlockSpec((1, LANES), lambda: (0, 0))
    return _pcall(body, name=name, in_specs=[whole, whole, vec], out_specs=[whole, vec],
                  out_shape=[jax.ShapeDtypeStruct((s, LANES), BF16), jax.ShapeDtypeStruct((1, LANES), F32)],
                  compiler_params=_cp(None, VMEM_BIG))(dc, f_logit, b_row)


EXTRA = HEAD_DIM
N_PIECES = 3


def _pieces(v):
    hi = v.astype(BF16).astype(F32)
    mid = (v - hi).astype(BF16).astype(F32)
    return hi, mid, (v - hi - mid).astype(BF16).astype(F32)


def _slot(main, lane, extras=None, ones_at=None):
    out = jnp.where(lane < HEAD_DIM, main, 0.0)
    if extras is not None:
        for r, e in enumerate(extras):
            out = jnp.where(lane == EXTRA + r, e, out)
    if ones_at is not None:
        out = jnp.where((lane >= ones_at) & (lane < ones_at + N_PIECES), 1.0, out)
    return out


def _lane_iota():
    return lax.broadcasted_iota(jnp.int32, (1, LANES), 1)


def fox_split(qkv, c, n_heads, name):
    s = qkv.shape[0]
    hd = n_heads * HEAD_DIM
    tr = _tile(s, 256)
    scale = HEAD_DIM ** -0.5

    def body(x_ref, c_ref, q_ref, k_ref, v_ref):
        lane = _lane_iota()
        cv = c_ref[...]
        for g in range(hd // LANES):
            for part, dst in enumerate((q_ref, k_ref, v_ref)):
                t = x_ref[:, part * hd + g * LANES:part * hd + (g + 1) * LANES].astype(F32)
                for a, main in enumerate((t, pltpu.roll(t, HEAD_DIM, 1))):
                    h = 2 * g + a
                    if part == 0:
                        val = _slot(main * scale, lane, ones_at=EXTRA)
                    elif part == 1:
                        ch = jnp.sum(jnp.where(lane == h, cv, 0.0), axis=1, keepdims=True)
                        val = _slot(main, lane, extras=_pieces(-ch), ones_at=EXTRA + N_PIECES)
                    else:
                        val = _slot(main, lane, ones_at=EXTRA)
                    dst[:, h * LANES:(h + 1) * LANES] = val.astype(BF16)

    slots = jax.ShapeDtypeStruct((s, n_heads * LANES), BF16)
    return _pcall(body, name=name, grid=(s // tr,), in_specs=[_row_spec(tr, 3 * hd), _row_spec(tr, LANES)],
                  out_specs=[_row_spec(tr, n_heads * LANES)] * 3, out_shape=[slots] * 3,
                  compiler_params=_cp(("parallel",), VMEM_BIG))(qkv, c)


def _causal_keep(t):
    return lax.broadcasted_iota(jnp.int32, (t, t), 1) <= lax.broadcasted_iota(jnp.int32, (t, t), 0)


def fox_fwd(qs, ks, vs, n_heads, name):
    s = qs.shape[0]
    pairs = n_heads // 2
    t = min(512, s)
    ratio = 2 if s % (2 * t) == 0 else 1
    tq = ratio * t
    wide = 2 * LANES

    def body(q_ref, k_ref, v_ref, o_ref, qb_ref, acc_ref, m_ref):
        i = pl.program_id(1)
        lane = _lane_iota()
        acc_ref[...] = jnp.zeros_like(acc_ref)
        m_ref[...] = jnp.full_like(m_ref, NEG)

        def step(j, diagonal):
            rows = pl.ds(pl.multiple_of(j * t, t), t)
            first = 0 if diagonal is None else diagonal * t
            live = slice(first, tq)
            for a in range(2):
                cols = slice(a * LANES, (a + 1) * LANES)
                sc = _nt(q_ref[live, cols], k_ref[rows, cols])
                if diagonal is not None:
                    shape = (tq - first, t)
                    sc = jnp.where(lax.broadcasted_iota(jnp.int32, shape, 1) <= lax.broadcasted_iota(jnp.int32, shape, 0), sc, NEG)
                m_old = m_ref[a, live, :]
                m_new = jnp.maximum(m_old, jnp.max(sc, axis=-1, keepdims=True))
                p = jnp.exp(sc - jnp.tile(m_new, (1, t // LANES)))
                acc_ref[a, live, :] = jnp.exp(m_old - m_new) * acc_ref[a, live, :] + jnp.dot(p.astype(BF16), v_ref[rows, cols],
                                                                                             preferred_element_type=F32)
                m_ref[a, live, :] = m_new

        def two_steps(j2, carry):
            step(2 * j2, None)
            step(2 * j2 + 1, None)
            return carry

        past = ratio * i
        lax.fori_loop(0, past // 2, two_steps, 0)
        if ratio % 2:
            @pl.when(past % 2 == 1)
            def _():
                step(past - 1, None)

        for u in range(ratio):
            step(past + u, u)
        outs = []
        for a in range(2):
            cols = slice(a * LANES, (a + 1) * LANES)
            acc = acc_ref[a]
            norm = acc[:, EXTRA:EXTRA + 1]
            outs.append(acc / norm)
            neg_lse = _pieces(-(m_ref[a] + jnp.log(norm)))
            qb = q_ref[:, cols].astype(F32)
            for r in range(N_PIECES):
                qb = jnp.where(lane == EXTRA + N_PIECES + r, neg_lse[r], qb)
            qb_ref[:, cols] = qb.astype(BF16)
        o_ref[...] = jnp.where(lane < HEAD_DIM, outs[0], pltpu.roll(outs[1], HEAD_DIM, 1)).astype(BF16)

    qblk = pl.BlockSpec((tq, wide), lambda p, i: (i, p))
    whole = pl.BlockSpec((s, wide), lambda p, i: (0, p))
    return _pcall(body, name=name, grid=(pairs, s // tq), in_specs=[qblk, whole, whole],
                  out_specs=[pl.BlockSpec((tq, LANES), lambda p, i: (i, p)), qblk],
                  out_shape=[jax.ShapeDtypeStruct((s, n_heads * HEAD_DIM), BF16), jax.ShapeDtypeStruct(qs.shape, BF16)],
                  scratch_shapes=[pltpu.VMEM((2, tq, LANES), F32)] * 2,
                  compiler_params=_cp(("parallel", "arbitrary"), VMEM_BIG))(qs, ks, vs)


def fox_dout_slots(dout, out, name):
    s, hd = dout.shape
    tr = _tile(s, 256)

    def body(d_ref, o_ref, s_ref):
        lane = _lane_iota()
        for g in range(hd // LANES):
            cols = slice(g * LANES, (g + 1) * LANES)
            d2 = d_ref[:, cols].astype(F32)
            prod = d2 * o_ref[:, cols].astype(F32)
            for a, main in enumerate((d2, pltpu.roll(d2, HEAD_DIM, 1))):
                delta = jnp.sum(jnp.where((lane // HEAD_DIM) == a, prod, 0.0), axis=1, keepdims=True)
                h = 2 * g + a
                s_ref[:, h * LANES:(h + 1) * LANES] = _slot(main, lane, extras=_pieces(-delta)).astype(BF16)

    return _pcall(body, name=name, grid=(s // tr,), in_specs=[_row_spec(tr, hd), _row_spec(tr, hd)],
                  out_specs=_row_spec(tr, 2 * hd), out_shape=jax.ShapeDtypeStruct((s, 2 * hd), BF16),
                  compiler_params=_cp(("parallel",)))(dout, out)


def fox_bwd(qb, ks, vs, dos, n_heads, name):
    s = qb.shape[0]
    pairs = n_heads // 2
    t = min(512, s)
    nblk = s // t
    wide = 2 * LANES

    def body(q_ref, k_ref, v_ref, do_ref, dq_ref, dk_ref, dv_ref, dka_ref, dva_ref):
        j = pl.program_id(1)

        @pl.when(j == 0)
        def _():
            dq_ref[...] = jnp.zeros_like(dq_ref)

        dka_ref[...] = jnp.zeros_like(dka_ref)
        dva_ref[...] = jnp.zeros_like(dva_ref)

        def step(i, diagonal, blocks=1):
            rows = pl.ds(pl.multiple_of(i * t, t), blocks * t)
            for a in range(2):
                cols = slice(a * LANES, (a + 1) * LANES)
                qa, doa, ka = q_ref[rows, cols], do_ref[rows, cols], k_ref[:, cols]
                sc = _nt(qa, ka)
                if diagonal:
                    sc = jnp.where(_causal_keep(t), sc, NEG)
                p = jnp.exp(sc)
                ds = (p * _nt(doa, v_ref[:, cols])).astype(BF16)
                dva_ref[a] += _tn(p.astype(BF16), doa)
                dka_ref[a] += _tn(ds, qa)
                dq_ref[rows, cols] += jnp.dot(ds, ka, preferred_element_type=F32)

        step(j, True)
        below = nblk - 1 - j

        def two_steps(i2, carry):
            step(j + 1 + 2 * i2, False, blocks=2)
            return carry

        lax.fori_loop(0, below // 2, two_steps, 0)

        @pl.when(below % 2 == 1)
        def _():
            step(nblk - 1, False)

        for a in range(2):
            cols = slice(a * LANES, (a + 1) * LANES)
            dk_ref[:, cols] = dka_ref[a]
            dv_ref[:, cols] = dva_ref[a].astype(BF16)

    whole = pl.BlockSpec((s, wide), lambda p, j: (0, p))
    blk = pl.BlockSpec((t, wide), lambda p, j: (j, p))
    return _pcall(body, name=name, grid=(pairs, nblk), in_specs=[whole, blk, blk, whole], out_specs=[whole, blk, blk],
                  out_shape=[jax.ShapeDtypeStruct(qb.shape, F32), jax.ShapeDtypeStruct(qb.shape, F32),
                             jax.ShapeDtypeStruct(qb.shape, BF16)],
                  scratch_shapes=[pltpu.VMEM((2, t, LANES), F32)] * 2,
                  compiler_params=_cp(("parallel", "arbitrary"), VMEM_BIG))(qb, ks, vs, dos)


def fox_merge_bwd(dqs, dks, dvs, n_heads, name):
    s = dqs.shape[0]
    hd = n_heads * HEAD_DIM
    tr = _tile(s, 128)
    scale = HEAD_DIM ** -0.5

    def body(dq_ref, dk_ref, dv_ref, o_ref, dc_ref):
        lane = _lane_iota()
        dc = jnp.zeros((tr, LANES), F32)
        for g in range(hd // LANES):
            even = slice(2 * g * LANES, (2 * g + 1) * LANES)
            odd = slice((2 * g + 1) * LANES, (2 * g + 2) * LANES)
            for part, (src, mul) in enumerate(((dq_ref, scale), (dk_ref, 1.0), (dv_ref, 1.0))):
                dense = jnp.where(lane < HEAD_DIM, src[:, even].astype(F32), pltpu.roll(src[:, odd].astype(F32), HEAD_DIM, 1))
                o_ref[:, part * hd + g * LANES:part * hd + (g + 1) * LANES] = (dense * mul).astype(BF16)
            for a, cols in enumerate((even, odd)):
                both = jnp.where(lane == EXTRA + N_PIECES, dq_ref[:, cols], 0.0) - jnp.where(lane == EXTRA, dk_ref[:, cols], 0.0)
                dc = jnp.where(lane == 2 * g + a, jnp.sum(both, axis=1, keepdims=True), dc)
        dc_ref[...] = dc

    wide = n_heads * LANES
    return _pcall(body, name=name, grid=(s // tr,), in_specs=[_row_spec(tr, wide)] * 3,
                  out_specs=[_row_spec(tr, 3 * hd), _row_spec(tr, LANES)],
                  out_shape=[jax.ShapeDtypeStruct((s, 3 * hd), BF16), jax.ShapeDtypeStruct((s, LANES), F32)],
                  compiler_params=_cp(("parallel",), VMEM_BIG))(dqs, dks, dvs)


def _w_spec(shape):
    _, r, c = shape
    if r % 16 == 0 or r <= 128:
        tr = _tile(r, 128)
        return r // tr, pl.BlockSpec((None, tr, c), lambda l, i: (l, i, 0))
    tc = 2 * LANES
    return c // tc, pl.BlockSpec((None, r, tc), lambda l, i: (l, 0, i))


def cast_into_slot(w, layer, place, dtype, name):
    tr = _tile(w.shape[1], 128)
    blk = (None, tr, w.shape[2])

    def body(me_ref, c_ref, w_ref, o_ref):
        o_ref[...] = w_ref[...].astype(dtype)

    grid_spec = pltpu.PrefetchScalarGridSpec(
        num_scalar_prefetch=2, grid=(w.shape[1] // tr,),
        in_specs=[pl.BlockSpec(blk, lambda i, me, c: (layer, i, 0))],
        out_specs=pl.BlockSpec(blk, lambda i, me, c: (me[0], i, 0)))
    return _pcall(body, name=name, grid_spec=grid_spec, out_shape=jax.ShapeDtypeStruct((N_CHIPS,) + w.shape[1:], dtype),
                  compiler_params=_cp(("parallel",)))(*place, w)


def adamw(w, g, m, v, name):
    steps, spec = _w_spec(w.shape)

    def body(w_ref, g_ref, m_ref, v_ref, d_ref, mo_ref, vo_ref):
        gg = g_ref[...]
        mn = ADAM_B1 * m_ref[...] + (1.0 - ADAM_B1) * gg
        vn = ADAM_B2 * v_ref[...] + (1.0 - ADAM_B2) * (gg * gg)
        m_hat = mn / (1.0 - ADAM_B1 ** ADAM_STEP)
        v_hat = vn / (1.0 - ADAM_B2 ** ADAM_STEP)
        d_ref[...] = -ADAM_LR * (m_hat / (jnp.sqrt(v_hat) + ADAM_EPS) + ADAM_WD * w_ref[...])
        mo_ref[...] = mn
        vo_ref[...] = vn

    out = jax.ShapeDtypeStruct(w.shape, F32)
    return _pcall(body, name=name, grid=(w.shape[0], steps), in_specs=[spec] * 4, out_specs=[spec] * 3,
                  out_shape=[out] * 3, compiler_params=_cp(("parallel", "parallel"), VMEM_BIG))(w, g, m, v)


def add_core_halves(g, recv, place, name):
    hr = recv.shape[1]
    tr = _tile(hr, 128)
    nb = hr // tr
    blk = (None, tr, g.shape[2])

    def body(me_ref, c_ref, g_ref, r_ref, o_ref):
        o_ref[...] = (g_ref[...].astype(F32) + r_ref[...].astype(F32)).astype(o_ref.dtype)

    grid_spec = pltpu.PrefetchScalarGridSpec(
        num_scalar_prefetch=2, grid=(N_CHIPS, nb),
        in_specs=[pl.BlockSpec(blk, lambda q, i, me, c: (q, c[0] * nb + i, 0)), pl.BlockSpec(blk, lambda q, i, me, c: (q, i, 0))],
        out_specs=pl.BlockSpec(blk, lambda q, i, me, c: (q, i, 0)))
    return _pcall(body, name=name, grid_spec=grid_spec, out_shape=jax.ShapeDtypeStruct(recv.shape, g.dtype),
                  compiler_params=_cp(("parallel", "parallel")))(*place, g, recv)


def add_chips(own, recv, into, layer, place, name):
    _, hr, cols = own.shape
    tr = _tile(hr, 128)
    nb = hr // tr
    blk = (None, tr, cols)

    def body(me_ref, c_ref, p0, p1, p2, p3, _, o_ref):
        o_ref[...] = ((p0[...].astype(F32) + p1[...].astype(F32)) + p2[...].astype(F32)) + p3[...].astype(F32)

    def peer(flip):
        return lambda i, me, c: (me[0] ^ flip, i, 0)

    grid_spec = pltpu.PrefetchScalarGridSpec(
        num_scalar_prefetch=2, grid=(nb,), in_specs=[pl.BlockSpec(blk, peer(f)) for f in (0, 2, 1, 3)] + [ANY],
        out_specs=pl.BlockSpec(blk, lambda i, me, c: (layer, c[0] * nb + i, 0)))
    return _pcall(body, name=name, grid_spec=grid_spec, out_shape=jax.ShapeDtypeStruct(into.shape, F32),
                  input_output_aliases={6: 0}, compiler_params=_cp(("parallel",)))(*place, own, recv, recv, recv, into)


def _place():
    x, y, c = lax.axis_index("x"), lax.axis_index("y"), lax.axis_index("c")
    others = [(1 - x, y), (x, 1 - y), (1 - x, 1 - y)]
    return x, y, c, others


def _chip_id(chip):
    return 2 * chip[0] + chip[1]


def _comm_call(body, name, n_in, out_shapes, n_sems, in_place=False):
    return _pcall(body, name=name, in_specs=[ANY] * n_in, out_specs=[ANY] * len(out_shapes), out_shape=out_shapes,
                  scratch_shapes=[pltpu.SemaphoreType.DMA((n_sems,)), pltpu.SemaphoreType.DMA((n_sems,))],
                  input_output_aliases={t: t for t in range(n_in)} if in_place else {},
                  compiler_params=pltpu.CompilerParams(has_side_effects=True))


def swap_core_halves(grads, name):
    n = len(grads)

    def body(*refs):
        ins, outs = refs[:n], refs[n:2 * n]
        send, recv = refs[2 * n:]
        x, y, c, _ = _place()
        cps = []
        for t in range(n):
            hr = ins[t].shape[1] // 2
            cp = pltpu.make_async_remote_copy(src_ref=ins[t].at[:, pl.ds((1 - c) * hr, hr)], dst_ref=outs[t],
                                              send_sem=send.at[t], recv_sem=recv.at[t],
                                              device_id=(x, y, 1 - c), device_id_type=MESH)
            cp.start()
            cps.append(cp)
        for cp in cps:
            cp.wait()

    outs = [jax.ShapeDtypeStruct((a.shape[0], a.shape[1] // 2) + a.shape[2:], a.dtype) for a in grads]
    return _comm_call(body, name, n, outs, n)(*grads)


HBM = pl.BlockSpec(memory_space=pltpu.HBM)
SEM = pl.BlockSpec(memory_space=pltpu.SEMAPHORE)
DATAFLOW = pltpu.SideEffectType.DATAFLOW_SIDE_EFFECTING


def _in_hbm(a):
    return pltpu.with_memory_space_constraint(a, pltpu.HBM)


def gather_start(slots, name):
    n = len(slots)

    def body(*refs):
        bufs = refs[:n]
        send, recv = refs[n], refs[n + 1]
        token = refs[-1]
        x, y, c, others = _place()
        me = _chip_id((x, y))
        for t in range(n):
            for j, chip in enumerate(others):
                pltpu.make_async_remote_copy(src_ref=bufs[t].at[me], dst_ref=bufs[t].at[me],
                                             send_sem=send.at[3 * t + j], recv_sem=recv.at[3 * t + j],
                                             device_id=(*chip, c), device_id_type=MESH).start()
        token[...] = jnp.zeros_like(token)

    sems = pltpu.SemaphoreType.DMA((3 * n,))
    res = _pcall(body, name=name, in_specs=[HBM] * n,
                 out_shape=(sems, sems, *[pltpu.HBM(a.shape, a.dtype) for a in slots], jax.ShapeDtypeStruct((8, LANES), F32)),
                 out_specs=(SEM, SEM, *[HBM] * n, pl.BlockSpec(memory_space=pltpu.VMEM)),
                 input_output_aliases={i: 2 + i for i in range(n)},
                 compiler_params=pltpu.CompilerParams(has_side_effects=DATAFLOW))(*[_in_hbm(a) for a in slots])
    return res[0], res[1], list(res[2:2 + n]), res[-1]


def gather_wait(send, recv, slot, t, after, name):
    def body(buf, send_ref, recv_ref, after_ref, out):
        x, y, c, others = _place()
        me = _chip_id((x, y))
        for j, chip in enumerate(others):
            pltpu.make_async_remote_copy(src_ref=buf.at[me], dst_ref=buf.at[_chip_id(chip)],
                                         send_sem=send_ref.at[3 * t + j], recv_sem=recv_ref.at[3 * t + j],
                                         device_id=(*chip, c), device_id_type=MESH).wait()

    return _pcall(body, name=name, in_specs=[HBM, SEM, SEM, ANY], out_shape=pltpu.HBM(slot.shape, slot.dtype),
                  out_specs=HBM, input_output_aliases={0: 0},
                  compiler_params=pltpu.CompilerParams(has_side_effects=DATAFLOW))(slot, send, recv, after)


def scatter_start(sums, name):
    n = len(sums)

    def body(*refs):
        ins, lands = refs[:n], refs[n:2 * n]
        send, recv = refs[2 * n], refs[2 * n + 1]
        token = refs[-1]
        x, y, c, others = _place()
        me = _chip_id((x, y))
        for t in range(n):
            for j, chip in enumerate(others):
                pltpu.make_async_remote_copy(src_ref=ins[t].at[_chip_id(chip)], dst_ref=lands[t].at[me],
                                             send_sem=send.at[3 * t + j], recv_sem=recv.at[3 * t + j],
                                             device_id=(*chip, c), device_id_type=MESH).start()
        token[...] = jnp.zeros_like(token)

    bufs = [pltpu.HBM(a.shape, a.dtype) for a in sums]
    sems = pltpu.SemaphoreType.DMA((3 * n,))
    res = _pcall(body, name=name, in_specs=[HBM] * (2 * n),
                 out_shape=(sems, sems, *bufs, *bufs, jax.ShapeDtypeStruct((8, LANES), F32)),
                 out_specs=(SEM, SEM, *[HBM] * (2 * n), pl.BlockSpec(memory_space=pltpu.VMEM)),
                 input_output_aliases={i: 2 + i for i in range(2 * n)},
                 compiler_params=pltpu.CompilerParams(has_side_effects=DATAFLOW))(
        *[_in_hbm(a) for a in sums], *[_in_hbm(lax.empty(a.shape, a.dtype)) for a in sums])
    return res[0], res[1], list(res[2:2 + n]), list(res[2 + n:2 + 2 * n]), res[-1]


def scatter_wait(send, recv, sums, lands, after, name):
    n = len(sums)

    def body(*refs):
        ins, bufs = refs[:n], refs[n:2 * n]
        send_ref, recv_ref = refs[2 * n], refs[2 * n + 1]
        x, y, c, others = _place()
        me = _chip_id((x, y))
        for t in range(n):
            for j, chip in enumerate(others):
                cp = pltpu.make_async_remote_copy(src_ref=ins[t].at[_chip_id(chip)], dst_ref=bufs[t].at[_chip_id(chip)],
                                                  send_sem=send_ref.at[3 * t + j], recv_sem=recv_ref.at[3 * t + j],
                                                  device_id=(*chip, c), device_id_type=MESH)
                cp.wait_send()
                cp.wait_recv()

    shapes = [pltpu.HBM(a.shape, a.dtype) for a in sums]
    res = _pcall(body, name=name, in_specs=[HBM] * (2 * n) + [SEM, SEM, ANY],
                 out_shape=(*shapes, *shapes), out_specs=tuple([HBM] * (2 * n)),
                 input_output_aliases={i: i for i in range(2 * n)},
                 compiler_params=pltpu.CompilerParams(has_side_effects=DATAFLOW))(*sums, *lands, send, recv, after)
    return list(res[:n]), list(res[n:])


def join_core_halves(fulls, name):
    n = len(fulls)

    def body(*refs):
        bufs = refs[n:2 * n]
        send, recv = refs[2 * n:]
        x, y, c, _ = _place()
        cps = []
        for t in range(n):
            hr = bufs[t].shape[1] // 2
            mine = bufs[t].at[:, pl.ds(c * hr, hr)]
            cp = pltpu.make_async_remote_copy(src_ref=mine, dst_ref=mine, send_sem=send.at[t], recv_sem=recv.at[t],
                                              device_id=(x, y, 1 - c), device_id_type=MESH)
            cp.start()
            cps.append(cp)
        for t in range(n):
            hr = bufs[t].shape[1] // 2
            theirs = bufs[t].at[:, pl.ds((1 - c) * hr, hr)]
            cps[t].wait_send()
            pltpu.make_async_remote_copy(src_ref=theirs, dst_ref=theirs, send_sem=send.at[t], recv_sem=recv.at[t],
                                         device_id=(x, y, c), device_id_type=MESH).wait_recv()

    outs = [jax.ShapeDtypeStruct(a.shape, a.dtype) for a in fulls]
    return _comm_call(body, name, n, outs, n, in_place=True)(*fulls)


def kernel(x, positions, norm_gains, swa_w_in, swa_sinks, swa_w_out, fox_w_in, fox_b_f, fox_w_out, ffn_w_gate_up, ffn_w_down, loss_target, m_norm_gains, m_swa_w_in, m_swa_sinks, m_swa_w_out, m_fox_w_in, m_fox_b_f, m_fox_w_out, m_ffn_w_gate_up, m_ffn_w_down, v_norm_gains, v_swa_w_in, v_swa_sinks, v_swa_w_out, v_fox_w_in, v_fox_b_f, v_fox_w_out, v_ffn_w_gate_up, v_ffn_w_down):
    s, d = x.shape[1], x.shape[2]
    depth = norm_gains.shape[0]
    n_heads = d // HEAD_DIM
    hd = n_heads * HEAD_DIM
    n_kv = (swa_w_in.shape[2] * N_CHIPS // HEAD_DIM - n_heads) // 2
    ff = ffn_w_down.shape[1] * N_CHIPS
    fox_cols = fox_w_in.shape[2]
    fox_pad = 3 * hd + LANES
    assert fox_cols * N_CHIPS == 3 * hd + n_heads and n_heads <= LANES
    x0 = x[0]
    target = loss_target[0]
    place = ((2 * lax.axis_index("x") + lax.axis_index("y")).astype(jnp.int32).reshape(1),
             lax.axis_index("c").astype(jnp.int32).reshape(1))

    order = [("norm_gains", norm_gains.reshape(1, depth * 4, norm_gains.shape[2]), 0, F32)]
    for layer in range(depth):
        j = layer // 2
        kind, w_i, w_o = ("swa", swa_w_in, swa_w_out) if layer % 2 == 0 else ("fox", fox_w_in, fox_w_out)
        order += [(f"{kind}_w_in_{j}", w_i, j, BF16), (f"{kind}_w_out_{j}", w_o, j, BF16),
                  (f"ffn_w_gate_up_{layer}", ffn_w_gate_up, layer, BF16), (f"ffn_w_down_{layer}", ffn_w_down, layer, BF16)]
    slots = [cast_into_slot(w, l, place, dt, "cast_" + nm) for nm, w, l, dt in order]
    g_send, g_recv, slots, g_token = gather_start(slots, "gather_start")
    slot_of = {entry[0]: t for t, entry in enumerate(order)}

    def weight(nm, after):
        t = slot_of[nm]
        return gather_wait(g_send, g_recv, slots[t], t, after, "gather_wait_" + nm)

    gains = jnp.transpose(weight("norm_gains", g_token), (1, 0, 2)).reshape(depth * 4, d)

    def gain(layer, which):
        return gains[layer * 4 + which][None, :]

    def fox_weight(w_in):
        parts = [w_in[q] for q in range(N_CHIPS)]
        parts.append(jnp.zeros((d, fox_pad - fox_cols * N_CHIPS), BF16))
        return jnp.concatenate(parts, axis=1)

    inv_freq = ROPE_THETA ** (-jnp.arange(0, ROT_DIM, 2, dtype=F32) / ROT_DIM)
    lane_d = jnp.arange(LANES) % HEAD_DIM
    invf_row = jnp.where(lane_d < ROT_DIM, inv_freq[lane_d % (ROT_DIM // 2)], 0.0)[None, :]
    tabs = rope_tables(positions.reshape(s, 1), invf_row, "rope_tables")

    n_sh_in = swa_w_in.shape[2]
    gu_sh = ffn_w_gate_up.shape[2]
    tn_gu = gu_sh // 2 if (gu_sh // 2) % LANES == 0 else gu_sh
    down_sh = ffn_w_down.shape[1]
    out_sh = swa_w_out.shape[1]
    tm = min(1024, s)

    saved = []
    xin = x0
    h = prenorm(xin, gain(0, 0), "prenorm_first")
    for layer in range(depth):
        j = layer // 2
        rec = {"x_in": xin, "h1": h}
        if layer % 2 == 0:
            w_in = weight(f"swa_w_in_{j}", h)
            proj = mm_nn(h, w_in, lambda n, k: (n, k, 0), n_sh_in * N_CHIPS, tm=tm, tn=n_sh_in, tk=d,
                         out_dtype=F32, name=f"swa_proj_{j}")
            q, kd, vd = swa_split(proj, tabs, n_heads, n_kv, f"swa_split_{j}")
            sink_row = jnp.repeat(swa_sinks[j], HEAD_DIM)[None, :]
            attn, lse = swa_fwd(q, kd, vd, sink_row, f"swa_fwd_{j}")
            rec.update(q=q, kd=kd, vd=vd, sink_row=sink_row, lse=lse, w_in=w_in)
            w_out = weight(f"swa_w_out_{j}", attn)
        else:
            wf = fox_weight(weight(f"fox_w_in_{j}", h))
            tn = 3 * hd // 6 if (3 * hd // 6) % LANES == 0 else LANES
            qkv = mm_nn(h, wf, lambda n, k: (k, n), 3 * hd, tm=tm, tn=tn, tk=d, out_dtype=BF16, name=f"fox_proj_{j}")
            f_off = 3 * hd // LANES
            f_logit = mm_nn(h, wf, lambda n, k: (k, f_off + n), LANES, tm=tm, tn=LANES, tk=d, out_dtype=F32, name=f"fox_gate_{j}")
            b_row = jnp.pad(fox_b_f[j], (0, LANES - n_heads))[None, :]
            c = forget_cumsum(f_logit, b_row, f"fox_cumsum_{j}")
            qs, ks, vs = fox_split(qkv, c, n_heads, f"fox_split_{j}")
            attn, qb = fox_fwd(qs, ks, vs, n_heads, f"fox_fwd_{j}")
            rec.update(wf=wf, f_logit=f_logit, b_row=b_row, qb=qb, ks=ks, vs=vs)
            w_out = weight(f"fox_w_out_{j}", attn)
        y = mm_nn(attn, w_out, lambda n, k: (k, 0, n), d, tm=tm, tn=d, tk=out_sh, out_dtype=F32, name=f"out_proj_{layer}")
        xmid, h2 = postnorm_residual(xin, y, gain(layer, 1), gain(layer, 2), f"postnorm_mixer_{layer}")
        npb = gu_sh // tn_gu
        w_gu = weight(f"ffn_w_gate_up_{layer}", h2)
        nbf = ff // tn_gu
        gate, up, act = ffn_up_swiglu(h2, w_gu, lambda n: (n // npb, 0, n % npb), lambda n: ((n + nbf) // npb, 0, (n + nbf) % npb),
                                      ff, tm=min(512, s), tn=tn_gu, name=f"ffn_up_{layer}")
        w_down = weight(f"ffn_w_down_{layer}", act)
        y2 = mm_nn(act, w_down, lambda n, k: (k, 0, n), d, tm=tm, tn=d, tk=down_sh, out_dtype=F32, name=f"ffn_down_{layer}")
        rec.update(attn=attn, y=y, x_mid=xmid, h2=h2, gate=gate, up=up, act=act, y2=y2, w_out=w_out, w_gu=w_gu, w_down=w_down)
        saved.append(rec)
        if layer + 1 < depth:
            xin, h = postnorm_residual(xmid, y2, gain(layer, 3), gain(layer + 1, 0), f"postnorm_ffn_{layer}")
    dx, loss_blk = postnorm_loss(xmid, y2, gain(depth - 1, 3), target, "loss")

    finals = {nm: lax.empty(w.shape, F32) for nm, w in (("swa_w_in", swa_w_in), ("swa_w_out", swa_w_out), ("fox_w_in", fox_w_in),
                                                        ("fox_w_out", fox_w_out), ("ffn_w_gate_up", ffn_w_gate_up),
                                                        ("ffn_w_down", ffn_w_down), ("norm_gains", order[0][1]))}
    in_flight = []

    def finish_reduce(after):
        send_, recv_, sums_, lands_, keys = in_flight.pop()
        sums_, got = scatter_wait(send_, recv_, sums_, lands_, after, "reduce_scatter_wait_" + keys[0][2])
        for own, rcv, (key, l, nm) in zip(sums_, got, keys):
            finals[key] = add_chips(own, rcv, finals[key], l, place, "reduce_add_chips_" + nm)

    def start_reduce(partials, keys):
        if in_flight:
            finish_reduce(partials[0])
        halves = swap_core_halves(partials, "reduce_swap_" + keys[0][2])
        sums_ = [add_core_halves(g, r, place, "reduce_add_cores_" + k[2]) for g, r, k in zip(partials, halves, keys)]
        send_, recv_, sums_, lands_, token = scatter_start(sums_, "reduce_scatter_start_" + keys[0][2])
        in_flight.append((send_, recv_, sums_, lands_, keys))
        return token

    dgains = [None] * (depth * 4)
    dsinks = [None] * ((depth + 1) // 2)
    dbf = [None] * (depth // 2)
    tko = d
    for layer in reversed(range(depth)):
        j = layer // 2
        rec = saved[layer]
        dy2, dgains[layer * 4 + 3] = postnorm_bwd(dx, rec["y2"], gain(layer, 3), f"postnorm_ffn_bwd_{layer}")
        tnd = d
        g_down = mm_tn(rec["act"], dy2, lambda i, n: (i, 0, n), tka=down_sh, tn=tnd, tm=tm, out_block=(None, down_sh, tnd),
                       name=f"ffn_down_dw_{layer}", out_shape=jax.ShapeDtypeStruct((N_CHIPS, down_sh, d), BF16))
        dgu = ffn_down_dx_swiglu(dy2, rec["w_down"], lambda n: (n, 0, 0), rec["gate"], rec["up"], tm=min(512, s), tn=down_sh,
                                 name=f"ffn_down_dx_{layer}")
        npb = gu_sh // tn_gu
        g_gu = mm_tn(rec["h2"], dgu, lambda i, n: (n // npb, i, n % npb), tka=tko, tn=tn_gu, tm=tm, out_block=(None, tko, tn_gu),
                     name=f"ffn_up_dw_{layer}", out_shape=jax.ShapeDtypeStruct((N_CHIPS, d, gu_sh), BF16))
        sent = start_reduce([g_gu, g_down], [("ffn_w_gate_up", layer, f"ffn_w_gate_up_{layer}"), ("ffn_w_down", layer, f"ffn_w_down_{layer}")])
        dh2 = mm_nt(dgu, rec["w_gu"], lambda o, n: (n // npb, o, n % npb), d, tm=tm, tko=tko, tn=tn_gu, out_dtype=F32,
                    name=f"ffn_up_dx_{layer}", after=sent)
        dxm, dgains[layer * 4 + 2] = prenorm_bwd(rec["x_mid"], dh2, dx, gain(layer, 2), f"prenorm_ffn_bwd_{layer}")
        dy, dgains[layer * 4 + 1] = postnorm_bwd(dxm, rec["y"], gain(layer, 1), f"postnorm_mixer_bwd_{layer}")
        g_out = mm_tn(rec["attn"], dy, lambda i, n: (i, 0, n), tka=out_sh, tn=tnd, tm=min(2048, s), out_block=(None, out_sh, tnd),
                      name=f"out_proj_dw_{layer}", out_shape=jax.ShapeDtypeStruct((N_CHIPS, out_sh, d), BF16))
        dattn = mm_nt(dy, rec["w_out"], lambda o, n: (o, 0, n), hd, tm=tm, tko=out_sh, tn=d, out_dtype=BF16, name=f"out_proj_dx_{layer}")
        if layer % 2 == 0:
            dq, dkd, dvd, dsk = swa_bwd(rec["q"], rec["kd"], rec["vd"], rec["sink_row"], rec["attn"], rec["lse"], dattn, f"swa_bwd_{j}")
            dsinks[j] = dsk[0].reshape(n_heads, HEAD_DIM)[:, 0]
            dproj = swa_merge_bwd(dq, dkd, dvd, tabs, f"swa_merge_bwd_{j}")
            g_in = mm_tn(rec["h1"], dproj, lambda i, n: (n, i, 0), tka=tko, tn=n_sh_in, tm=tm, out_block=(None, tko, n_sh_in),
                         name=f"swa_proj_dw_{j}", out_shape=jax.ShapeDtypeStruct((N_CHIPS, d, n_sh_in), BF16))
            sent = start_reduce([g_in, g_out], [("swa_w_in", j, f"swa_w_in_{j}"), ("swa_w_out", j, f"swa_w_out_{j}")])
            dh1 = mm_nt(dproj, rec["w_in"], lambda o, n: (n, o, 0), d, tm=tm, tko=tko, tn=n_sh_in, out_dtype=F32,
                        name=f"swa_proj_dx_{j}", after=sent)
        else:
            dos = fox_dout_slots(dattn, rec["attn"], f"fox_dout_slots_{j}")
            dqs, dks, dvs = fox_bwd(rec["qb"], rec["ks"], rec["vs"], dos, n_heads, f"fox_bwd_{j}")
            dqkv, dc = fox_merge_bwd(dqs, dks, dvs, n_heads, f"fox_merge_bwd_{j}")
            df, db = forget_gate_bwd(dc, rec["f_logit"], rec["b_row"], f"fox_gate_bwd_{j}")
            dbf[j] = db[0, :n_heads]
            dproj = jnp.concatenate([dqkv, df], axis=1)
            tn_f = LANES * max(k for k in range(1, 9) if (fox_pad // LANES) % k == 0)
            dwf = mm_tn(rec["h1"], dproj, lambda i, n: (i, n), tka=tko, tn=tn_f, tm=tm, out_block=(tko, tn_f),
                        name=f"fox_proj_dw_{j}", out_shape=jax.ShapeDtypeStruct((d, fox_pad), BF16))
            g_in = jnp.stack([dwf[:, q * fox_cols:(q + 1) * fox_cols] for q in range(N_CHIPS)])
            sent = start_reduce([g_in, g_out], [("fox_w_in", j, f"fox_w_in_{j}"), ("fox_w_out", j, f"fox_w_out_{j}")])
            dh1 = mm_nt(dproj, rec["wf"], lambda o, n: (o, n), d, tm=tm, tko=tko, tn=tn_f, out_dtype=F32,
                        name=f"fox_proj_dx_{j}", after=sent)
        dx, dgains[layer * 4] = prenorm_bwd(rec["x_in"], dh1, dxm, gain(layer, 0), f"prenorm_mixer_bwd_{layer}")
    grad_x = dx[None]

    dgain_full = jnp.concatenate(dgains, axis=0)
    start_reduce([jnp.transpose(dgain_full.reshape(depth * 4, N_CHIPS, -1), (1, 0, 2))], [("norm_gains", 0, "norm_gains")])
    finish_reduce(dx)
    keys = ["swa_w_in", "swa_w_out", "fox_w_in", "fox_w_out", "ffn_w_gate_up", "ffn_w_down", "norm_gains"]
    full = join_core_halves([finals[k] for k in keys], "reduce_join_cores")
    g_swa_in_f, g_swa_out_f, g_fox_in_f, g_fox_out_f, g_gu_f, g_down_f, g_gains_f = full
    g_gains_f = g_gains_f.reshape(norm_gains.shape)

    n_swa, n_fox = len(dsinks), len(dbf)
    small = jnp.concatenate([loss_blk[0, :1]] + dsinks + dbf)
    small = lax.psum(small, ("x", "y", "c"))
    loss = small[0]
    g_sinks = small[1:1 + n_swa * n_heads].reshape(n_swa, n_heads)
    g_bf = small[1 + n_swa * n_heads:].reshape(n_fox, n_heads)

    def pad_small(a):
        return jnp.pad(a, ((0, 8 - a.shape[0]), (0, LANES - a.shape[1])))[None]

    def update(w, g, m, v, nm):
        if w.ndim == 2:
            dl, mn, vn = adamw(pad_small(w), pad_small(g), pad_small(m), pad_small(v), "adamw_" + nm)
            return tuple(a[0, :w.shape[0], :w.shape[1]] for a in (dl, mn, vn))
        if w.shape[2] % LANES and w.shape[1] % LANES == 0:
            turn = lambda a: jnp.swapaxes(a, 1, 2)
            return tuple(turn(a) for a in adamw(turn(w), turn(g), turn(m), turn(v), "adamw_" + nm))
        return adamw(w, g, m, v, "adamw_" + nm)

    grads = [g_gains_f, g_swa_in_f, g_sinks, g_swa_out_f, g_fox_in_f, g_bf, g_fox_out_f, g_gu_f, g_down_f]
    ws = [norm_gains, swa_w_in, swa_sinks, swa_w_out, fox_w_in, fox_b_f, fox_w_out, ffn_w_gate_up, ffn_w_down]
    ms = [m_norm_gains, m_swa_w_in, m_swa_sinks, m_swa_w_out, m_fox_w_in, m_fox_b_f, m_fox_w_out, m_ffn_w_gate_up, m_ffn_w_down]
    vs = [v_norm_gains, v_swa_w_in, v_swa_sinks, v_swa_w_out, v_fox_w_in, v_fox_b_f, v_fox_w_out, v_ffn_w_gate_up, v_ffn_w_down]
    nms = ["norm_gains", "swa_w_in", "swa_sinks", "swa_w_out", "fox_w_in", "fox_b_f", "fox_w_out", "ffn_w_gate_up", "ffn_w_down"]
    upd = [update(w, g, m, v, nm) for w, g, m, v, nm in zip(ws, grads, ms, vs, nms)]
    return (loss, grad_x, *grads, *[u[0] for u in upd], *[u[1] for u in upd], *[u[2] for u in upd])
```

```python
import jax
import jax.numpy as jnp
from jax import lax
from jax.experimental import pallas as pl
from jax.experimental.pallas import tpu as pltpu

F32 = jnp.float32
BF16 = jnp.bfloat16
HEAD_DIM = 64
LANES = 128
WINDOW = 128
ROPE_THETA = 500000.0
ROT_DIM = HEAD_DIM // 4
RMS_EPS = 1e-6
ADAM_LR, ADAM_B1, ADAM_B2, ADAM_EPS, ADAM_WD, ADAM_STEP = 0.001, 0.9, 0.999, 1e-08, 0.01, 10
NEG = -1e30
VMEM_BIG = 56 * 1024 * 1024
N_CHIPS = 4
MESH = pl.DeviceIdType.MESH
ANY = pl.BlockSpec(memory_space=pl.ANY)


def _pcall(body, **kw):
    return pl.pallas_call(body, **kw)


def _cp(sem=None, vmem=None):
    return pltpu.CompilerParams(dimension_semantics=sem, vmem_limit_bytes=vmem)


def _tile(n, pref):
    if n <= pref:
        return n
    t = pref - pref % 16
    while n % t:
        t -= 16
    return t


_DIMS = {"nn": (((1,), (0,)), ((), ())), "nt": (((1,), (1,)), ((), ())), "tn": (((0,), (0,)), ((), ()))}


def _matmul(kind, a, b, *, grid, a_spec, b_spec, o_spec, out_shape, acc_shape, name, after=None):
    nk = grid[2]
    dims = _DIMS[kind]

    def body(a_ref, b_ref, *rest):
        rest = rest[1:] if after is not None else rest
        o_ref = rest[0]

        def prod():
            return lax.dot_general(a_ref[...], b_ref[...], dims, preferred_element_type=F32)

        if nk == 1:
            o_ref[...] = prod().astype(o_ref.dtype)
        else:
            acc = rest[1]
            k = pl.program_id(2)

            @pl.when(k == 0)
            def _():
                acc[...] = prod()

            @pl.when(k > 0)
            def _():
                acc[...] += prod()

            @pl.when(k == nk - 1)
            def _():
                o_ref[...] = acc[...].astype(o_ref.dtype)

    args, in_specs = [a, b], [a_spec, b_spec]
    if after is not None:
        args.append(after)
        in_specs.append(ANY)
    return _pcall(body, name=name, grid=grid, in_specs=in_specs, out_specs=o_spec, out_shape=out_shape,
                  scratch_shapes=[] if nk == 1 else [pltpu.VMEM(acc_shape, F32)],
                  compiler_params=_cp(("parallel", "parallel", "arbitrary"), VMEM_BIG))(*args)


def mm_nn(a, w, w_map, n_out, *, tm, tn, tk, out_dtype, name):
    m, kdim = a.shape
    tm = min(tm, m)
    lead = (None,) * (w.ndim - 2)
    return _matmul("nn", a, w, grid=(m // tm, n_out // tn, kdim // tk),
                   a_spec=pl.BlockSpec((tm, tk), lambda i, j, k: (i, k)),
                   b_spec=pl.BlockSpec(lead + (tk, tn), lambda i, j, k: w_map(j, k)),
                   o_spec=pl.BlockSpec((tm, tn), lambda i, j, k: (i, j)),
                   out_shape=jax.ShapeDtypeStruct((m, n_out), out_dtype), acc_shape=(tm, tn), name=name)


def _split_cols(a, tn):
    if a.ndim == 2:
        return a.shape[0], a.shape[1], (), lambda row, col: (row, col)
    per = a.shape[2] // tn
    return a.shape[1], a.shape[0] * a.shape[2], (None,), lambda row, col: (col // per, row, col % per)


def mm_nt(a, w, w_map, k_out, *, tm, tko, tn, out_dtype, name, after=None):
    m, ndim, a_lead, a_map = _split_cols(a, tn)
    tm = min(tm, m)
    lead = (None,) * (w.ndim - 2)
    return _matmul("nt", a, w, grid=(m // tm, k_out // tko, ndim // tn),
                   a_spec=pl.BlockSpec(a_lead + (tm, tn), lambda i, j, n: a_map(i, n)),
                   b_spec=pl.BlockSpec(lead + (tko, tn), lambda i, j, n: w_map(j, n)),
                   o_spec=pl.BlockSpec((tm, tko), lambda i, j, n: (i, j)),
                   out_shape=jax.ShapeDtypeStruct((m, k_out), out_dtype), acc_shape=(tm, tko), name=name, after=after)


def mm_tn(a, b, o_map, *, tka, tn, tm, out_block, name, out_shape):
    m, kdim = a.shape
    _, n, b_lead, b_map = _split_cols(b, tn)
    tm = min(tm, m)
    return _matmul("tn", a, b, grid=(kdim // tka, n // tn, m // tm),
                   a_spec=pl.BlockSpec((tm, tka), lambda i, j, mm: (mm, i)),
                   b_spec=pl.BlockSpec(b_lead + (tm, tn), lambda i, j, mm: b_map(mm, j)),
                   o_spec=pl.BlockSpec(out_block, lambda i, j, mm: o_map(i, j)),
                   out_shape=out_shape, acc_shape=(tka, tn), name=name)


def _rstd(v):
    return lax.rsqrt(jnp.mean(v * v, axis=-1, keepdims=True) + RMS_EPS)


def _row_spec(tr, d):
    return pl.BlockSpec((tr, d), lambda i: (i, 0))


def _vec_spec(d):
    return pl.BlockSpec((1, d), lambda i: (0, 0))


def prenorm(x, g, name):
    s, d = x.shape
    tr = _tile(s, 256)

    def body(x_ref, g_ref, h_ref):
        v = x_ref[...]
        h_ref[...] = (v * _rstd(v) * g_ref[...]).astype(BF16)

    return _pcall(body, name=name, grid=(s // tr,), in_specs=[_row_spec(tr, d), _vec_spec(d)],
                  out_specs=_row_spec(tr, d), out_shape=jax.ShapeDtypeStruct((s, d), BF16),
                  compiler_params=_cp(("parallel",)))(x, g)


def postnorm_residual(x, y, g_post, g_next, name):
    s, d = x.shape
    tr = _tile(s, 256)

    def body(x_ref, y_ref, gp_ref, gn_ref, xo_ref, h_ref):
        v = y_ref[...]
        xn = x_ref[...] + v * _rstd(v) * gp_ref[...]
        xo_ref[...] = xn
        h_ref[...] = (xn * _rstd(xn) * gn_ref[...]).astype(BF16)

    return _pcall(body, name=name, grid=(s // tr,),
                  in_specs=[_row_spec(tr, d), _row_spec(tr, d), _vec_spec(d), _vec_spec(d)],
                  out_specs=[_row_spec(tr, d), _row_spec(tr, d)],
                  out_shape=[jax.ShapeDtypeStruct((s, d), F32), jax.ShapeDtypeStruct((s, d), BF16)],
                  compiler_params=_cp(("parallel",)))(x, y, g_post, g_next)


def postnorm_loss(x, y, g_post, target, name):
    s, d = x.shape
    tr = _tile(s, 256)

    def body(x_ref, y_ref, gp_ref, t_ref, dx_ref, loss_ref):
        v = y_ref[...]
        err = x_ref[...] + v * _rstd(v) * gp_ref[...] - t_ref[...]
        dx_ref[...] = err / d
        part = 0.5 * jnp.sum(jnp.mean(err * err, axis=-1, keepdims=True), axis=0, keepdims=True)

        @pl.when(pl.program_id(0) == 0)
        def _():
            loss_ref[...] = jnp.zeros_like(loss_ref)

        loss_ref[...] += part

    return _pcall(body, name=name, grid=(s // tr,),
                  in_specs=[_row_spec(tr, d), _row_spec(tr, d), _vec_spec(d), _row_spec(tr, d)],
                  out_specs=[_row_spec(tr, d), pl.BlockSpec((8, LANES), lambda i: (0, 0))],
                  out_shape=[jax.ShapeDtypeStruct((s, d), F32), jax.ShapeDtypeStruct((8, LANES), F32)],
                  compiler_params=_cp(("arbitrary",)))(x, y, g_post, target)


def _norm_bwd(v, g, dz):
    r = _rstd(v)
    vhat = v * r
    u = dz * g
    dv = r * (u - vhat * jnp.mean(u * vhat, axis=-1, keepdims=True))
    return dv, jnp.sum(dz * vhat, axis=0, keepdims=True)


def _acc_rows(ref, val):
    @pl.when(pl.program_id(0) == 0)
    def _():
        ref[...] = jnp.zeros_like(ref)

    ref[...] += val


def postnorm_bwd(dz, y, g, name):
    s, d = y.shape
    tr = _tile(s, 256)

    def body(dz_ref, y_ref, g_ref, dy_ref, dg_ref):
        dv, dg = _norm_bwd(y_ref[...], g_ref[...], dz_ref[...])
        dy_ref[...] = dv.astype(BF16)
        _acc_rows(dg_ref, dg)

    return _pcall(body, name=name, grid=(s // tr,), in_specs=[_row_spec(tr, d), _row_spec(tr, d), _vec_spec(d)],
                  out_specs=[_row_spec(tr, d), _vec_spec(d)],
                  out_shape=[jax.ShapeDtypeStruct((s, d), BF16), jax.ShapeDtypeStruct((1, d), F32)],
                  compiler_params=_cp(("arbitrary",)))(dz, y, g)


def prenorm_bwd(x, dh, dskip, g, name):
    s, d = x.shape
    tr = _tile(s, 256)

    def body(x_ref, dh_ref, ds_ref, g_ref, dx_ref, dg_ref):
        dv, dg = _norm_bwd(x_ref[...], g_ref[...], dh_ref[...])
        dx_ref[...] = ds_ref[...] + dv
        _acc_rows(dg_ref, dg)

    return _pcall(body, name=name, grid=(s // tr,),
                  in_specs=[_row_spec(tr, d), _row_spec(tr, d), _row_spec(tr, d), _vec_spec(d)],
                  out_specs=[_row_spec(tr, d), _vec_spec(d)],
                  out_shape=[jax.ShapeDtypeStruct((s, d), F32), jax.ShapeDtypeStruct((1, d), F32)],
                  compiler_params=_cp(("arbitrary",)))(x, dh, dskip, g)


def ffn_up_swiglu(h, w, gate_map, up_map, ff, *, tm, tn, name):
    m, kdim = h.shape
    tm = min(tm, m)
    lead = (None,) * (w.ndim - 2)

    def body(h_ref, wg_ref, wu_ref, g_ref, u_ref, act_ref):
        a = h_ref[...]
        gate = jnp.dot(a, wg_ref[...], preferred_element_type=F32)
        up = jnp.dot(a, wu_ref[...], preferred_element_type=F32)
        g_ref[...] = gate.astype(BF16)
        u_ref[...] = up.astype(BF16)
        act_ref[...] = (gate * jax.nn.sigmoid(gate) * up).astype(BF16)

    out = jax.ShapeDtypeStruct((m, ff), BF16)
    oblk = pl.BlockSpec((tm, tn), lambda n, i: (i, n))
    return _pcall(body, name=name, grid=(ff // tn, m // tm),
                  in_specs=[pl.BlockSpec((tm, kdim), lambda n, i: (i, 0)), pl.BlockSpec(lead + (kdim, tn), lambda n, i: gate_map(n)),
                            pl.BlockSpec(lead + (kdim, tn), lambda n, i: up_map(n))],
                  out_specs=[oblk] * 3, out_shape=[out] * 3,
                  compiler_params=_cp(("parallel", "parallel"), VMEM_BIG))(h, w, w)


def ffn_down_dx_swiglu(dy, w, w_map, gate, up, *, tm, tn, name):
    m, d = dy.shape
    ff = gate.shape[1]
    tm = min(tm, m)
    lead = (None,) * (w.ndim - 2)

    def body(dy_ref, w_ref, g_ref, u_ref, o_ref):
        da = _nt(dy_ref[...], w_ref[...])
        gt = g_ref[...].astype(F32)
        sig = jax.nn.sigmoid(gt)
        o_ref[0] = (da * u_ref[...].astype(F32) * sig * (1.0 + gt * (1.0 - sig))).astype(BF16)
        o_ref[1] = (da * gt * sig).astype(BF16)

    blk = pl.BlockSpec((tm, tn), lambda n, i: (i, n))
    return _pcall(body, name=name, grid=(ff // tn, m // tm),
                  in_specs=[pl.BlockSpec((tm, d), lambda n, i: (i, 0)), pl.BlockSpec(lead + (tn, d), lambda n, i: w_map(n)), blk, blk],
                  out_specs=pl.BlockSpec((2, tm, tn), lambda n, i: (0, i, n)), out_shape=jax.ShapeDtypeStruct((2, m, ff), BF16),
                  compiler_params=_cp(("parallel", "parallel"), VMEM_BIG))(dy, w, gate, up)


def rope_tables(pos_col, invf_row, name):
    s = pos_col.shape[0]
    tr = _tile(s, 1024)

    def body(p_ref, f_ref, c_ref, sa_ref, sb_ref):
        ang = p_ref[...].astype(F32) * f_ref[...]
        d = lax.broadcasted_iota(jnp.int32, (1, LANES), 1) % HEAD_DIM
        cs, sn = jnp.cos(ang), jnp.sin(ang)
        c_ref[...] = jnp.where(d < ROT_DIM, cs, 1.0)
        sa_ref[...] = jnp.where(d < ROT_DIM // 2, -sn, 0.0)
        sb_ref[...] = jnp.where((d >= ROT_DIM // 2) & (d < ROT_DIM), sn, 0.0)

    tab = jax.ShapeDtypeStruct((s, LANES), F32)
    return _pcall(body, name=name, grid=(s // tr,),
                  in_specs=[pl.BlockSpec((tr, 1), lambda i: (i, 0)), _vec_spec(LANES)],
                  out_specs=[_row_spec(tr, LANES)] * 3, out_shape=[tab] * 3, compiler_params=_cp(("parallel",)))(pos_col, invf_row)


def _rot(t, c, sa, sb):
    half = ROT_DIM // 2
    return t * c + pltpu.roll(t, LANES - half, 1) * sa + pltpu.roll(t, half, 1) * sb


def _rot_t(t, c, sa, sb):
    half = ROT_DIM // 2
    return t * c + pltpu.roll(t * sa, half, 1) + pltpu.roll(t * sb, LANES - half, 1)


def swa_split(proj, tabs, n_heads, n_kv, name):
    s, width = proj.shape
    qd, kd = n_heads * HEAD_DIM, n_kv * HEAD_DIM
    tr = _tile(s, 256)

    def body(p_ref, c_ref, sa_ref, sb_ref, q_ref, k_ref, v_ref):
        c, sa, sb = c_ref[...], sa_ref[...], sb_ref[...]
        low = lax.broadcasted_iota(jnp.int32, (1, LANES), 1) < HEAD_DIM
        for g in range(qd // LANES):
            q_ref[:, g * LANES:(g + 1) * LANES] = _rot(p_ref[:, g * LANES:(g + 1) * LANES], c, sa, sb).astype(BF16)
        for g in range(kd // LANES):
            for src, dst, rot in ((qd, k_ref, True), (qd + kd, v_ref, False)):
                t = p_ref[:, src + g * LANES:src + (g + 1) * LANES]
                t = _rot(t, c, sa, sb) if rot else t
                sw = pltpu.roll(t, HEAD_DIM, 1)
                dst[:, (2 * g) * LANES:(2 * g + 1) * LANES] = jnp.where(low, t, sw).astype(BF16)
                dst[:, (2 * g + 1) * LANES:(2 * g + 2) * LANES] = jnp.where(low, sw, t).astype(BF16)

    return _pcall(body, name=name, grid=(s // tr,),
                  in_specs=[_row_spec(tr, width)] + [_row_spec(tr, LANES)] * 3,
                  out_specs=[_row_spec(tr, qd), _row_spec(tr, 2 * kd), _row_spec(tr, 2 * kd)],
                  out_shape=[jax.ShapeDtypeStruct((s, qd), BF16), jax.ShapeDtypeStruct((s, 2 * kd), BF16),
                             jax.ShapeDtypeStruct((s, 2 * kd), BF16)],
                  compiler_params=_cp(("parallel",)))(proj, *tabs)


def swa_merge_bwd(dq, dkd, dvd, tabs, name):
    s, qd = dq.shape
    kd = dkd.shape[1] // 2
    width = qd + 2 * kd
    tr = _tile(s, 256)

    def body(dq_ref, dk_ref, dv_ref, c_ref, sa_ref, sb_ref, o_ref):
        c, sa, sb = c_ref[...], sa_ref[...], sb_ref[...]
        low = lax.broadcasted_iota(jnp.int32, (1, LANES), 1) < HEAD_DIM
        for g in range(qd // LANES):
            t = dq_ref[:, g * LANES:(g + 1) * LANES].astype(F32)
            o_ref[:, g * LANES:(g + 1) * LANES] = _rot_t(t, c, sa, sb).astype(BF16)
        for g in range(kd // LANES):
            for dst, src, rot in ((qd, dk_ref, True), (qd + kd, dv_ref, False)):
                e = src[:, (2 * g) * LANES:(2 * g + 1) * LANES]
                o = src[:, (2 * g + 1) * LANES:(2 * g + 2) * LANES]
                t = jnp.where(low, e + pltpu.roll(e, HEAD_DIM, 1), o + pltpu.roll(o, HEAD_DIM, 1))
                t = _rot_t(t, c, sa, sb) if rot else t
                o_ref[:, dst + g * LANES:dst + (g + 1) * LANES] = t.astype(BF16)

    return _pcall(body, name=name, grid=(s // tr,),
                  in_specs=[_row_spec(tr, qd), _row_spec(tr, 2 * kd), _row_spec(tr, 2 * kd)] + [_row_spec(tr, LANES)] * 3,
                  out_specs=_row_spec(tr, width), out_shape=jax.ShapeDtypeStruct((s, width), BF16),
                  compiler_params=_cp(("parallel",)))(dq, dkd, dvd, *tabs)


def _halves():
    lane_half = lax.broadcasted_iota(jnp.int32, (1, LANES), 1) // HEAD_DIM
    return [lane_half == 0, lane_half == 1]


def _nt(a, b):
    return lax.dot_general(a, b, _DIMS["nt"], preferred_element_type=F32)


def _tn(a, b):
    return lax.dot_general(a, b, _DIMS["tn"], preferred_element_type=F32)


def _band_mask(i, tq):
    w = tq + WINDOW
    rel = lax.broadcasted_iota(jnp.int32, (tq, w), 1) - lax.broadcasted_iota(jnp.int32, (tq, w), 0)
    first = lax.broadcasted_iota(jnp.int32, (tq, w), 1) >= jnp.where(i > 0, 0, WINDOW)
    return (rel >= 1) & (rel <= WINDOW) & first


def swa_fwd(q, kd, vd, sink_row, name):
    s, qd = q.shape
    tq = min(256, s)
    r = tq // WINDOW
    n_kv = kd.shape[1] // LANES
    gw = qd // n_kv
    scale = HEAD_DIM ** -0.5

    def body(q_ref, kp_ref, kc_ref, vp_ref, vc_ref, sk_ref, o_ref, lse_ref):
        i = pl.program_id(1)
        k = jnp.concatenate([kp_ref[...], kc_ref[...]], axis=0)
        v = jnp.concatenate([vp_ref[...], vc_ref[...]], axis=0)
        mask = _band_mask(i, tq)
        hm0 = _halves()[0]
        for pp in range(gw // LANES):
            cols = slice(pp * LANES, (pp + 1) * LANES)
            q2 = q_ref[:, cols]
            outs, lses = [], []
            for a, hm in enumerate(_halves()):
                sc = _nt(jnp.where(hm, q2, jnp.zeros_like(q2)), k) * scale
                sc = jnp.where(mask, sc, NEG)
                sink = sk_ref[:, pp * LANES + a * HEAD_DIM:pp * LANES + a * HEAD_DIM + 1]
                m = jnp.maximum(jnp.max(sc, axis=-1, keepdims=True), sink)
                p = jnp.exp(sc - m)
                den = jnp.sum(p, axis=-1, keepdims=True) + jnp.exp(sink - m)
                outs.append(jnp.dot(p.astype(BF16), v, preferred_element_type=F32) / den)
                lses.append(m + jnp.log(den))
            o_ref[:, cols] = jnp.where(hm0, outs[0], outs[1]).astype(BF16)
            lse_ref[:, cols] = jnp.where(hm0, lses[0], lses[1])

    prev = lambda g, i: (jnp.maximum(i * r - 1, 0), g)
    cur = lambda g, i: (i, g)
    wide = pl.BlockSpec((tq, gw), cur)
    return _pcall(body, name=name, grid=(n_kv, s // tq),
                  in_specs=[wide, pl.BlockSpec((WINDOW, LANES), prev), pl.BlockSpec((tq, LANES), cur),
                            pl.BlockSpec((WINDOW, LANES), prev), pl.BlockSpec((tq, LANES), cur),
                            pl.BlockSpec((1, gw), lambda g, i: (0, g))],
                  out_specs=[wide, wide],
                  out_shape=[jax.ShapeDtypeStruct((s, qd), BF16), jax.ShapeDtypeStruct((s, qd), F32)],
                  compiler_params=_cp(("parallel", "parallel")))(q, kd, kd, vd, vd, sink_row)


def swa_bwd(q, kd, vd, sink_row, out, lse, dout, name):
    s, qd = q.shape
    tq = min(256, s)
    r = tq // WINDOW
    n_kv = kd.shape[1] // LANES
    gw = qd // n_kv
    scale = HEAD_DIM ** -0.5

    def body(q_ref, kp_ref, kc_ref, vp_ref, vc_ref, sk_ref, o_ref, lse_ref, do_ref, dq_ref, dk_ref, dv_ref, dsk_ref):
        i = pl.program_id(1)

        @pl.when(i == 0)
        def _():
            dk_ref[...] = jnp.zeros_like(dk_ref)
            dv_ref[...] = jnp.zeros_like(dv_ref)
            dsk_ref[...] = jnp.zeros_like(dsk_ref)

        k = jnp.concatenate([kp_ref[...], kc_ref[...]], axis=0)
        v = jnp.concatenate([vp_ref[...], vc_ref[...]], axis=0)
        mask = _band_mask(i, tq)
        dk = jnp.zeros((tq + WINDOW, LANES), F32)
        dv = jnp.zeros((tq + WINDOW, LANES), F32)
        for pp in range(gw // LANES):
            cols = slice(pp * LANES, (pp + 1) * LANES)
            q2, do2 = q_ref[:, cols], do_ref[:, cols]
            prod = do2.astype(F32) * o_ref[:, cols].astype(F32)
            dq2 = jnp.zeros((tq, LANES), F32)
            dsk = jnp.zeros((1, LANES), F32)
            for a, hm in enumerate(_halves()):
                qa = jnp.where(hm, q2, jnp.zeros_like(q2))
                doa = jnp.where(hm, do2, jnp.zeros_like(do2))
                lse_a = lse_ref[:, pp * LANES + a * HEAD_DIM:pp * LANES + a * HEAD_DIM + 1]
                sc = jnp.where(mask, _nt(qa, k) * scale, NEG)
                p = jnp.exp(sc - lse_a)
                delta = jnp.sum(jnp.where(hm, prod, 0.0), axis=-1, keepdims=True)
                ds = (p * (_nt(doa, v) - delta) * scale).astype(BF16)
                dv = dv + _tn(p.astype(BF16), doa)
                dk = dk + _tn(ds, qa)
                dq2 = dq2 + jnp.where(hm, jnp.dot(ds, k, preferred_element_type=F32), 0.0)
                sink = sk_ref[:, pp * LANES + a * HEAD_DIM:pp * LANES + a * HEAD_DIM + 1]
                dsink = -jnp.sum(jnp.exp(sink - lse_a) * delta, axis=0, keepdims=True)
                dsk = dsk + jnp.where(hm, dsink, 0.0)
            dq_ref[:, cols] = dq2.astype(BF16)
            dsk_ref[0:1, cols] += dsk
        start = pl.multiple_of(i * tq, tq)
        dk_ref[pl.ds(start, tq), :] += dk[WINDOW:, :]
        dv_ref[pl.ds(start, tq), :] += dv[WINDOW:, :]

        @pl.when(i > 0)
        def _():
            before = pl.multiple_of(i * tq - WINDOW, WINDOW)
            dk_ref[pl.ds(before, WINDOW), :] += dk[:WINDOW, :]
            dv_ref[pl.ds(before, WINDOW), :] += dv[:WINDOW, :]

    prev = lambda g, i: (jnp.maximum(i * r - 1, 0), g)
    cur = lambda g, i: (i, g)
    wide = pl.BlockSpec((tq, gw), cur)
    full = pl.BlockSpec((s, LANES), lambda g, i: (0, g))
    return _pcall(body, name=name, grid=(n_kv, s // tq),
                  in_specs=[wide, pl.BlockSpec((WINDOW, LANES), prev), pl.BlockSpec((tq, LANES), cur),
                            pl.BlockSpec((WINDOW, LANES), prev), pl.BlockSpec((tq, LANES), cur),
                            pl.BlockSpec((1, gw), lambda g, i: (0, g)), wide, wide, wide],
                  out_specs=[wide, full, full, pl.BlockSpec((8, gw), lambda g, i: (0, g))],
                  out_shape=[jax.ShapeDtypeStruct((s, qd), BF16), jax.ShapeDtypeStruct(kd.shape, F32),
                             jax.ShapeDtypeStruct(kd.shape, F32), jax.ShapeDtypeStruct((8, qd), F32)],
                  compiler_params=_cp(("parallel", "arbitrary"), VMEM_BIG))(q, kd, kd, vd, vd, sink_row, out, lse, dout)


def forget_cumsum(f_logit, b_row, name):
    s = f_logit.shape[0]

    def body(f_ref, b_ref, c_ref):
        z = f_ref[...] + b_ref[...]
        acc = jnp.minimum(z, 0.0) - jnp.log(1.0 + jnp.exp(-jnp.abs(z)))
        row = lax.broadcasted_iota(jnp.int32, (s, LANES), 0)
        d = 1
        while d < s:
            acc = acc + jnp.where(row >= d, pltpu.roll(acc, d, 0), 0.0)
            d *= 2
        c_ref[...] = acc

    return _pcall(body, name=name, in_specs=[pl.BlockSpec((s, LANES), lambda: (0, 0)), pl.BlockSpec((1, LANES), lambda: (0, 0))],
                  out_specs=pl.BlockSpec((s, LANES), lambda: (0, 0)), out_shape=jax.ShapeDtypeStruct((s, LANES), F32),
                  compiler_params=_cp(None, VMEM_BIG))(f_logit, b_row)


def forget_gate_bwd(dc, f_logit, b_row, name):
    s = dc.shape[0]

    def body(dc_ref, f_ref, b_ref, df_ref, db_ref):
        acc = dc_ref[...]
        row = lax.broadcasted_iota(jnp.int32, (s, LANES), 0)
        d = 1
        while d < s:
            acc = acc + jnp.where(row < s - d, pltpu.roll(acc, s - d, 0), 0.0)
            d *= 2
        df = acc * jax.nn.sigmoid(-(f_ref[...] + b_ref[...]))
        df_ref[...] = df.astype(BF16)
        db_ref[...] = jnp.sum(df, axis=0, keepdims=True)

    whole = pl.BlockSpec((s, LANES), lambda: (0, 0))
    vec = pl.BlockSpec((1, LANES), lambda: (0, 0))
    return _pcall(body, name=name, in_specs=[whole, whole, vec], out_specs=[whole, vec],
                  out_shape=[jax.ShapeDtypeStruct((s, LANES), BF16), jax.ShapeDtypeStruct((1, LANES), F32)],
                  compiler_params=_cp(None, VMEM_BIG))(dc, f_logit, b_row)


EXTRA = HEAD_DIM
N_PIECES = 3


def _pieces(v):
    hi = v.astype(BF16).astype(F32)
    mid = (v - hi).astype(BF16).astype(F32)
    return hi, mid, (v - hi - mid).astype(BF16).astype(F32)


def _slot(main, lane, extras=None, ones_at=None):
    out = jnp.where(lane < HEAD_DIM, main, 0.0)
    if extras is not None:
        for r, e in enumerate(extras):
            out = jnp.where(lane == EXTRA + r, e, out)
    if ones_at is not None:
        out = jnp.where((lane >= ones_at) & (lane < ones_at + N_PIECES), 1.0, out)
    return out


def _lane_iota():
    return lax.broadcasted_iota(jnp.int32, (1, LANES), 1)


def fox_split(qkv, c, n_heads, name):
    s = qkv.shape[0]
    hd = n_heads * HEAD_DIM
    tr = _tile(s, 256)
    scale = HEAD_DIM ** -0.5

    def body(x_ref, c_ref, q_ref, k_ref, v_ref):
        lane = _lane_iota()
        cv = c_ref[...]
        for g in range(hd // LANES):
            for part, dst in enumerate((q_ref, k_ref, v_ref)):
                t = x_ref[:, part * hd + g * LANES:part * hd + (g + 1) * LANES].astype(F32)
                for a, main in enumerate((t, pltpu.roll(t, HEAD_DIM, 1))):
                    h = 2 * g + a
                    if part == 0:
                        val = _slot(main * scale, lane, ones_at=EXTRA)
                    elif part == 1:
                        ch = jnp.sum(jnp.where(lane == h, cv, 0.0), axis=1, keepdims=True)
                        val = _slot(main, lane, extras=_pieces(-ch), ones_at=EXTRA + N_PIECES)
                    else:
                        val = _slot(main, lane, ones_at=EXTRA)
                    dst[:, h * LANES:(h + 1) * LANES] = val.astype(BF16)

    slots = jax.ShapeDtypeStruct((s, n_heads * LANES), BF16)
    return _pcall(body, name=name, grid=(s // tr,), in_specs=[_row_spec(tr, 3 * hd), _row_spec(tr, LANES)],
                  out_specs=[_row_spec(tr, n_heads * LANES)] * 3, out_shape=[slots] * 3,
                  compiler_params=_cp(("parallel",), VMEM_BIG))(qkv, c)


def _causal_keep(t):
    return lax.broadcasted_iota(jnp.int32, (t, t), 1) <= lax.broadcasted_iota(jnp.int32, (t, t), 0)


def fox_fwd(qs, ks, vs, n_heads, name):
    s = qs.shape[0]
    pairs = n_heads // 2
    t = min(512, s)
    ratio = 2 if s % (2 * t) == 0 else 1
    tq = ratio * t
    wide = 2 * LANES

    def body(q_ref, k_ref, v_ref, o_ref, qb_ref, acc_ref, m_ref):
        i = pl.program_id(1)
        lane = _lane_iota()
        acc_ref[...] = jnp.zeros_like(acc_ref)
        m_ref[...] = jnp.full_like(m_ref, NEG)

        def step(j, diagonal):
            rows = pl.ds(pl.multiple_of(j * t, t), t)
            first = 0 if diagonal is None else diagonal * t
            live = slice(first, tq)
            for a in range(2):
                cols = slice(a * LANES, (a + 1) * LANES)
                sc = _nt(q_ref[live, cols], k_ref[rows, cols])
                if diagonal is not None:
                    shape = (tq - first, t)
                    sc = jnp.where(lax.broadcasted_iota(jnp.int32, shape, 1) <= lax.broadcasted_iota(jnp.int32, shape, 0), sc, NEG)
                m_old = m_ref[a, live, :]
                m_new = jnp.maximum(m_old, jnp.max(sc, axis=-1, keepdims=True))
                p = jnp.exp(sc - jnp.tile(m_new, (1, t // LANES)))
                acc_ref[a, live, :] = jnp.exp(m_old - m_new) * acc_ref[a, live, :] + jnp.dot(p.astype(BF16), v_ref[rows, cols],
                                                                                             preferred_element_type=F32)
                m_ref[a, live, :] = m_new

        def two_steps(j2, carry):
            step(2 * j2, None)
            step(2 * j2 + 1, None)
            return carry

        past = ratio * i
        lax.fori_loop(0, past // 2, two_steps, 0)
        if ratio % 2:
            @pl.when(past % 2 == 1)
            def _():
                step(past - 1, None)

        for u in range(ratio):
            step(past + u, u)
        outs = []
        for a in range(2):
            cols = slice(a * LANES, (a + 1) * LANES)
            acc = acc_ref[a]
            norm = acc[:, EXTRA:EXTRA + 1]
            outs.append(acc / norm)
            neg_lse = _pieces(-(m_ref[a] + jnp.log(norm)))
            qb = q_ref[:, cols].astype(F32)
            for r in range(N_PIECES):
                qb = jnp.where(lane == EXTRA + N_PIECES + r, neg_lse[r], qb)
            qb_ref[:, cols] = qb.astype(BF16)
        o_ref[...] = jnp.where(lane < HEAD_DIM, outs[0], pltpu.roll(outs[1], HEAD_DIM, 1)).astype(BF16)

    qblk = pl.BlockSpec((tq, wide), lambda p, i: (i, p))
    whole = pl.BlockSpec((s, wide), lambda p, i: (0, p))
    return _pcall(body, name=name, grid=(pairs, s // tq), in_specs=[qblk, whole, whole],
                  out_specs=[pl.BlockSpec((tq, LANES), lambda p, i: (i, p)), qblk],
                  out_shape=[jax.ShapeDtypeStruct((s, n_heads * HEAD_DIM), BF16), jax.ShapeDtypeStruct(qs.shape, BF16)],
                  scratch_shapes=[pltpu.VMEM((2, tq, LANES), F32)] * 2,
                  compiler_params=_cp(("parallel", "arbitrary"), VMEM_BIG))(qs, ks, vs)


def fox_dout_slots(dout, out, name):
    s, hd = dout.shape
    tr = _tile(s, 256)

    def body(d_ref, o_ref, s_ref):
        lane = _lane_iota()
        for g in range(hd // LANES):
            cols = slice(g * LANES, (g + 1) * LANES)
            d2 = d_ref[:, cols].astype(F32)
            prod = d2 * o_ref[:, cols].astype(F32)
            for a, main in enumerate((d2, pltpu.roll(d2, HEAD_DIM, 1))):
                delta = jnp.sum(jnp.where((lane // HEAD_DIM) == a, prod, 0.0), axis=1, keepdims=True)
                h = 2 * g + a
                s_ref[:, h * LANES:(h + 1) * LANES] = _slot(main, lane, extras=_pieces(-delta)).astype(BF16)

    return _pcall(body, name=name, grid=(s // tr,), in_specs=[_row_spec(tr, hd), _row_spec(tr, hd)],
                  out_specs=_row_spec(tr, 2 * hd), out_shape=jax.ShapeDtypeStruct((s, 2 * hd), BF16),
                  compiler_params=_cp(("parallel",)))(dout, out)


def fox_bwd(qb, ks, vs, dos, n_heads, name):
    s = qb.shape[0]
    pairs = n_heads // 2
    t = min(512, s)
    nblk = s // t
    wide = 2 * LANES

    def body(q_ref, k_ref, v_ref, do_ref, dq_ref, dk_ref, dv_ref, dka_ref, dva_ref):
        j = pl.program_id(1)

        @pl.when(j == 0)
        def _():
            dq_ref[...] = jnp.zeros_like(dq_ref)

        dka_ref[...] = jnp.zeros_like(dka_ref)
        dva_ref[...] = jnp.zeros_like(dva_ref)

        def step(i, diagonal, blocks=1):
            rows = pl.ds(pl.multiple_of(i * t, t), blocks * t)
            for a in range(2):
                cols = slice(a * LANES, (a + 1) * LANES)
                qa, doa, ka = q_ref[rows, cols], do_ref[rows, cols], k_ref[:, cols]
                sc = _nt(qa, ka)
                if diagonal:
                    sc = jnp.where(_causal_keep(t), sc, NEG)
                p = jnp.exp(sc)
                ds = (p * _nt(doa, v_ref[:, cols])).astype(BF16)
                dva_ref[a] += _tn(p.astype(BF16), doa)
                dka_ref[a] += _tn(ds, qa)
                dq_ref[rows, cols] += jnp.dot(ds, ka, preferred_element_type=F32)

        step(j, True)
        below = nblk - 1 - j

        def two_steps(i2, carry):
            step(j + 1 + 2 * i2, False, blocks=2)
            return carry

        lax.fori_loop(0, below // 2, two_steps, 0)

        @pl.when(below % 2 == 1)
        def _():
            step(nblk - 1, False)

        for a in range(2):
            cols = slice(a * LANES, (a + 1) * LANES)
            dk_ref[:, cols] = dka_ref[a]
            dv_ref[:, cols] = dva_ref[a].astype(BF16)

    whole = pl.BlockSpec((s, wide), lambda p, j: (0, p))
    blk = pl.BlockSpec((t, wide), lambda p, j: (j, p))
    return _pcall(body, name=name, grid=(pairs, nblk), in_specs=[whole, blk, blk, whole], out_specs=[whole, blk, blk],
                  out_shape=[jax.ShapeDtypeStruct(qb.shape, F32), jax.ShapeDtypeStruct(qb.shape, F32),
                             jax.ShapeDtypeStruct(qb.shape, BF16)],
                  scratch_shapes=[pltpu.VMEM((2, t, LANES), F32)] * 2,
                  compiler_params=_cp(("parallel", "arbitrary"), VMEM_BIG))(qb, ks, vs, dos)


def fox_merge_bwd(dqs, dks, dvs, n_heads, name):
    s = dqs.shape[0]
    hd = n_heads * HEAD_DIM
    tr = _tile(s, 128)
    scale = HEAD_DIM ** -0.5

    def body(dq_ref, dk_ref, dv_ref, o_ref, dc_ref):
        lane = _lane_iota()
        dc = jnp.zeros((tr, LANES), F32)
        for g in range(hd // LANES):
            even = slice(2 * g * LANES, (2 * g + 1) * LANES)
            odd = slice((2 * g + 1) * LANES, (2 * g + 2) * LANES)
            for part, (src, mul) in enumerate(((dq_ref, scale), (dk_ref, 1.0), (dv_ref, 1.0))):
                dense = jnp.where(lane < HEAD_DIM, src[:, even].astype(F32), pltpu.roll(src[:, odd].astype(F32), HEAD_DIM, 1))
                o_ref[:, part * hd + g * LANES:part * hd + (g + 1) * LANES] = (dense * mul).astype(BF16)
            for a, cols in enumerate((even, odd)):
                both = jnp.where(lane == EXTRA + N_PIECES, dq_ref[:, cols], 0.0) - jnp.where(lane == EXTRA, dk_ref[:, cols], 0.0)
                dc = jnp.where(lane == 2 * g + a, jnp.sum(both, axis=1, keepdims=True), dc)
        dc_ref[...] = dc

    wide = n_heads * LANES
    return _pcall(body, name=name, grid=(s // tr,), in_specs=[_row_spec(tr, wide)] * 3,
                  out_specs=[_row_spec(tr, 3 * hd), _row_spec(tr, LANES)],
                  out_shape=[jax.ShapeDtypeStruct((s, 3 * hd), BF16), jax.ShapeDtypeStruct((s, LANES), F32)],
                  compiler_params=_cp(("parallel",), VMEM_BIG))(dqs, dks, dvs)


def _w_spec(shape):
    _, r, c = shape
    if r % 16 == 0 or r <= 128:
        tr = _tile(r, 128)
        return r // tr, pl.BlockSpec((None, tr, c), lambda l, i: (l, i, 0))
    tc = 2 * LANES
    return c // tc, pl.BlockSpec((None, r, tc), lambda l, i: (l, 0, i))


def cast_into_slot(w, layer, place, dtype, name):
    tr = _tile(w.shape[1], 128)
    blk = (None, tr, w.shape[2])

    def body(me_ref, c_ref, w_ref, o_ref):
        o_ref[...] = w_ref[...].astype(dtype)

    grid_spec = pltpu.PrefetchScalarGridSpec(
        num_scalar_prefetch=2, grid=(w.shape[1] // tr,),
        in_specs=[pl.BlockSpec(blk, lambda i, me, c: (layer, i, 0))],
        out_specs=pl.BlockSpec(blk, lambda i, me, c: (me[0], i, 0)))
    return _pcall(body, name=name, grid_spec=grid_spec, out_shape=jax.ShapeDtypeStruct((N_CHIPS,) + w.shape[1:], dtype),
                  compiler_params=_cp(("parallel",)))(*place, w)


def adamw(w, g, m, v, name):
    steps, spec = _w_spec(w.shape)

    def body(w_ref, g_ref, m_ref, v_ref, d_ref, mo_ref, vo_ref):
        gg = g_ref[...]
        mn = ADAM_B1 * m_ref[...] + (1.0 - ADAM_B1) * gg
        vn = ADAM_B2 * v_ref[...] + (1.0 - ADAM_B2) * (gg * gg)
        m_hat = mn / (1.0 - ADAM_B1 ** ADAM_STEP)
        v_hat = vn / (1.0 - ADAM_B2 ** ADAM_STEP)
        d_ref[...] = -ADAM_LR * (m_hat / (jnp.sqrt(v_hat) + ADAM_EPS) + ADAM_WD * w_ref[...])
        mo_ref[...] = mn
        vo_ref[...] = vn

    out = jax.ShapeDtypeStruct(w.shape, F32)
    return _pcall(body, name=name, grid=(w.shape[0], steps), in_specs=[spec] * 4, out_specs=[spec] * 3,
                  out_shape=[out] * 3, compiler_params=_cp(("parallel", "parallel"), VMEM_BIG))(w, g, m, v)


def add_core_halves(g, recv, place, name):
    hr = recv.shape[1]
    tr = _tile(hr, 128)
    nb = hr // tr
    blk = (None, tr, g.shape[2])

    def body(me_ref, c_ref, g_ref, r_ref, o_ref):
        o_ref[...] = (g_ref[...].astype(F32) + r_ref[...].astype(F32)).astype(o_ref.dtype)

    grid_spec = pltpu.PrefetchScalarGridSpec(
        num_scalar_prefetch=2, grid=(N_CHIPS, nb),
        in_specs=[pl.BlockSpec(blk, lambda q, i, me, c: (q, c[0] * nb + i, 0)), pl.BlockSpec(blk, lambda q, i, me, c: (q, i, 0))],
        out_specs=pl.BlockSpec(blk, lambda q, i, me, c: (q, i, 0)))
    return _pcall(body, name=name, grid_spec=grid_spec, out_shape=jax.ShapeDtypeStruct(recv.shape, g.dtype),
                  compiler_params=_cp(("parallel", "parallel")))(*place, g, recv)


def add_chips(own, recv, into, layer, place, name):
    _, hr, cols = own.shape
    tr = _tile(hr, 128)
    nb = hr // tr
    blk = (None, tr, cols)

    def body(me_ref, c_ref, p0, p1, p2, p3, _, o_ref):
        o_ref[...] = ((p0[...].astype(F32) + p1[...].astype(F32)) + p2[...].astype(F32)) + p3[...].astype(F32)

    def peer(flip):
        return lambda i, me, c: (me[0] ^ flip, i, 0)

    grid_spec = pltpu.PrefetchScalarGridSpec(
        num_scalar_prefetch=2, grid=(nb,), in_specs=[pl.BlockSpec(blk, peer(f)) for f in (0, 2, 1, 3)] + [ANY],
        out_specs=pl.BlockSpec(blk, lambda i, me, c: (layer, c[0] * nb + i, 0)))
    return _pcall(body, name=name, grid_spec=grid_spec, out_shape=jax.ShapeDtypeStruct(into.shape, F32),
                  input_output_aliases={6: 0}, compiler_params=_cp(("parallel",)))(*place, own, recv, recv, recv, into)


def _place():
    x, y, c = lax.axis_index("x"), lax.axis_index("y"), lax.axis_index("c")
    others = [(1 - x, y), (x, 1 - y), (1 - x, 1 - y)]
    return x, y, c, others


def _chip_id(chip):
    return 2 * chip[0] + chip[1]


def _comm_call(body, name, n_in, out_shapes, n_sems, in_place=False):
    return _pcall(body, name=name, in_specs=[ANY] * n_in, out_specs=[ANY] * len(out_shapes), out_shape=out_shapes,
                  scratch_shapes=[pltpu.SemaphoreType.DMA((n_sems,)), pltpu.SemaphoreType.DMA((n_sems,))],
                  input_output_aliases={t: t for t in range(n_in)} if in_place else {},
                  compiler_params=pltpu.CompilerParams(has_side_effects=True))


def swap_core_halves(grads, name):
    n = len(grads)

    def body(*refs):
        ins, outs = refs[:n], refs[n:2 * n]
        send, recv = refs[2 * n:]
        x, y, c, _ = _place()
        cps = []
        for t in range(n):
            hr = ins[t].shape[1] // 2
            cp = pltpu.make_async_remote_copy(src_ref=ins[t].at[:, pl.ds((1 - c) * hr, hr)], dst_ref=outs[t],
                                              send_sem=send.at[t], recv_sem=recv.at[t],
                                              device_id=(x, y, 1 - c), device_id_type=MESH)
            cp.start()
            cps.append(cp)
        for cp in cps:
            cp.wait()

    outs = [jax.ShapeDtypeStruct((a.shape[0], a.shape[1] // 2) + a.shape[2:], a.dtype) for a in grads]
    return _comm_call(body, name, n, outs, n)(*grads)


HBM = pl.BlockSpec(memory_space=pltpu.HBM)
SEM = pl.BlockSpec(memory_space=pltpu.SEMAPHORE)
DATAFLOW = pltpu.SideEffectType.DATAFLOW_SIDE_EFFECTING


def _in_hbm(a):
    return pltpu.with_memory_space_constraint(a, pltpu.HBM)


def gather_start(slots, name):
    n = len(slots)

    def body(*refs):
        bufs = refs[:n]
        send, recv = refs[n], refs[n + 1]
        token = refs[-1]
        x, y, c, others = _place()
        me = _chip_id((x, y))
        for t in range(n):
            for j, chip in enumerate(others):
                pltpu.make_async_remote_copy(src_ref=bufs[t].at[me], dst_ref=bufs[t].at[me],
                                             send_sem=send.at[3 * t + j], recv_sem=recv.at[3 * t + j],
                                             device_id=(*chip, c), device_id_type=MESH).start()
        token[...] = jnp.zeros_like(token)

    sems = pltpu.SemaphoreType.DMA((3 * n,))
    res = _pcall(body, name=name, in_specs=[HBM] * n,
                 out_shape=(sems, sems, *[pltpu.HBM(a.shape, a.dtype) for a in slots], jax.ShapeDtypeStruct((8, LANES), F32)),
                 out_specs=(SEM, SEM, *[HBM] * n, pl.BlockSpec(memory_space=pltpu.VMEM)),
                 input_output_aliases={i: 2 + i for i in range(n)},
                 compiler_params=pltpu.CompilerParams(has_side_effects=DATAFLOW))(*[_in_hbm(a) for a in slots])
    return res[0], res[1], list(res[2:2 + n]), res[-1]


def gather_wait(send, recv, slot, t, after, name):
    def body(buf, send_ref, recv_ref, after_ref, out):
        x, y, c, others = _place()
        me = _chip_id((x, y))
        for j, chip in enumerate(others):
            pltpu.make_async_remote_copy(src_ref=buf.at[me], dst_ref=buf.at[_chip_id(chip)],
                                         send_sem=send_ref.at[3 * t + j], recv_sem=recv_ref.at[3 * t + j],
                                         device_id=(*chip, c), device_id_type=MESH).wait()

    return _pcall(body, name=name, in_specs=[HBM, SEM, SEM, ANY], out_shape=pltpu.HBM(slot.shape, slot.dtype),
                  out_specs=HBM, input_output_aliases={0: 0},
                  compiler_params=pltpu.CompilerParams(has_side_effects=DATAFLOW))(slot, send, recv, after)


def scatter_start(sums, name):
    n = len(sums)

    def body(*refs):
        ins, lands = refs[:n], refs[n:2 * n]
        send, recv = refs[2 * n], refs[2 * n + 1]
        token = refs[-1]
        x, y, c, others = _place()
        me = _chip_id((x, y))
        for t in range(n):
            for j, chip in enumerate(others):
                pltpu.make_async_remote_copy(src_ref=ins[t].at[_chip_id(chip)], dst_ref=lands[t].at[me],
                                             send_sem=send.at[3 * t + j], recv_sem=recv.at[3 * t + j],
                                             device_id=(*chip, c), device_id_type=MESH).start()
        token[...] = jnp.zeros_like(token)

    bufs = [pltpu.HBM(a.shape, a.dtype) for a in sums]
    sems = pltpu.SemaphoreType.DMA((3 * n,))
    res = _pcall(body, name=name, in_specs=[HBM] * (2 * n),
                 out_shape=(sems, sems, *bufs, *bufs, jax.ShapeDtypeStruct((8, LANES), F32)),
                 out_specs=(SEM, SEM, *[HBM] * (2 * n), pl.BlockSpec(memory_space=pltpu.VMEM)),
                 input_output_aliases={i: 2 + i for i in range(2 * n)},
                 compiler_params=pltpu.CompilerParams(has_side_effects=DATAFLOW))(
        *[_in_hbm(a) for a in sums], *[_in_hbm(lax.empty(a.shape, a.dtype)) for a in sums])
    return res[0], res[1], list(res[2:2 + n]), list(res[2 + n:2 + 2 * n]), res[-1]


def scatter_wait(send, recv, sums, lands, after, name):
    n = len(sums)

    def body(*refs):
        ins, bufs = refs[:n], refs[n:2 * n]
        send_ref, recv_ref = refs[2 * n], refs[2 * n + 1]
        x, y, c, others = _place()
        me = _chip_id((x, y))
        for t in range(n):
            for j, chip in enumerate(others):
                cp = pltpu.make_async_remote_copy(src_ref=ins[t].at[_chip_id(chip)], dst_ref=bufs[t].at[_chip_id(chip)],
                                                  send_sem=send_ref.at[3 * t + j], recv_sem=recv_ref.at[3 * t + j],
                                                  device_id=(*chip, c), device_id_type=MESH)
                cp.wait_send()
                cp.wait_recv()

    shapes = [pltpu.HBM(a.shape, a.dtype) for a in sums]
    res = _pcall(body, name=name, in_specs=[HBM] * (2 * n) + [SEM, SEM, ANY],
                 out_shape=(*shapes, *shapes), out_specs=tuple([HBM] * (2 * n)),
                 input_output_aliases={i: i for i in range(2 * n)},
                 compiler_params=pltpu.CompilerParams(has_side_effects=DATAFLOW))(*sums, *lands, send, recv, after)
    return list(res[:n]), list(res[n:])


def join_core_halves(fulls, name):
    n = len(fulls)

    def body(*refs):
        bufs = refs[n:2 * n]
        send, recv = refs[2 * n:]
        x, y, c, _ = _place()
        cps = []
        for t in range(n):
            hr = bufs[t].shape[1] // 2
            mine = bufs[t].at[:, pl.ds(c * hr, hr)]
            cp = pltpu.make_async_remote_copy(src_ref=mine, dst_ref=mine, send_sem=send.at[t], recv_sem=recv.at[t],
                                              device_id=(x, y, 1 - c), device_id_type=MESH)
            cp.start()
            cps.append(cp)
        for t in range(n):
            hr = bufs[t].shape[1] // 2
            theirs = bufs[t].at[:, pl.ds((1 - c) * hr, hr)]
            cps[t].wait_send()
            pltpu.make_async_remote_copy(src_ref=theirs, dst_ref=theirs, send_sem=send.at[t], recv_sem=recv.at[t],
                                         device_id=(x, y, c), device_id_type=MESH).wait_recv()

    outs = [jax.ShapeDtypeStruct(a.shape, a.dtype) for a in fulls]
    return _comm_call(body, name, n, outs, n, in_place=True)(*fulls)


def kernel(x, positions, norm_gains, swa_w_in, swa_sinks, swa_w_out, fox_w_in, fox_b_f, fox_w_out, ffn_w_gate_up, ffn_w_down, loss_target, m_norm_gains, m_swa_w_in, m_swa_sinks, m_swa_w_out, m_fox_w_in, m_fox_b_f, m_fox_w_out, m_ffn_w_gate_up, m_ffn_w_down, v_norm_gains, v_swa_w_in, v_swa_sinks, v_swa_w_out, v_fox_w_in, v_fox_b_f, v_fox_w_out, v_ffn_w_gate_up, v_ffn_w_down):
    s, d = x.shape[1], x.shape[2]
    depth = norm_gains.shape[0]
    n_heads = d // HEAD_DIM
    hd = n_heads * HEAD_DIM
    n_kv = (swa_w_in.shape[2] * N_CHIPS // HEAD_DIM - n_heads) // 2
    ff = ffn_w_down.shape[1] * N_CHIPS
    fox_cols = fox_w_in.shape[2]
    fox_pad = 3 * hd + LANES
    assert fox_cols * N_CHIPS == 3 * hd + n_heads and n_heads <= LANES
    x0 = x[0]
    target = loss_target[0]
    place = ((2 * lax.axis_index("x") + lax.axis_index("y")).astype(jnp.int32).reshape(1),
             lax.axis_index("c").astype(jnp.int32).reshape(1))

    order = [("norm_gains", norm_gains.reshape(1, depth * 4, norm_gains.shape[2]), 0, F32)]
    for layer in range(depth):
        j = layer // 2
        kind, w_i, w_o = ("swa", swa_w_in, swa_w_out) if layer % 2 == 0 else ("fox", fox_w_in, fox_w_out)
        order += [(f"{kind}_w_in_{j}", w_i, j, BF16), (f"{kind}_w_out_{j}", w_o, j, BF16),
                  (f"ffn_w_gate_up_{layer}", ffn_w_gate_up, layer, BF16), (f"ffn_w_down_{layer}", ffn_w_down, layer, BF16)]
    slots = [cast_into_slot(w, l, place, dt, "cast_" + nm) for nm, w, l, dt in order]
    g_send, g_recv, slots, g_token = gather_start(slots, "gather_start")
    slot_of = {entry[0]: t for t, entry in enumerate(order)}

    def weight(nm, after):
        t = slot_of[nm]
        return gather_wait(g_send, g_recv, slots[t], t, after, "gather_wait_" + nm)

    gains = jnp.transpose(weight("norm_gains", g_token), (1, 0, 2)).reshape(depth * 4, d)

    def gain(layer, which):
        return gains[layer * 4 + which][None, :]

    def fox_weight(w_in):
        parts = [w_in[q] for q in range(N_CHIPS)]
        parts.append(jnp.zeros((d, fox_pad - fox_cols * N_CHIPS), BF16))
        return jnp.concatenate(parts, axis=1)

    inv_freq = ROPE_THETA ** (-jnp.arange(0, ROT_DIM, 2, dtype=F32) / ROT_DIM)
    lane_d = jnp.arange(LANES) % HEAD_DIM
    invf_row = jnp.where(lane_d < ROT_DIM, inv_freq[lane_d % (ROT_DIM // 2)], 0.0)[None, :]
    tabs = rope_tables(positions.reshape(s, 1), invf_row, "rope_tables")

    n_sh_in = swa_w_in.shape[2]
    gu_sh = ffn_w_gate_up.shape[2]
    tn_gu = gu_sh // 2 if (gu_sh // 2) % LANES == 0 else gu_sh
    down_sh = ffn_w_down.shape[1]
    out_sh = swa_w_out.shape[1]
    tm = min(1024, s)

    saved = []
    xin = x0
    h = prenorm(xin, gain(0, 0), "prenorm_first")
    for layer in range(depth):
        j = layer // 2
        rec = {"x_in": xin, "h1": h}
        if layer % 2 == 0:
            w_in = weight(f"swa_w_in_{j}", h)
            proj = mm_nn(h, w_in, lambda n, k: (n, k, 0), n_sh_in * N_CHIPS, tm=tm, tn=n_sh_in, tk=d,
                         out_dtype=F32, name=f"swa_proj_{j}")
            q, kd, vd = swa_split(proj, tabs, n_heads, n_kv, f"swa_split_{j}")
            sink_row = jnp.repeat(swa_sinks[j], HEAD_DIM)[None, :]
            attn, lse = swa_fwd(q, kd, vd, sink_row, f"swa_fwd_{j}")
            rec.update(q=q, kd=kd, vd=vd, sink_row=sink_row, lse=lse, w_in=w_in)
            w_out = weight(f"swa_w_out_{j}", attn)
        else:
            wf = fox_weight(weight(f"fox_w_in_{j}", h))
            tn = 3 * hd // 6 if (3 * hd // 6) % LANES == 0 else LANES
            qkv = mm_nn(h, wf, lambda n, k: (k, n), 3 * hd, tm=tm, tn=tn, tk=d, out_dtype=BF16, name=f"fox_proj_{j}")
            f_off = 3 * hd // LANES
            f_logit = mm_nn(h, wf, lambda n, k: (k, f_off + n), LANES, tm=tm, tn=LANES, tk=d, out_dtype=F32, name=f"fox_gate_{j}")
            b_row = jnp.pad(fox_b_f[j], (0, LANES - n_heads))[None, :]
            c = forget_cumsum(f_logit, b_row, f"fox_cumsum_{j}")
            qs, ks, vs = fox_split(qkv, c, n_heads, f"fox_split_{j}")
            attn, qb = fox_fwd(qs, ks, vs, n_heads, f"fox_fwd_{j}")
            rec.update(wf=wf, f_logit=f_logit, b_row=b_row, qb=qb, ks=ks, vs=vs)
            w_out = weight(f"fox_w_out_{j}", attn)
        y = mm_nn(attn, w_out, lambda n, k: (k, 0, n), d, tm=tm, tn=d, tk=out_sh, out_dtype=F32, name=f"out_proj_{layer}")
        xmid, h2 = postnorm_residual(xin, y, gain(layer, 1), gain(layer, 2), f"postnorm_mixer_{layer}")
        npb = gu_sh // tn_gu
        w_gu = weight(f"ffn_w_gate_up_{layer}", h2)
        nbf = ff // tn_gu
        gate, up, act = ffn_up_swiglu(h2, w_gu, lambda n: (n // npb, 0, n % npb), lambda n: ((n + nbf) // npb, 0, (n + nbf) % npb),
                                      ff, tm=min(512, s), tn=tn_gu, name=f"ffn_up_{layer}")
        w_down = weight(f"ffn_w_down_{layer}", act)
        y2 = mm_nn(act, w_down, lambda n, k: (k, 0, n), d, tm=tm, tn=d, tk=down_sh, out_dtype=F32, name=f"ffn_down_{layer}")
        rec.update(attn=attn, y=y, x_mid=xmid, h2=h2, gate=gate, up=up, act=act, y2=y2, w_out=w_out, w_gu=w_gu, w_down=w_down)
        saved.append(rec)
        if layer + 1 < depth:
            xin, h = postnorm_residual(xmid, y2, gain(layer, 3), gain(layer + 1, 0), f"postnorm_ffn_{layer}")
    dx, loss_blk = postnorm_loss(xmid, y2, gain(depth - 1, 3), target, "loss")

    finals = {nm: lax.empty(w.shape, F32) for nm, w in (("swa_w_in", swa_w_in), ("swa_w_out", swa_w_out), ("fox_w_in", fox_w_in),
                                                        ("fox_w_out", fox_w_out), ("ffn_w_gate_up", ffn_w_gate_up),
                                                        ("ffn_w_down", ffn_w_down), ("norm_gains", order[0][1]))}
    in_flight = []

    def finish_reduce(after):
        send_, recv_, sums_, lands_, keys = in_flight.pop()
        sums_, got = scatter_wait(send_, recv_, sums_, lands_, after, "reduce_scatter_wait_" + keys[0][2])
        for own, rcv, (key, l, nm) in zip(sums_, got, keys):
            finals[key] = add_chips(own, rcv, finals[key], l, place, "reduce_add_chips_" + nm)

    def start_reduce(partials, keys):
        if in_flight:
            finish_reduce(partials[0])
        halves = swap_core_halves(partials, "reduce_swap_" + keys[0][2])
        sums_ = [add_core_halves(g, r, place, "reduce_add_cores_" + k[2]) for g, r, k in zip(partials, halves, keys)]
        send_, recv_, sums_, lands_, token = scatter_start(sums_, "reduce_scatter_start_" + keys[0][2])
        in_flight.append((send_, recv_, sums_, lands_, keys))
        return token

    dgains = [None] * (depth * 4)
    dsinks = [None] * ((depth + 1) // 2)
    dbf = [None] * (depth // 2)
    tko = d
    for layer in reversed(range(depth)):
        j = layer // 2
        rec = saved[layer]
        dy2, dgains[layer * 4 + 3] = postnorm_bwd(dx, rec["y2"], gain(layer, 3), f"postnorm_ffn_bwd_{layer}")
        tnd = d
        g_down = mm_tn(rec["act"], dy2, lambda i, n: (i, 0, n), tka=down_sh, tn=tnd, tm=tm, out_block=(None, down_sh, tnd),
                       name=f"ffn_down_dw_{layer}", out_shape=jax.ShapeDtypeStruct((N_CHIPS, down_sh, d), BF16))
        dgu = ffn_down_dx_swiglu(dy2, rec["w_down"], lambda n: (n, 0, 0), rec["gate"], rec["up"], tm=min(512, s), tn=down_sh,
                                 name=f"ffn_down_dx_{layer}")
        npb = gu_sh // tn_gu
        g_gu = mm_tn(rec["h2"], dgu, lambda i, n: (n // npb, i, n % npb), tka=tko, tn=tn_gu, tm=tm, out_block=(None, tko, tn_gu),
                     name=f"ffn_up_dw_{layer}", out_shape=jax.ShapeDtypeStruct((N_CHIPS, d, gu_sh), BF16))
        sent = start_reduce([g_gu, g_down], [("ffn_w_gate_up", layer, f"ffn_w_gate_up_{layer}"), ("ffn_w_down", layer, f"ffn_w_down_{layer}")])
        dh2 = mm_nt(dgu, rec["w_gu"], lambda o, n: (n // npb, o, n % npb), d, tm=tm, tko=tko, tn=tn_gu, out_dtype=F32,
                    name=f"ffn_up_dx_{layer}", after=sent)
        dxm, dgains[layer * 4 + 2] = prenorm_bwd(rec["x_mid"], dh2, dx, gain(layer, 2), f"prenorm_ffn_bwd_{layer}")
        dy, dgains[layer * 4 + 1] = postnorm_bwd(dxm, rec["y"], gain(layer, 1), f"postnorm_mixer_bwd_{layer}")
        g_out = mm_tn(rec["attn"], dy, lambda i, n: (i, 0, n), tka=out_sh, tn=tnd, tm=min(2048, s), out_block=(None, out_sh, tnd),
                      name=f"out_proj_dw_{layer}", out_shape=jax.ShapeDtypeStruct((N_CHIPS, out_sh, d), BF16))
        dattn = mm_nt(dy, rec["w_out"], lambda o, n: (o, 0, n), hd, tm=tm, tko=out_sh, tn=d, out_dtype=BF16, name=f"out_proj_dx_{layer}")
        if layer % 2 == 0:
            dq, dkd, dvd, dsk = swa_bwd(rec["q"], rec["kd"], rec["vd"], rec["sink_row"], rec["attn"], rec["lse"], dattn, f"swa_bwd_{j}")
            dsinks[j] = dsk[0].reshape(n_heads, HEAD_DIM)[:, 0]
            dproj = swa_merge_bwd(dq, dkd, dvd, tabs, f"swa_merge_bwd_{j}")
            g_in = mm_tn(rec["h1"], dproj, lambda i, n: (n, i, 0), tka=tko, tn=n_sh_in, tm=tm, out_block=(None, tko, n_sh_in),
                         name=f"swa_proj_dw_{j}", out_shape=jax.ShapeDtypeStruct((N_CHIPS, d, n_sh_in), BF16))
            sent = start_reduce([g_in, g_out], [("swa_w_in", j, f"swa_w_in_{j}"), ("swa_w_out", j, f"swa_w_out_{j}")])
            dh1 = mm_nt(dproj, rec["w_in"], lambda o, n: (n, o, 0), d, tm=tm, tko=tko, tn=n_sh_in, out_dtype=F32,
                        name=f"swa_proj_dx_{j}", after=sent)
        else:
            dos = fox_dout_slots(dattn, rec["attn"], f"fox_dout_slots_{j}")
            dqs, dks, dvs = fox_bwd(rec["qb"], rec["ks"], rec["vs"], dos, n_heads, f"fox_bwd_{j}")
            dqkv, dc = fox_merge_bwd(dqs, dks, dvs, n_heads, f"fox_merge_bwd_{j}")
            df, db = forget_gate_bwd(dc, rec["f_logit"], rec["b_row"], f"fox_gate_bwd_{j}")
            dbf[j] = db[0, :n_heads]
            dproj = jnp.concatenate([dqkv, df], axis=1)
            tn_f = LANES * max(k for k in range(1, 9) if (fox_pad // LANES) % k == 0)
            dwf = mm_tn(rec["h1"], dproj, lambda i, n: (i, n), tka=tko, tn=tn_f, tm=tm, out_block=(tko, tn_f),
                        name=f"fox_proj_dw_{j}", out_shape=jax.ShapeDtypeStruct((d, fox_pad), BF16))
            g_in = jnp.stack([dwf[:, q * fox_cols:(q + 1) * fox_cols] for q in range(N_CHIPS)])
            sent = start_reduce([g_in, g_out], [("fox_w_in", j, f"fox_w_in_{j}"), ("fox_w_out", j, f"fox_w_out_{j}")])
            dh1 = mm_nt(dproj, rec["wf"], lambda o, n: (o, n), d, tm=tm, tko=tko, tn=tn_f, out_dtype=F32,
                        name=f"fox_proj_dx_{j}", after=sent)
        dx, dgains[layer * 4] = prenorm_bwd(rec["x_in"], dh1, dxm, gain(layer, 0), f"prenorm_mixer_bwd_{layer}")
    grad_x = dx[None]

    dgain_full = jnp.concatenate(dgains, axis=0)
    start_reduce([jnp.transpose(dgain_full.reshape(depth * 4, N_CHIPS, -1), (1, 0, 2))], [("norm_gains", 0, "norm_gains")])
    finish_reduce(dx)
    keys = ["swa_w_in", "swa_w_out", "fox_w_in", "fox_w_out", "ffn_w_gate_up", "ffn_w_down", "norm_gains"]
    full = join_core_halves([finals[k] for k in keys], "reduce_join_cores")
    g_swa_in_f, g_swa_out_f, g_fox_in_f, g_fox_out_f, g_gu_f, g_down_f, g_gains_f = full
    g_gains_f = g_gains_f.reshape(norm_gains.shape)

    n_swa, n_fox = len(dsinks), len(dbf)
    small = jnp.concatenate([loss_blk[0, :1]] + dsinks + dbf)
    small = lax.psum(small, ("x", "y", "c"))
    loss = small[0]
    g_sinks = small[1:1 + n_swa * n_heads].reshape(n_swa, n_heads)
    g_bf = small[1 + n_swa * n_heads:].reshape(n_fox, n_heads)

    def pad_small(a):
        return jnp.pad(a, ((0, 8 - a.shape[0]), (0, LANES - a.shape[1])))[None]

    def update(w, g, m, v, nm):
        if w.ndim == 2:
            dl, mn, vn = adamw(pad_small(w), pad_small(g), pad_small(m), pad_small(v), "adamw_" + nm)
            return tuple(a[0, :w.shape[0], :w.shape[1]] for a in (dl, mn, vn))
        if w.shape[2] % LANES and w.shape[1] % LANES == 0:
            turn = lambda a: jnp.swapaxes(a, 1, 2)
            return tuple(turn(a) for a in adamw(turn(w), turn(g), turn(m), turn(v), "adamw_" + nm))
        return adamw(w, g, m, v, "adamw_" + nm)

    grads = [g_gains_f, g_swa_in_f, g_sinks, g_swa_out_f, g_fox_in_f, g_bf, g_fox_out_f, g_gu_f, g_down_f]
    ws = [norm_gains, swa_w_in, swa_sinks, swa_w_out, fox_w_in, fox_b_f, fox_w_out, ffn_w_gate_up, ffn_w_down]
    ms = [m_norm_gains, m_swa_w_in, m_swa_sinks, m_swa_w_out, m_fox_w_in, m_fox_b_f, m_fox_w_out, m_ffn_w_gate_up, m_ffn_w_down]
    vs = [v_norm_gains, v_swa_w_in, v_swa_sinks, v_swa_w_out, v_fox_w_in, v_fox_b_f, v_fox_w_out, v_ffn_w_gate_up, v_ffn_w_down]
    nms = ["norm_gains", "swa_w_in", "swa_sinks", "swa_w_out", "fox_w_in", "fox_b_f", "fox_w_out", "ffn_w_gate_up", "ffn_w_down"]
    upd = [update(w, g, m, v, nm) for w, g, m, v, nm in zip(ws, grads, ms, vs, nms)]
    return (loss, grad_x, *grads, *[u[0] for u in upd], *[u[1] for u in upd], *[u[2] for u in upd])
```
